```python
import math
import jax
import jax.numpy as jnp
from jax import lax
import numpy as np

D_MODEL = 2048
BATCH = 2
SEQ = 4096
DEPTH = 1

GRID_W = 64
N_ATTN_HEADS = 8
ATTN_WIDTH = D_MODEL // 2
HEAD_DIM = ATTN_WIDTH // N_ATTN_HEADS
HYENA_WIDTH = D_MODEL - ATTN_WIDTH
N_HYENA_GROUPS = 8
HYENA_ORDER = 2
N_DIRS = 2
SHORT_CONV_W = 3
FILTER_EMB = 33
FILTER_HIDDEN = 64
DECAY_TARGET = 1e-2
FAST_DECAY_PCT = 0.3
SLOW_DECAY_PCT = 1.5
WIN_ROWS_MAX = 8
WIN_COLS = 16
N_EXPERTS = 16
EC_CAPACITY_FACTOR = 2
EXPERT_FF = 1024
IN_WIDTH = 3 * ATTN_WIDTH + (HYENA_ORDER + 1) * HYENA_WIDTH
RMS_EPS = 1e-6

kernel_name = 'hybrid_na2d_hyena_ecmoe_block'


def rms_norm(x, g):
    xf = x.astype(jnp.float32)
    y = xf * lax.rsqrt(jnp.mean(xf * xf, axis=-1, keepdims=True) + RMS_EPS)
    return (y * g.astype(jnp.float32)).astype(x.dtype)


def group_rms_norm(y, g, n_groups):
    b, s, w = y.shape
    yg = y.reshape(b, s, n_groups, w // n_groups)
    return rms_norm(yg, g.reshape(n_groups, w // n_groups)).reshape(b, s, w)


def neighbourhood_attention(q, k, v, rpb):
    b, s, h, hd = q.shape
    rows = s // GRID_W
    win_r = min(WIN_ROWS_MAX, rows)
    r = jnp.arange(rows)
    c = jnp.arange(GRID_W)
    key_rows = jnp.clip(r - win_r // 2, 0, rows - win_r)[:, None] + jnp.arange(win_r)[None, :]
    c0 = jnp.clip(c - WIN_COLS // 2, 0, GRID_W - WIN_COLS)
    in_win = (c[None, :] >= c0[:, None]) & (c[None, :] < c0[:, None] + WIN_COLS)
    dr_idx = key_rows - r[:, None] + WIN_ROWS_MAX - 1
    dc_idx = jnp.clip(c[None, :] - c[:, None] + WIN_COLS - 1, 0, 2 * WIN_COLS - 2)
    bias = rpb[:, dr_idx[:, None, :, None], dc_idx[None, :, None, :]].astype(jnp.float32)
    qg = q.reshape(b, rows, GRID_W, h, hd)
    kb = k.reshape(b, rows, GRID_W, h, hd)[:, key_rows]
    vb = v.reshape(b, rows, GRID_W, h, hd)[:, key_rows]
    scores = jnp.einsum('brqhd,brwkhd->bhrqwk', qg, kb).astype(jnp.float32) * (hd ** -0.5) + bias[None]
    scores = jnp.where(in_win[None, None, None, :, None, :], scores, -jnp.inf)
    probs = jax.nn.softmax(scores.reshape(b, h, rows, GRID_W, win_r * GRID_W), axis=-1)
    probs = probs.reshape(b, h, rows, GRID_W, win_r, GRID_W).astype(v.dtype)
    out = jnp.einsum('bhrqwk,brwkhd->brqhd', probs, vb)
    return out.reshape(b, s, h * hd)


def sinusoidal_position_features(length):
    t = jnp.linspace(0.0, 1.0, length, dtype=jnp.float32)[:, None]
    bands = (FILTER_EMB - 1) // 2
    w = 2.0 * math.pi * jnp.arange(length, dtype=jnp.float32)[:, None] / length
    f = jnp.linspace(1e-4, bands - 1, bands, dtype=jnp.float32)[None, :]
    z = jnp.concatenate([t, jnp.cos(f * w), -jnp.sin(f * w)], axis=-1)
    return t, z


def implicit_filters(length, w1, b1, w2, b2, w3, freq):
    f32 = jnp.float32
    t, z = sinusoidal_position_features(length)
    fr = freq.astype(f32)
    hid = jnp.sin(fr * (z @ w1.astype(f32) + b1.astype(f32)))
    hid = jnp.sin(fr * (hid @ w2.astype(f32) + b2.astype(f32)))
    filt = (hid @ w3.astype(f32)).reshape(length, HYENA_ORDER, N_DIRS, HYENA_WIDTH)
    max_decay = math.log(DECAY_TARGET) / FAST_DECAY_PCT
    min_decay = math.log(DECAY_TARGET) / SLOW_DECAY_PCT
    deltas = jnp.abs(jnp.linspace(min_decay, max_decay, HYENA_WIDTH, dtype=f32))
    window = jnp.exp(-t * deltas[None, :])
    return filt * window[:, None, None, :]


def bidirectional_fft_conv(u, h_fwd, h_bwd, skip):
    length = u.shape[1]
    n = 2 * length
    taps = jnp.concatenate([h_fwd, jnp.zeros_like(h_fwd[:1]), h_bwd[:0:-1]], axis=0)
    uf = u.astype(jnp.float32)
    spec = jnp.fft.rfft(uf, n=n, axis=1) * jnp.fft.rfft(taps, n=n, axis=0)[None]
    y = jnp.fft.irfft(spec, n=n, axis=1)[:, :length]
    return (y + uf * skip.astype(jnp.float32)).astype(u.dtype)


def short_conv(u, w, bias):
    length = u.shape[1]
    half = SHORT_CONV_W // 2
    up = jnp.pad(u, ((0, 0), (half, SHORT_CONV_W - 1 - half), (0, 0)))
    out = bias
    for j in range(SHORT_CONV_W):
        out = out + up[:, j:j + length] * w[j]
    return out


def hyena_mixer(u, conv_w, conv_b, filters, filt_bias):
    u = short_conv(u, conv_w, conv_b)
    z = u[..., HYENA_ORDER * HYENA_WIDTH:]
    for o in range(HYENA_ORDER):
        gate = u[..., o * HYENA_WIDTH:(o + 1) * HYENA_WIDTH]
        z = gate * bidirectional_fft_conv(z, filters[:, o, 0], filters[:, o, 1], filt_bias[o])
    return z


def expert_choice_moe(h, w_router, w_gate, w_up, w_down):
    b, s, d = h.shape
    cap = EC_CAPACITY_FACTOR * s // N_EXPERTS
    logits = jnp.einsum('bsd,de->bse', h, w_router).astype(jnp.float32)
    affinity = jax.nn.softmax(logits, axis=-1)
    gate, idx = lax.top_k(jnp.swapaxes(affinity, 1, 2), cap)
    xg = jax.vmap(lambda hb, ib: hb[ib])(h, idx)
    a = jnp.einsum('becd,edf->becf', xg, w_gate)
    up = jnp.einsum('becd,edf->becf', xg, w_up)
    ye = jnp.einsum('becf,efd->becd', jax.nn.silu(a) * up, w_down)
    ye = ye * gate[..., None].astype(ye.dtype)
    return jax.vmap(lambda yb, ib: jnp.zeros((s, d), yb.dtype).at[ib.reshape(-1)].add(yb.reshape(-1, d)))(ye, idx)


def setup_inputs(seed: int = 0) -> dict:
    key = jax.random.key(seed)
    ks = jax.random.split(key, 24)
    f32 = jnp.float32

    def nrm(k, shape, scale):
        return jax.random.normal(k, shape, f32) * scale

    def gain(k, shape):
        return 1.0 + nrm(k, shape, 0.01)

    L = DEPTH
    return {
        'x': nrm(ks[0], (BATCH, SEQ, D_MODEL), 1.0),
        'mix_norm_g': gain(ks[1], (L, D_MODEL)),
        'w_in': nrm(ks[2], (L, D_MODEL, IN_WIDTH), D_MODEL ** -0.5),
        'q_norm_g': gain(ks[3], (L, HEAD_DIM)),
        'k_norm_g': gain(ks[4], (L, HEAD_DIM)),
        'rpb': nrm(ks[5], (L, N_ATTN_HEADS, 2 * WIN_ROWS_MAX - 1, 2 * WIN_COLS - 1), 0.02),
        'conv_w': nrm(ks[6], (L, SHORT_CONV_W, (HYENA_ORDER + 1) * HYENA_WIDTH), SHORT_CONV_W ** -0.5),
        'conv_b': nrm(ks[7], (L, (HYENA_ORDER + 1) * HYENA_WIDTH), 0.02),
        'filt_w1': nrm(ks[8], (L, FILTER_EMB, FILTER_HIDDEN), FILTER_EMB ** -0.5),
        'filt_b1': nrm(ks[9], (L, FILTER_HIDDEN), 0.02),
        'filt_w2': nrm(ks[10], (L, FILTER_HIDDEN, FILTER_HIDDEN), FILTER_HIDDEN ** -0.5),
        'filt_b2': nrm(ks[11], (L, FILTER_HIDDEN), 0.02),
        'filt_w3': nrm(ks[12], (L, FILTER_HIDDEN, HYENA_ORDER * N_DIRS * HYENA_WIDTH), FILTER_HIDDEN ** -0.5),
        'filt_freq': gain(ks[13], (L, FILTER_HIDDEN)),
        'filt_bias': nrm(ks[14], (L, HYENA_ORDER, HYENA_WIDTH), 0.5),
        'attn_out_g': gain(ks[15], (L, ATTN_WIDTH)),
        'hyena_out_g': gain(ks[16], (L, HYENA_WIDTH)),
        'w_out': nrm(ks[17], (L, D_MODEL, D_MODEL), D_MODEL ** -0.5),
        'ffn_norm_g': gain(ks[18], (L, D_MODEL)),
        'w_router': nrm(ks[19], (L, D_MODEL, N_EXPERTS), D_MODEL ** -0.5),
        'w_gate': nrm(ks[20], (L, N_EXPERTS, D_MODEL, EXPERT_FF), D_MODEL ** -0.5),
        'w_up': nrm(ks[21], (L, N_EXPERTS, D_MODEL, EXPERT_FF), D_MODEL ** -0.5),
        'w_down': nrm(ks[22], (L, N_EXPERTS, EXPERT_FF, D_MODEL), EXPERT_FF ** -0.5),
    }


def reference(x, mix_norm_g, w_in, q_norm_g, k_norm_g, rpb, conv_w, conv_b, filt_w1, filt_b1, filt_w2, filt_b2, filt_w3, filt_freq, filt_bias, attn_out_g, hyena_out_g, w_out, ffn_norm_g, w_router, w_gate, w_up, w_down):
    b, s, _ = x.shape
    for i in range(DEPTH):
        h = rms_norm(x, mix_norm_g[i])
        proj = jnp.einsum('bsd,de->bse', h, w_in[i])
        q, k, v = jnp.split(proj[..., :3 * ATTN_WIDTH], 3, axis=-1)
        q = rms_norm(q.reshape(b, s, N_ATTN_HEADS, HEAD_DIM), q_norm_g[i])
        k = rms_norm(k.reshape(b, s, N_ATTN_HEADS, HEAD_DIM), k_norm_g[i])
        v = v.reshape(b, s, N_ATTN_HEADS, HEAD_DIM)
        attn = neighbourhood_attention(q, k, v, rpb[i])
        filters = implicit_filters(s, filt_w1[i], filt_b1[i], filt_w2[i], filt_b2[i], filt_w3[i], filt_freq[i])
        hy = hyena_mixer(proj[..., 3 * ATTN_WIDTH:], conv_w[i], conv_b[i], filters, filt_bias[i])
        mixed = jnp.concatenate([group_rms_norm(attn, attn_out_g[i], N_ATTN_HEADS),
                                 group_rms_norm(hy, hyena_out_g[i], N_HYENA_GROUPS)], axis=-1)
        x = x + jnp.einsum('bsd,de->bse', mixed, w_out[i])
        x = x + expert_choice_moe(rms_norm(x, ffn_norm_g[i]), w_router[i], w_gate[i], w_up[i], w_down[i])
    return x
```

```python
import functools
import math

import numpy as np
import jax
import jax.numpy as jnp
from jax import lax
from jax.experimental import pallas as pl
from jax.experimental.pallas import tpu as pltpu

D_MODEL = 2048
GRID_W = 64
N_ATTN_HEADS = 8
ATTN_WIDTH = D_MODEL // 2
HEAD_DIM = ATTN_WIDTH // N_ATTN_HEADS
HYENA_WIDTH = D_MODEL - ATTN_WIDTH
N_HYENA_GROUPS = 8
HYENA_ORDER = 2
N_DIRS = 2
SHORT_CONV_W = 3
FILTER_EMB = 33
DECAY_TARGET = 1e-2
FAST_DECAY_PCT = 0.3
SLOW_DECAY_PCT = 1.5
WIN_ROWS_MAX = 8
WIN_COLS = 16
N_EXPERTS = 16
EC_CAPACITY_FACTOR = 2
EXPERT_FF = 1024
IN_WIDTH = 3 * ATTN_WIDTH + (HYENA_ORDER + 1) * HYENA_WIDTH
RMS_EPS = 1e-6

F32 = jnp.float32
BF16 = jnp.bfloat16
NEG_BIG = -1e30
VMEM_LIMIT = 56 * 1024 * 1024


def _rms(x, eps=RMS_EPS):
    return x * lax.rsqrt(jnp.mean(x * x, axis=-1, keepdims=True) + eps)


def _inproj_kernel(x_ref, g_ref, w_ref, qg_ref, kg_ref, o_ref, h_ref):
    n = pl.program_id(1)

    @pl.when(n == 0)
    def _():
        h_ref[...] = (_rms(x_ref[...]) * g_ref[...]).astype(BF16)

    acc = jnp.dot(h_ref[...], w_ref[...], preferred_element_type=F32)
    bn = acc.shape[1]

    def head_norm(gain):
        for j in range(bn // HEAD_DIM):
            sl = slice(j * HEAD_DIM, (j + 1) * HEAD_DIM)
            o_ref[:, sl] = (_rms(acc[:, sl]) * gain).astype(BF16)

    @pl.when(n == 0)
    def _():
        head_norm(qg_ref[...] * (HEAD_DIM ** -0.5))

    @pl.when(n == 1)
    def _():
        head_norm(kg_ref[...])

    @pl.when(n >= 2)
    def _():
        o_ref[...] = acc.astype(BF16)


def _inproj(x2, g, w_bf, qg, kg, bm=1024, bn=1024):
    m, d = x2.shape
    nw = w_bf.shape[1]
    return pl.pallas_call(
        _inproj_kernel,
        grid=(m // bm, nw // bn),
        in_specs=[
            pl.BlockSpec((bm, d), lambda i, j: (i, 0)),
            pl.BlockSpec((1, d), lambda i, j: (0, 0)),
            pl.BlockSpec((d, bn), lambda i, j: (0, j)),
            pl.BlockSpec((1, HEAD_DIM), lambda i, j: (0, 0)),
            pl.BlockSpec((1, HEAD_DIM), lambda i, j: (0, 0)),
        ],
        out_specs=pl.BlockSpec((bm, bn), lambda i, j: (i, j)),
        out_shape=jax.ShapeDtypeStruct((m, nw), BF16),
        scratch_shapes=[pltpu.VMEM((bm, d), BF16)],
        compiler_params=pltpu.CompilerParams(
            dimension_semantics=("arbitrary", "arbitrary"),
            vmem_limit_bytes=VMEM_LIMIT),
        name="inproj",
    )(x2, g, w_bf, qg, kg)


def _attn_kernel(q_ref, k_ref, v_ref, bias_ref, g_ref, o_ref, *, rows, win_r):
    band = win_r * GRID_W

    def body(r, carry):
        start = jnp.clip(r - win_r // 2, 0, rows - win_r)
        case = r - start
        q = q_ref[pl.ds(pl.multiple_of(r * GRID_W, GRID_W), GRID_W), :]
        k0 = pl.multiple_of(start * GRID_W, GRID_W)
        k = k_ref[pl.ds(k0, band), :]
        v = v_ref[pl.ds(k0, band), :]
        s = lax.dot_general(q, k, (((1,), (1,)), ((), ())), preferred_element_type=F32)
        s = s + bias_ref[0, case]
        mx = jnp.max(s, axis=-1, keepdims=True)
        p = jnp.exp(s - mx)
        l = jnp.sum(p, axis=-1, keepdims=True)
        o = jnp.dot(p.astype(BF16), v, preferred_element_type=F32) / l
        o = _rms(o) * g_ref[...]
        o_ref[pl.ds(pl.multiple_of(r * GRID_W, GRID_W), GRID_W), :] = o.astype(BF16)
        return carry

    lax.fori_loop(0, rows, body, 0)


def _attn_bias_table(rpb, rows, win_r):
    c = np.arange(GRID_W)
    c0 = np.clip(c - WIN_COLS // 2, 0, GRID_W - WIN_COLS)
    in_win = (c[None, :] >= c0[:, None]) & (c[None, :] < c0[:, None] + WIN_COLS)
    dc = np.clip(c[None, :] - c[:, None] + WIN_COLS - 1, 0, 2 * WIN_COLS - 2)
    case = np.arange(win_r)
    dr = np.arange(win_r)[None, :] - case[:, None] + WIN_ROWS_MAX - 1
    b = rpb[:, dr[:, None, :, None], dc[None, :, None, :]].astype(F32)
    b = jnp.where(in_win[None, None, :, None, :], b, NEG_BIG)
    return b.reshape(rpb.shape[0], win_r, GRID_W, win_r * GRID_W)


def _attention(proj, bias_tab, out_g, batch, seq):
    rows = seq // GRID_W
    win_r = min(WIN_ROWS_MAX, rows)
    h = N_ATTN_HEADS
    kern = functools.partial(_attn_kernel, rows=rows, win_r=win_r)
    return pl.pallas_call(
        kern,
        grid=(batch, h),
        in_specs=[
            pl.BlockSpec((seq, HEAD_DIM), lambda b, i: (b, i)),
            pl.BlockSpec((seq, HEAD_DIM), lambda b, i: (b, h + i)),
            pl.BlockSpec((seq, HEAD_DIM), lambda b, i: (b, 2 * h + i)),
            pl.BlockSpec((1, win_r, GRID_W, win_r * GRID_W), lambda b, i: (i, 0, 0, 0)),
            pl.BlockSpec((1, HEAD_DIM), lambda b, i: (0, i)),
        ],
        out_specs=pl.BlockSpec((seq, HEAD_DIM), lambda b, i: (b, i)),
        out_shape=jax.ShapeDtypeStruct((batch * seq, ATTN_WIDTH), BF16),
        compiler_params=pltpu.CompilerParams(
            dimension_semantics=("arbitrary", "arbitrary"),
            vmem_limit_bytes=VMEM_LIMIT),
        name="na2d_attn",
    )(proj, proj, proj, bias_tab, out_g)


def _outproj_kernel(a_ref, hy_ref, x_ref, wo_ref, g_ref, wr_ref, x1_ref, hn_ref, lg_ref):
    half = a_ref.shape[1]
    acc = x_ref[...]
    acc = acc + jnp.dot(a_ref[...], wo_ref[:half, :], preferred_element_type=F32)
    acc = acc + jnp.dot(hy_ref[...], wo_ref[half:, :], preferred_element_type=F32)
    x1_ref[...] = acc
    hn = (_rms(acc) * g_ref[...]).astype(BF16)
    hn_ref[...] = hn
    lg_ref[...] = jnp.dot(hn, wr_ref[...], preferred_element_type=F32)


def _outproj(attn, hy, x2, wo_bf, g, wr_pad, bm=512):
    m, d = x2.shape
    half = attn.shape[1]
    npad = wr_pad.shape[1]
    return pl.pallas_call(
        _outproj_kernel,
        grid=(m // bm,),
        in_specs=[
            pl.BlockSpec((bm, half), lambda i: (i, 0)),
            pl.BlockSpec((bm, half), lambda i: (i, 0)),
            pl.BlockSpec((bm, d), lambda i: (i, 0)),
            pl.BlockSpec((d, d), lambda i: (0, 0)),
            pl.BlockSpec((1, d), lambda i: (0, 0)),
            pl.BlockSpec((d, npad), lambda i: (0, 0)),
        ],
        out_specs=[
            pl.BlockSpec((bm, d), lambda i: (i, 0)),
            pl.BlockSpec((bm, d), lambda i: (i, 0)),
            pl.BlockSpec((bm, npad), lambda i: (i, 0)),
        ],
        out_shape=[
            jax.ShapeDtypeStruct((m, d), F32),
            jax.ShapeDtypeStruct((m, d), BF16),
            jax.ShapeDtypeStruct((m, npad), F32),
        ],
        compiler_params=pltpu.CompilerParams(
            dimension_semantics=("arbitrary",),
            vmem_limit_bytes=VMEM_LIMIT),
        name="outproj",
    )(attn, hy, x2, wo_bf, g, wr_pad)


def _expert_kernel(xg_ref, wg_ref, wu_ref, wd_ref, gate_ref, o_ref):
    xg = xg_ref[0, 0]
    a = jnp.dot(xg, wg_ref[0], preferred_element_type=F32)
    u = jnp.dot(xg, wu_ref[0], preferred_element_type=F32)
    hmid = (a * (1.0 / (1.0 + jnp.exp(-a))) * u).astype(BF16)
    y = jnp.dot(hmid, wd_ref[0], preferred_element_type=F32)
    o_ref[0, 0] = y * gate_ref[0, 0]


def _experts(xg, wg, wu, wd, gate):
    b, e, c, d = xg.shape
    ff = wg.shape[2]
    return pl.pallas_call(
        _expert_kernel,
        grid=(e, b),
        in_specs=[
            pl.BlockSpec((1, 1, c, d), lambda i, j: (j, i, 0, 0)),
            pl.BlockSpec((1, d, ff), lambda i, j: (i, 0, 0)),
            pl.BlockSpec((1, d, ff), lambda i, j: (i, 0, 0)),
            pl.BlockSpec((1, ff, d), lambda i, j: (i, 0, 0)),
            pl.BlockSpec((1, 1, c, 1), lambda i, j: (j, i, 0, 0)),
        ],
        out_specs=pl.BlockSpec((1, 1, c, d), lambda i, j: (j, i, 0, 0)),
        out_shape=jax.ShapeDtypeStruct((b, e, c, d), F32),
        compiler_params=pltpu.CompilerParams(
            dimension_semantics=("arbitrary", "arbitrary"),
            vmem_limit_bytes=VMEM_LIMIT),
        name="experts",
    )(xg, wg, wu, wd, gate)


def _implicit_filters(length, w1, b1, w2, b2, w3, freq):
    t = jnp.linspace(0.0, 1.0, length, dtype=F32)[:, None]
    bands = (FILTER_EMB - 1) // 2
    w = 2.0 * math.pi * jnp.arange(length, dtype=F32)[:, None] / length
    f = jnp.linspace(1e-4, bands - 1, bands, dtype=F32)[None, :]
    z = jnp.concatenate([t, jnp.cos(f * w), -jnp.sin(f * w)], axis=-1)
    hp = lax.Precision.HIGHEST
    hid = jnp.sin(freq * (jnp.dot(z, w1, precision=hp) + b1))
    hid = jnp.sin(freq * (jnp.dot(hid, w2, precision=hp) + b2))
    filt = jnp.dot(hid, w3, precision=hp).reshape(length, HYENA_ORDER, N_DIRS, HYENA_WIDTH)
    max_decay = math.log(DECAY_TARGET) / FAST_DECAY_PCT
    min_decay = math.log(DECAY_TARGET) / SLOW_DECAY_PCT
    deltas = jnp.abs(jnp.linspace(min_decay, max_decay, HYENA_WIDTH, dtype=F32))
    window = jnp.exp(-t * deltas[None, :])
    return filt * window[:, None, None, :]


def _bidir_fft_conv(u, h_fwd, h_bwd, skip):
    length = u.shape[1]
    n = 2 * length
    taps = jnp.concatenate([h_fwd, jnp.zeros_like(h_fwd[:1]), h_bwd[:0:-1]], axis=0)
    spec = jnp.fft.rfft(u, n=n, axis=1) * jnp.fft.rfft(taps, n=n, axis=0)[None]
    y = jnp.fft.irfft(spec, n=n, axis=1)[:, :length]
    return y + u * skip


def _hyena(u, conv_w, conv_b, filters, filt_bias):
    length = u.shape[1]
    half = SHORT_CONV_W // 2
    up = jnp.pad(u, ((0, 0), (half, SHORT_CONV_W - 1 - half), (0, 0)))
    out = conv_b
    for j in range(SHORT_CONV_W):
        out = out + up[:, j:j + length] * conv_w[j]
    u = out
    z = u[..., HYENA_ORDER * HYENA_WIDTH:]
    for o in range(HYENA_ORDER):
        gate = u[..., o * HYENA_WIDTH:(o + 1) * HYENA_WIDTH]
        z = gate * _bidir_fft_conv(z, filters[:, o, 0], filters[:, o, 1], filt_bias[o])
    return z


def kernel(x, mix_norm_g, w_in, q_norm_g, k_norm_g, rpb, conv_w, conv_b, filt_w1, filt_b1, filt_w2, filt_b2, filt_w3, filt_freq, filt_bias, attn_out_g, hyena_out_g, w_out, ffn_norm_g, w_router, w_gate, w_up, w_down):
    b, s, d = x.shape
    rows = s // GRID_W
    win_r = min(WIN_ROWS_MAX, rows)
    cap = EC_CAPACITY_FACTOR * s // N_EXPERTS
    x2 = x.reshape(b * s, d)
    for i in range(mix_norm_g.shape[0]):
        proj = _inproj(x2, mix_norm_g[i][None], w_in[i].astype(BF16),
                       q_norm_g[i][None], k_norm_g[i][None])
        attn = _attention(proj, _attn_bias_table(rpb[i], rows, win_r), attn_out_g[i][None], b, s)

        filters = _implicit_filters(s, filt_w1[i], filt_b1[i], filt_w2[i], filt_b2[i], filt_w3[i], filt_freq[i])
        uh = proj[:, 3 * ATTN_WIDTH:].astype(F32).reshape(b, s, -1)
        hy = _hyena(uh, conv_w[i], conv_b[i], filters, filt_bias[i])
        hyg = hy.reshape(b, s, N_HYENA_GROUPS, -1)
        hyn = (_rms(hyg) * hyena_out_g[i].reshape(N_HYENA_GROUPS, -1)).reshape(b * s, -1).astype(BF16)

        wr_pad = jnp.zeros((d, 128), BF16).at[:, :N_EXPERTS].set(w_router[i].astype(BF16))
        x1, hn, logits = _outproj(attn, hyn, x2, w_out[i].astype(BF16), ffn_norm_g[i][None], wr_pad)

        affinity = jax.nn.softmax(logits[:, :N_EXPERTS].reshape(b, s, N_EXPERTS), axis=-1)
        gate, idx = lax.top_k(jnp.swapaxes(affinity, 1, 2), cap)
        hn3 = hn.reshape(b, s, d)
        xg = jax.vmap(lambda hb, ib: hb[ib])(hn3, idx)
        ye = _experts(xg, w_gate[i].astype(BF16), w_up[i].astype(BF16), w_down[i].astype(BF16),
                      gate[..., None])
        moe = jax.vmap(lambda yb, ib: jnp.zeros((s, d), F32).at[ib.reshape(-1)].add(yb.reshape(-1, d)))(ye, idx)
        x2 = x1 + moe.reshape(b * s, d)
    return x2.reshape(b, s, d)
```

```python
import functools
import math

import numpy as np
import jax
import jax.numpy as jnp
from jax import lax
from jax.experimental import pallas as pl
from jax.experimental.pallas import tpu as pltpu

D_MODEL = 2048
GRID_W = 64
N_ATTN_HEADS = 8
ATTN_WIDTH = D_MODEL // 2
HEAD_DIM = ATTN_WIDTH // N_ATTN_HEADS
HYENA_WIDTH = D_MODEL - ATTN_WIDTH
N_HYENA_GROUPS = 8
HYENA_ORDER = 2
N_DIRS = 2
SHORT_CONV_W = 3
FILTER_EMB = 33
DECAY_TARGET = 1e-2
FAST_DECAY_PCT = 0.3
SLOW_DECAY_PCT = 1.5
WIN_ROWS_MAX = 8
WIN_COLS = 16
N_EXPERTS = 16
EC_CAPACITY_FACTOR = 2
EXPERT_FF = 1024
IN_WIDTH = 3 * ATTN_WIDTH + (HYENA_ORDER + 1) * HYENA_WIDTH
RMS_EPS = 1e-6

F32 = jnp.float32
BF16 = jnp.bfloat16
NEG_BIG = -1e30
VMEM_LIMIT = 56 * 1024 * 1024


def _rms(x, eps=RMS_EPS):
    return x * lax.rsqrt(jnp.mean(x * x, axis=-1, keepdims=True) + eps)


def _inproj_kernel(x_ref, g_ref, w_ref, qg_ref, kg_ref, o_ref, h_ref):
    n = pl.program_id(1)

    @pl.when(n == 0)
    def _():
        h_ref[...] = (_rms(x_ref[...]) * g_ref[...]).astype(BF16)

    acc = jnp.dot(h_ref[...], w_ref[...], preferred_element_type=F32)
    bn = acc.shape[1]

    def head_norm(gain):
        for j in range(bn // HEAD_DIM):
            sl = slice(j * HEAD_DIM, (j + 1) * HEAD_DIM)
            o_ref[:, sl] = (_rms(acc[:, sl]) * gain).astype(BF16)

    @pl.when(n == 0)
    def _():
        head_norm(qg_ref[...] * (HEAD_DIM ** -0.5))

    @pl.when(n == 1)
    def _():
        head_norm(kg_ref[...])

    @pl.when(n >= 2)
    def _():
        o_ref[...] = acc.astype(BF16)


def _inproj(x2, g, w_bf, qg, kg, bm=1024, bn=1024):
    m, d = x2.shape
    nw = w_bf.shape[1]
    return pl.pallas_call(
        _inproj_kernel,
        grid=(m // bm, nw // bn),
        in_specs=[
            pl.BlockSpec((bm, d), lambda i, j: (i, 0)),
            pl.BlockSpec((1, d), lambda i, j: (0, 0)),
            pl.BlockSpec((d, bn), lambda i, j: (0, j)),
            pl.BlockSpec((1, HEAD_DIM), lambda i, j: (0, 0)),
            pl.BlockSpec((1, HEAD_DIM), lambda i, j: (0, 0)),
        ],
        out_specs=pl.BlockSpec((bm, bn), lambda i, j: (i, j)),
        out_shape=jax.ShapeDtypeStruct((m, nw), BF16),
        scratch_shapes=[pltpu.VMEM((bm, d), BF16)],
        compiler_params=pltpu.CompilerParams(
            dimension_semantics=("arbitrary", "arbitrary"),
            vmem_limit_bytes=VMEM_LIMIT),
        name="inproj",
    )(x2, g, w_bf, qg, kg)


def _attn_kernel(q_ref, k_ref, v_ref, bias_ref, g_ref, o_ref, *, rows, win_r):
    band = win_r * GRID_W

    def body(r, carry):
        start = jnp.clip(r - win_r // 2, 0, rows - win_r)
        case = r - start
        q = q_ref[pl.ds(pl.multiple_of(r * GRID_W, GRID_W), GRID_W), :]
        k0 = pl.multiple_of(start * GRID_W, GRID_W)
        k = k_ref[pl.ds(k0, band), :]
        v = v_ref[pl.ds(k0, band), :]
        s = lax.dot_general(q, k, (((1,), (1,)), ((), ())), preferred_element_type=F32)
        s = s + bias_ref[0, case]
        mx = jnp.max(s, axis=-1, keepdims=True)
        p = jnp.exp(s - mx)
        l = jnp.sum(p, axis=-1, keepdims=True)
        o = jnp.dot(p.astype(BF16), v, preferred_element_type=F32) / l
        o = _rms(o) * g_ref[...]
        o_ref[pl.ds(pl.multiple_of(r * GRID_W, GRID_W), GRID_W), :] = o.astype(BF16)
        return carry

    lax.fori_loop(0, rows, body, 0)


def _attn_bias_table(rpb, rows, win_r):
    c = np.arange(GRID_W)
    c0 = np.clip(c - WIN_COLS // 2, 0, GRID_W - WIN_COLS)
    in_win = (c[None, :] >= c0[:, None]) & (c[None, :] < c0[:, None] + WIN_COLS)
    dc = np.clip(c[None, :] - c[:, None] + WIN_COLS - 1, 0, 2 * WIN_COLS - 2)
    case = np.arange(win_r)
    dr = np.arange(win_r)[None, :] - case[:, None] + WIN_ROWS_MAX - 1
    b = rpb[:, dr[:, None, :, None], dc[None, :, None, :]].astype(F32)
    b = jnp.where(in_win[None, None, :, None, :], b, NEG_BIG)
    return b.reshape(rpb.shape[0], win_r, GRID_W, win_r * GRID_W)


def _attention(proj, bias_tab, out_g, batch, seq):
    rows = seq // GRID_W
    win_r = min(WIN_ROWS_MAX, rows)
    h = N_ATTN_HEADS
    kern = functools.partial(_attn_kernel, rows=rows, win_r=win_r)
    return pl.pallas_call(
        kern,
        grid=(batch, h),
        in_specs=[
            pl.BlockSpec((seq, HEAD_DIM), lambda b, i: (b, i)),
            pl.BlockSpec((seq, HEAD_DIM), lambda b, i: (b, h + i)),
            pl.BlockSpec((seq, HEAD_DIM), lambda b, i: (b, 2 * h + i)),
            pl.BlockSpec((1, win_r, GRID_W, win_r * GRID_W), lambda b, i: (i, 0, 0, 0)),
            pl.BlockSpec((1, HEAD_DIM), lambda b, i: (0, i)),
        ],
        out_specs=pl.BlockSpec((seq, HEAD_DIM), lambda b, i: (b, i)),
        out_shape=jax.ShapeDtypeStruct((batch * seq, ATTN_WIDTH), BF16),
        compiler_params=pltpu.CompilerParams(
            dimension_semantics=("arbitrary", "arbitrary"),
            vmem_limit_bytes=VMEM_LIMIT),
        name="na2d_attn",
    )(proj, proj, proj, bias_tab, out_g)


def _outproj_kernel(a_ref, hy_ref, x_ref, wo_ref, g_ref, wr_ref, x1_ref, hn_ref, lg_ref):
    half = a_ref.shape[1]
    acc = x_ref[...]
    acc = acc + jnp.dot(a_ref[...], wo_ref[:half, :], preferred_element_type=F32)
    acc = acc + jnp.dot(hy_ref[...], wo_ref[half:, :], preferred_element_type=F32)
    x1_ref[...] = acc
    hn = (_rms(acc) * g_ref[...]).astype(BF16)
    hn_ref[...] = hn
    lg_ref[...] = jnp.dot(hn, wr_ref[...], preferred_element_type=F32)


def _outproj(attn, hy, x2, wo_bf, g, wr_pad, bm=512):
    m, d = x2.shape
    half = attn.shape[1]
    npad = wr_pad.shape[1]
    return pl.pallas_call(
        _outproj_kernel,
        grid=(m // bm,),
        in_specs=[
            pl.BlockSpec((bm, half), lambda i: (i, 0)),
            pl.BlockSpec((bm, half), lambda i: (i, 0)),
            pl.BlockSpec((bm, d), lambda i: (i, 0)),
            pl.BlockSpec((d, d), lambda i: (0, 0)),
            pl.BlockSpec((1, d), lambda i: (0, 0)),
            pl.BlockSpec((d, npad), lambda i: (0, 0)),
        ],
        out_specs=[
            pl.BlockSpec((bm, d), lambda i: (i, 0)),
            pl.BlockSpec((bm, d), lambda i: (i, 0)),
            pl.BlockSpec((bm, npad), lambda i: (i, 0)),
        ],
        out_shape=[
            jax.ShapeDtypeStruct((m, d), F32),
            jax.ShapeDtypeStruct((m, d), BF16),
            jax.ShapeDtypeStruct((m, npad), F32),
        ],
        compiler_params=pltpu.CompilerParams(
            dimension_semantics=("arbitrary",),
            vmem_limit_bytes=VMEM_LIMIT),
        name="outproj",
    )(attn, hy, x2, wo_bf, g, wr_pad)


def _expert_kernel(xg_ref, wg_ref, wu_ref, wd_ref, gate_ref, o_ref):
    xg = xg_ref[0, 0]
    a = jnp.dot(xg, wg_ref[0], preferred_element_type=F32)
    u = jnp.dot(xg, wu_ref[0], preferred_element_type=F32)
    hmid = (a * (1.0 / (1.0 + jnp.exp(-a))) * u).astype(BF16)
    y = jnp.dot(hmid, wd_ref[0], preferred_element_type=F32)
    o_ref[0, 0] = y * gate_ref[0, 0]


def _experts(xg, wg, wu, wd, gate):
    b, e, c, d = xg.shape
    ff = wg.shape[2]
    return pl.pallas_call(
        _expert_kernel,
        grid=(e, b),
        in_specs=[
            pl.BlockSpec((1, 1, c, d), lambda i, j: (j, i, 0, 0)),
            pl.BlockSpec((1, d, ff), lambda i, j: (i, 0, 0)),
            pl.BlockSpec((1, d, ff), lambda i, j: (i, 0, 0)),
            pl.BlockSpec((1, ff, d), lambda i, j: (i, 0, 0)),
            pl.BlockSpec((1, 1, c, 1), lambda i, j: (j, i, 0, 0)),
        ],
        out_specs=pl.BlockSpec((1, 1, c, d), lambda i, j: (j, i, 0, 0)),
        out_shape=jax.ShapeDtypeStruct((b, e, c, d), F32),
        compiler_params=pltpu.CompilerParams(
            dimension_semantics=("arbitrary", "arbitrary"),
            vmem_limit_bytes=VMEM_LIMIT),
        name="experts",
    )(xg, wg, wu, wd, gate)


FFT_N1 = 128
FFT_N2 = 64
FFT_H1 = FFT_N1 // 2
K1_PAD = 72
Z_PITCH = 72
AB_PITCH = 136
HY_CB = 256
HP = lax.Precision.HIGHEST


def _dft_constants():
    n, n1, n2, h1 = FFT_N1 * FFT_N2, FFT_N1, FFT_N2, FFT_H1
    k1 = np.arange(h1 + 1, dtype=np.float64)
    s1 = np.arange(h1, dtype=np.float64)
    s2 = np.arange(n2, dtype=np.float64)
    tw = s2[:, None, None] * k1[None, :, None] / n

    def stage1(phase_s1):
        th = -2.0 * np.pi * (phase_s1 + tw)
        m = np.zeros((n2, 2 * K1_PAD, h1))
        m[:, :h1 + 1] = np.cos(th)
        m[:, K1_PAD:K1_PAD + h1 + 1] = np.sin(th)
        return m

    f1 = stage1(s1[None, None, :] * k1[None, :, None] / n1)
    f1b = stage1(((n1 - 1) - s1)[None, None, :] * k1[None, :, None] / n1)
    f1b0 = stage1((n1 - s1)[None, None, :] * k1[None, :, None] / n1)[0]
    f1b0[:, 0] = 0.0
    f1b[0] = f1b0

    a = 2.0 * np.pi * np.outer(np.arange(n2), np.arange(n2)) / n2
    c, s = np.cos(a), np.sin(a)
    f2 = np.block([[c, s], [-s, c]])
    f2i = np.block([[c, -s], [s, c]])

    t1 = np.arange(h1, dtype=np.float64)
    th = 2.0 * np.pi * (t1[None, :, None] * k1[None, None, :] / n1 + s2[:, None, None] * k1[None, None, :] / n)
    wgt = np.where((k1 == 0) | (k1 == h1), 1.0, 2.0) / n
    g = np.zeros((n2, h1, 2 * K1_PAD))
    g[:, :, :h1 + 1] = wgt * np.cos(th)
    g[:, :, K1_PAD:K1_PAD + h1 + 1] = -wgt * np.sin(th)
    return tuple(jnp.asarray(m, dtype=BF16) for m in (f1, f1b, f2, f2i, g))


def _hid_kernel(f_ref, w1t_ref, w1c_ref, w1s_ref, b1_ref, w2_ref, b2_ref, fr_ref, o_ref, *, length):
    bm = o_ref.shape[0]
    row = (pl.program_id(0) * bm + lax.broadcasted_iota(jnp.int32, (bm, 1), 0)).astype(F32)
    t = row / (length - 1.0)
    ang = f_ref[...] * (2.0 * math.pi * row / length)
    pre = (t * w1t_ref[...]
           + jnp.dot(jnp.cos(ang), w1c_ref[...], precision=HP, preferred_element_type=F32)
           - jnp.dot(jnp.sin(ang), w1s_ref[...], precision=HP, preferred_element_type=F32))
    fr = fr_ref[...]
    hid = jnp.sin(fr * (pre + b1_ref[...]))
    hid = jnp.sin(fr * (jnp.dot(hid, w2_ref[...], precision=HP, preferred_element_type=F32) + b2_ref[...]))
    o_ref[...] = hid


def _filter_hidden(length, w1, b1, w2, b2, freq, bm=512):
    bands = (FILTER_EMB - 1) // 2
    f = jnp.linspace(1e-4, bands - 1, bands, dtype=F32)[None, :]
    hdim = w1.shape[1]
    full = lambda a: pl.BlockSpec(a.shape, lambda i: (0,) * a.ndim)
    args = (f, w1[:1], w1[1:1 + bands], w1[1 + bands:], b1[None], w2, b2[None], freq[None])
    return pl.pallas_call(
        functools.partial(_hid_kernel, length=length),
        grid=(length // bm,),
        in_specs=[full(a) for a in args],
        out_specs=pl.BlockSpec((bm, hdim), lambda i: (i, 0)),
        out_shape=jax.ShapeDtypeStruct((length, hdim), F32),
        name="filter_hidden",
    )(*args)


LANES = 128


def _ld(ref, start, size, stride=None):
    idx = pl.ds(start, size) if stride is None else pl.ds(start, size, stride=stride)
    return jnp.concatenate([ref[h, idx, :] for h in range(ref.shape[0])], axis=-1)


def _st(ref, start, size, val, stride=None):
    idx = pl.ds(start, size) if stride is None else pl.ds(start, size, stride=stride)
    for h in range(ref.shape[0]):
        ref[h, idx, :] = val[:, h * LANES:(h + 1) * LANES]


def _stage1_store(ab_ref, s2, a):
    _st(ab_ref, s2, K1_PAD, a[:K1_PAD], stride=AB_PITCH)
    _st(ab_ref, FFT_N2 + s2, K1_PAD, a[K1_PAD:], stride=AB_PITCH)


def _spectrum_kernel(hid_ref, w3f_ref, w3b_ref, dl_ref, skip_ref, f1_ref, f1b_ref, f2_ref, o_ref,
                     hf_ref, hb_ref, ab_ref, *, length):
    def gen(j, c):
        row = (j * FFT_N2 + lax.broadcasted_iota(jnp.int32, (FFT_N2, 1), 0)).astype(F32)
        win = jnp.exp(-(row / (length - 1.0)) * dl_ref[...])
        hid = hid_ref[pl.ds(pl.multiple_of(j * FFT_N2, FFT_N2), FFT_N2), :]
        dst = pl.multiple_of(j * Z_PITCH, 8)
        _st(hf_ref, dst, FFT_N2, jnp.dot(hid, w3f_ref[...], precision=HP, preferred_element_type=F32) * win)
        _st(hb_ref, dst, FFT_N2, jnp.dot(hid, w3b_ref[...], precision=HP, preferred_element_type=F32) * win)
        return c

    lax.fori_loop(0, FFT_H1, gen, 0)

    def stage1(s2, c):
        xf = _ld(hf_ref, s2, FFT_H1, Z_PITCH).astype(BF16)
        xb = _ld(hb_ref, (FFT_N2 - s2) % FFT_N2, FFT_H1, Z_PITCH).astype(BF16)
        a = (jnp.dot(f1_ref[s2], xf, preferred_element_type=F32)
             + jnp.dot(f1b_ref[s2], xb, preferred_element_type=F32))
        _stage1_store(ab_ref, s2, a)
        return c

    lax.fori_loop(0, FFT_N2, stage1, 0)

    def stage2(k1, c):
        a = _ld(ab_ref, pl.multiple_of(k1 * AB_PITCH, 8), 2 * FFT_N2).astype(BF16)
        x = jnp.dot(f2_ref[...], a, preferred_element_type=F32)
        o_ref[0, k1, :FFT_N2, :] = (x[:FFT_N2] + skip_ref[0]).astype(BF16)
        o_ref[0, k1, FFT_N2:, :] = x[FFT_N2:].astype(BF16)
        return c

    lax.fori_loop(0, FFT_H1 + 1, stage2, 0)


def _filter_spectrum(hid, w3, deltas, filt_bias, consts, length):
    f1, f1b, f2, _, _ = consts
    cb = HY_CB
    ncb = HYENA_WIDTH // cb
    hdim = hid.shape[1]
    full = lambda a: pl.BlockSpec(a.shape, lambda o, c: (0,) * a.ndim)
    return pl.pallas_call(
        functools.partial(_spectrum_kernel, length=length),
        grid=(HYENA_ORDER, ncb),
        in_specs=[
            full(hid),
            pl.BlockSpec((hdim, cb), lambda o, c: (0, (o * N_DIRS) * ncb + c)),
            pl.BlockSpec((hdim, cb), lambda o, c: (0, (o * N_DIRS + 1) * ncb + c)),
            pl.BlockSpec((1, cb), lambda o, c: (0, c)),
            pl.BlockSpec((1, 1, cb), lambda o, c: (o, 0, c)),
            full(f1), full(f1b), full(f2),
        ],
        out_specs=pl.BlockSpec((1, FFT_H1 + 1, 2 * FFT_N2, cb), lambda o, c: (o, 0, 0, c)),
        out_shape=jax.ShapeDtypeStruct((HYENA_ORDER, FFT_H1 + 1, 2 * FFT_N2, HYENA_WIDTH), BF16),
        scratch_shapes=[
            pltpu.VMEM((cb // LANES, FFT_H1 * Z_PITCH, LANES), F32),
            pltpu.VMEM((cb // LANES, FFT_H1 * Z_PITCH, LANES), F32),
            pltpu.VMEM((cb // LANES, K1_PAD * AB_PITCH, LANES), F32),
        ],
        compiler_params=pltpu.CompilerParams(
            dimension_semantics=("arbitrary", "arbitrary"),
            vmem_limit_bytes=VMEM_LIMIT),
        name="filter_spectrum",
    )(hid, w3, w3, deltas, filt_bias[:, None, :], f1, f1b, f2)


def _short_conv_block(s_ref, j, w_ref, b_ref):
    big = s_ref[pl.ds(pl.multiple_of(j * FFT_N2, FFT_N2), FFT_N2 + 16), :]
    rows = FFT_N2 + 16
    prev = pltpu.roll(big, 1, 0)[8:8 + FFT_N2]
    nxt = pltpu.roll(big, rows - 1, 0)[8:8 + FFT_N2]
    cur = big[8:8 + FFT_N2]
    return b_ref[...] + prev * w_ref[0:1, :] + cur * w_ref[1:2, :] + nxt * w_ref[2:3, :]


def _load_stream(s_ref, src_ref, length):
    zeros = jnp.zeros((8, s_ref.shape[1]), F32)
    s_ref[0:8, :] = zeros
    s_ref[length + 8:length + 16, :] = zeros
    chunk = 512

    def cp(i, c):
        r = pl.multiple_of(i * chunk, chunk)
        s_ref[pl.ds(r + 8, chunk), :] = src_ref[pl.ds(r, chunk), :].astype(F32)
        return c

    lax.fori_loop(0, length // chunk, cp, 0)


def _hyena_kernel(z_ref, gs_ref, cwz_ref, cbz_ref, cwg_ref, cbg_ref, h_ref, f1_ref, f2_ref, f2i_ref, g_ref,
                  og_ref, o_ref, s_ref, zy_ref, ab_ref, *, length, conv_z, norm):
    if conv_z:
        _load_stream(s_ref, z_ref, length)

    def prep(j, c):
        if conv_z:
            zblk = _short_conv_block(s_ref, j, cwz_ref, cbz_ref)
        else:
            zblk = z_ref[pl.ds(pl.multiple_of(j * FFT_N2, FFT_N2), FFT_N2), :].astype(F32)
        _st(zy_ref, pl.multiple_of(j * Z_PITCH, 8), FFT_N2, zblk)
        return c

    lax.fori_loop(0, FFT_H1, prep, 0)

    def stage1(s2, c):
        xs = _ld(zy_ref, s2, FFT_H1, Z_PITCH).astype(BF16)
        _stage1_store(ab_ref, s2, jnp.dot(f1_ref[s2], xs, preferred_element_type=F32))
        return c

    lax.fori_loop(0, FFT_N2, stage1, 0)

    def stage2(k1, c):
        blk = pl.multiple_of(k1 * AB_PITCH, 8)
        x = jnp.dot(f2_ref[...], _ld(ab_ref, blk, 2 * FFT_N2).astype(BF16), preferred_element_type=F32)
        hk = h_ref[0, k1].astype(F32)
        xr, xi, hr, hi = x[:FFT_N2], x[FFT_N2:], hk[:FFT_N2], hk[FFT_N2:]
        p = jnp.concatenate([xr * hr - xi * hi, xr * hi + xi * hr], axis=0).astype(BF16)
        _st(ab_ref, blk, 2 * FFT_N2, jnp.dot(f2i_ref[...], p, preferred_element_type=F32))
        return c

    lax.fori_loop(0, FFT_H1 + 1, stage2, 0)

    def stage3(t2, c):
        br = _ld(ab_ref, t2, K1_PAD, AB_PITCH)
        bi = _ld(ab_ref, FFT_N2 + t2, K1_PAD, AB_PITCH)
        rhs = jnp.concatenate([br, bi], axis=0).astype(BF16)
        _st(zy_ref, t2, FFT_H1, jnp.dot(g_ref[t2], rhs, preferred_element_type=F32), stride=Z_PITCH)
        return c

    lax.fori_loop(0, FFT_N2, stage3, 0)

    _load_stream(s_ref, gs_ref, length)

    def fin(j, c):
        gate = _short_conv_block(s_ref, j, cwg_ref, cbg_ref)
        y = gate * _ld(zy_ref, pl.multiple_of(j * Z_PITCH, 8), FFT_N2)
        dst = pl.ds(pl.multiple_of(j * FFT_N2, FFT_N2), FFT_N2)
        if norm:
            gw = HYENA_WIDTH // N_HYENA_GROUPS
            for q in range(y.shape[1] // gw):
                sl = slice(q * gw, (q + 1) * gw)
                o_ref[dst, sl] = (_rms(y[:, sl]) * og_ref[:, sl]).astype(o_ref.dtype)
        else:
            o_ref[dst, :] = y.astype(o_ref.dtype)
        return c

    lax.fori_loop(0, FFT_H1, fin, 0)


def _hyena_order(zsrc, zcol, proj, order, conv_w, conv_b, hspec, consts, out_g, batch, length, conv_z, norm):
    f1, _, f2, f2i, g = consts
    cb = HY_CB
    ncb = HYENA_WIDTH // cb
    pcol = 3 * ATTN_WIDTH // cb
    zconv = HYENA_ORDER * ncb
    full = lambda a: pl.BlockSpec(a.shape, lambda c, b: (0,) * a.ndim)
    kern = functools.partial(_hyena_kernel, length=length, conv_z=conv_z, norm=norm)
    return pl.pallas_call(
        kern,
        grid=(ncb, batch),
        in_specs=[
            pl.BlockSpec((length, cb), lambda c, b: (b, zcol + c)),
            pl.BlockSpec((length, cb), lambda c, b: (b, pcol + order * ncb + c)),
            pl.BlockSpec((SHORT_CONV_W, cb), lambda c, b: (0, zconv + c)),
            pl.BlockSpec((1, cb), lambda c, b: (0, zconv + c)),
            pl.BlockSpec((SHORT_CONV_W, cb), lambda c, b: (0, order * ncb + c)),
            pl.BlockSpec((1, cb), lambda c, b: (0, order * ncb + c)),
            pl.BlockSpec((1, FFT_H1 + 1, 2 * FFT_N2, cb), lambda c, b: (order, 0, 0, c)),
            full(f1), full(f2), full(f2i), full(g),
            pl.BlockSpec((1, cb), lambda c, b: (0, c)),
        ],
        out_specs=pl.BlockSpec((length, cb), lambda c, b: (b, c)),
        out_shape=jax.ShapeDtypeStruct((batch * length, HYENA_WIDTH), BF16),
        scratch_shapes=[
            pltpu.VMEM((length + 16, cb), F32),
            pltpu.VMEM((cb // LANES, FFT_H1 * Z_PITCH, LANES), F32),
            pltpu.VMEM((cb // LANES, K1_PAD * AB_PITCH, LANES), F32),
        ],
        compiler_params=pltpu.CompilerParams(
            dimension_semantics=("arbitrary", "arbitrary"),
            vmem_limit_bytes=VMEM_LIMIT),
        name=f"hyena_order{order}",
    )(zsrc, proj, conv_w, conv_b[None], conv_w, conv_b[None], hspec, f1, f2, f2i, g, out_g)


def _hyena(proj, conv_w, conv_b, w1, b1, w2, b2, w3, freq, filt_bias, out_g, batch, length):
    assert 2 * length == FFT_N1 * FFT_N2 and HYENA_ORDER == 2
    consts = _dft_constants()
    max_decay = math.log(DECAY_TARGET) / FAST_DECAY_PCT
    min_decay = math.log(DECAY_TARGET) / SLOW_DECAY_PCT
    deltas = jnp.abs(jnp.linspace(min_decay, max_decay, HYENA_WIDTH, dtype=F32))[None]
    hid = _filter_hidden(length, w1, b1, w2, b2, freq)
    hspec = _filter_spectrum(hid, w3, deltas, filt_bias, consts, length)
    vcol = (3 * ATTN_WIDTH + HYENA_ORDER * HYENA_WIDTH) // HY_CB
    z1 = _hyena_order(proj, vcol, proj, 0, conv_w, conv_b, hspec, consts, out_g, batch, length, True, False)
    return _hyena_order(z1, 0, proj, 1, conv_w, conv_b, hspec, consts, out_g, batch, length, False, True)


def kernel(x, mix_norm_g, w_in, q_norm_g, k_norm_g, rpb, conv_w, conv_b, filt_w1, filt_b1, filt_w2, filt_b2, filt_w3, filt_freq, filt_bias, attn_out_g, hyena_out_g, w_out, ffn_norm_g, w_router, w_gate, w_up, w_down):
    b, s, d = x.shape
    rows = s // GRID_W
    win_r = min(WIN_ROWS_MAX, rows)
    cap = EC_CAPACITY_FACTOR * s // N_EXPERTS
    x2 = x.reshape(b * s, d)
    for i in range(mix_norm_g.shape[0]):
        proj = _inproj(x2, mix_norm_g[i][None], w_in[i].astype(BF16),
                       q_norm_g[i][None], k_norm_g[i][None])
        attn = _attention(proj, _attn_bias_table(rpb[i], rows, win_r), attn_out_g[i][None], b, s)

        hyn = _hyena(proj, conv_w[i], conv_b[i], filt_w1[i], filt_b1[i], filt_w2[i], filt_b2[i], filt_w3[i],
                     filt_freq[i], filt_bias[i], hyena_out_g[i][None], b, s)

        wr_pad = jnp.zeros((d, 128), BF16).at[:, :N_EXPERTS].set(w_router[i].astype(BF16))
        x1, hn, logits = _outproj(attn, hyn, x2, w_out[i].astype(BF16), ffn_norm_g[i][None], wr_pad)

        affinity = jax.nn.softmax(logits[:, :N_EXPERTS].reshape(b, s, N_EXPERTS), axis=-1)
        gate, idx = lax.top_k(jnp.swapaxes(affinity, 1, 2), cap)
        hn3 = hn.reshape(b, s, d)
        xg = jax.vmap(lambda hb, ib: hb[ib])(hn3, idx)
        ye = _experts(xg, w_gate[i].astype(BF16), w_up[i].astype(BF16), w_down[i].astype(BF16),
                      gate[..., None])
        moe = jax.vmap(lambda yb, ib: jnp.zeros((s, d), F32).at[ib.reshape(-1)].add(yb.reshape(-1, d)))(ye, idx)
        x2 = x1 + moe.reshape(b * s, d)
    return x2.reshape(b, s, d)
```

```python
import functools
import math

import numpy as np
import jax
import jax.numpy as jnp
from jax import lax
from jax.experimental import pallas as pl
from jax.experimental.pallas import tpu as pltpu

D_MODEL = 2048
GRID_W = 64
N_ATTN_HEADS = 8
ATTN_WIDTH = D_MODEL // 2
HEAD_DIM = ATTN_WIDTH // N_ATTN_HEADS
HYENA_WIDTH = D_MODEL - ATTN_WIDTH
N_HYENA_GROUPS = 8
HYENA_ORDER = 2
N_DIRS = 2
SHORT_CONV_W = 3
FILTER_EMB = 33
DECAY_TARGET = 1e-2
FAST_DECAY_PCT = 0.3
SLOW_DECAY_PCT = 1.5
WIN_ROWS_MAX = 8
WIN_COLS = 16
N_EXPERTS = 16
EC_CAPACITY_FACTOR = 2
EXPERT_FF = 1024
IN_WIDTH = 3 * ATTN_WIDTH + (HYENA_ORDER + 1) * HYENA_WIDTH
RMS_EPS = 1e-6

F32 = jnp.float32
BF16 = jnp.bfloat16
NEG_BIG = -1e30
VMEM_LIMIT = 56 * 1024 * 1024


def _rms(x, eps=RMS_EPS):
    return x * lax.rsqrt(jnp.mean(x * x, axis=-1, keepdims=True) + eps)


def _inproj_kernel(x_ref, g_ref, w_ref, qg_ref, kg_ref, o_ref, h_ref):
    n = pl.program_id(1)

    @pl.when(n == 0)
    def _():
        h_ref[...] = (_rms(x_ref[...]) * g_ref[...]).astype(BF16)

    acc = jnp.dot(h_ref[...], w_ref[...], preferred_element_type=F32)
    bn = acc.shape[1]

    def head_norm(gain):
        for j in range(bn // HEAD_DIM):
            sl = slice(j * HEAD_DIM, (j + 1) * HEAD_DIM)
            o_ref[:, sl] = (_rms(acc[:, sl]) * gain).astype(BF16)

    @pl.when(n == 0)
    def _():
        head_norm(qg_ref[...] * (HEAD_DIM ** -0.5))

    @pl.when(n == 1)
    def _():
        head_norm(kg_ref[...])

    @pl.when(n >= 2)
    def _():
        o_ref[...] = acc.astype(BF16)


def _inproj(x2, g, w_bf, qg, kg, bm=1024, bn=1024):
    m, d = x2.shape
    nw = w_bf.shape[1]
    return pl.pallas_call(
        _inproj_kernel,
        grid=(m // bm, nw // bn),
        in_specs=[
            pl.BlockSpec((bm, d), lambda i, j: (i, 0)),
            pl.BlockSpec((1, d), lambda i, j: (0, 0)),
            pl.BlockSpec((d, bn), lambda i, j: (0, j)),
            pl.BlockSpec((1, HEAD_DIM), lambda i, j: (0, 0)),
            pl.BlockSpec((1, HEAD_DIM), lambda i, j: (0, 0)),
        ],
        out_specs=pl.BlockSpec((bm, bn), lambda i, j: (i, j)),
        out_shape=jax.ShapeDtypeStruct((m, nw), BF16),
        scratch_shapes=[pltpu.VMEM((bm, d), BF16)],
        compiler_params=pltpu.CompilerParams(
            dimension_semantics=("arbitrary", "arbitrary"),
            vmem_limit_bytes=VMEM_LIMIT),
        name="inproj",
    )(x2, g, w_bf, qg, kg)


def _attn_kernel(q_ref, k_ref, v_ref, bias_ref, g_ref, o_ref, *, rows, win_r):
    band = win_r * GRID_W

    def body(r, carry):
        start = jnp.clip(r - win_r // 2, 0, rows - win_r)
        case = r - start
        q = q_ref[pl.ds(pl.multiple_of(r * GRID_W, GRID_W), GRID_W), :]
        k0 = pl.multiple_of(start * GRID_W, GRID_W)
        k = k_ref[pl.ds(k0, band), :]
        v = v_ref[pl.ds(k0, band), :]
        s = lax.dot_general(q, k, (((1,), (1,)), ((), ())), preferred_element_type=F32)
        s = s + bias_ref[0, case]
        mx = jnp.max(s, axis=-1, keepdims=True)
        p = jnp.exp(s - mx)
        l = jnp.sum(p, axis=-1, keepdims=True)
        o = jnp.dot(p.astype(BF16), v, preferred_element_type=F32) / l
        o = _rms(o) * g_ref[...]
        o_ref[pl.ds(pl.multiple_of(r * GRID_W, GRID_W), GRID_W), :] = o.astype(BF16)
        return carry

    lax.fori_loop(0, rows, body, 0)


def _attn_bias_table(rpb, rows, win_r):
    c = np.arange(GRID_W)
    c0 = np.clip(c - WIN_COLS // 2, 0, GRID_W - WIN_COLS)
    in_win = (c[None, :] >= c0[:, None]) & (c[None, :] < c0[:, None] + WIN_COLS)
    dc = np.clip(c[None, :] - c[:, None] + WIN_COLS - 1, 0, 2 * WIN_COLS - 2)
    case = np.arange(win_r)
    dr = np.arange(win_r)[None, :] - case[:, None] + WIN_ROWS_MAX - 1
    b = rpb[:, dr[:, None, :, None], dc[None, :, None, :]].astype(F32)
    b = jnp.where(in_win[None, None, :, None, :], b, NEG_BIG)
    return b.reshape(rpb.shape[0], win_r, GRID_W, win_r * GRID_W)


def _attention(proj, bias_tab, out_g, batch, seq):
    rows = seq // GRID_W
    win_r = min(WIN_ROWS_MAX, rows)
    h = N_ATTN_HEADS
    kern = functools.partial(_attn_kernel, rows=rows, win_r=win_r)
    return pl.pallas_call(
        kern,
        grid=(batch, h),
        in_specs=[
            pl.BlockSpec((seq, HEAD_DIM), lambda b, i: (b, i)),
            pl.BlockSpec((seq, HEAD_DIM), lambda b, i: (b, h + i)),
            pl.BlockSpec((seq, HEAD_DIM), lambda b, i: (b, 2 * h + i)),
            pl.BlockSpec((1, win_r, GRID_W, win_r * GRID_W), lambda b, i: (i, 0, 0, 0)),
            pl.BlockSpec((1, HEAD_DIM), lambda b, i: (0, i)),
        ],
        out_specs=pl.BlockSpec((seq, HEAD_DIM), lambda b, i: (b, i)),
        out_shape=jax.ShapeDtypeStruct((batch * seq, ATTN_WIDTH), BF16),
        compiler_params=pltpu.CompilerParams(
            dimension_semantics=("arbitrary", "arbitrary"),
            vmem_limit_bytes=VMEM_LIMIT),
        name="na2d_attn",
    )(proj, proj, proj, bias_tab, out_g)


def _outproj_kernel(a_ref, hy_ref, x_ref, wo_ref, g_ref, wr_ref, x1_ref, hn_ref, lg_ref):
    half = a_ref.shape[1]
    acc = x_ref[...]
    acc = acc + jnp.dot(a_ref[...], wo_ref[:half, :], preferred_element_type=F32)
    acc = acc + jnp.dot(hy_ref[...], wo_ref[half:, :], preferred_element_type=F32)
    x1_ref[...] = acc
    hn = _rms(acc) * g_ref[...]
    hn_ref[...] = hn
    lg_ref[...] = jnp.dot(hn.astype(BF16), wr_ref[...], preferred_element_type=F32)


def _outproj(attn, hy, x2, wo_bf, g, wr_pad, bm=512):
    m, d = x2.shape
    half = attn.shape[1]
    npad = wr_pad.shape[1]
    return pl.pallas_call(
        _outproj_kernel,
        grid=(m // bm,),
        in_specs=[
            pl.BlockSpec((bm, half), lambda i: (i, 0)),
            pl.BlockSpec((bm, half), lambda i: (i, 0)),
            pl.BlockSpec((bm, d), lambda i: (i, 0)),
            pl.BlockSpec((d, d), lambda i: (0, 0)),
            pl.BlockSpec((1, d), lambda i: (0, 0)),
            pl.BlockSpec((d, npad), lambda i: (0, 0)),
        ],
        out_specs=[
            pl.BlockSpec((bm, d), lambda i: (i, 0)),
            pl.BlockSpec((bm, d), lambda i: (i, 0)),
            pl.BlockSpec((bm, npad), lambda i: (i, 0)),
        ],
        out_shape=[
            jax.ShapeDtypeStruct((m, d), F32),
            jax.ShapeDtypeStruct((m, d), F32),
            jax.ShapeDtypeStruct((m, npad), F32),
        ],
        compiler_params=pltpu.CompilerParams(
            dimension_semantics=("arbitrary",),
            vmem_limit_bytes=VMEM_LIMIT),
        name="outproj",
    )(attn, hy, x2, wo_bf, g, wr_pad)


ROUTE_BLK = 128
ROUTE_CHUNK = 512
ROUTE_MATCH = 256
F32_INF_BITS = 0x7F800000


def _route_kernel(lg_ref, tri_ref, idx_ref, gate_ref, aff_ref, pos_ref, *, cap):
    s, lanes = lg_ref.shape
    lane = lax.broadcasted_iota(jnp.int32, (1, lanes), 1)

    def softmax_chunk(i, c):
        rows = pl.ds(pl.multiple_of(i * ROUTE_CHUNK, ROUTE_CHUNK), ROUTE_CHUNK)
        lg = jnp.where(lane < N_EXPERTS, lg_ref[rows, :], NEG_BIG)
        ex = jnp.exp(lg - jnp.max(lg, axis=-1, keepdims=True))
        aff_ref[rows, :] = ex / jnp.sum(ex, axis=-1, keepdims=True)
        return c

    lax.fori_loop(0, s // ROUTE_CHUNK, softmax_chunk, 0)

    def as_value(bits):
        return pltpu.bitcast(bits, F32)

    def count_ge(th):
        def body(i, acc):
            rows = pl.ds(pl.multiple_of(i * ROUTE_CHUNK, ROUTE_CHUNK), ROUTE_CHUNK)
            return acc + jnp.sum(jnp.where(aff_ref[rows, :] >= th, 1.0, 0.0), axis=0, keepdims=True)
        return lax.fori_loop(0, s // ROUTE_CHUNK, body, jnp.zeros((1, lanes), F32))

    def bisect(_, c):
        lo, hi = c
        mid = lo + ((hi - lo) >> 1)
        ok = count_ge(as_value(mid)) >= cap
        return jnp.where(ok, mid, lo), jnp.where(ok, hi, mid)

    tau_bits, _ = lax.fori_loop(0, 31, bisect, (jnp.zeros((1, lanes), jnp.int32),
                                                jnp.full((1, lanes), F32_INF_BITS, jnp.int32)))
    tau, above = as_value(tau_bits), as_value(tau_bits + 1)
    ties_wanted = cap - count_ge(above)

    def scan_block(j, c):
        tie_carry, sel_carry = c
        rows = pl.ds(pl.multiple_of(j * ROUTE_BLK, ROUTE_BLK), ROUTE_BLK)
        aff = aff_ref[rows, :]
        gt = aff >= above
        eq = jnp.where((aff >= tau) & (aff < above), 1.0, 0.0)
        tie_incl = jnp.dot(tri_ref[...], eq.astype(BF16), preferred_element_type=F32) + tie_carry
        sel = jnp.where(gt | ((eq > 0.0) & (tie_incl - eq < ties_wanted)), 1.0, 0.0)
        sel_incl = jnp.dot(tri_ref[...], sel.astype(BF16), preferred_element_type=F32) + sel_carry
        pos_ref[rows, :] = jnp.where(sel > 0.0, sel_incl - sel, -1.0)
        return tie_incl[ROUTE_BLK - 1:, :], sel_incl[ROUTE_BLK - 1:, :]

    zero = jnp.zeros((1, lanes), F32)
    lax.fori_loop(0, s // ROUTE_BLK, scan_block, (zero, zero))

    slot = lax.broadcasted_iota(jnp.int32, (1, cap), 1).astype(F32)
    for e in range(N_EXPERTS):
        def compact(c, acc, e=e):
            ia, ga = acc
            r0 = pl.multiple_of(c * ROUTE_MATCH, ROUTE_MATCH)
            match = pos_ref[pl.ds(r0, ROUTE_MATCH), e:e + 1] == slot
            tok = (r0 + lax.broadcasted_iota(jnp.int32, (ROUTE_MATCH, 1), 0)).astype(F32)
            ia = ia + jnp.sum(jnp.where(match, tok, 0.0), axis=0, keepdims=True)
            ga = ga + jnp.sum(jnp.where(match, aff_ref[pl.ds(r0, ROUTE_MATCH), e:e + 1], 0.0),
                              axis=0, keepdims=True)
            return ia, ga

        zc = jnp.zeros((1, cap), F32)
        ia, ga = lax.fori_loop(0, s // ROUTE_MATCH, compact, (zc, zc))
        idx_ref[0, e:e + 1, :] = ia.astype(jnp.int32)
        gate_ref[0, e:e + 1, :] = ga


def _route(logits, batch, seq, cap):
    lanes = logits.shape[1]
    tri = jnp.asarray(np.tril(np.ones((ROUTE_BLK, ROUTE_BLK))), dtype=BF16)
    return pl.pallas_call(
        functools.partial(_route_kernel, cap=cap),
        grid=(batch,),
        in_specs=[
            pl.BlockSpec((seq, lanes), lambda b: (b, 0)),
            pl.BlockSpec((ROUTE_BLK, ROUTE_BLK), lambda b: (0, 0)),
        ],
        out_specs=[
            pl.BlockSpec((1, N_EXPERTS, cap), lambda b: (b, 0, 0)),
            pl.BlockSpec((1, N_EXPERTS, cap), lambda b: (b, 0, 0)),
        ],
        out_shape=[
            jax.ShapeDtypeStruct((batch, N_EXPERTS, cap), jnp.int32),
            jax.ShapeDtypeStruct((batch, N_EXPERTS, cap), F32),
        ],
        scratch_shapes=[
            pltpu.VMEM((seq, lanes), F32),
            pltpu.VMEM((seq, lanes), F32),
        ],
        compiler_params=pltpu.CompilerParams(dimension_semantics=("arbitrary",)),
        name="route",
    )(logits, tri)


ROW_UNROLL = 8


def _moe_kernel(idx_ref, hn_hbm, wg_ref, wu_ref, wd_ref, gate_ref, x1_hbm, out_hbm,
                xg_buf, acc_buf, sem_x, sem_a, sem_s, *, cap, seq):
    del x1_hbm
    e, b = pl.program_id(0), pl.program_id(1)
    n_e, n_b = pl.num_programs(0), pl.num_programs(1)

    def for_rows(fn):
        def body(i, c):
            for k in range(ROW_UNROLL):
                fn(i * ROW_UNROLL + k)
            return c
        lax.fori_loop(0, cap // ROW_UNROLL, body, 0)

    def rows_of(ee, bb):
        base = (bb * n_e + ee) * cap
        return lambda p: bb * seq + idx_ref[base + p]

    def gather_x(ee, bb):
        row = rows_of(ee, bb)
        for_rows(lambda p: pltpu.make_async_copy(
            hn_hbm.at[pl.ds(row(p), 1), :], xg_buf.at[bb, pl.ds(p, 1), :], sem_x.at[bb]).start())

    def wait_all(buf, sem):
        pltpu.make_async_copy(hn_hbm.at[pl.ds(0, cap), :], buf, sem).wait()

    @pl.when((e == 0) & (b == 0))
    def _():
        gather_x(e, b)

    nb = jnp.where(b + 1 == n_b, 0, b + 1)
    ne = jnp.where(b + 1 == n_b, e + 1, e)

    @pl.when(ne < n_e)
    def _():
        gather_x(ne, nb)

    @pl.when(e > 0)
    def _():
        wait_all(acc_buf.at[b], sem_s.at[b])

    row = rows_of(e, b)
    for_rows(lambda p: pltpu.make_async_copy(
        out_hbm.at[pl.ds(row(p), 1), :], acc_buf.at[b, pl.ds(p, 1), :], sem_a.at[b]).start())

    wait_all(xg_buf.at[b], sem_x.at[b])
    xg = xg_buf[b].astype(BF16)
    a = jnp.dot(xg, wg_ref[0], preferred_element_type=F32)
    u = jnp.dot(xg, wu_ref[0], preferred_element_type=F32)
    hmid = (a * (1.0 / (1.0 + jnp.exp(-a))) * u).astype(BF16)
    y = jnp.dot(hmid, wd_ref[0], preferred_element_type=F32) * gate_ref[0, 0]

    wait_all(acc_buf.at[b], sem_a.at[b])
    acc_buf[b] = acc_buf[b] + y
    for_rows(lambda p: pltpu.make_async_copy(
        acc_buf.at[b, pl.ds(p, 1), :], out_hbm.at[pl.ds(row(p), 1), :], sem_s.at[b]).start())

    @pl.when(e == n_e - 1)
    def _():
        wait_all(acc_buf.at[b], sem_s.at[b])


def _moe(idx, gate, hn, x1, wg, wu, wd, batch, seq):
    n_e, d, ff = wg.shape
    cap = idx.shape[-1]
    grid_spec = pltpu.PrefetchScalarGridSpec(
        num_scalar_prefetch=1,
        grid=(n_e, batch),
        in_specs=[
            pl.BlockSpec(memory_space=pl.ANY),
            pl.BlockSpec((1, d, ff), lambda i, j, s: (i, 0, 0)),
            pl.BlockSpec((1, d, ff), lambda i, j, s: (i, 0, 0)),
            pl.BlockSpec((1, ff, d), lambda i, j, s: (i, 0, 0)),
            pl.BlockSpec((1, 1, cap, 1), lambda i, j, s: (j, i, 0, 0)),
            pl.BlockSpec(memory_space=pl.ANY),
        ],
        out_specs=pl.BlockSpec(memory_space=pl.ANY),
        scratch_shapes=[
            pltpu.VMEM((batch, cap, d), F32),
            pltpu.VMEM((batch, cap, d), F32),
            pltpu.SemaphoreType.DMA((batch,)),
            pltpu.SemaphoreType.DMA((batch,)),
            pltpu.SemaphoreType.DMA((batch,)),
        ],
    )
    return pl.pallas_call(
        functools.partial(_moe_kernel, cap=cap, seq=seq),
        grid_spec=grid_spec,
        out_shape=jax.ShapeDtypeStruct(x1.shape, x1.dtype),
        input_output_aliases={6: 0},
        compiler_params=pltpu.CompilerParams(
            dimension_semantics=("arbitrary", "arbitrary"),
            vmem_limit_bytes=VMEM_LIMIT),
        name="moe_experts",
    )(idx.reshape(-1), hn, wg, wu, wd, gate[..., None], x1)


FFT_N1 = 128
FFT_N2 = 64
FFT_H1 = FFT_N1 // 2
K1_PAD = 72
Z_PITCH = 72
AB_PITCH = 136
HY_CB = 256
HP = lax.Precision.HIGHEST


def _dft_constants():
    n, n1, n2, h1 = FFT_N1 * FFT_N2, FFT_N1, FFT_N2, FFT_H1
    k1 = np.arange(h1 + 1, dtype=np.float64)
    s1 = np.arange(h1, dtype=np.float64)
    s2 = np.arange(n2, dtype=np.float64)
    tw = s2[:, None, None] * k1[None, :, None] / n

    def stage1(phase_s1):
        th = -2.0 * np.pi * (phase_s1 + tw)
        m = np.zeros((n2, 2 * K1_PAD, h1))
        m[:, :h1 + 1] = np.cos(th)
        m[:, K1_PAD:K1_PAD + h1 + 1] = np.sin(th)
        return m

    f1 = stage1(s1[None, None, :] * k1[None, :, None] / n1)
    f1b = stage1(((n1 - 1) - s1)[None, None, :] * k1[None, :, None] / n1)
    f1b0 = stage1((n1 - s1)[None, None, :] * k1[None, :, None] / n1)[0]
    f1b0[:, 0] = 0.0
    f1b[0] = f1b0

    a = 2.0 * np.pi * np.outer(np.arange(n2), np.arange(n2)) / n2
    c, s = np.cos(a), np.sin(a)
    f2 = np.block([[c, s], [-s, c]])
    f2i = np.block([[c, -s], [s, c]])

    t1 = np.arange(h1, dtype=np.float64)
    th = 2.0 * np.pi * (t1[None, :, None] * k1[None, None, :] / n1 + s2[:, None, None] * k1[None, None, :] / n)
    wgt = np.where((k1 == 0) | (k1 == h1), 1.0, 2.0) / n
    g = np.zeros((n2, h1, 2 * K1_PAD))
    g[:, :, :h1 + 1] = wgt * np.cos(th)
    g[:, :, K1_PAD:K1_PAD + h1 + 1] = -wgt * np.sin(th)
    return tuple(jnp.asarray(m, dtype=F32).astype(BF16) for m in (f1, f1b, f2, f2i, g))


def _hid_kernel(f_ref, w1t_ref, w1c_ref, w1s_ref, b1_ref, w2_ref, b2_ref, fr_ref, o_ref, *, length):
    bm = o_ref.shape[0]
    row = (pl.program_id(0) * bm + lax.broadcasted_iota(jnp.int32, (bm, 1), 0)).astype(F32)
    t = row / (length - 1.0)
    ang = f_ref[...] * (2.0 * math.pi * row / length)
    pre = (t * w1t_ref[...]
           + jnp.dot(jnp.cos(ang), w1c_ref[...], precision=HP, preferred_element_type=F32)
           - jnp.dot(jnp.sin(ang), w1s_ref[...], precision=HP, preferred_element_type=F32))
    fr = fr_ref[...]
    hid = jnp.sin(fr * (pre + b1_ref[...]))
    hid = jnp.sin(fr * (jnp.dot(hid, w2_ref[...], precision=HP, preferred_element_type=F32) + b2_ref[...]))
    o_ref[...] = hid


def _filter_hidden(length, w1, b1, w2, b2, freq, bm=512):
    bands = (FILTER_EMB - 1) // 2
    f = jnp.linspace(1e-4, bands - 1, bands, dtype=F32)[None, :]
    hdim = w1.shape[1]
    full = lambda a: pl.BlockSpec(a.shape, lambda i: (0,) * a.ndim)
    args = (f, w1[:1], w1[1:1 + bands], w1[1 + bands:], b1[None], w2, b2[None], freq[None])
    return pl.pallas_call(
        functools.partial(_hid_kernel, length=length),
        grid=(length // bm,),
        in_specs=[full(a) for a in args],
        out_specs=pl.BlockSpec((bm, hdim), lambda i: (i, 0)),
        out_shape=jax.ShapeDtypeStruct((length, hdim), F32),
        name="filter_hidden",
    )(*args)


LANES = 128


def _ld(ref, start, size, stride=None):
    idx = pl.ds(start, size) if stride is None else pl.ds(start, size, stride=stride)
    return jnp.concatenate([ref[h, idx, :] for h in range(ref.shape[0])], axis=-1)


def _st(ref, start, size, val, stride=None):
    idx = pl.ds(start, size) if stride is None else pl.ds(start, size, stride=stride)
    for h in range(ref.shape[0]):
        ref[h, idx, :] = val[:, h * LANES:(h + 1) * LANES]


def _stage1_store(ab_ref, s2, a):
    _st(ab_ref, s2, K1_PAD, a[:K1_PAD], stride=AB_PITCH)
    _st(ab_ref, FFT_N2 + s2, K1_PAD, a[K1_PAD:], stride=AB_PITCH)


def _spectrum_kernel(hid_ref, w3f_ref, w3b_ref, dl_ref, skip_ref, f1_ref, f1b_ref, f2_ref, o_ref,
                     hf_ref, hb_ref, ab_ref, *, length):
    def gen(j, c):
        row = (j * FFT_N2 + lax.broadcasted_iota(jnp.int32, (FFT_N2, 1), 0)).astype(F32)
        win = jnp.exp(-(row / (length - 1.0)) * dl_ref[...])
        hid = hid_ref[pl.ds(pl.multiple_of(j * FFT_N2, FFT_N2), FFT_N2), :]
        dst = pl.multiple_of(j * Z_PITCH, 8)
        _st(hf_ref, dst, FFT_N2, jnp.dot(hid, w3f_ref[...], precision=HP, preferred_element_type=F32) * win)
        _st(hb_ref, dst, FFT_N2, jnp.dot(hid, w3b_ref[...], precision=HP, preferred_element_type=F32) * win)
        return c

    lax.fori_loop(0, FFT_H1, gen, 0)

    def stage1(s2, c):
        xf = _ld(hf_ref, s2, FFT_H1, Z_PITCH).astype(BF16)
        xb = _ld(hb_ref, (FFT_N2 - s2) % FFT_N2, FFT_H1, Z_PITCH).astype(BF16)
        a = (jnp.dot(f1_ref[s2], xf, preferred_element_type=F32)
             + jnp.dot(f1b_ref[s2], xb, preferred_element_type=F32))
        _stage1_store(ab_ref, s2, a)
        return c

    lax.fori_loop(0, FFT_N2, stage1, 0)

    def stage2(k1, c):
        a = _ld(ab_ref, pl.multiple_of(k1 * AB_PITCH, 8), 2 * FFT_N2).astype(BF16)
        x = jnp.dot(f2_ref[...], a, preferred_element_type=F32)
        o_ref[0, k1, :FFT_N2, :] = (x[:FFT_N2] + skip_ref[0]).astype(BF16)
        o_ref[0, k1, FFT_N2:, :] = x[FFT_N2:].astype(BF16)
        return c

    lax.fori_loop(0, FFT_H1 + 1, stage2, 0)


def _filter_spectrum(hid, w3, deltas, filt_bias, consts, length):
    f1, f1b, f2, _, _ = consts
    cb = HY_CB
    ncb = HYENA_WIDTH // cb
    hdim = hid.shape[1]
    full = lambda a: pl.BlockSpec(a.shape, lambda o, c: (0,) * a.ndim)
    return pl.pallas_call(
        functools.partial(_spectrum_kernel, length=length),
        grid=(HYENA_ORDER, ncb),
        in_specs=[
            full(hid),
            pl.BlockSpec((hdim, cb), lambda o, c: (0, (o * N_DIRS) * ncb + c)),
            pl.BlockSpec((hdim, cb), lambda o, c: (0, (o * N_DIRS + 1) * ncb + c)),
            pl.BlockSpec((1, cb), lambda o, c: (0, c)),
            pl.BlockSpec((1, 1, cb), lambda o, c: (o, 0, c)),
            full(f1), full(f1b), full(f2),
        ],
        out_specs=pl.BlockSpec((1, FFT_H1 + 1, 2 * FFT_N2, cb), lambda o, c: (o, 0, 0, c)),
        out_shape=jax.ShapeDtypeStruct((HYENA_ORDER, FFT_H1 + 1, 2 * FFT_N2, HYENA_WIDTH), BF16),
        scratch_shapes=[
            pltpu.VMEM((cb // LANES, FFT_H1 * Z_PITCH, LANES), F32),
            pltpu.VMEM((cb // LANES, FFT_H1 * Z_PITCH, LANES), F32),
            pltpu.VMEM((cb // LANES, K1_PAD * AB_PITCH, LANES), F32),
        ],
        compiler_params=pltpu.CompilerParams(
            dimension_semantics=("arbitrary", "arbitrary"),
            vmem_limit_bytes=VMEM_LIMIT),
        name="filter_spectrum",
    )(hid, w3, w3, deltas, filt_bias[:, None, :], f1, f1b, f2)


def _short_conv_block(s_ref, j, w_ref, b_ref):
    big = s_ref[pl.ds(pl.multiple_of(j * FFT_N2, FFT_N2), FFT_N2 + 16), :]
    rows = FFT_N2 + 16
    prev = pltpu.roll(big, 1, 0)[8:8 + FFT_N2]
    nxt = pltpu.roll(big, rows - 1, 0)[8:8 + FFT_N2]
    cur = big[8:8 + FFT_N2]
    return b_ref[...] + prev * w_ref[0:1, :] + cur * w_ref[1:2, :] + nxt * w_ref[2:3, :]


def _load_stream(s_ref, src_ref, length):
    zeros = jnp.zeros((8, s_ref.shape[1]), F32)
    s_ref[0:8, :] = zeros
    s_ref[length + 8:length + 16, :] = zeros
    chunk = 512

    def cp(i, c):
        r = pl.multiple_of(i * chunk, chunk)
        s_ref[pl.ds(r + 8, chunk), :] = src_ref[pl.ds(r, chunk), :].astype(F32)
        return c

    lax.fori_loop(0, length // chunk, cp, 0)


def _hyena_kernel(z_ref, gs_ref, cwz_ref, cbz_ref, cwg_ref, cbg_ref, h_ref, f1_ref, f2_ref, f2i_ref, g_ref,
                  og_ref, o_ref, s_ref, zy_ref, ab_ref, *, length, conv_z, norm):
    if conv_z:
        _load_stream(s_ref, z_ref, length)

    def prep(j, c):
        if conv_z:
            zblk = _short_conv_block(s_ref, j, cwz_ref, cbz_ref)
        else:
            zblk = z_ref[pl.ds(pl.multiple_of(j * FFT_N2, FFT_N2), FFT_N2), :].astype(F32)
        _st(zy_ref, pl.multiple_of(j * Z_PITCH, 8), FFT_N2, zblk)
        return c

    lax.fori_loop(0, FFT_H1, prep, 0)

    def stage1(s2, c):
        xs = _ld(zy_ref, s2, FFT_H1, Z_PITCH).astype(BF16)
        _stage1_store(ab_ref, s2, jnp.dot(f1_ref[s2], xs, preferred_element_type=F32))
        return c

    lax.fori_loop(0, FFT_N2, stage1, 0)

    def stage2(k1, c):
        blk = pl.multiple_of(k1 * AB_PITCH, 8)
        x = jnp.dot(f2_ref[...], _ld(ab_ref, blk, 2 * FFT_N2).astype(BF16), preferred_element_type=F32)
        hk = h_ref[0, k1].astype(F32)
        xr, xi, hr, hi = x[:FFT_N2], x[FFT_N2:], hk[:FFT_N2], hk[FFT_N2:]
        p = jnp.concatenate([xr * hr - xi * hi, xr * hi + xi * hr], axis=0).astype(BF16)
        _st(ab_ref, blk, 2 * FFT_N2, jnp.dot(f2i_ref[...], p, preferred_element_type=F32))
        return c

    lax.fori_loop(0, FFT_H1 + 1, stage2, 0)

    def stage3(t2, c):
        br = _ld(ab_ref, t2, K1_PAD, AB_PITCH)
        bi = _ld(ab_ref, FFT_N2 + t2, K1_PAD, AB_PITCH)
        rhs = jnp.concatenate([br, bi], axis=0).astype(BF16)
        _st(zy_ref, t2, FFT_H1, jnp.dot(g_ref[t2], rhs, preferred_element_type=F32), stride=Z_PITCH)
        return c

    lax.fori_loop(0, FFT_N2, stage3, 0)

    _load_stream(s_ref, gs_ref, length)

    def fin(j, c):
        gate = _short_conv_block(s_ref, j, cwg_ref, cbg_ref)
        y = gate * _ld(zy_ref, pl.multiple_of(j * Z_PITCH, 8), FFT_N2)
        dst = pl.ds(pl.multiple_of(j * FFT_N2, FFT_N2), FFT_N2)
        if norm:
            gw = HYENA_WIDTH // N_HYENA_GROUPS
            for q in range(y.shape[1] // gw):
                sl = slice(q * gw, (q + 1) * gw)
                o_ref[dst, sl] = (_rms(y[:, sl]) * og_ref[:, sl]).astype(o_ref.dtype)
        else:
            o_ref[dst, :] = y.astype(o_ref.dtype)
        return c

    lax.fori_loop(0, FFT_H1, fin, 0)


def _hyena_order(zsrc, zcol, proj, order, conv_w, conv_b, hspec, consts, out_g, batch, length, conv_z, norm):
    f1, _, f2, f2i, g = consts
    cb = HY_CB
    ncb = HYENA_WIDTH // cb
    pcol = 3 * ATTN_WIDTH // cb
    zconv = HYENA_ORDER * ncb
    full = lambda a: pl.BlockSpec(a.shape, lambda c, b: (0,) * a.ndim)
    kern = functools.partial(_hyena_kernel, length=length, conv_z=conv_z, norm=norm)
    return pl.pallas_call(
        kern,
        grid=(ncb, batch),
        in_specs=[
            pl.BlockSpec((length, cb), lambda c, b: (b, zcol + c)),
            pl.BlockSpec((length, cb), lambda c, b: (b, pcol + order * ncb + c)),
            pl.BlockSpec((SHORT_CONV_W, cb), lambda c, b: (0, zconv + c)),
            pl.BlockSpec((1, cb), lambda c, b: (0, zconv + c)),
            pl.BlockSpec((SHORT_CONV_W, cb), lambda c, b: (0, order * ncb + c)),
            pl.BlockSpec((1, cb), lambda c, b: (0, order * ncb + c)),
            pl.BlockSpec((1, FFT_H1 + 1, 2 * FFT_N2, cb), lambda c, b: (order, 0, 0, c)),
            full(f1), full(f2), full(f2i), full(g),
            pl.BlockSpec((1, cb), lambda c, b: (0, c)),
        ],
        out_specs=pl.BlockSpec((length, cb), lambda c, b: (b, c)),
        out_shape=jax.ShapeDtypeStruct((batch * length, HYENA_WIDTH), BF16),
        scratch_shapes=[
            pltpu.VMEM((length + 16, cb), F32),
            pltpu.VMEM((cb // LANES, FFT_H1 * Z_PITCH, LANES), F32),
            pltpu.VMEM((cb // LANES, K1_PAD * AB_PITCH, LANES), F32),
        ],
        compiler_params=pltpu.CompilerParams(
            dimension_semantics=("arbitrary", "arbitrary"),
            vmem_limit_bytes=VMEM_LIMIT),
        name=f"hyena_order{order}",
    )(zsrc, proj, conv_w, conv_b[None], conv_w, conv_b[None], hspec, f1, f2, f2i, g, out_g)


def _hyena(proj, conv_w, conv_b, w1, b1, w2, b2, w3, freq, filt_bias, out_g, batch, length):
    assert 2 * length == FFT_N1 * FFT_N2 and HYENA_ORDER == 2
    consts = _dft_constants()
    max_decay = math.log(DECAY_TARGET) / FAST_DECAY_PCT
    min_decay = math.log(DECAY_TARGET) / SLOW_DECAY_PCT
    deltas = jnp.abs(jnp.linspace(min_decay, max_decay, HYENA_WIDTH, dtype=F32))[None]
    hid = _filter_hidden(length, w1, b1, w2, b2, freq)
    hspec = _filter_spectrum(hid, w3, deltas, filt_bias, consts, length)
    vcol = (3 * ATTN_WIDTH + HYENA_ORDER * HYENA_WIDTH) // HY_CB
    z1 = _hyena_order(proj, vcol, proj, 0, conv_w, conv_b, hspec, consts, out_g, batch, length, True, False)
    return _hyena_order(z1, 0, proj, 1, conv_w, conv_b, hspec, consts, out_g, batch, length, False, True)


def kernel(x, mix_norm_g, w_in, q_norm_g, k_norm_g, rpb, conv_w, conv_b, filt_w1, filt_b1, filt_w2, filt_b2, filt_w3, filt_freq, filt_bias, attn_out_g, hyena_out_g, w_out, ffn_norm_g, w_router, w_gate, w_up, w_down):
    b, s, d = x.shape
    rows = s // GRID_W
    win_r = min(WIN_ROWS_MAX, rows)
    cap = EC_CAPACITY_FACTOR * s // N_EXPERTS
    x2 = x.reshape(b * s, d)
    for i in range(mix_norm_g.shape[0]):
        proj = _inproj(x2, mix_norm_g[i][None], w_in[i].astype(BF16),
                       q_norm_g[i][None], k_norm_g[i][None])
        attn = _attention(proj, _attn_bias_table(rpb[i], rows, win_r), attn_out_g[i][None], b, s)

        hyn = _hyena(proj, conv_w[i], conv_b[i], filt_w1[i], filt_b1[i], filt_w2[i], filt_b2[i], filt_w3[i],
                     filt_freq[i], filt_bias[i], hyena_out_g[i][None], b, s)

        wr_pad = jnp.zeros((d, 128), BF16).at[:, :N_EXPERTS].set(w_router[i].astype(BF16))
        x1, hn, logits = _outproj(attn, hyn, x2, w_out[i].astype(BF16), ffn_norm_g[i][None], wr_pad)

        idx, gate = _route(logits, b, s, cap)
        x2 = _moe(idx, gate, hn, x1, w_gate[i].astype(BF16), w_up[i].astype(BF16), w_down[i].astype(BF16), b, s)
    return x2.reshape(b, s, d)
```

```python
import functools
import math

import numpy as np
import jax
import jax.numpy as jnp
from jax import lax
from jax.experimental import pallas as pl
from jax.experimental.pallas import tpu as pltpu

D_MODEL = 2048
GRID_W = 64
N_ATTN_HEADS = 8
ATTN_WIDTH = D_MODEL // 2
HEAD_DIM = ATTN_WIDTH // N_ATTN_HEADS
HYENA_WIDTH = D_MODEL - ATTN_WIDTH
N_HYENA_GROUPS = 8
HYENA_ORDER = 2
N_DIRS = 2
SHORT_CONV_W = 3
FILTER_EMB = 33
DECAY_TARGET = 1e-2
FAST_DECAY_PCT = 0.3
SLOW_DECAY_PCT = 1.5
WIN_ROWS_MAX = 8
WIN_COLS = 16
N_EXPERTS = 16
EC_CAPACITY_FACTOR = 2
EXPERT_FF = 1024
IN_WIDTH = 3 * ATTN_WIDTH + (HYENA_ORDER + 1) * HYENA_WIDTH
RMS_EPS = 1e-6

F32 = jnp.float32
BF16 = jnp.bfloat16
NEG_BIG = -1e30
VMEM_LIMIT = 56 * 1024 * 1024
LANES = 128


def _rms(x, eps=RMS_EPS):
    return x * lax.rsqrt(jnp.mean(x * x, axis=-1, keepdims=True) + eps)


def _inproj_kernel(x_ref, g_ref, w_ref, qg_ref, kg_ref, o_ref, h_ref):
    n = pl.program_id(1)

    @pl.when(n == 0)
    def _():
        h_ref[...] = (_rms(x_ref[...]) * g_ref[...]).astype(BF16)

    acc = jnp.dot(h_ref[...], w_ref[...], preferred_element_type=F32)
    bn = acc.shape[1]

    def head_norm(gain):
        for j in range(bn // HEAD_DIM):
            sl = slice(j * HEAD_DIM, (j + 1) * HEAD_DIM)
            o_ref[:, sl] = (_rms(acc[:, sl]) * gain).astype(BF16)

    @pl.when(n == 0)
    def _():
        head_norm(qg_ref[...] * (HEAD_DIM ** -0.5))

    @pl.when(n == 1)
    def _():
        head_norm(kg_ref[...])

    @pl.when(n >= 2)
    def _():
        o_ref[...] = acc.astype(BF16)


def _inproj(x2, g, w_bf, qg, kg, bm=1024, bn=1024):
    m, d = x2.shape
    nw = w_bf.shape[1]
    return pl.pallas_call(
        _inproj_kernel,
        grid=(m // bm, nw // bn),
        in_specs=[
            pl.BlockSpec((bm, d), lambda i, j: (i, 0)),
            pl.BlockSpec((1, d), lambda i, j: (0, 0)),
            pl.BlockSpec((d, bn), lambda i, j: (0, j)),
            pl.BlockSpec((1, HEAD_DIM), lambda i, j: (0, 0)),
            pl.BlockSpec((1, HEAD_DIM), lambda i, j: (0, 0)),
        ],
        out_specs=pl.BlockSpec((bm, bn), lambda i, j: (i, j)),
        out_shape=jax.ShapeDtypeStruct((m, nw), BF16),
        scratch_shapes=[pltpu.VMEM((bm, d), BF16)],
        compiler_params=pltpu.CompilerParams(
            dimension_semantics=("arbitrary", "arbitrary"),
            vmem_limit_bytes=VMEM_LIMIT),
        name="inproj",
    )(x2, g, w_bf, qg, kg)


def _attn_kernel(q_ref, k_ref, v_ref, rpb_ref, mask_ref, g_ref, o_ref, pt_ref, *, rows, win_r):
    band = win_r * GRID_W
    lanes = pt_ref.shape[-1]

    def toeplitz(d):
        row = jnp.broadcast_to(rpb_ref[0, d:d + 1, :], (GRID_W, lanes))
        return pltpu.roll(row, lanes - (WIN_COLS - 1), 1, stride=1, stride_axis=0)

    lane = lax.broadcasted_iota(jnp.int32, (GRID_W, lanes), 1)
    tiles = [toeplitz(d) for d in range(2 * WIN_ROWS_MAX - 1)]
    for d in range(2 * WIN_ROWS_MAX - 2):
        pt_ref[d] = jnp.where(lane < GRID_W, tiles[d], pltpu.roll(tiles[d + 1], GRID_W, 1))

    def body(r, carry):
        start = jnp.clip(r - win_r // 2, 0, rows - win_r)
        case = r - start
        q = q_ref[pl.ds(pl.multiple_of(r * GRID_W, GRID_W), GRID_W), :]
        k0 = pl.multiple_of(start * GRID_W, GRID_W)
        k = k_ref[pl.ds(k0, band), :]
        v = v_ref[pl.ds(k0, band), :]
        s = lax.dot_general(q, k, (((1,), (1,)), ((), ())), preferred_element_type=F32)
        d0 = WIN_ROWS_MAX - 1 - case
        bias = jnp.concatenate([pt_ref[d0 + 2 * m] for m in range(win_r // 2)], axis=1)
        s = s + bias + mask_ref[...]
        mx = jnp.max(s, axis=-1, keepdims=True)
        p = jnp.exp(s - mx)
        l = jnp.sum(p, axis=-1, keepdims=True)
        o = jnp.dot(p.astype(BF16), v, preferred_element_type=F32) / l
        o = _rms(o) * g_ref[...]
        o_ref[pl.ds(pl.multiple_of(r * GRID_W, GRID_W), GRID_W), :] = o.astype(BF16)
        return carry

    lax.fori_loop(0, rows, body, 0, unroll=2)


def _attn_window_mask(win_r):
    c = np.arange(GRID_W)
    c0 = np.clip(c - WIN_COLS // 2, 0, GRID_W - WIN_COLS)
    in_win = (c[None, :] >= c0[:, None]) & (c[None, :] < c0[:, None] + WIN_COLS)
    return jnp.asarray(np.tile(np.where(in_win, 0.0, NEG_BIG), (1, win_r)), dtype=F32)


def _attention(proj, rpb, out_g, batch, seq):
    rows = seq // GRID_W
    win_r = min(WIN_ROWS_MAX, rows)
    assert win_r == WIN_ROWS_MAX and win_r % 2 == 0 and 2 * GRID_W == LANES
    h = N_ATTN_HEADS
    n_dr, n_dc = rpb.shape[1:]
    rpb_pad = jnp.zeros((h, 2 * WIN_ROWS_MAX, LANES), F32).at[:, :n_dr, :n_dc].set(rpb)
    mask = _attn_window_mask(win_r)
    kern = functools.partial(_attn_kernel, rows=rows, win_r=win_r)
    return pl.pallas_call(
        kern,
        grid=(batch, h),
        in_specs=[
            pl.BlockSpec((seq, HEAD_DIM), lambda b, i: (b, i)),
            pl.BlockSpec((seq, HEAD_DIM), lambda b, i: (b, h + i)),
            pl.BlockSpec((seq, HEAD_DIM), lambda b, i: (b, 2 * h + i)),
            pl.BlockSpec((1, 2 * WIN_ROWS_MAX, LANES), lambda b, i: (i, 0, 0)),
            pl.BlockSpec(mask.shape, lambda b, i: (0, 0)),
            pl.BlockSpec((1, HEAD_DIM), lambda b, i: (0, i)),
        ],
        out_specs=pl.BlockSpec((seq, HEAD_DIM), lambda b, i: (b, i)),
        out_shape=jax.ShapeDtypeStruct((batch * seq, ATTN_WIDTH), BF16),
        scratch_shapes=[pltpu.VMEM((2 * WIN_ROWS_MAX - 2, GRID_W, LANES), F32)],
        compiler_params=pltpu.CompilerParams(
            dimension_semantics=("arbitrary", "arbitrary"),
            vmem_limit_bytes=VMEM_LIMIT),
        name="na2d_attn",
    )(proj, proj, proj, rpb_pad, mask, out_g)


def _outproj_kernel(a_ref, hy_ref, x_ref, wo_ref, g_ref, wr_ref, x1_ref, hn_ref, lg_ref):
    half = a_ref.shape[1]
    acc = x_ref[...]
    acc = acc + jnp.dot(a_ref[...], wo_ref[:half, :], preferred_element_type=F32)
    acc = acc + jnp.dot(hy_ref[...], wo_ref[half:, :], preferred_element_type=F32)
    x1_ref[...] = acc
    hn = _rms(acc) * g_ref[...]
    hn_ref[...] = hn
    lg_ref[...] = jnp.dot(hn.astype(BF16), wr_ref[...], preferred_element_type=F32)


def _outproj(attn, hy, x2, wo_bf, g, wr_pad, bm=512):
    m, d = x2.shape
    half = attn.shape[1]
    npad = wr_pad.shape[1]
    return pl.pallas_call(
        _outproj_kernel,
        grid=(m // bm,),
        in_specs=[
            pl.BlockSpec((bm, half), lambda i: (i, 0)),
            pl.BlockSpec((bm, half), lambda i: (i, 0)),
            pl.BlockSpec((bm, d), lambda i: (i, 0)),
            pl.BlockSpec((d, d), lambda i: (0, 0)),
            pl.BlockSpec((1, d), lambda i: (0, 0)),
            pl.BlockSpec((d, npad), lambda i: (0, 0)),
        ],
        out_specs=[
            pl.BlockSpec((bm, d), lambda i: (i, 0)),
            pl.BlockSpec((bm, d), lambda i: (i, 0)),
            pl.BlockSpec((bm, npad), lambda i: (i, 0)),
        ],
        out_shape=[
            jax.ShapeDtypeStruct((m, d), F32),
            jax.ShapeDtypeStruct((m, d), F32),
            jax.ShapeDtypeStruct((m, npad), F32),
        ],
        compiler_params=pltpu.CompilerParams(
            dimension_semantics=("arbitrary",),
            vmem_limit_bytes=VMEM_LIMIT),
        name="outproj",
    )(attn, hy, x2, wo_bf, g, wr_pad)


ROUTE_BLK = 128
ROUTE_CHUNK = 512
ROUTE_MATCH = 256
F32_INF_BITS = 0x7F800000


def _route_kernel(lg_ref, tri_ref, idx_ref, gate_ref, aff_ref, pos_ref, *, cap):
    s, lanes = lg_ref.shape
    lane = lax.broadcasted_iota(jnp.int32, (1, lanes), 1)

    def softmax_chunk(i, c):
        rows = pl.ds(pl.multiple_of(i * ROUTE_CHUNK, ROUTE_CHUNK), ROUTE_CHUNK)
        lg = jnp.where(lane < N_EXPERTS, lg_ref[rows, :], NEG_BIG)
        ex = jnp.exp(lg - jnp.max(lg, axis=-1, keepdims=True))
        aff_ref[rows, :] = ex / jnp.sum(ex, axis=-1, keepdims=True)
        return c

    lax.fori_loop(0, s // ROUTE_CHUNK, softmax_chunk, 0)

    def as_value(bits):
        return pltpu.bitcast(bits, F32)

    def count_ge(th):
        def body(i, acc):
            rows = pl.ds(pl.multiple_of(i * ROUTE_CHUNK, ROUTE_CHUNK), ROUTE_CHUNK)
            return acc + jnp.sum(jnp.where(aff_ref[rows, :] >= th, 1.0, 0.0), axis=0, keepdims=True)
        return lax.fori_loop(0, s // ROUTE_CHUNK, body, jnp.zeros((1, lanes), F32))

    def bisect(_, c):
        lo, hi = c
        mid = lo + ((hi - lo) >> 1)
        ok = count_ge(as_value(mid)) >= cap
        return jnp.where(ok, mid, lo), jnp.where(ok, hi, mid)

    tau_bits, _ = lax.fori_loop(0, 31, bisect, (jnp.zeros((1, lanes), jnp.int32),
                                                jnp.full((1, lanes), F32_INF_BITS, jnp.int32)))
    tau, above = as_value(tau_bits), as_value(tau_bits + 1)
    ties_wanted = cap - count_ge(above)

    def scan_block(j, c):
        tie_carry, sel_carry = c
        rows = pl.ds(pl.multiple_of(j * ROUTE_BLK, ROUTE_BLK), ROUTE_BLK)
        aff = aff_ref[rows, :]
        gt = aff >= above
        eq = jnp.where((aff >= tau) & (aff < above), 1.0, 0.0)
        tie_incl = jnp.dot(tri_ref[...], eq.astype(BF16), preferred_element_type=F32) + tie_carry
        sel = jnp.where(gt | ((eq > 0.0) & (tie_incl - eq < ties_wanted)), 1.0, 0.0)
        sel_incl = jnp.dot(tri_ref[...], sel.astype(BF16), preferred_element_type=F32) + sel_carry
        pos_ref[rows, :] = jnp.where(sel > 0.0, sel_incl - sel, -1.0)
        return tie_incl[ROUTE_BLK - 1:, :], sel_incl[ROUTE_BLK - 1:, :]

    zero = jnp.zeros((1, lanes), F32)
    lax.fori_loop(0, s // ROUTE_BLK, scan_block, (zero, zero))

    slot = lax.broadcasted_iota(jnp.int32, (1, cap), 1).astype(F32)
    for e in range(N_EXPERTS):
        def compact(c, acc, e=e):
            ia, ga = acc
            r0 = pl.multiple_of(c * ROUTE_MATCH, ROUTE_MATCH)
            match = pos_ref[pl.ds(r0, ROUTE_MATCH), e:e + 1] == slot
            tok = (r0 + lax.broadcasted_iota(jnp.int32, (ROUTE_MATCH, 1), 0)).astype(F32)
            ia = ia + jnp.sum(jnp.where(match, tok, 0.0), axis=0, keepdims=True)
            ga = ga + jnp.sum(jnp.where(match, aff_ref[pl.ds(r0, ROUTE_MATCH), e:e + 1], 0.0),
                              axis=0, keepdims=True)
            return ia, ga

        zc = jnp.zeros((1, cap), F32)
        ia, ga = lax.fori_loop(0, s // ROUTE_MATCH, compact, (zc, zc))
        idx_ref[0, e:e + 1, :] = ia.astype(jnp.int32)
        gate_ref[0, e:e + 1, :] = ga


def _route(logits, batch, seq, cap):
    lanes = logits.shape[1]
    tri = jnp.asarray(np.tril(np.ones((ROUTE_BLK, ROUTE_BLK))), dtype=BF16)
    return pl.pallas_call(
        functools.partial(_route_kernel, cap=cap),
        grid=(batch,),
        in_specs=[
            pl.BlockSpec((seq, lanes), lambda b: (b, 0)),
            pl.BlockSpec((ROUTE_BLK, ROUTE_BLK), lambda b: (0, 0)),
        ],
        out_specs=[
            pl.BlockSpec((1, N_EXPERTS, cap), lambda b: (b, 0, 0)),
            pl.BlockSpec((1, N_EXPERTS, cap), lambda b: (b, 0, 0)),
        ],
        out_shape=[
            jax.ShapeDtypeStruct((batch, N_EXPERTS, cap), jnp.int32),
            jax.ShapeDtypeStruct((batch, N_EXPERTS, cap), F32),
        ],
        scratch_shapes=[
            pltpu.VMEM((seq, lanes), F32),
            pltpu.VMEM((seq, lanes), F32),
        ],
        compiler_params=pltpu.CompilerParams(dimension_semantics=("arbitrary",)),
        name="route",
    )(logits, tri)


ROW_UNROLL = 8


def _moe_kernel(idx_ref, hn_hbm, wg_ref, wu_ref, wd_ref, gate_ref, x1_hbm, out_hbm,
                xg_buf, acc_buf, sem_x, sem_a, sem_s, *, cap, seq):
    del x1_hbm
    e, b = pl.program_id(0), pl.program_id(1)
    n_e, n_b = pl.num_programs(0), pl.num_programs(1)

    def for_rows(fn):
        def body(i, c):
            for k in range(ROW_UNROLL):
                fn(i * ROW_UNROLL + k)
            return c
        lax.fori_loop(0, cap // ROW_UNROLL, body, 0)

    def rows_of(ee, bb):
        base = (bb * n_e + ee) * cap
        return lambda p: bb * seq + idx_ref[base + p]

    def gather_x(ee, bb):
        row = rows_of(ee, bb)
        for_rows(lambda p: pltpu.make_async_copy(
            hn_hbm.at[pl.ds(row(p), 1), :], xg_buf.at[bb, pl.ds(p, 1), :], sem_x.at[bb]).start())

    def wait_all(buf, sem):
        pltpu.make_async_copy(hn_hbm.at[pl.ds(0, cap), :], buf, sem).wait()

    @pl.when((e == 0) & (b == 0))
    def _():
        gather_x(e, b)

    nb = jnp.where(b + 1 == n_b, 0, b + 1)
    ne = jnp.where(b + 1 == n_b, e + 1, e)

    @pl.when(ne < n_e)
    def _():
        gather_x(ne, nb)

    @pl.when(e > 0)
    def _():
        wait_all(acc_buf.at[b], sem_s.at[b])

    row = rows_of(e, b)
    for_rows(lambda p: pltpu.make_async_copy(
        out_hbm.at[pl.ds(row(p), 1), :], acc_buf.at[b, pl.ds(p, 1), :], sem_a.at[b]).start())

    wait_all(xg_buf.at[b], sem_x.at[b])
    xg = xg_buf[b].astype(BF16)
    a = jnp.dot(xg, wg_ref[0], preferred_element_type=F32)
    u = jnp.dot(xg, wu_ref[0], preferred_element_type=F32)
    hmid = (a * (1.0 / (1.0 + jnp.exp(-a))) * u).astype(BF16)
    y = jnp.dot(hmid, wd_ref[0], preferred_element_type=F32) * gate_ref[0, 0]

    wait_all(acc_buf.at[b], sem_a.at[b])
    acc_buf[b] = acc_buf[b] + y
    for_rows(lambda p: pltpu.make_async_copy(
        acc_buf.at[b, pl.ds(p, 1), :], out_hbm.at[pl.ds(row(p), 1), :], sem_s.at[b]).start())

    @pl.when(e == n_e - 1)
    def _():
        wait_all(acc_buf.at[b], sem_s.at[b])


def _moe(idx, gate, hn, x1, wg, wu, wd, batch, seq):
    n_e, d, ff = wg.shape
    cap = idx.shape[-1]
    grid_spec = pltpu.PrefetchScalarGridSpec(
        num_scalar_prefetch=1,
        grid=(n_e, batch),
        in_specs=[
            pl.BlockSpec(memory_space=pl.ANY),
            pl.BlockSpec((1, d, ff), lambda i, j, s: (i, 0, 0)),
            pl.BlockSpec((1, d, ff), lambda i, j, s: (i, 0, 0)),
            pl.BlockSpec((1, ff, d), lambda i, j, s: (i, 0, 0)),
            pl.BlockSpec((1, 1, cap, 1), lambda i, j, s: (j, i, 0, 0)),
            pl.BlockSpec(memory_space=pl.ANY),
        ],
        out_specs=pl.BlockSpec(memory_space=pl.ANY),
        scratch_shapes=[
            pltpu.VMEM((batch, cap, d), F32),
            pltpu.VMEM((batch, cap, d), F32),
            pltpu.SemaphoreType.DMA((batch,)),
            pltpu.SemaphoreType.DMA((batch,)),
            pltpu.SemaphoreType.DMA((batch,)),
        ],
    )
    return pl.pallas_call(
        functools.partial(_moe_kernel, cap=cap, seq=seq),
        grid_spec=grid_spec,
        out_shape=jax.ShapeDtypeStruct(x1.shape, x1.dtype),
        input_output_aliases={6: 0},
        compiler_params=pltpu.CompilerParams(
            dimension_semantics=("arbitrary", "arbitrary"),
            vmem_limit_bytes=VMEM_LIMIT),
        name="moe_experts",
    )(idx.reshape(-1), hn, wg, wu, wd, gate[..., None], x1)


FFT_N1 = 128
FFT_N2 = 64
FFT_H1 = FFT_N1 // 2
K1_PAD = 72
Z_PITCH = 72
AB_PITCH = 136
HY_CB = 256
HP = lax.Precision.HIGHEST


def _dft_constants():
    n, n1, n2, h1 = FFT_N1 * FFT_N2, FFT_N1, FFT_N2, FFT_H1
    k1 = np.arange(h1 + 1, dtype=np.float64)
    s1 = np.arange(h1, dtype=np.float64)
    s2 = np.arange(n2, dtype=np.float64)
    tw = s2[:, None, None] * k1[None, :, None] / n

    def stage1(phase_s1):
        th = -2.0 * np.pi * (phase_s1 + tw)
        m = np.zeros((n2, 2 * K1_PAD, h1))
        m[:, :h1 + 1] = np.cos(th)
        m[:, K1_PAD:K1_PAD + h1 + 1] = np.sin(th)
        return m

    f1 = stage1(s1[None, None, :] * k1[None, :, None] / n1)
    f1b = stage1(((n1 - 1) - s1)[None, None, :] * k1[None, :, None] / n1)
    f1b0 = stage1((n1 - s1)[None, None, :] * k1[None, :, None] / n1)[0]
    f1b0[:, 0] = 0.0
    f1b[0] = f1b0

    a = 2.0 * np.pi * np.outer(np.arange(n2), np.arange(n2)) / n2
    c, s = np.cos(a), np.sin(a)
    f2 = np.block([[c, s], [-s, c]])
    f2i = np.block([[c, -s], [s, c]])

    t1 = np.arange(h1, dtype=np.float64)
    th = 2.0 * np.pi * (t1[None, :, None] * k1[None, None, :] / n1 + s2[:, None, None] * k1[None, None, :] / n)
    wgt = np.where((k1 == 0) | (k1 == h1), 1.0, 2.0) / n
    g = np.zeros((n2, h1, 2 * K1_PAD))
    g[:, :, :h1 + 1] = wgt * np.cos(th)
    g[:, :, K1_PAD:K1_PAD + h1 + 1] = -wgt * np.sin(th)
    return tuple(jnp.asarray(m, dtype=F32).astype(BF16) for m in (f1, f1b, f2, f2i, g))


def _hid_kernel(f_ref, w1t_ref, w1c_ref, w1s_ref, b1_ref, w2_ref, b2_ref, fr_ref, o_ref, *, length):
    bm = o_ref.shape[0]
    row = (pl.program_id(0) * bm + lax.broadcasted_iota(jnp.int32, (bm, 1), 0)).astype(F32)
    t = row / (length - 1.0)
    ang = f_ref[...] * (2.0 * math.pi * row / length)
    pre = (t * w1t_ref[...]
           + jnp.dot(jnp.cos(ang), w1c_ref[...], precision=HP, preferred_element_type=F32)
           - jnp.dot(jnp.sin(ang), w1s_ref[...], precision=HP, preferred_element_type=F32))
    fr = fr_ref[...]
    hid = jnp.sin(fr * (pre + b1_ref[...]))
    hid = jnp.sin(fr * (jnp.dot(hid, w2_ref[...], precision=HP, preferred_element_type=F32) + b2_ref[...]))
    o_ref[...] = hid


def _filter_hidden(length, w1, b1, w2, b2, freq, bm=512):
    bands = (FILTER_EMB - 1) // 2
    f = jnp.linspace(1e-4, bands - 1, bands, dtype=F32)[None, :]
    hdim = w1.shape[1]
    full = lambda a: pl.BlockSpec(a.shape, lambda i: (0,) * a.ndim)
    args = (f, w1[:1], w1[1:1 + bands], w1[1 + bands:], b1[None], w2, b2[None], freq[None])
    return pl.pallas_call(
        functools.partial(_hid_kernel, length=length),
        grid=(length // bm,),
        in_specs=[full(a) for a in args],
        out_specs=pl.BlockSpec((bm, hdim), lambda i: (i, 0)),
        out_shape=jax.ShapeDtypeStruct((length, hdim), F32),
        name="filter_hidden",
    )(*args)


def _ld(ref, start, size, stride=None):
    idx = pl.ds(start, size) if stride is None else pl.ds(start, size, stride=stride)
    return jnp.concatenate([ref[h, idx, :] for h in range(ref.shape[0])], axis=-1)


def _st(ref, start, size, val, stride=None):
    idx = pl.ds(start, size) if stride is None else pl.ds(start, size, stride=stride)
    for h in range(ref.shape[0]):
        ref[h, idx, :] = val[:, h * LANES:(h + 1) * LANES]


def _stage1_store(ab_ref, s2, a):
    _st(ab_ref, s2, K1_PAD, a[:K1_PAD], stride=AB_PITCH)
    _st(ab_ref, FFT_N2 + s2, K1_PAD, a[K1_PAD:], stride=AB_PITCH)


def _spectrum_kernel(hid_ref, w3f_ref, w3b_ref, dl_ref, skip_ref, f1_ref, f1b_ref, f2_ref, o_ref,
                     hf_ref, hb_ref, ab_ref, *, length):
    def gen(j, c):
        row = (j * FFT_N2 + lax.broadcasted_iota(jnp.int32, (FFT_N2, 1), 0)).astype(F32)
        win = jnp.exp(-(row / (length - 1.0)) * dl_ref[...])
        hid = hid_ref[pl.ds(pl.multiple_of(j * FFT_N2, FFT_N2), FFT_N2), :]
        dst = pl.multiple_of(j * Z_PITCH, 8)
        _st(hf_ref, dst, FFT_N2, jnp.dot(hid, w3f_ref[...], precision=HP, preferred_element_type=F32) * win)
        _st(hb_ref, dst, FFT_N2, jnp.dot(hid, w3b_ref[...], precision=HP, preferred_element_type=F32) * win)
        return c

    lax.fori_loop(0, FFT_H1, gen, 0)

    def stage1(s2, c):
        xf = _ld(hf_ref, s2, FFT_H1, Z_PITCH).astype(BF16)
        xb = _ld(hb_ref, (FFT_N2 - s2) % FFT_N2, FFT_H1, Z_PITCH).astype(BF16)
        a = (jnp.dot(f1_ref[s2], xf, preferred_element_type=F32)
             + jnp.dot(f1b_ref[s2], xb, preferred_element_type=F32))
        _stage1_store(ab_ref, s2, a)
        return c

    lax.fori_loop(0, FFT_N2, stage1, 0)

    def stage2(k1, c):
        a = _ld(ab_ref, pl.multiple_of(k1 * AB_PITCH, 8), 2 * FFT_N2).astype(BF16)
        x = jnp.dot(f2_ref[...], a, preferred_element_type=F32)
        o_ref[0, k1, :FFT_N2, :] = (x[:FFT_N2] + skip_ref[0]).astype(BF16)
        o_ref[0, k1, FFT_N2:, :] = x[FFT_N2:].astype(BF16)
        return c

    lax.fori_loop(0, FFT_H1 + 1, stage2, 0)


def _filter_spectrum(hid, w3, deltas, filt_bias, consts, length):
    f1, f1b, f2, _, _ = consts
    cb = HY_CB
    ncb = HYENA_WIDTH // cb
    hdim = hid.shape[1]
    full = lambda a: pl.BlockSpec(a.shape, lambda o, c: (0,) * a.ndim)
    return pl.pallas_call(
        functools.partial(_spectrum_kernel, length=length),
        grid=(HYENA_ORDER, ncb),
        in_specs=[
            full(hid),
            pl.BlockSpec((hdim, cb), lambda o, c: (0, (o * N_DIRS) * ncb + c)),
            pl.BlockSpec((hdim, cb), lambda o, c: (0, (o * N_DIRS + 1) * ncb + c)),
            pl.BlockSpec((1, cb), lambda o, c: (0, c)),
            pl.BlockSpec((1, 1, cb), lambda o, c: (o, 0, c)),
            full(f1), full(f1b), full(f2),
        ],
        out_specs=pl.BlockSpec((1, FFT_H1 + 1, 2 * FFT_N2, cb), lambda o, c: (o, 0, 0, c)),
        out_shape=jax.ShapeDtypeStruct((HYENA_ORDER, FFT_H1 + 1, 2 * FFT_N2, HYENA_WIDTH), BF16),
        scratch_shapes=[
            pltpu.VMEM((cb // LANES, FFT_H1 * Z_PITCH, LANES), F32),
            pltpu.VMEM((cb // LANES, FFT_H1 * Z_PITCH, LANES), F32),
            pltpu.VMEM((cb // LANES, K1_PAD * AB_PITCH, LANES), F32),
        ],
        compiler_params=pltpu.CompilerParams(
            dimension_semantics=("arbitrary", "arbitrary"),
            vmem_limit_bytes=VMEM_LIMIT),
        name="filter_spectrum",
    )(hid, w3, w3, deltas, filt_bias[:, None, :], f1, f1b, f2)


def _short_conv_block(s_ref, j, w_ref, b_ref):
    big = s_ref[pl.ds(pl.multiple_of(j * FFT_N2, FFT_N2), FFT_N2 + 16), :]
    rows = FFT_N2 + 16
    prev = pltpu.roll(big, 1, 0)[8:8 + FFT_N2]
    nxt = pltpu.roll(big, rows - 1, 0)[8:8 + FFT_N2]
    cur = big[8:8 + FFT_N2]
    return b_ref[...] + prev * w_ref[0:1, :] + cur * w_ref[1:2, :] + nxt * w_ref[2:3, :]


def _load_stream(s_ref, src_ref, length):
    zeros = jnp.zeros((8, s_ref.shape[1]), F32)
    s_ref[0:8, :] = zeros
    s_ref[length + 8:length + 16, :] = zeros
    chunk = 512

    def cp(i, c):
        r = pl.multiple_of(i * chunk, chunk)
        s_ref[pl.ds(r + 8, chunk), :] = src_ref[pl.ds(r, chunk), :].astype(F32)
        return c

    lax.fori_loop(0, length // chunk, cp, 0)


def _hyena_kernel(z_ref, gs_ref, cwz_ref, cbz_ref, cwg_ref, cbg_ref, h_ref, f1_ref, f2_ref, f2i_ref, g_ref,
                  og_ref, o_ref, s_ref, zy_ref, ab_ref, *, length, conv_z, norm):
    if conv_z:
        _load_stream(s_ref, z_ref, length)

    def prep(j, c):
        if conv_z:
            zblk = _short_conv_block(s_ref, j, cwz_ref, cbz_ref)
        else:
            zblk = z_ref[pl.ds(pl.multiple_of(j * FFT_N2, FFT_N2), FFT_N2), :].astype(F32)
        _st(zy_ref, pl.multiple_of(j * Z_PITCH, 8), FFT_N2, zblk)
        return c

    lax.fori_loop(0, FFT_H1, prep, 0)

    def stage1(s2, c):
        xs = _ld(zy_ref, s2, FFT_H1, Z_PITCH).astype(BF16)
        _stage1_store(ab_ref, s2, jnp.dot(f1_ref[s2], xs, preferred_element_type=F32))
        return c

    lax.fori_loop(0, FFT_N2, stage1, 0)

    def stage2(k1, c):
        blk = pl.multiple_of(k1 * AB_PITCH, 8)
        x = jnp.dot(f2_ref[...], _ld(ab_ref, blk, 2 * FFT_N2).astype(BF16), preferred_element_type=F32)
        hk = h_ref[0, k1].astype(F32)
        xr, xi, hr, hi = x[:FFT_N2], x[FFT_N2:], hk[:FFT_N2], hk[FFT_N2:]
        p = jnp.concatenate([xr * hr - xi * hi, xr * hi + xi * hr], axis=0).astype(BF16)
        _st(ab_ref, blk, 2 * FFT_N2, jnp.dot(f2i_ref[...], p, preferred_element_type=F32))
        return c

    lax.fori_loop(0, FFT_H1 + 1, stage2, 0)

    def stage3(t2, c):
        br = _ld(ab_ref, t2, K1_PAD, AB_PITCH)
        bi = _ld(ab_ref, FFT_N2 + t2, K1_PAD, AB_PITCH)
        rhs = jnp.concatenate([br, bi], axis=0).astype(BF16)
        _st(zy_ref, t2, FFT_H1, jnp.dot(g_ref[t2], rhs, preferred_element_type=F32), stride=Z_PITCH)
        return c

    lax.fori_loop(0, FFT_N2, stage3, 0)

    _load_stream(s_ref, gs_ref, length)

    def fin(j, c):
        gate = _short_conv_block(s_ref, j, cwg_ref, cbg_ref)
        y = gate * _ld(zy_ref, pl.multiple_of(j * Z_PITCH, 8), FFT_N2)
        dst = pl.ds(pl.multiple_of(j * FFT_N2, FFT_N2), FFT_N2)
        if norm:
            gw = HYENA_WIDTH // N_HYENA_GROUPS
            for q in range(y.shape[1] // gw):
                sl = slice(q * gw, (q + 1) * gw)
                o_ref[dst, sl] = (_rms(y[:, sl]) * og_ref[:, sl]).astype(o_ref.dtype)
        else:
            o_ref[dst, :] = y.astype(o_ref.dtype)
        return c

    lax.fori_loop(0, FFT_H1, fin, 0)


def _hyena_order(zsrc, zcol, proj, order, conv_w, conv_b, hspec, consts, out_g, batch, length, conv_z, norm):
    f1, _, f2, f2i, g = consts
    cb = HY_CB
    ncb = HYENA_WIDTH // cb
    pcol = 3 * ATTN_WIDTH // cb
    zconv = HYENA_ORDER * ncb
    full = lambda a: pl.BlockSpec(a.shape, lambda c, b: (0,) * a.ndim)
    kern = functools.partial(_hyena_kernel, length=length, conv_z=conv_z, norm=norm)
    return pl.pallas_call(
        kern,
        grid=(ncb, batch),
        in_specs=[
            pl.BlockSpec((length, cb), lambda c, b: (b, zcol + c)),
            pl.BlockSpec((length, cb), lambda c, b: (b, pcol + order * ncb + c)),
            pl.BlockSpec((SHORT_CONV_W, cb), lambda c, b: (0, zconv + c)),
            pl.BlockSpec((1, cb), lambda c, b: (0, zconv + c)),
            pl.BlockSpec((SHORT_CONV_W, cb), lambda c, b: (0, order * ncb + c)),
            pl.BlockSpec((1, cb), lambda c, b: (0, order * ncb + c)),
            pl.BlockSpec((1, FFT_H1 + 1, 2 * FFT_N2, cb), lambda c, b: (order, 0, 0, c)),
            full(f1), full(f2), full(f2i), full(g),
            pl.BlockSpec((1, cb), lambda c, b: (0, c)),
        ],
        out_specs=pl.BlockSpec((length, cb), lambda c, b: (b, c)),
        out_shape=jax.ShapeDtypeStruct((batch * length, HYENA_WIDTH), BF16),
        scratch_shapes=[
            pltpu.VMEM((length + 16, cb), F32),
            pltpu.VMEM((cb // LANES, FFT_H1 * Z_PITCH, LANES), F32),
            pltpu.VMEM((cb // LANES, K1_PAD * AB_PITCH, LANES), F32),
        ],
        compiler_params=pltpu.CompilerParams(
            dimension_semantics=("arbitrary", "arbitrary"),
            vmem_limit_bytes=VMEM_LIMIT),
        name=f"hyena_order{order}",
    )(zsrc, proj, conv_w, conv_b[None], conv_w, conv_b[None], hspec, f1, f2, f2i, g, out_g)


def _hyena(proj, conv_w, conv_b, w1, b1, w2, b2, w3, freq, filt_bias, out_g, batch, length):
    assert 2 * length == FFT_N1 * FFT_N2 and HYENA_ORDER == 2
    consts = _dft_constants()
    max_decay = math.log(DECAY_TARGET) / FAST_DECAY_PCT
    min_decay = math.log(DECAY_TARGET) / SLOW_DECAY_PCT
    deltas = jnp.abs(jnp.linspace(min_decay, max_decay, HYENA_WIDTH, dtype=F32))[None]
    hid = _filter_hidden(length, w1, b1, w2, b2, freq)
    hspec = _filter_spectrum(hid, w3, deltas, filt_bias, consts, length)
    vcol = (3 * ATTN_WIDTH + HYENA_ORDER * HYENA_WIDTH) // HY_CB
    z1 = _hyena_order(proj, vcol, proj, 0, conv_w, conv_b, hspec, consts, out_g, batch, length, True, False)
    return _hyena_order(z1, 0, proj, 1, conv_w, conv_b, hspec, consts, out_g, batch, length, False, True)


def kernel(x, mix_norm_g, w_in, q_norm_g, k_norm_g, rpb, conv_w, conv_b, filt_w1, filt_b1, filt_w2, filt_b2, filt_w3, filt_freq, filt_bias, attn_out_g, hyena_out_g, w_out, ffn_norm_g, w_router, w_gate, w_up, w_down):
    b, s, d = x.shape
    rows = s // GRID_W
    win_r = min(WIN_ROWS_MAX, rows)
    cap = EC_CAPACITY_FACTOR * s // N_EXPERTS
    x2 = x.reshape(b * s, d)
    for i in range(mix_norm_g.shape[0]):
        proj = _inproj(x2, mix_norm_g[i][None], w_in[i].astype(BF16),
                       q_norm_g[i][None], k_norm_g[i][None])
        attn = _attention(proj, rpb[i], attn_out_g[i][None], b, s)

        hyn = _hyena(proj, conv_w[i], conv_b[i], filt_w1[i], filt_b1[i], filt_w2[i], filt_b2[i], filt_w3[i],
                     filt_freq[i], filt_bias[i], hyena_out_g[i][None], b, s)

        wr_pad = jnp.zeros((d, 128), BF16).at[:, :N_EXPERTS].set(w_router[i].astype(BF16))
        x1, hn, logits = _outproj(attn, hyn, x2, w_out[i].astype(BF16), ffn_norm_g[i][None], wr_pad)

        idx, gate = _route(logits, b, s, cap)
        x2 = _moe(idx, gate, hn, x1, w_gate[i].astype(BF16), w_up[i].astype(BF16), w_down[i].astype(BF16), b, s)
    return x2.reshape(b, s, d)
```

```python
import functools
import math

import numpy as np
import jax
import jax.numpy as jnp
from jax import lax
from jax.experimental import pallas as pl
from jax.experimental.pallas import tpu as pltpu

D_MODEL = 2048
GRID_W = 64
N_ATTN_HEADS = 8
ATTN_WIDTH = D_MODEL // 2
HEAD_DIM = ATTN_WIDTH // N_ATTN_HEADS
HYENA_WIDTH = D_MODEL - ATTN_WIDTH
N_HYENA_GROUPS = 8
HYENA_ORDER = 2
N_DIRS = 2
SHORT_CONV_W = 3
FILTER_EMB = 33
DECAY_TARGET = 1e-2
FAST_DECAY_PCT = 0.3
SLOW_DECAY_PCT = 1.5
WIN_ROWS_MAX = 8
WIN_COLS = 16
N_EXPERTS = 16
EC_CAPACITY_FACTOR = 2
EXPERT_FF = 1024
IN_WIDTH = 3 * ATTN_WIDTH + (HYENA_ORDER + 1) * HYENA_WIDTH
RMS_EPS = 1e-6

F32 = jnp.float32
BF16 = jnp.bfloat16
NEG_BIG = -1e30
VMEM_LIMIT = 56 * 1024 * 1024
LANES = 128


def _rms(x, eps=RMS_EPS):
    return x * lax.rsqrt(jnp.mean(x * x, axis=-1, keepdims=True) + eps)


def _inproj_kernel(x_ref, g_ref, w_ref, qg_ref, kg_ref, o_ref, h_ref):
    n = pl.program_id(1)

    @pl.when(n == 0)
    def _():
        h_ref[...] = (_rms(x_ref[...]) * g_ref[...]).astype(BF16)

    acc = jnp.dot(h_ref[...], w_ref[...], preferred_element_type=F32)
    bn = acc.shape[1]

    def head_norm(gain):
        for j in range(bn // HEAD_DIM):
            sl = slice(j * HEAD_DIM, (j + 1) * HEAD_DIM)
            o_ref[:, sl] = (_rms(acc[:, sl]) * gain).astype(BF16)

    @pl.when(n == 0)
    def _():
        head_norm(qg_ref[...] * (HEAD_DIM ** -0.5))

    @pl.when(n == 1)
    def _():
        head_norm(kg_ref[...])

    @pl.when(n >= 2)
    def _():
        o_ref[...] = acc.astype(BF16)


def _inproj(x2, g, w_bf, qg, kg, bm=1024, bn=1024):
    m, d = x2.shape
    nw = w_bf.shape[1]
    return pl.pallas_call(
        _inproj_kernel,
        grid=(m // bm, nw // bn),
        in_specs=[
            pl.BlockSpec((bm, d), lambda i, j: (i, 0)),
            pl.BlockSpec((1, d), lambda i, j: (0, 0)),
            pl.BlockSpec((d, bn), lambda i, j: (0, j)),
            pl.BlockSpec((1, HEAD_DIM), lambda i, j: (0, 0)),
            pl.BlockSpec((1, HEAD_DIM), lambda i, j: (0, 0)),
        ],
        out_specs=pl.BlockSpec((bm, bn), lambda i, j: (i, j)),
        out_shape=jax.ShapeDtypeStruct((m, nw), BF16),
        scratch_shapes=[pltpu.VMEM((bm, d), BF16)],
        compiler_params=pltpu.CompilerParams(
            dimension_semantics=("arbitrary", "arbitrary"),
            vmem_limit_bytes=VMEM_LIMIT),
        name="inproj",
    )(x2, g, w_bf, qg, kg)


def _attn_kernel(q_ref, k_ref, v_ref, rpb_ref, mask_ref, g_ref, o_ref, pt_ref, *, rows, win_r):
    band = win_r * GRID_W
    lanes = pt_ref.shape[-1]

    def toeplitz(d):
        row = jnp.broadcast_to(rpb_ref[0, d:d + 1, :], (GRID_W, lanes))
        return pltpu.roll(row, lanes - (WIN_COLS - 1), 1, stride=1, stride_axis=0)

    lane = lax.broadcasted_iota(jnp.int32, (GRID_W, lanes), 1)
    tiles = [toeplitz(d) for d in range(2 * WIN_ROWS_MAX - 1)]
    for d in range(2 * WIN_ROWS_MAX - 2):
        pt_ref[d] = jnp.where(lane < GRID_W, tiles[d], pltpu.roll(tiles[d + 1], GRID_W, 1))

    def body(r, carry):
        start = jnp.clip(r - win_r // 2, 0, rows - win_r)
        case = r - start
        q = q_ref[pl.ds(pl.multiple_of(r * GRID_W, GRID_W), GRID_W), :]
        k0 = pl.multiple_of(start * GRID_W, GRID_W)
        k = k_ref[pl.ds(k0, band), :]
        v = v_ref[pl.ds(k0, band), :]
        s = lax.dot_general(q, k, (((1,), (1,)), ((), ())), preferred_element_type=F32)
        d0 = WIN_ROWS_MAX - 1 - case
        bias = jnp.concatenate([pt_ref[d0 + 2 * m] for m in range(win_r // 2)], axis=1)
        s = s + bias + mask_ref[...]
        mx = jnp.max(s, axis=-1, keepdims=True)
        p = jnp.exp(s - mx)
        l = jnp.sum(p, axis=-1, keepdims=True)
        o = jnp.dot(p.astype(BF16), v, preferred_element_type=F32) / l
        o = _rms(o) * g_ref[...]
        o_ref[pl.ds(pl.multiple_of(r * GRID_W, GRID_W), GRID_W), :] = o.astype(BF16)
        return carry

    lax.fori_loop(0, rows, body, 0, unroll=4)


def _attn_window_mask(win_r):
    c = np.arange(GRID_W)
    c0 = np.clip(c - WIN_COLS // 2, 0, GRID_W - WIN_COLS)
    in_win = (c[None, :] >= c0[:, None]) & (c[None, :] < c0[:, None] + WIN_COLS)
    return jnp.asarray(np.tile(np.where(in_win, 0.0, NEG_BIG), (1, win_r)), dtype=F32)


def _attention(proj, rpb, out_g, batch, seq):
    rows = seq // GRID_W
    win_r = min(WIN_ROWS_MAX, rows)
    assert win_r == WIN_ROWS_MAX and win_r % 2 == 0 and 2 * GRID_W == LANES
    h = N_ATTN_HEADS
    n_dr, n_dc = rpb.shape[1:]
    rpb_pad = jnp.zeros((h, 2 * WIN_ROWS_MAX, LANES), F32).at[:, :n_dr, :n_dc].set(rpb)
    mask = _attn_window_mask(win_r)
    kern = functools.partial(_attn_kernel, rows=rows, win_r=win_r)
    return pl.pallas_call(
        kern,
        grid=(batch, h),
        in_specs=[
            pl.BlockSpec((seq, HEAD_DIM), lambda b, i: (b, i)),
            pl.BlockSpec((seq, HEAD_DIM), lambda b, i: (b, h + i)),
            pl.BlockSpec((seq, HEAD_DIM), lambda b, i: (b, 2 * h + i)),
            pl.BlockSpec((1, 2 * WIN_ROWS_MAX, LANES), lambda b, i: (i, 0, 0)),
            pl.BlockSpec(mask.shape, lambda b, i: (0, 0)),
            pl.BlockSpec((1, HEAD_DIM), lambda b, i: (0, i)),
        ],
        out_specs=pl.BlockSpec((seq, HEAD_DIM), lambda b, i: (b, i)),
        out_shape=jax.ShapeDtypeStruct((batch * seq, ATTN_WIDTH), BF16),
        scratch_shapes=[pltpu.VMEM((2 * WIN_ROWS_MAX - 2, GRID_W, LANES), F32)],
        compiler_params=pltpu.CompilerParams(
            dimension_semantics=("arbitrary", "arbitrary"),
            vmem_limit_bytes=VMEM_LIMIT),
        name="na2d_attn",
    )(proj, proj, proj, rpb_pad, mask, out_g)


def _outproj_kernel(a_ref, hy_ref, x_ref, wo_ref, g_ref, wr_ref, x1_ref, hn_ref, lg_ref):
    half = a_ref.shape[1]
    acc = x_ref[...]
    acc = acc + jnp.dot(a_ref[...], wo_ref[:half, :], preferred_element_type=F32)
    acc = acc + jnp.dot(hy_ref[...], wo_ref[half:, :], preferred_element_type=F32)
    x1_ref[...] = acc
    hn = _rms(acc) * g_ref[...]
    hn_ref[...] = hn
    lg_ref[...] = jnp.dot(hn.astype(BF16), wr_ref[...], preferred_element_type=F32)


def _outproj(attn, hy, x2, wo_bf, g, wr_pad, bm=512):
    m, d = x2.shape
    half = attn.shape[1]
    npad = wr_pad.shape[1]
    return pl.pallas_call(
        _outproj_kernel,
        grid=(m // bm,),
        in_specs=[
            pl.BlockSpec((bm, half), lambda i: (i, 0)),
            pl.BlockSpec((bm, half), lambda i: (i, 0)),
            pl.BlockSpec((bm, d), lambda i: (i, 0)),
            pl.BlockSpec((d, d), lambda i: (0, 0)),
            pl.BlockSpec((1, d), lambda i: (0, 0)),
            pl.BlockSpec((d, npad), lambda i: (0, 0)),
        ],
        out_specs=[
            pl.BlockSpec((bm, d), lambda i: (i, 0)),
            pl.BlockSpec((bm, d), lambda i: (i, 0)),
            pl.BlockSpec((bm, npad), lambda i: (i, 0)),
        ],
        out_shape=[
            jax.ShapeDtypeStruct((m, d), F32),
            jax.ShapeDtypeStruct((m, d), F32),
            jax.ShapeDtypeStruct((m, npad), F32),
        ],
        compiler_params=pltpu.CompilerParams(
            dimension_semantics=("arbitrary",),
            vmem_limit_bytes=VMEM_LIMIT),
        name="outproj",
    )(attn, hy, x2, wo_bf, g, wr_pad)


ROUTE_BLK = 128
ROUTE_CHUNK = 512
ROUTE_MATCH = 256
F32_INF_BITS = 0x7F800000


def _route_kernel(lg_ref, tri_ref, idx_ref, gate_ref, aff_ref, pos_ref, *, cap):
    s, lanes = lg_ref.shape
    lane = lax.broadcasted_iota(jnp.int32, (1, lanes), 1)

    def softmax_chunk(i, c):
        rows = pl.ds(pl.multiple_of(i * ROUTE_CHUNK, ROUTE_CHUNK), ROUTE_CHUNK)
        lg = jnp.where(lane < N_EXPERTS, lg_ref[rows, :], NEG_BIG)
        ex = jnp.exp(lg - jnp.max(lg, axis=-1, keepdims=True))
        aff_ref[rows, :] = ex / jnp.sum(ex, axis=-1, keepdims=True)
        return c

    lax.fori_loop(0, s // ROUTE_CHUNK, softmax_chunk, 0)

    def as_value(bits):
        return pltpu.bitcast(bits, F32)

    def count_ge(th):
        def body(i, acc):
            rows = pl.ds(pl.multiple_of(i * ROUTE_CHUNK, ROUTE_CHUNK), ROUTE_CHUNK)
            return acc + jnp.sum(jnp.where(aff_ref[rows, :] >= th, 1.0, 0.0), axis=0, keepdims=True)
        return lax.fori_loop(0, s // ROUTE_CHUNK, body, jnp.zeros((1, lanes), F32))

    def bisect(_, c):
        lo, hi = c
        mid = lo + ((hi - lo) >> 1)
        ok = count_ge(as_value(mid)) >= cap
        return jnp.where(ok, mid, lo), jnp.where(ok, hi, mid)

    tau_bits, _ = lax.fori_loop(0, 31, bisect, (jnp.zeros((1, lanes), jnp.int32),
                                                jnp.full((1, lanes), F32_INF_BITS, jnp.int32)))
    tau, above = as_value(tau_bits), as_value(tau_bits + 1)
    ties_wanted = cap - count_ge(above)

    def scan_block(j, c):
        tie_carry, sel_carry = c
        rows = pl.ds(pl.multiple_of(j * ROUTE_BLK, ROUTE_BLK), ROUTE_BLK)
        aff = aff_ref[rows, :]
        gt = aff >= above
        eq = jnp.where((aff >= tau) & (aff < above), 1.0, 0.0)
        tie_incl = jnp.dot(tri_ref[...], eq.astype(BF16), preferred_element_type=F32) + tie_carry
        sel = jnp.where(gt | ((eq > 0.0) & (tie_incl - eq < ties_wanted)), 1.0, 0.0)
        sel_incl = jnp.dot(tri_ref[...], sel.astype(BF16), preferred_element_type=F32) + sel_carry
        pos_ref[rows, :] = jnp.where(sel > 0.0, sel_incl - sel, -1.0)
        return tie_incl[ROUTE_BLK - 1:, :], sel_incl[ROUTE_BLK - 1:, :]

    zero = jnp.zeros((1, lanes), F32)
    lax.fori_loop(0, s // ROUTE_BLK, scan_block, (zero, zero))

    slot = lax.broadcasted_iota(jnp.int32, (1, cap), 1).astype(F32)
    for e in range(N_EXPERTS):
        def compact(c, acc, e=e):
            ia, ga = acc
            r0 = pl.multiple_of(c * ROUTE_MATCH, ROUTE_MATCH)
            match = pos_ref[pl.ds(r0, ROUTE_MATCH), e:e + 1] == slot
            tok = (r0 + lax.broadcasted_iota(jnp.int32, (ROUTE_MATCH, 1), 0)).astype(F32)
            ia = ia + jnp.sum(jnp.where(match, tok, 0.0), axis=0, keepdims=True)
            ga = ga + jnp.sum(jnp.where(match, aff_ref[pl.ds(r0, ROUTE_MATCH), e:e + 1], 0.0),
                              axis=0, keepdims=True)
            return ia, ga

        zc = jnp.zeros((1, cap), F32)
        ia, ga = lax.fori_loop(0, s // ROUTE_MATCH, compact, (zc, zc))
        idx_ref[0, e:e + 1, :] = ia.astype(jnp.int32)
        gate_ref[0, e:e + 1, :] = ga


def _route(logits, batch, seq, cap):
    lanes = logits.shape[1]
    tri = jnp.asarray(np.tril(np.ones((ROUTE_BLK, ROUTE_BLK))), dtype=BF16)
    return pl.pallas_call(
        functools.partial(_route_kernel, cap=cap),
        grid=(batch,),
        in_specs=[
            pl.BlockSpec((seq, lanes), lambda b: (b, 0)),
            pl.BlockSpec((ROUTE_BLK, ROUTE_BLK), lambda b: (0, 0)),
        ],
        out_specs=[
            pl.BlockSpec((1, N_EXPERTS, cap), lambda b: (b, 0, 0)),
            pl.BlockSpec((1, N_EXPERTS, cap), lambda b: (b, 0, 0)),
        ],
        out_shape=[
            jax.ShapeDtypeStruct((batch, N_EXPERTS, cap), jnp.int32),
            jax.ShapeDtypeStruct((batch, N_EXPERTS, cap), F32),
        ],
        scratch_shapes=[
            pltpu.VMEM((seq, lanes), F32),
            pltpu.VMEM((seq, lanes), F32),
        ],
        compiler_params=pltpu.CompilerParams(dimension_semantics=("arbitrary",)),
        name="route",
    )(logits, tri)


ROW_UNROLL = 8


def _moe_kernel(idx_ref, hn_hbm, wg_ref, wu_ref, wd_ref, gate_ref, x1_hbm, out_hbm,
                xg_buf, acc_buf, sem_x, sem_a, sem_s, *, cap, seq):
    del x1_hbm
    e, b = pl.program_id(0), pl.program_id(1)
    n_e, n_b = pl.num_programs(0), pl.num_programs(1)

    def for_rows(fn):
        def body(i, c):
            for k in range(ROW_UNROLL):
                fn(i * ROW_UNROLL + k)
            return c
        lax.fori_loop(0, cap // ROW_UNROLL, body, 0)

    def rows_of(ee, bb):
        base = (bb * n_e + ee) * cap
        return lambda p: bb * seq + idx_ref[base + p]

    def gather_x(ee, bb):
        row = rows_of(ee, bb)
        for_rows(lambda p: pltpu.make_async_copy(
            hn_hbm.at[pl.ds(row(p), 1), :], xg_buf.at[bb, pl.ds(p, 1), :], sem_x.at[bb]).start())

    def wait_all(buf, sem):
        pltpu.make_async_copy(hn_hbm.at[pl.ds(0, cap), :], buf, sem).wait()

    @pl.when((e == 0) & (b == 0))
    def _():
        gather_x(e, b)

    nb = jnp.where(b + 1 == n_b, 0, b + 1)
    ne = jnp.where(b + 1 == n_b, e + 1, e)

    @pl.when(ne < n_e)
    def _():
        gather_x(ne, nb)

    @pl.when(e > 0)
    def _():
        wait_all(acc_buf.at[b], sem_s.at[b])

    row = rows_of(e, b)
    for_rows(lambda p: pltpu.make_async_copy(
        out_hbm.at[pl.ds(row(p), 1), :], acc_buf.at[b, pl.ds(p, 1), :], sem_a.at[b]).start())

    wait_all(xg_buf.at[b], sem_x.at[b])
    xg = xg_buf[b].astype(BF16)
    a = jnp.dot(xg, wg_ref[0], preferred_element_type=F32)
    u = jnp.dot(xg, wu_ref[0], preferred_element_type=F32)
    hmid = (a * (1.0 / (1.0 + jnp.exp(-a))) * u).astype(BF16)
    y = jnp.dot(hmid, wd_ref[0], preferred_element_type=F32) * gate_ref[0, 0]

    wait_all(acc_buf.at[b], sem_a.at[b])
    acc_buf[b] = acc_buf[b] + y
    for_rows(lambda p: pltpu.make_async_copy(
        acc_buf.at[b, pl.ds(p, 1), :], out_hbm.at[pl.ds(row(p), 1), :], sem_s.at[b]).start())

    @pl.when(e == n_e - 1)
    def _():
        wait_all(acc_buf.at[b], sem_s.at[b])


def _moe(idx, gate, hn, x1, wg, wu, wd, batch, seq):
    n_e, d, ff = wg.shape
    cap = idx.shape[-1]
    grid_spec = pltpu.PrefetchScalarGridSpec(
        num_scalar_prefetch=1,
        grid=(n_e, batch),
        in_specs=[
            pl.BlockSpec(memory_space=pl.ANY),
            pl.BlockSpec((1, d, ff), lambda i, j, s: (i, 0, 0)),
            pl.BlockSpec((1, d, ff), lambda i, j, s: (i, 0, 0)),
            pl.BlockSpec((1, ff, d), lambda i, j, s: (i, 0, 0)),
            pl.BlockSpec((1, 1, cap, 1), lambda i, j, s: (j, i, 0, 0)),
            pl.BlockSpec(memory_space=pl.ANY),
        ],
        out_specs=pl.BlockSpec(memory_space=pl.ANY),
        scratch_shapes=[
            pltpu.VMEM((batch, cap, d), F32),
            pltpu.VMEM((batch, cap, d), F32),
            pltpu.SemaphoreType.DMA((batch,)),
            pltpu.SemaphoreType.DMA((batch,)),
            pltpu.SemaphoreType.DMA((batch,)),
        ],
    )
    return pl.pallas_call(
        functools.partial(_moe_kernel, cap=cap, seq=seq),
        grid_spec=grid_spec,
        out_shape=jax.ShapeDtypeStruct(x1.shape, x1.dtype),
        input_output_aliases={6: 0},
        compiler_params=pltpu.CompilerParams(
            dimension_semantics=("arbitrary", "arbitrary"),
            vmem_limit_bytes=VMEM_LIMIT),
        name="moe_experts",
    )(idx.reshape(-1), hn, wg, wu, wd, gate[..., None], x1)


FFT_N1 = 128
FFT_N2 = 64
FFT_H1 = FFT_N1 // 2
K1_PAD = 72
Z_PITCH = 72
AB_PITCH = 136
HY_CB = 256
FFT_UNROLL = 8
HP = lax.Precision.HIGHEST


def _dft_constants():
    n, n1, n2, h1 = FFT_N1 * FFT_N2, FFT_N1, FFT_N2, FFT_H1
    k1 = np.arange(h1 + 1, dtype=np.float64)
    s1 = np.arange(h1, dtype=np.float64)
    s2 = np.arange(n2, dtype=np.float64)
    tw = s2[:, None, None] * k1[None, :, None] / n

    def stage1(phase_s1):
        th = -2.0 * np.pi * (phase_s1 + tw)
        m = np.zeros((n2, 2 * K1_PAD, h1))
        m[:, :h1 + 1] = np.cos(th)
        m[:, K1_PAD:K1_PAD + h1 + 1] = np.sin(th)
        return m

    f1 = stage1(s1[None, None, :] * k1[None, :, None] / n1)
    f1b = stage1(((n1 - 1) - s1)[None, None, :] * k1[None, :, None] / n1)
    f1b0 = stage1((n1 - s1)[None, None, :] * k1[None, :, None] / n1)[0]
    f1b0[:, 0] = 0.0
    f1b[0] = f1b0

    a = 2.0 * np.pi * np.outer(np.arange(n2), np.arange(n2)) / n2
    c, s = np.cos(a), np.sin(a)
    f2 = np.block([[c, s], [-s, c]])
    f2i = np.block([[c, -s], [s, c]])

    t1 = np.arange(h1, dtype=np.float64)
    th = 2.0 * np.pi * (t1[None, :, None] * k1[None, None, :] / n1 + s2[:, None, None] * k1[None, None, :] / n)
    wgt = np.where((k1 == 0) | (k1 == h1), 1.0, 2.0) / n
    g = np.zeros((n2, h1, 2 * K1_PAD))
    g[:, :, :h1 + 1] = wgt * np.cos(th)
    g[:, :, K1_PAD:K1_PAD + h1 + 1] = -wgt * np.sin(th)
    return tuple(jnp.asarray(m, dtype=F32).astype(BF16) for m in (f1, f1b, f2, f2i, g))


def _hid_kernel(f_ref, w1t_ref, w1c_ref, w1s_ref, b1_ref, w2_ref, b2_ref, fr_ref, o_ref, *, length):
    bm = o_ref.shape[0]
    row = (pl.program_id(0) * bm + lax.broadcasted_iota(jnp.int32, (bm, 1), 0)).astype(F32)
    t = row / (length - 1.0)
    ang = f_ref[...] * (2.0 * math.pi * row / length)
    pre = (t * w1t_ref[...]
           + jnp.dot(jnp.cos(ang), w1c_ref[...], precision=HP, preferred_element_type=F32)
           - jnp.dot(jnp.sin(ang), w1s_ref[...], precision=HP, preferred_element_type=F32))
    fr = fr_ref[...]
    hid = jnp.sin(fr * (pre + b1_ref[...]))
    hid = jnp.sin(fr * (jnp.dot(hid, w2_ref[...], precision=HP, preferred_element_type=F32) + b2_ref[...]))
    o_ref[...] = hid


def _filter_hidden(length, w1, b1, w2, b2, freq, bm=512):
    bands = (FILTER_EMB - 1) // 2
    f = jnp.linspace(1e-4, bands - 1, bands, dtype=F32)[None, :]
    hdim = w1.shape[1]
    full = lambda a: pl.BlockSpec(a.shape, lambda i: (0,) * a.ndim)
    args = (f, w1[:1], w1[1:1 + bands], w1[1 + bands:], b1[None], w2, b2[None], freq[None])
    return pl.pallas_call(
        functools.partial(_hid_kernel, length=length),
        grid=(length // bm,),
        in_specs=[full(a) for a in args],
        out_specs=pl.BlockSpec((bm, hdim), lambda i: (i, 0)),
        out_shape=jax.ShapeDtypeStruct((length, hdim), F32),
        name="filter_hidden",
    )(*args)


def _ld(ref, start, size, stride=None):
    idx = pl.ds(start, size) if stride is None else pl.ds(start, size, stride=stride)
    return jnp.concatenate([ref[h, idx, :] for h in range(ref.shape[0])], axis=-1)


def _st(ref, start, size, val, stride=None):
    idx = pl.ds(start, size) if stride is None else pl.ds(start, size, stride=stride)
    for h in range(ref.shape[0]):
        ref[h, idx, :] = val[:, h * LANES:(h + 1) * LANES]


def _stage1_store(ab_ref, s2, a):
    _st(ab_ref, s2, K1_PAD, a[:K1_PAD], stride=AB_PITCH)
    _st(ab_ref, FFT_N2 + s2, K1_PAD, a[K1_PAD:], stride=AB_PITCH)


def _split_bf16(a):
    hi = a.astype(BF16)
    return hi, (a - hi.astype(F32)).astype(BF16)


def _dot_split(a_hi, a_lo, b_hi, b_lo):
    d = functools.partial(jnp.dot, preferred_element_type=F32)
    return d(a_hi, b_hi) + (d(a_hi, b_lo) + d(a_lo, b_hi))


def _spectrum_kernel(hid_ref, w3f_ref, w3b_ref, dl_ref, skip_ref, f1_ref, f1b_ref, f2_ref, o_ref,
                     hf_ref, hb_ref, ab_ref, *, length):
    blocks = 4
    rows = blocks * FFT_N2
    w3f, w3b = _split_bf16(w3f_ref[...]), _split_bf16(w3b_ref[...])

    def gen(j, c):
        row = (j * rows + lax.broadcasted_iota(jnp.int32, (rows, 1), 0)).astype(F32)
        win = jnp.exp(-(row / (length - 1.0)) * dl_ref[...])
        hid = hid_ref[pl.ds(pl.multiple_of(j * rows, rows), rows), :]
        hid_hi, hid_lo = _split_bf16(hid)
        hf = _dot_split(hid_hi, hid_lo, *w3f) * win
        hb = _dot_split(hid_hi, hid_lo, *w3b) * win
        for q in range(blocks):
            dst = pl.multiple_of((j * blocks + q) * Z_PITCH, 8)
            _st(hf_ref, dst, FFT_N2, hf[q * FFT_N2:(q + 1) * FFT_N2])
            _st(hb_ref, dst, FFT_N2, hb[q * FFT_N2:(q + 1) * FFT_N2])
        return c

    lax.fori_loop(0, FFT_H1 // blocks, gen, 0)

    def stage1(s2, c):
        xf = _ld(hf_ref, s2, FFT_H1, Z_PITCH).astype(BF16)
        xb = _ld(hb_ref, (FFT_N2 - s2) % FFT_N2, FFT_H1, Z_PITCH).astype(BF16)
        a = (jnp.dot(f1_ref[s2], xf, preferred_element_type=F32)
             + jnp.dot(f1b_ref[s2], xb, preferred_element_type=F32))
        _stage1_store(ab_ref, s2, a)
        return c

    lax.fori_loop(0, FFT_N2, stage1, 0, unroll=FFT_UNROLL)

    def stage2(k1, c):
        a = _ld(ab_ref, pl.multiple_of(k1 * AB_PITCH, 8), 2 * FFT_N2).astype(BF16)
        x = jnp.dot(f2_ref[...], a, preferred_element_type=F32)
        o_ref[0, k1, :FFT_N2, :] = (x[:FFT_N2] + skip_ref[0]).astype(BF16)
        o_ref[0, k1, FFT_N2:, :] = x[FFT_N2:].astype(BF16)
        return c

    lax.fori_loop(0, FFT_H1 + 1, stage2, 0, unroll=FFT_UNROLL)


def _filter_spectrum(hid, w3, deltas, filt_bias, consts, length):
    f1, f1b, f2, _, _ = consts
    cb = HY_CB
    ncb = HYENA_WIDTH // cb
    hdim = hid.shape[1]
    full = lambda a: pl.BlockSpec(a.shape, lambda o, c: (0,) * a.ndim)
    return pl.pallas_call(
        functools.partial(_spectrum_kernel, length=length),
        grid=(HYENA_ORDER, ncb),
        in_specs=[
            full(hid),
            pl.BlockSpec((hdim, cb), lambda o, c: (0, (o * N_DIRS) * ncb + c)),
            pl.BlockSpec((hdim, cb), lambda o, c: (0, (o * N_DIRS + 1) * ncb + c)),
            pl.BlockSpec((1, cb), lambda o, c: (0, c)),
            pl.BlockSpec((1, 1, cb), lambda o, c: (o, 0, c)),
            full(f1), full(f1b), full(f2),
        ],
        out_specs=pl.BlockSpec((1, FFT_H1 + 1, 2 * FFT_N2, cb), lambda o, c: (o, 0, 0, c)),
        out_shape=jax.ShapeDtypeStruct((HYENA_ORDER, FFT_H1 + 1, 2 * FFT_N2, HYENA_WIDTH), BF16),
        scratch_shapes=[
            pltpu.VMEM((cb // LANES, FFT_H1 * Z_PITCH, LANES), F32),
            pltpu.VMEM((cb // LANES, FFT_H1 * Z_PITCH, LANES), F32),
            pltpu.VMEM((cb // LANES, K1_PAD * AB_PITCH, LANES), F32),
        ],
        compiler_params=pltpu.CompilerParams(
            dimension_semantics=("arbitrary", "arbitrary"),
            vmem_limit_bytes=VMEM_LIMIT),
        name="filter_spectrum",
    )(hid, w3, w3, deltas, filt_bias[:, None, :], f1, f1b, f2)


def _short_conv_block(s_ref, j, w_ref, b_ref):
    big = s_ref[pl.ds(pl.multiple_of(j * FFT_N2, FFT_N2), FFT_N2 + 16), :]
    rows = FFT_N2 + 16
    prev = pltpu.roll(big, 1, 0)[8:8 + FFT_N2]
    nxt = pltpu.roll(big, rows - 1, 0)[8:8 + FFT_N2]
    cur = big[8:8 + FFT_N2]
    return b_ref[...] + prev * w_ref[0:1, :] + cur * w_ref[1:2, :] + nxt * w_ref[2:3, :]


def _load_stream(s_ref, src_ref, length):
    zeros = jnp.zeros((8, s_ref.shape[1]), F32)
    s_ref[0:8, :] = zeros
    s_ref[length + 8:length + 16, :] = zeros
    chunk = 512

    def cp(i, c):
        r = pl.multiple_of(i * chunk, chunk)
        s_ref[pl.ds(r + 8, chunk), :] = src_ref[pl.ds(r, chunk), :].astype(F32)
        return c

    lax.fori_loop(0, length // chunk, cp, 0)


def _hyena_kernel(z_ref, gs_ref, cwz_ref, cbz_ref, cwg_ref, cbg_ref, h_ref, f1_ref, f2_ref, f2i_ref, g_ref,
                  og_ref, o_ref, s_ref, zy_ref, ab_ref, *, length, conv_z, norm):
    if conv_z:
        _load_stream(s_ref, z_ref, length)

    def prep(j, c):
        if conv_z:
            zblk = _short_conv_block(s_ref, j, cwz_ref, cbz_ref)
        else:
            zblk = z_ref[pl.ds(pl.multiple_of(j * FFT_N2, FFT_N2), FFT_N2), :].astype(F32)
        _st(zy_ref, pl.multiple_of(j * Z_PITCH, 8), FFT_N2, zblk)
        return c

    lax.fori_loop(0, FFT_H1, prep, 0)

    def stage1(s2, c):
        xs = _ld(zy_ref, s2, FFT_H1, Z_PITCH).astype(BF16)
        _stage1_store(ab_ref, s2, jnp.dot(f1_ref[s2], xs, preferred_element_type=F32))
        return c

    lax.fori_loop(0, FFT_N2, stage1, 0, unroll=FFT_UNROLL)

    def stage2(k1, c):
        blk = pl.multiple_of(k1 * AB_PITCH, 8)
        x = jnp.dot(f2_ref[...], _ld(ab_ref, blk, 2 * FFT_N2).astype(BF16), preferred_element_type=F32)
        hk = h_ref[0, k1].astype(F32)
        xr, xi, hr, hi = x[:FFT_N2], x[FFT_N2:], hk[:FFT_N2], hk[FFT_N2:]
        p = jnp.concatenate([xr * hr - xi * hi, xr * hi + xi * hr], axis=0).astype(BF16)
        _st(ab_ref, blk, 2 * FFT_N2, jnp.dot(f2i_ref[...], p, preferred_element_type=F32))
        return c

    lax.fori_loop(0, FFT_H1 + 1, stage2, 0, unroll=FFT_UNROLL)

    def stage3(t2, c):
        br = _ld(ab_ref, t2, K1_PAD, AB_PITCH)
        bi = _ld(ab_ref, FFT_N2 + t2, K1_PAD, AB_PITCH)
        rhs = jnp.concatenate([br, bi], axis=0).astype(BF16)
        _st(zy_ref, t2, FFT_H1, jnp.dot(g_ref[t2], rhs, preferred_element_type=F32), stride=Z_PITCH)
        return c

    lax.fori_loop(0, FFT_N2, stage3, 0, unroll=FFT_UNROLL)

    _load_stream(s_ref, gs_ref, length)

    def fin(j, c):
        gate = _short_conv_block(s_ref, j, cwg_ref, cbg_ref)
        y = gate * _ld(zy_ref, pl.multiple_of(j * Z_PITCH, 8), FFT_N2)
        dst = pl.ds(pl.multiple_of(j * FFT_N2, FFT_N2), FFT_N2)
        if norm:
            gw = HYENA_WIDTH // N_HYENA_GROUPS
            for q in range(y.shape[1] // gw):
                sl = slice(q * gw, (q + 1) * gw)
                o_ref[dst, sl] = (_rms(y[:, sl]) * og_ref[:, sl]).astype(o_ref.dtype)
        else:
            o_ref[dst, :] = y.astype(o_ref.dtype)
        return c

    lax.fori_loop(0, FFT_H1, fin, 0)


def _hyena_order(zsrc, zcol, proj, order, conv_w, conv_b, hspec, consts, out_g, batch, length, conv_z, norm):
    f1, _, f2, f2i, g = consts
    cb = HY_CB
    ncb = HYENA_WIDTH // cb
    pcol = 3 * ATTN_WIDTH // cb
    zconv = HYENA_ORDER * ncb
    full = lambda a: pl.BlockSpec(a.shape, lambda c, b: (0,) * a.ndim)
    kern = functools.partial(_hyena_kernel, length=length, conv_z=conv_z, norm=norm)
    return pl.pallas_call(
        kern,
        grid=(ncb, batch),
        in_specs=[
            pl.BlockSpec((length, cb), lambda c, b: (b, zcol + c)),
            pl.BlockSpec((length, cb), lambda c, b: (b, pcol + order * ncb + c)),
            pl.BlockSpec((SHORT_CONV_W, cb), lambda c, b: (0, zconv + c)),
            pl.BlockSpec((1, cb), lambda c, b: (0, zconv + c)),
            pl.BlockSpec((SHORT_CONV_W, cb), lambda c, b: (0, order * ncb + c)),
            pl.BlockSpec((1, cb), lambda c, b: (0, order * ncb + c)),
            pl.BlockSpec((1, FFT_H1 + 1, 2 * FFT_N2, cb), lambda c, b: (order, 0, 0, c)),
            full(f1), full(f2), full(f2i), full(g),
            pl.BlockSpec((1, cb), lambda c, b: (0, c)),
        ],
        out_specs=pl.BlockSpec((length, cb), lambda c, b: (b, c)),
        out_shape=jax.ShapeDtypeStruct((batch * length, HYENA_WIDTH), BF16),
        scratch_shapes=[
            pltpu.VMEM((length + 16, cb), F32),
            pltpu.VMEM((cb // LANES, FFT_H1 * Z_PITCH, LANES), F32),
            pltpu.VMEM((cb // LANES, K1_PAD * AB_PITCH, LANES), F32),
        ],
        compiler_params=pltpu.CompilerParams(
            dimension_semantics=("arbitrary", "arbitrary"),
            vmem_limit_bytes=VMEM_LIMIT),
        name=f"hyena_order{order}",
    )(zsrc, proj, conv_w, conv_b[None], conv_w, conv_b[None], hspec, f1, f2, f2i, g, out_g)


def _hyena(proj, conv_w, conv_b, w1, b1, w2, b2, w3, freq, filt_bias, out_g, batch, length):
    assert 2 * length == FFT_N1 * FFT_N2 and HYENA_ORDER == 2
    consts = _dft_constants()
    max_decay = math.log(DECAY_TARGET) / FAST_DECAY_PCT
    min_decay = math.log(DECAY_TARGET) / SLOW_DECAY_PCT
    deltas = jnp.abs(jnp.linspace(min_decay, max_decay, HYENA_WIDTH, dtype=F32))[None]
    hid = _filter_hidden(length, w1, b1, w2, b2, freq)
    hspec = _filter_spectrum(hid, w3, deltas, filt_bias, consts, length)
    vcol = (3 * ATTN_WIDTH + HYENA_ORDER * HYENA_WIDTH) // HY_CB
    z1 = _hyena_order(proj, vcol, proj, 0, conv_w, conv_b, hspec, consts, out_g, batch, length, True, False)
    return _hyena_order(z1, 0, proj, 1, conv_w, conv_b, hspec, consts, out_g, batch, length, False, True)


def kernel(x, mix_norm_g, w_in, q_norm_g, k_norm_g, rpb, conv_w, conv_b, filt_w1, filt_b1, filt_w2, filt_b2, filt_w3, filt_freq, filt_bias, attn_out_g, hyena_out_g, w_out, ffn_norm_g, w_router, w_gate, w_up, w_down):
    b, s, d = x.shape
    rows = s // GRID_W
    win_r = min(WIN_ROWS_MAX, rows)
    cap = EC_CAPACITY_FACTOR * s // N_EXPERTS
    x2 = x.reshape(b * s, d)
    for i in range(mix_norm_g.shape[0]):
        proj = _inproj(x2, mix_norm_g[i][None], w_in[i].astype(BF16),
                       q_norm_g[i][None], k_norm_g[i][None])
        attn = _attention(proj, rpb[i], attn_out_g[i][None], b, s)

        hyn = _hyena(proj, conv_w[i], conv_b[i], filt_w1[i], filt_b1[i], filt_w2[i], filt_b2[i], filt_w3[i],
                     filt_freq[i], filt_bias[i], hyena_out_g[i][None], b, s)

        wr_pad = jnp.zeros((d, 128), BF16).at[:, :N_EXPERTS].set(w_router[i].astype(BF16))
        x1, hn, logits = _outproj(attn, hyn, x2, w_out[i].astype(BF16), ffn_norm_g[i][None], wr_pad)

        idx, gate = _route(logits, b, s, cap)
        x2 = _moe(idx, gate, hn, x1, w_gate[i].astype(BF16), w_up[i].astype(BF16), w_down[i].astype(BF16), b, s)
    return x2.reshape(b, s, d)
```

```python
import functools
import math

import numpy as np
import jax
import jax.numpy as jnp
from jax import lax
from jax.experimental import pallas as pl
from jax.experimental.pallas import tpu as pltpu

D_MODEL = 2048
GRID_W = 64
N_ATTN_HEADS = 8
ATTN_WIDTH = D_MODEL // 2
HEAD_DIM = ATTN_WIDTH // N_ATTN_HEADS
HYENA_WIDTH = D_MODEL - ATTN_WIDTH
N_HYENA_GROUPS = 8
HYENA_ORDER = 2
N_DIRS = 2
SHORT_CONV_W = 3
FILTER_EMB = 33
DECAY_TARGET = 1e-2
FAST_DECAY_PCT = 0.3
SLOW_DECAY_PCT = 1.5
WIN_ROWS_MAX = 8
WIN_COLS = 16
N_EXPERTS = 16
EC_CAPACITY_FACTOR = 2
EXPERT_FF = 1024
IN_WIDTH = 3 * ATTN_WIDTH + (HYENA_ORDER + 1) * HYENA_WIDTH
RMS_EPS = 1e-6

F32 = jnp.float32
BF16 = jnp.bfloat16
NEG_BIG = -1e30
VMEM_LIMIT = 56 * 1024 * 1024
LANES = 128


def _rms(x, eps=RMS_EPS):
    return x * lax.rsqrt(jnp.mean(x * x, axis=-1, keepdims=True) + eps)


def _inproj_kernel(x_ref, g_ref, w_ref, qg_ref, kg_ref, o_ref, h_ref):
    n = pl.program_id(1)

    @pl.when(n == 0)
    def _():
        h_ref[...] = (_rms(x_ref[...]) * g_ref[...]).astype(BF16)

    acc = jnp.dot(h_ref[...], w_ref[...], preferred_element_type=F32)
    bn = acc.shape[1]

    def head_norm(gain):
        for j in range(bn // HEAD_DIM):
            sl = slice(j * HEAD_DIM, (j + 1) * HEAD_DIM)
            o_ref[:, sl] = (_rms(acc[:, sl]) * gain).astype(BF16)

    @pl.when(n == 0)
    def _():
        head_norm(qg_ref[...] * (HEAD_DIM ** -0.5))

    @pl.when(n == 1)
    def _():
        head_norm(kg_ref[...])

    @pl.when(n >= 2)
    def _():
        o_ref[...] = acc.astype(BF16)


def _inproj(x2, g, w_bf, qg, kg, bm=1024, bn=1024):
    m, d = x2.shape
    nw = w_bf.shape[1]
    return pl.pallas_call(
        _inproj_kernel,
        grid=(m // bm, nw // bn),
        in_specs=[
            pl.BlockSpec((bm, d), lambda i, j: (i, 0)),
            pl.BlockSpec((1, d), lambda i, j: (0, 0)),
            pl.BlockSpec((d, bn), lambda i, j: (0, j)),
            pl.BlockSpec((1, HEAD_DIM), lambda i, j: (0, 0)),
            pl.BlockSpec((1, HEAD_DIM), lambda i, j: (0, 0)),
        ],
        out_specs=pl.BlockSpec((bm, bn), lambda i, j: (i, j)),
        out_shape=jax.ShapeDtypeStruct((m, nw), BF16),
        scratch_shapes=[pltpu.VMEM((bm, d), BF16)],
        compiler_params=pltpu.CompilerParams(
            dimension_semantics=("arbitrary", "arbitrary"),
            vmem_limit_bytes=VMEM_LIMIT),
        name="inproj",
    )(x2, g, w_bf, qg, kg)


def _attn_kernel(q_ref, k_ref, v_ref, rpb_ref, mask_ref, g_ref, o_ref, pt_ref, *, rows, win_r):
    band = win_r * GRID_W
    lanes = pt_ref.shape[-1]

    def toeplitz(d):
        row = jnp.broadcast_to(rpb_ref[0, d:d + 1, :], (GRID_W, lanes))
        return pltpu.roll(row, lanes - (WIN_COLS - 1), 1, stride=1, stride_axis=0)

    lane = lax.broadcasted_iota(jnp.int32, (GRID_W, lanes), 1)
    tiles = [toeplitz(d) for d in range(2 * WIN_ROWS_MAX - 1)]
    for d in range(2 * WIN_ROWS_MAX - 2):
        pt_ref[d] = jnp.where(lane < GRID_W, tiles[d], pltpu.roll(tiles[d + 1], GRID_W, 1))

    def body(r, carry):
        start = jnp.clip(r - win_r // 2, 0, rows - win_r)
        case = r - start
        q = q_ref[pl.ds(pl.multiple_of(r * GRID_W, GRID_W), GRID_W), :]
        k0 = pl.multiple_of(start * GRID_W, GRID_W)
        k = k_ref[pl.ds(k0, band), :]
        v = v_ref[pl.ds(k0, band), :]
        s = lax.dot_general(q, k, (((1,), (1,)), ((), ())), preferred_element_type=F32)
        d0 = WIN_ROWS_MAX - 1 - case
        bias = jnp.concatenate([pt_ref[d0 + 2 * m] for m in range(win_r // 2)], axis=1)
        s = s + bias + mask_ref[...]
        mx = jnp.max(s, axis=-1, keepdims=True)
        p = jnp.exp(s - mx)
        l = jnp.sum(p, axis=-1, keepdims=True)
        o = jnp.dot(p.astype(BF16), v, preferred_element_type=F32) / l
        o = _rms(o) * g_ref[...]
        o_ref[pl.ds(pl.multiple_of(r * GRID_W, GRID_W), GRID_W), :] = o.astype(BF16)
        return carry

    lax.fori_loop(0, rows, body, 0, unroll=4)


def _attn_window_mask(win_r):
    c = np.arange(GRID_W)
    c0 = np.clip(c - WIN_COLS // 2, 0, GRID_W - WIN_COLS)
    in_win = (c[None, :] >= c0[:, None]) & (c[None, :] < c0[:, None] + WIN_COLS)
    return jnp.asarray(np.tile(np.where(in_win, 0.0, NEG_BIG), (1, win_r)), dtype=F32)


def _attention(proj, rpb, out_g, batch, seq):
    rows = seq // GRID_W
    win_r = min(WIN_ROWS_MAX, rows)
    assert win_r == WIN_ROWS_MAX and win_r % 2 == 0 and 2 * GRID_W == LANES
    h = N_ATTN_HEADS
    n_dr, n_dc = rpb.shape[1:]
    rpb_pad = jnp.zeros((h, 2 * WIN_ROWS_MAX, LANES), F32).at[:, :n_dr, :n_dc].set(rpb)
    mask = _attn_window_mask(win_r)
    kern = functools.partial(_attn_kernel, rows=rows, win_r=win_r)
    return pl.pallas_call(
        kern,
        grid=(batch, h),
        in_specs=[
            pl.BlockSpec((seq, HEAD_DIM), lambda b, i: (b, i)),
            pl.BlockSpec((seq, HEAD_DIM), lambda b, i: (b, h + i)),
            pl.BlockSpec((seq, HEAD_DIM), lambda b, i: (b, 2 * h + i)),
            pl.BlockSpec((1, 2 * WIN_ROWS_MAX, LANES), lambda b, i: (i, 0, 0)),
            pl.BlockSpec(mask.shape, lambda b, i: (0, 0)),
            pl.BlockSpec((1, HEAD_DIM), lambda b, i: (0, i)),
        ],
        out_specs=pl.BlockSpec((seq, HEAD_DIM), lambda b, i: (b, i)),
        out_shape=jax.ShapeDtypeStruct((batch * seq, ATTN_WIDTH), BF16),
        scratch_shapes=[pltpu.VMEM((2 * WIN_ROWS_MAX - 2, GRID_W, LANES), F32)],
        compiler_params=pltpu.CompilerParams(
            dimension_semantics=("arbitrary", "arbitrary"),
            vmem_limit_bytes=VMEM_LIMIT),
        name="na2d_attn",
    )(proj, proj, proj, rpb_pad, mask, out_g)


def _outproj_kernel(a_ref, hy_ref, x_ref, wo_ref, g_ref, wr_ref, x1_ref, hn_ref, lg_ref):
    half = a_ref.shape[1]
    acc = x_ref[...]
    acc = acc + jnp.dot(a_ref[...], wo_ref[:half, :], preferred_element_type=F32)
    acc = acc + jnp.dot(hy_ref[...], wo_ref[half:, :], preferred_element_type=F32)
    x1_ref[...] = acc
    hn = _rms(acc) * g_ref[...]
    hn_ref[...] = hn
    lg_ref[...] = jnp.dot(hn.astype(BF16), wr_ref[...], preferred_element_type=F32)


def _outproj(attn, hy, x2, wo_bf, g, wr_pad, bm=512):
    m, d = x2.shape
    half = attn.shape[1]
    npad = wr_pad.shape[1]
    return pl.pallas_call(
        _outproj_kernel,
        grid=(m // bm,),
        in_specs=[
            pl.BlockSpec((bm, half), lambda i: (i, 0)),
            pl.BlockSpec((bm, half), lambda i: (i, 0)),
            pl.BlockSpec((bm, d), lambda i: (i, 0)),
            pl.BlockSpec((d, d), lambda i: (0, 0)),
            pl.BlockSpec((1, d), lambda i: (0, 0)),
            pl.BlockSpec((d, npad), lambda i: (0, 0)),
        ],
        out_specs=[
            pl.BlockSpec((bm, d), lambda i: (i, 0)),
            pl.BlockSpec((bm, d), lambda i: (i, 0)),
            pl.BlockSpec((bm, npad), lambda i: (i, 0)),
        ],
        out_shape=[
            jax.ShapeDtypeStruct((m, d), F32),
            jax.ShapeDtypeStruct((m, d), F32),
            jax.ShapeDtypeStruct((m, npad), F32),
        ],
        compiler_params=pltpu.CompilerParams(
            dimension_semantics=("arbitrary",),
            vmem_limit_bytes=VMEM_LIMIT),
        name="outproj",
    )(attn, hy, x2, wo_bf, g, wr_pad)


ROUTE_BLK = 128
ROUTE_CHUNK = 512
ROUTE_MATCH = 256
F32_INF_BITS = 0x7F800000


def _route_kernel(lg_ref, tri_ref, idx_ref, gate_ref, aff_ref, pos_ref, *, cap):
    s, lanes = lg_ref.shape
    lane = lax.broadcasted_iota(jnp.int32, (1, lanes), 1)

    def softmax_chunk(i, c):
        rows = pl.ds(pl.multiple_of(i * ROUTE_CHUNK, ROUTE_CHUNK), ROUTE_CHUNK)
        lg = jnp.where(lane < N_EXPERTS, lg_ref[rows, :], NEG_BIG)
        ex = jnp.exp(lg - jnp.max(lg, axis=-1, keepdims=True))
        aff_ref[rows, :] = ex / jnp.sum(ex, axis=-1, keepdims=True)
        return c

    lax.fori_loop(0, s // ROUTE_CHUNK, softmax_chunk, 0)

    def as_value(bits):
        return pltpu.bitcast(bits, F32)

    def count_ge(th):
        def body(i, acc):
            rows = pl.ds(pl.multiple_of(i * ROUTE_CHUNK, ROUTE_CHUNK), ROUTE_CHUNK)
            return acc + jnp.sum(jnp.where(aff_ref[rows, :] >= th, 1.0, 0.0), axis=0, keepdims=True)
        return lax.fori_loop(0, s // ROUTE_CHUNK, body, jnp.zeros((1, lanes), F32))

    def bisect(_, c):
        lo, hi = c
        mid = lo + ((hi - lo) >> 1)
        ok = count_ge(as_value(mid)) >= cap
        return jnp.where(ok, mid, lo), jnp.where(ok, hi, mid)

    tau_bits, _ = lax.fori_loop(0, 31, bisect, (jnp.zeros((1, lanes), jnp.int32),
                                                jnp.full((1, lanes), F32_INF_BITS, jnp.int32)))
    tau, above = as_value(tau_bits), as_value(tau_bits + 1)
    ties_wanted = cap - count_ge(above)

    def scan_block(j, c):
        tie_carry, sel_carry = c
        rows = pl.ds(pl.multiple_of(j * ROUTE_BLK, ROUTE_BLK), ROUTE_BLK)
        aff = aff_ref[rows, :]
        gt = aff >= above
        eq = jnp.where((aff >= tau) & (aff < above), 1.0, 0.0)
        tie_incl = jnp.dot(tri_ref[...], eq.astype(BF16), preferred_element_type=F32) + tie_carry
        sel = jnp.where(gt | ((eq > 0.0) & (tie_incl - eq < ties_wanted)), 1.0, 0.0)
        sel_incl = jnp.dot(tri_ref[...], sel.astype(BF16), preferred_element_type=F32) + sel_carry
        pos_ref[rows, :] = jnp.where(sel > 0.0, sel_incl - sel, -1.0)
        return tie_incl[ROUTE_BLK - 1:, :], sel_incl[ROUTE_BLK - 1:, :]

    zero = jnp.zeros((1, lanes), F32)
    lax.fori_loop(0, s // ROUTE_BLK, scan_block, (zero, zero))

    slot = lax.broadcasted_iota(jnp.int32, (1, cap), 1).astype(F32)
    for e in range(N_EXPERTS):
        def compact(c, acc, e=e):
            ia, ga = acc
            r0 = pl.multiple_of(c * ROUTE_MATCH, ROUTE_MATCH)
            match = pos_ref[pl.ds(r0, ROUTE_MATCH), e:e + 1] == slot
            tok = (r0 + lax.broadcasted_iota(jnp.int32, (ROUTE_MATCH, 1), 0)).astype(F32)
            ia = ia + jnp.sum(jnp.where(match, tok, 0.0), axis=0, keepdims=True)
            ga = ga + jnp.sum(jnp.where(match, aff_ref[pl.ds(r0, ROUTE_MATCH), e:e + 1], 0.0),
                              axis=0, keepdims=True)
            return ia, ga

        zc = jnp.zeros((1, cap), F32)
        ia, ga = lax.fori_loop(0, s // ROUTE_MATCH, compact, (zc, zc))
        idx_ref[0, e:e + 1, :] = ia.astype(jnp.int32)
        gate_ref[0, e:e + 1, :] = ga


def _route(logits, batch, seq, cap):
    lanes = logits.shape[1]
    tri = jnp.asarray(np.tril(np.ones((ROUTE_BLK, ROUTE_BLK))), dtype=BF16)
    return pl.pallas_call(
        functools.partial(_route_kernel, cap=cap),
        grid=(batch,),
        in_specs=[
            pl.BlockSpec((seq, lanes), lambda b: (b, 0)),
            pl.BlockSpec((ROUTE_BLK, ROUTE_BLK), lambda b: (0, 0)),
        ],
        out_specs=[
            pl.BlockSpec((1, N_EXPERTS, cap), lambda b: (b, 0, 0)),
            pl.BlockSpec((1, N_EXPERTS, cap), lambda b: (b, 0, 0)),
        ],
        out_shape=[
            jax.ShapeDtypeStruct((batch, N_EXPERTS, cap), jnp.int32),
            jax.ShapeDtypeStruct((batch, N_EXPERTS, cap), F32),
        ],
        scratch_shapes=[
            pltpu.VMEM((seq, lanes), F32),
            pltpu.VMEM((seq, lanes), F32),
        ],
        compiler_params=pltpu.CompilerParams(dimension_semantics=("arbitrary",)),
        name="route",
    )(logits, tri)


MOE_FF_SPLIT = 4
ROW_UNROLL = 8


def _moe_kernel(idx_ref, hn_hbm, wg_ref, wu_ref, wd_ref, gate_ref, x1_hbm, out_hbm,
                xg0, xg1, acc0, acc1, y0, y1, sem_x, sem_a, sem_s, *, cap, seq):
    del x1_hbm
    e, f = pl.program_id(0), pl.program_id(1)
    n_e = pl.num_programs(0)
    xg, acc = (xg0, xg1), (acc0, acc1)
    prev_e, next_e = jnp.maximum(e - 1, 0), jnp.minimum(e + 1, n_e - 1)

    def row_of(ee, b):
        base = (b * n_e + ee) * cap
        return lambda p: b * seq + idx_ref[base + p]

    def copy_rows(ee, b, kind, rolled=False):
        row = row_of(ee, b)

        def one(p):
            if kind == "xg":
                src, dst, sem = hn_hbm.at[pl.ds(row(p), 1), :], xg[b].at[pl.ds(p, 1), :], sem_x.at[b]
            elif kind == "ag":
                src, dst, sem = out_hbm.at[pl.ds(row(p), 1), :], acc[b].at[pl.ds(p, 1), :], sem_a.at[b]
            else:
                src, dst, sem = acc[b].at[pl.ds(p, 1), :], out_hbm.at[pl.ds(row(p), 1), :], sem_s.at[b]
            pltpu.make_async_copy(src, dst, sem).start()

        if rolled:
            def body(i, c):
                for k in range(ROW_UNROLL):
                    one(i * ROW_UNROLL + k)
                return c
            lax.fori_loop(0, cap // ROW_UNROLL, body, 0)
        else:
            for p in range(cap):
                one(p)

    def wait_rows(buf, sem):
        pltpu.make_async_copy(hn_hbm.at[pl.ds(0, cap), :], buf, sem).wait()

    def partial_out(b, wg, wu, wd):
        x = xg[b][...].astype(BF16)
        a = jnp.dot(x, wg, preferred_element_type=F32)
        u = jnp.dot(x, wu, preferred_element_type=F32)
        hmid = (a * (1.0 / (1.0 + jnp.exp(-a))) * u).astype(BF16)
        return jnp.dot(hmid, wd, preferred_element_type=F32)

    def weights():
        return wg_ref[0].astype(BF16), wu_ref[0].astype(BF16), wd_ref[0].astype(BF16)

    @pl.when((e == 0) & (f == 0))
    def _():
        copy_rows(e, 0, "xg", rolled=True)
        copy_rows(e, 0, "ag", rolled=True)
        copy_rows(e, 1, "ag", rolled=True)
        wait_rows(acc0, sem_a.at[0])
        wait_rows(acc1, sem_a.at[1])

    @pl.when(f == 0)
    def _():
        w = weights()
        wait_rows(xg0, sem_x.at[0])
        copy_rows(e, 1, "xg")
        y0[...] = partial_out(0, *w)
        wait_rows(xg1, sem_x.at[1])
        copy_rows(prev_e, 0, "sc")
        y1[...] = partial_out(1, *w)

    @pl.when(f == 1)
    def _():
        w = weights()
        copy_rows(prev_e, 1, "sc")
        y0[...] += partial_out(0, *w)
        wait_rows(acc0, sem_s.at[0])
        copy_rows(e, 0, "ag")
        y1[...] += partial_out(1, *w)

    @pl.when(f == 2)
    def _():
        w = weights()
        wait_rows(acc1, sem_s.at[1])
        copy_rows(e, 1, "ag")
        y0[...] += partial_out(0, *w)
        y1[...] += partial_out(1, *w)

    @pl.when(f == 3)
    def _():
        w = weights()
        t0 = y0[...] + partial_out(0, *w)
        wait_rows(acc0, sem_a.at[0])
        acc0[...] += t0 * gate_ref[0, 0]
        copy_rows(next_e, 0, "xg")
        t1 = y1[...] + partial_out(1, *w)
        wait_rows(acc1, sem_a.at[1])
        acc1[...] += t1 * gate_ref[1, 0]

    @pl.when((e == n_e - 1) & (f == MOE_FF_SPLIT - 1))
    def _():
        copy_rows(e, 0, "sc", rolled=True)
        copy_rows(e, 1, "sc", rolled=True)
        wait_rows(acc0, sem_s.at[0])
        wait_rows(acc1, sem_s.at[1])
        wait_rows(xg0, sem_x.at[0])


def _moe(idx, gate, hn, x1, wg, wu, wd, batch, seq):
    n_e, d, ff = wg.shape
    cap = idx.shape[-1]
    assert batch == 2 and MOE_FF_SPLIT == 4 and ff % MOE_FF_SPLIT == 0
    fq = ff // MOE_FF_SPLIT
    grid_spec = pltpu.PrefetchScalarGridSpec(
        num_scalar_prefetch=1,
        grid=(n_e, MOE_FF_SPLIT),
        in_specs=[
            pl.BlockSpec(memory_space=pl.ANY),
            pl.BlockSpec((1, d, fq), lambda i, j, s: (i, 0, j)),
            pl.BlockSpec((1, d, fq), lambda i, j, s: (i, 0, j)),
            pl.BlockSpec((1, fq, d), lambda i, j, s: (i, j, 0)),
            pl.BlockSpec((batch, 1, cap, 1), lambda i, j, s: (0, i, 0, 0)),
            pl.BlockSpec(memory_space=pl.ANY),
        ],
        out_specs=pl.BlockSpec(memory_space=pl.ANY),
        scratch_shapes=(
            [pltpu.VMEM((cap, d), F32) for _ in range(6)]
            + [pltpu.SemaphoreType.DMA((batch,)) for _ in range(3)]),
    )
    return pl.pallas_call(
        functools.partial(_moe_kernel, cap=cap, seq=seq),
        grid_spec=grid_spec,
        out_shape=jax.ShapeDtypeStruct(x1.shape, x1.dtype),
        input_output_aliases={6: 0},
        compiler_params=pltpu.CompilerParams(
            dimension_semantics=("arbitrary", "arbitrary"),
            vmem_limit_bytes=VMEM_LIMIT),
        name="moe_experts",
    )(idx.reshape(-1), hn, wg, wu, wd, gate[..., None], x1)


FFT_N1 = 128
FFT_N2 = 64
FFT_H1 = FFT_N1 // 2
K1_PAD = 72
Z_PITCH = 72
AB_PITCH = 136
HY_CB = 256
FFT_UNROLL = 8
HP = lax.Precision.HIGHEST


def _dft_constants():
    n, n1, n2, h1 = FFT_N1 * FFT_N2, FFT_N1, FFT_N2, FFT_H1
    k1 = np.arange(h1 + 1, dtype=np.float64)
    s1 = np.arange(h1, dtype=np.float64)
    s2 = np.arange(n2, dtype=np.float64)
    tw = s2[:, None, None] * k1[None, :, None] / n

    def stage1(phase_s1):
        th = -2.0 * np.pi * (phase_s1 + tw)
        m = np.zeros((n2, 2 * K1_PAD, h1))
        m[:, :h1 + 1] = np.cos(th)
        m[:, K1_PAD:K1_PAD + h1 + 1] = np.sin(th)
        return m

    f1 = stage1(s1[None, None, :] * k1[None, :, None] / n1)
    f1b = stage1(((n1 - 1) - s1)[None, None, :] * k1[None, :, None] / n1)
    f1b0 = stage1((n1 - s1)[None, None, :] * k1[None, :, None] / n1)[0]
    f1b0[:, 0] = 0.0
    f1b[0] = f1b0

    a = 2.0 * np.pi * np.outer(np.arange(n2), np.arange(n2)) / n2
    c, s = np.cos(a), np.sin(a)
    f2 = np.block([[c, s], [-s, c]])
    f2i = np.block([[c, -s], [s, c]])

    t1 = np.arange(h1, dtype=np.float64)
    th = 2.0 * np.pi * (t1[None, :, None] * k1[None, None, :] / n1 + s2[:, None, None] * k1[None, None, :] / n)
    wgt = np.where((k1 == 0) | (k1 == h1), 1.0, 2.0) / n
    g = np.zeros((n2, h1, 2 * K1_PAD))
    g[:, :, :h1 + 1] = wgt * np.cos(th)
    g[:, :, K1_PAD:K1_PAD + h1 + 1] = -wgt * np.sin(th)
    return tuple(jnp.asarray(m, dtype=F32).astype(BF16) for m in (f1, f1b, f2, f2i, g))


def _hid_kernel(f_ref, w1t_ref, w1c_ref, w1s_ref, b1_ref, w2_ref, b2_ref, fr_ref, o_ref, *, length):
    bm = o_ref.shape[0]
    row = (pl.program_id(0) * bm + lax.broadcasted_iota(jnp.int32, (bm, 1), 0)).astype(F32)
    t = row / (length - 1.0)
    ang = f_ref[...] * (2.0 * math.pi * row / length)
    pre = (t * w1t_ref[...]
           + jnp.dot(jnp.cos(ang), w1c_ref[...], precision=HP, preferred_element_type=F32)
           - jnp.dot(jnp.sin(ang), w1s_ref[...], precision=HP, preferred_element_type=F32))
    fr = fr_ref[...]
    hid = jnp.sin(fr * (pre + b1_ref[...]))
    hid = jnp.sin(fr * (jnp.dot(hid, w2_ref[...], precision=HP, preferred_element_type=F32) + b2_ref[...]))
    o_ref[...] = hid


def _filter_hidden(length, w1, b1, w2, b2, freq, bm=512):
    bands = (FILTER_EMB - 1) // 2
    f = jnp.linspace(1e-4, bands - 1, bands, dtype=F32)[None, :]
    hdim = w1.shape[1]
    full = lambda a: pl.BlockSpec(a.shape, lambda i: (0,) * a.ndim)
    args = (f, w1[:1], w1[1:1 + bands], w1[1 + bands:], b1[None], w2, b2[None], freq[None])
    return pl.pallas_call(
        functools.partial(_hid_kernel, length=length),
        grid=(length // bm,),
        in_specs=[full(a) for a in args],
        out_specs=pl.BlockSpec((bm, hdim), lambda i: (i, 0)),
        out_shape=jax.ShapeDtypeStruct((length, hdim), F32),
        name="filter_hidden",
    )(*args)


def _ld(ref, start, size, stride=None):
    idx = pl.ds(start, size) if stride is None else pl.ds(start, size, stride=stride)
    return jnp.concatenate([ref[h, idx, :] for h in range(ref.shape[0])], axis=-1)


def _st(ref, start, size, val, stride=None):
    idx = pl.ds(start, size) if stride is None else pl.ds(start, size, stride=stride)
    for h in range(ref.shape[0]):
        ref[h, idx, :] = val[:, h * LANES:(h + 1) * LANES]


def _stage1_store(ab_ref, s2, a):
    _st(ab_ref, s2, K1_PAD, a[:K1_PAD], stride=AB_PITCH)
    _st(ab_ref, FFT_N2 + s2, K1_PAD, a[K1_PAD:], stride=AB_PITCH)


def _split_bf16(a):
    hi = a.astype(BF16)
    return hi, (a - hi.astype(F32)).astype(BF16)


def _dot_split(a_hi, a_lo, b_hi, b_lo):
    d = functools.partial(jnp.dot, preferred_element_type=F32)
    return d(a_hi, b_hi) + (d(a_hi, b_lo) + d(a_lo, b_hi))


def _spectrum_kernel(hid_ref, w3f_ref, w3b_ref, dl_ref, skip_ref, f1_ref, f1b_ref, f2_ref, o_ref,
                     hf_ref, hb_ref, ab_ref, *, length):
    blocks = 4
    rows = blocks * FFT_N2
    w3f, w3b = _split_bf16(w3f_ref[...]), _split_bf16(w3b_ref[...])

    def gen(j, c):
        row = (j * rows + lax.broadcasted_iota(jnp.int32, (rows, 1), 0)).astype(F32)
        win = jnp.exp(-(row / (length - 1.0)) * dl_ref[...])
        hid = hid_ref[pl.ds(pl.multiple_of(j * rows, rows), rows), :]
        hid_hi, hid_lo = _split_bf16(hid)
        hf = _dot_split(hid_hi, hid_lo, *w3f) * win
        hb = _dot_split(hid_hi, hid_lo, *w3b) * win
        for q in range(blocks):
            dst = pl.multiple_of((j * blocks + q) * Z_PITCH, 8)
            _st(hf_ref, dst, FFT_N2, hf[q * FFT_N2:(q + 1) * FFT_N2])
            _st(hb_ref, dst, FFT_N2, hb[q * FFT_N2:(q + 1) * FFT_N2])
        return c

    lax.fori_loop(0, FFT_H1 // blocks, gen, 0)

    def stage1(s2, c):
        xf = _ld(hf_ref, s2, FFT_H1, Z_PITCH).astype(BF16)
        xb = _ld(hb_ref, (FFT_N2 - s2) % FFT_N2, FFT_H1, Z_PITCH).astype(BF16)
        a = (jnp.dot(f1_ref[s2], xf, preferred_element_type=F32)
             + jnp.dot(f1b_ref[s2], xb, preferred_element_type=F32))
        _stage1_store(ab_ref, s2, a)
        return c

    lax.fori_loop(0, FFT_N2, stage1, 0, unroll=FFT_UNROLL)

    def stage2(k1, c):
        a = _ld(ab_ref, pl.multiple_of(k1 * AB_PITCH, 8), 2 * FFT_N2).astype(BF16)
        x = jnp.dot(f2_ref[...], a, preferred_element_type=F32)
        o_ref[0, k1, :FFT_N2, :] = (x[:FFT_N2] + skip_ref[0]).astype(BF16)
        o_ref[0, k1, FFT_N2:, :] = x[FFT_N2:].astype(BF16)
        return c

    lax.fori_loop(0, FFT_H1 + 1, stage2, 0, unroll=FFT_UNROLL)


def _filter_spectrum(hid, w3, deltas, filt_bias, consts, length):
    f1, f1b, f2, _, _ = consts
    cb = HY_CB
    ncb = HYENA_WIDTH // cb
    hdim = hid.shape[1]
    full = lambda a: pl.BlockSpec(a.shape, lambda o, c: (0,) * a.ndim)
    return pl.pallas_call(
        functools.partial(_spectrum_kernel, length=length),
        grid=(HYENA_ORDER, ncb),
        in_specs=[
            full(hid),
            pl.BlockSpec((hdim, cb), lambda o, c: (0, (o * N_DIRS) * ncb + c)),
            pl.BlockSpec((hdim, cb), lambda o, c: (0, (o * N_DIRS + 1) * ncb + c)),
            pl.BlockSpec((1, cb), lambda o, c: (0, c)),
            pl.BlockSpec((1, 1, cb), lambda o, c: (o, 0, c)),
            full(f1), full(f1b), full(f2),
        ],
        out_specs=pl.BlockSpec((1, FFT_H1 + 1, 2 * FFT_N2, cb), lambda o, c: (o, 0, 0, c)),
        out_shape=jax.ShapeDtypeStruct((HYENA_ORDER, FFT_H1 + 1, 2 * FFT_N2, HYENA_WIDTH), BF16),
        scratch_shapes=[
            pltpu.VMEM((cb // LANES, FFT_H1 * Z_PITCH, LANES), F32),
            pltpu.VMEM((cb // LANES, FFT_H1 * Z_PITCH, LANES), F32),
            pltpu.VMEM((cb // LANES, K1_PAD * AB_PITCH, LANES), F32),
        ],
        compiler_params=pltpu.CompilerParams(
            dimension_semantics=("arbitrary", "arbitrary"),
            vmem_limit_bytes=VMEM_LIMIT),
        name="filter_spectrum",
    )(hid, w3, w3, deltas, filt_bias[:, None, :], f1, f1b, f2)


def _short_conv_block(s_ref, j, w_ref, b_ref):
    big = s_ref[pl.ds(pl.multiple_of(j * FFT_N2, FFT_N2), FFT_N2 + 16), :]
    rows = FFT_N2 + 16
    prev = pltpu.roll(big, 1, 0)[8:8 + FFT_N2]
    nxt = pltpu.roll(big, rows - 1, 0)[8:8 + FFT_N2]
    cur = big[8:8 + FFT_N2]
    return b_ref[...] + prev * w_ref[0:1, :] + cur * w_ref[1:2, :] + nxt * w_ref[2:3, :]


def _load_stream(s_ref, src_ref, length):
    zeros = jnp.zeros((8, s_ref.shape[1]), F32)
    s_ref[0:8, :] = zeros
    s_ref[length + 8:length + 16, :] = zeros
    chunk = 512

    def cp(i, c):
        r = pl.multiple_of(i * chunk, chunk)
        s_ref[pl.ds(r + 8, chunk), :] = src_ref[pl.ds(r, chunk), :].astype(F32)
        return c

    lax.fori_loop(0, length // chunk, cp, 0)


def _hyena_kernel(z_ref, gs_ref, cwz_ref, cbz_ref, cwg_ref, cbg_ref, h_ref, f1_ref, f2_ref, f2i_ref, g_ref,
                  og_ref, o_ref, s_ref, zy_ref, ab_ref, *, length, conv_z, norm):
    if conv_z:
        _load_stream(s_ref, z_ref, length)

    def prep(j, c):
        if conv_z:
            zblk = _short_conv_block(s_ref, j, cwz_ref, cbz_ref)
        else:
            zblk = z_ref[pl.ds(pl.multiple_of(j * FFT_N2, FFT_N2), FFT_N2), :].astype(F32)
        _st(zy_ref, pl.multiple_of(j * Z_PITCH, 8), FFT_N2, zblk)
        return c

    lax.fori_loop(0, FFT_H1, prep, 0)

    def stage1(s2, c):
        xs = _ld(zy_ref, s2, FFT_H1, Z_PITCH).astype(BF16)
        _stage1_store(ab_ref, s2, jnp.dot(f1_ref[s2], xs, preferred_element_type=F32))
        return c

    lax.fori_loop(0, FFT_N2, stage1, 0, unroll=FFT_UNROLL)

    def stage2(k1, c):
        blk = pl.multiple_of(k1 * AB_PITCH, 8)
        x = jnp.dot(f2_ref[...], _ld(ab_ref, blk, 2 * FFT_N2).astype(BF16), preferred_element_type=F32)
        hk = h_ref[0, k1].astype(F32)
        xr, xi, hr, hi = x[:FFT_N2], x[FFT_N2:], hk[:FFT_N2], hk[FFT_N2:]
        p = jnp.concatenate([xr * hr - xi * hi, xr * hi + xi * hr], axis=0).astype(BF16)
        _st(ab_ref, blk, 2 * FFT_N2, jnp.dot(f2i_ref[...], p, preferred_element_type=F32))
        return c

    lax.fori_loop(0, FFT_H1 + 1, stage2, 0, unroll=FFT_UNROLL)

    def stage3(t2, c):
        br = _ld(ab_ref, t2, K1_PAD, AB_PITCH)
        bi = _ld(ab_ref, FFT_N2 + t2, K1_PAD, AB_PITCH)
        rhs = jnp.concatenate([br, bi], axis=0).astype(BF16)
        _st(zy_ref, t2, FFT_H1, jnp.dot(g_ref[t2], rhs, preferred_element_type=F32), stride=Z_PITCH)
        return c

    lax.fori_loop(0, FFT_N2, stage3, 0, unroll=FFT_UNROLL)

    _load_stream(s_ref, gs_ref, length)

    def fin(j, c):
        gate = _short_conv_block(s_ref, j, cwg_ref, cbg_ref)
        y = gate * _ld(zy_ref, pl.multiple_of(j * Z_PITCH, 8), FFT_N2)
        dst = pl.ds(pl.multiple_of(j * FFT_N2, FFT_N2), FFT_N2)
        if norm:
            gw = HYENA_WIDTH // N_HYENA_GROUPS
            for q in range(y.shape[1] // gw):
                sl = slice(q * gw, (q + 1) * gw)
                o_ref[dst, sl] = (_rms(y[:, sl]) * og_ref[:, sl]).astype(o_ref.dtype)
        else:
            o_ref[dst, :] = y.astype(o_ref.dtype)
        return c

    lax.fori_loop(0, FFT_H1, fin, 0)


def _hyena_order(zsrc, zcol, proj, order, conv_w, conv_b, hspec, consts, out_g, batch, length, conv_z, norm):
    f1, _, f2, f2i, g = consts
    cb = HY_CB
    ncb = HYENA_WIDTH // cb
    pcol = 3 * ATTN_WIDTH // cb
    zconv = HYENA_ORDER * ncb
    full = lambda a: pl.BlockSpec(a.shape, lambda c, b: (0,) * a.ndim)
    kern = functools.partial(_hyena_kernel, length=length, conv_z=conv_z, norm=norm)
    return pl.pallas_call(
        kern,
        grid=(ncb, batch),
        in_specs=[
            pl.BlockSpec((length, cb), lambda c, b: (b, zcol + c)),
            pl.BlockSpec((length, cb), lambda c, b: (b, pcol + order * ncb + c)),
            pl.BlockSpec((SHORT_CONV_W, cb), lambda c, b: (0, zconv + c)),
            pl.BlockSpec((1, cb), lambda c, b: (0, zconv + c)),
            pl.BlockSpec((SHORT_CONV_W, cb), lambda c, b: (0, order * ncb + c)),
            pl.BlockSpec((1, cb), lambda c, b: (0, order * ncb + c)),
            pl.BlockSpec((1, FFT_H1 + 1, 2 * FFT_N2, cb), lambda c, b: (order, 0, 0, c)),
            full(f1), full(f2), full(f2i), full(g),
            pl.BlockSpec((1, cb), lambda c, b: (0, c)),
        ],
        out_specs=pl.BlockSpec((length, cb), lambda c, b: (b, c)),
        out_shape=jax.ShapeDtypeStruct((batch * length, HYENA_WIDTH), BF16),
        scratch_shapes=[
            pltpu.VMEM((length + 16, cb), F32),
            pltpu.VMEM((cb // LANES, FFT_H1 * Z_PITCH, LANES), F32),
            pltpu.VMEM((cb // LANES, K1_PAD * AB_PITCH, LANES), F32),
        ],
        compiler_params=pltpu.CompilerParams(
            dimension_semantics=("arbitrary", "arbitrary"),
            vmem_limit_bytes=VMEM_LIMIT),
        name=f"hyena_order{order}",
    )(zsrc, proj, conv_w, conv_b[None], conv_w, conv_b[None], hspec, f1, f2, f2i, g, out_g)


def _hyena(proj, conv_w, conv_b, w1, b1, w2, b2, w3, freq, filt_bias, out_g, batch, length):
    assert 2 * length == FFT_N1 * FFT_N2 and HYENA_ORDER == 2
    consts = _dft_constants()
    max_decay = math.log(DECAY_TARGET) / FAST_DECAY_PCT
    min_decay = math.log(DECAY_TARGET) / SLOW_DECAY_PCT
    deltas = jnp.abs(jnp.linspace(min_decay, max_decay, HYENA_WIDTH, dtype=F32))[None]
    hid = _filter_hidden(length, w1, b1, w2, b2, freq)
    hspec = _filter_spectrum(hid, w3, deltas, filt_bias, consts, length)
    vcol = (3 * ATTN_WIDTH + HYENA_ORDER * HYENA_WIDTH) // HY_CB
    z1 = _hyena_order(proj, vcol, proj, 0, conv_w, conv_b, hspec, consts, out_g, batch, length, True, False)
    return _hyena_order(z1, 0, proj, 1, conv_w, conv_b, hspec, consts, out_g, batch, length, False, True)


def kernel(x, mix_norm_g, w_in, q_norm_g, k_norm_g, rpb, conv_w, conv_b, filt_w1, filt_b1, filt_w2, filt_b2, filt_w3, filt_freq, filt_bias, attn_out_g, hyena_out_g, w_out, ffn_norm_g, w_router, w_gate, w_up, w_down):
    b, s, d = x.shape
    rows = s // GRID_W
    win_r = min(WIN_ROWS_MAX, rows)
    cap = EC_CAPACITY_FACTOR * s // N_EXPERTS
    x2 = x.reshape(b * s, d)
    for i in range(mix_norm_g.shape[0]):
        proj = _inproj(x2, mix_norm_g[i][None], w_in[i].astype(BF16),
                       q_norm_g[i][None], k_norm_g[i][None])
        attn = _attention(proj, rpb[i], attn_out_g[i][None], b, s)

        hyn = _hyena(proj, conv_w[i], conv_b[i], filt_w1[i], filt_b1[i], filt_w2[i], filt_b2[i], filt_w3[i],
                     filt_freq[i], filt_bias[i], hyena_out_g[i][None], b, s)

        wr_pad = jnp.zeros((d, 128), BF16).at[:, :N_EXPERTS].set(w_router[i].astype(BF16))
        x1, hn, logits = _outproj(attn, hyn, x2, w_out[i].astype(BF16), ffn_norm_g[i][None], wr_pad)

        idx, gate = _route(logits, b, s, cap)
        x2 = _moe(idx, gate, hn, x1, w_gate[i], w_up[i], w_down[i], b, s)
    return x2.reshape(b, s, d)
```

```python
import functools
import math

import numpy as np
import jax
import jax.numpy as jnp
from jax import lax
from jax.experimental import pallas as pl
from jax.experimental.pallas import tpu as pltpu

D_MODEL = 2048
GRID_W = 64
N_ATTN_HEADS = 8
ATTN_WIDTH = D_MODEL // 2
HEAD_DIM = ATTN_WIDTH // N_ATTN_HEADS
HYENA_WIDTH = D_MODEL - ATTN_WIDTH
N_HYENA_GROUPS = 8
HYENA_ORDER = 2
N_DIRS = 2
SHORT_CONV_W = 3
FILTER_EMB = 33
DECAY_TARGET = 1e-2
FAST_DECAY_PCT = 0.3
SLOW_DECAY_PCT = 1.5
WIN_ROWS_MAX = 8
WIN_COLS = 16
N_EXPERTS = 16
EC_CAPACITY_FACTOR = 2
EXPERT_FF = 1024
IN_WIDTH = 3 * ATTN_WIDTH + (HYENA_ORDER + 1) * HYENA_WIDTH
RMS_EPS = 1e-6

F32 = jnp.float32
BF16 = jnp.bfloat16
NEG_BIG = -1e30
VMEM_LIMIT = 56 * 1024 * 1024
LANES = 128


def _rms(x, eps=RMS_EPS):
    return x * lax.rsqrt(jnp.mean(x * x, axis=-1, keepdims=True) + eps)


def _inproj_kernel(x_ref, g_ref, w_ref, qg_ref, kg_ref, o_ref, h_ref):
    n = pl.program_id(1)

    @pl.when(n == 0)
    def _():
        h_ref[...] = (_rms(x_ref[...]) * g_ref[...]).astype(BF16)

    acc = jnp.dot(h_ref[...], w_ref[...], preferred_element_type=F32)
    bn = acc.shape[1]

    def head_norm(gain):
        for j in range(bn // HEAD_DIM):
            sl = slice(j * HEAD_DIM, (j + 1) * HEAD_DIM)
            o_ref[:, sl] = (_rms(acc[:, sl]) * gain).astype(BF16)

    @pl.when(n == 0)
    def _():
        head_norm(qg_ref[...] * (HEAD_DIM ** -0.5))

    @pl.when(n == 1)
    def _():
        head_norm(kg_ref[...])

    @pl.when(n >= 2)
    def _():
        o_ref[...] = acc.astype(BF16)


def _inproj(x2, g, w_bf, qg, kg, bm=1024, bn=1024):
    m, d = x2.shape
    nw = w_bf.shape[1]
    return pl.pallas_call(
        _inproj_kernel,
        grid=(m // bm, nw // bn),
        in_specs=[
            pl.BlockSpec((bm, d), lambda i, j: (i, 0)),
            pl.BlockSpec((1, d), lambda i, j: (0, 0)),
            pl.BlockSpec((d, bn), lambda i, j: (0, j)),
            pl.BlockSpec((1, HEAD_DIM), lambda i, j: (0, 0)),
            pl.BlockSpec((1, HEAD_DIM), lambda i, j: (0, 0)),
        ],
        out_specs=pl.BlockSpec((bm, bn), lambda i, j: (i, j)),
        out_shape=jax.ShapeDtypeStruct((m, nw), BF16),
        scratch_shapes=[pltpu.VMEM((bm, d), BF16)],
        compiler_params=pltpu.CompilerParams(
            dimension_semantics=("arbitrary", "arbitrary"),
            vmem_limit_bytes=VMEM_LIMIT),
        name="inproj",
    )(x2, g, w_bf, qg, kg)


ATTN_ROWS = 4
ATTN_BAND = 12


def _attn_kernel(q_ref, k_ref, v_ref, rpb_ref, mask_ref, g_ref, o_ref, pt_ref, *, rows, win_r):
    lanes = pt_ref.shape[-1]
    n_dr = 2 * WIN_ROWS_MAX - 1

    def toeplitz(d):
        row = jnp.broadcast_to(rpb_ref[0, d:d + 1, :], (GRID_W, lanes))
        return pltpu.roll(row, lanes - (WIN_COLS - 1), 1, stride=1, stride_axis=0)

    lane = lax.broadcasted_iota(jnp.int32, (GRID_W, lanes), 1)
    left = lane < GRID_W
    tiles = [toeplitz(d) for d in range(n_dr)]
    for d in range(-1, n_dr):
        lo, hi = tiles[max(d, 0)], tiles[min(d + 1, n_dr - 1)]
        pt_ref[d + 1] = jnp.where(left, lo, pltpu.roll(hi, GRID_W, 1)) + mask_ref[...]

    def body(i, carry):
        r0 = i * ATTN_ROWS
        bs = jnp.minimum(jnp.clip(r0 - win_r // 2, 0, rows - win_r), rows - ATTN_BAND)
        q0 = pl.multiple_of(r0 * GRID_W, ATTN_ROWS * GRID_W)
        k0 = pl.multiple_of(bs * GRID_W, 2 * GRID_W)
        q = q_ref[pl.ds(q0, ATTN_ROWS * GRID_W), :]
        k = k_ref[pl.ds(k0, ATTN_BAND * GRID_W), :]
        v = v_ref[pl.ds(k0, ATTN_BAND * GRID_W), :]
        s = lax.dot_general(q, k, (((1,), (1,)), ((), ())), preferred_element_type=F32)

        bias_rows = []
        for j in range(ATTN_ROWS):
            r = r0 + j
            start = jnp.clip(r - win_r // 2, 0, rows - win_r)
            pieces = []
            for m in range(ATTN_BAND // 2):
                kr = bs + 2 * m
                pen_l = jnp.where((kr >= start) & (kr < start + win_r), 0.0, NEG_BIG)
                pen_r = jnp.where((kr + 1 >= start) & (kr + 1 < start + win_r), 0.0, NEG_BIG)
                d = jnp.clip(kr - r + WIN_ROWS_MAX - 1, -1, n_dr - 1)
                pieces.append(pt_ref[d + 1] + jnp.where(left, pen_l, pen_r))
            bias_rows.append(jnp.concatenate(pieces, axis=1))
        s = s + jnp.concatenate(bias_rows, axis=0)

        mx = jnp.max(s, axis=-1, keepdims=True)
        p = jnp.exp(s - mx)
        l = jnp.sum(p, axis=-1, keepdims=True)
        o = jnp.dot(p.astype(BF16), v, preferred_element_type=F32) / l
        o = _rms(o) * g_ref[...]
        o_ref[pl.ds(q0, ATTN_ROWS * GRID_W), :] = o.astype(BF16)
        return carry

    lax.fori_loop(0, rows // ATTN_ROWS, body, 0, unroll=2)


def _attn_window_mask():
    c = np.arange(GRID_W)
    c0 = np.clip(c - WIN_COLS // 2, 0, GRID_W - WIN_COLS)
    in_win = (c[None, :] >= c0[:, None]) & (c[None, :] < c0[:, None] + WIN_COLS)
    return jnp.asarray(np.tile(np.where(in_win, 0.0, NEG_BIG), (1, 2)), dtype=F32)


def _attention(proj, rpb, out_g, batch, seq):
    rows = seq // GRID_W
    win_r = min(WIN_ROWS_MAX, rows)
    assert win_r == WIN_ROWS_MAX and 2 * GRID_W == LANES
    assert rows % ATTN_ROWS == 0 and ATTN_BAND >= win_r + ATTN_ROWS - 1 and ATTN_BAND % 2 == 0
    h = N_ATTN_HEADS
    n_dr, n_dc = rpb.shape[1:]
    rpb_pad = jnp.zeros((h, 2 * WIN_ROWS_MAX, LANES), F32).at[:, :n_dr, :n_dc].set(rpb)
    mask = _attn_window_mask()
    kern = functools.partial(_attn_kernel, rows=rows, win_r=win_r)
    return pl.pallas_call(
        kern,
        grid=(batch, h),
        in_specs=[
            pl.BlockSpec((seq, HEAD_DIM), lambda b, i: (b, i)),
            pl.BlockSpec((seq, HEAD_DIM), lambda b, i: (b, h + i)),
            pl.BlockSpec((seq, HEAD_DIM), lambda b, i: (b, 2 * h + i)),
            pl.BlockSpec((1, 2 * WIN_ROWS_MAX, LANES), lambda b, i: (i, 0, 0)),
            pl.BlockSpec(mask.shape, lambda b, i: (0, 0)),
            pl.BlockSpec((1, HEAD_DIM), lambda b, i: (0, i)),
        ],
        out_specs=pl.BlockSpec((seq, HEAD_DIM), lambda b, i: (b, i)),
        out_shape=jax.ShapeDtypeStruct((batch * seq, ATTN_WIDTH), BF16),
        scratch_shapes=[pltpu.VMEM((2 * WIN_ROWS_MAX, GRID_W, LANES), F32)],
        compiler_params=pltpu.CompilerParams(
            dimension_semantics=("arbitrary", "arbitrary"),
            vmem_limit_bytes=VMEM_LIMIT),
        name="na2d_attn",
    )(proj, proj, proj, rpb_pad, mask, out_g)


def _outproj_kernel(a_ref, hy_ref, x_ref, wo_ref, g_ref, wr_ref, x1_ref, hn_ref, lg_ref):
    half = a_ref.shape[1]
    acc = x_ref[...]
    acc = acc + jnp.dot(a_ref[...], wo_ref[:half, :], preferred_element_type=F32)
    acc = acc + jnp.dot(hy_ref[...], wo_ref[half:, :], preferred_element_type=F32)
    x1_ref[...] = acc
    hn = _rms(acc) * g_ref[...]
    hn_ref[...] = hn
    lg_ref[...] = jnp.dot(hn.astype(BF16), wr_ref[...], preferred_element_type=F32)


def _outproj(attn, hy, x2, wo_bf, g, wr_pad, bm=512):
    m, d = x2.shape
    half = attn.shape[1]
    npad = wr_pad.shape[1]
    return pl.pallas_call(
        _outproj_kernel,
        grid=(m // bm,),
        in_specs=[
            pl.BlockSpec((bm, half), lambda i: (i, 0)),
            pl.BlockSpec((bm, half), lambda i: (i, 0)),
            pl.BlockSpec((bm, d), lambda i: (i, 0)),
            pl.BlockSpec((d, d), lambda i: (0, 0)),
            pl.BlockSpec((1, d), lambda i: (0, 0)),
            pl.BlockSpec((d, npad), lambda i: (0, 0)),
        ],
        out_specs=[
            pl.BlockSpec((bm, d), lambda i: (i, 0)),
            pl.BlockSpec((bm, d), lambda i: (i, 0)),
            pl.BlockSpec((bm, npad), lambda i: (i, 0)),
        ],
        out_shape=[
            jax.ShapeDtypeStruct((m, d), F32),
            jax.ShapeDtypeStruct((m, d), F32),
            jax.ShapeDtypeStruct((m, npad), F32),
        ],
        compiler_params=pltpu.CompilerParams(
            dimension_semantics=("arbitrary",),
            vmem_limit_bytes=VMEM_LIMIT),
        name="outproj",
    )(attn, hy, x2, wo_bf, g, wr_pad)


ROUTE_BLK = 128
ROUTE_CHUNK = 512
ROUTE_MATCH = 256
F32_INF_BITS = 0x7F800000


def _route_kernel(lg_ref, tri_ref, idx_ref, gate_ref, aff_ref, pos_ref, *, cap):
    s, lanes = lg_ref.shape
    lane = lax.broadcasted_iota(jnp.int32, (1, lanes), 1)

    def softmax_chunk(i, c):
        rows = pl.ds(pl.multiple_of(i * ROUTE_CHUNK, ROUTE_CHUNK), ROUTE_CHUNK)
        lg = jnp.where(lane < N_EXPERTS, lg_ref[rows, :], NEG_BIG)
        ex = jnp.exp(lg - jnp.max(lg, axis=-1, keepdims=True))
        aff_ref[rows, :] = ex / jnp.sum(ex, axis=-1, keepdims=True)
        return c

    lax.fori_loop(0, s // ROUTE_CHUNK, softmax_chunk, 0)

    def as_value(bits):
        return pltpu.bitcast(bits, F32)

    def count_ge(th):
        def body(i, acc):
            rows = pl.ds(pl.multiple_of(i * ROUTE_CHUNK, ROUTE_CHUNK), ROUTE_CHUNK)
            return acc + jnp.sum(jnp.where(aff_ref[rows, :] >= th, 1.0, 0.0), axis=0, keepdims=True)
        return lax.fori_loop(0, s // ROUTE_CHUNK, body, jnp.zeros((1, lanes), F32))

    def bisect(_, c):
        lo, hi = c
        mid = lo + ((hi - lo) >> 1)
        ok = count_ge(as_value(mid)) >= cap
        return jnp.where(ok, mid, lo), jnp.where(ok, hi, mid)

    tau_bits, _ = lax.fori_loop(0, 31, bisect, (jnp.zeros((1, lanes), jnp.int32),
                                                jnp.full((1, lanes), F32_INF_BITS, jnp.int32)))
    tau, above = as_value(tau_bits), as_value(tau_bits + 1)
    ties_wanted = cap - count_ge(above)

    def scan_block(j, c):
        tie_carry, sel_carry = c
        rows = pl.ds(pl.multiple_of(j * ROUTE_BLK, ROUTE_BLK), ROUTE_BLK)
        aff = aff_ref[rows, :]
        gt = aff >= above
        eq = jnp.where((aff >= tau) & (aff < above), 1.0, 0.0)
        tie_incl = jnp.dot(tri_ref[...], eq.astype(BF16), preferred_element_type=F32) + tie_carry
        sel = jnp.where(gt | ((eq > 0.0) & (tie_incl - eq < ties_wanted)), 1.0, 0.0)
        sel_incl = jnp.dot(tri_ref[...], sel.astype(BF16), preferred_element_type=F32) + sel_carry
        pos_ref[rows, :] = jnp.where(sel > 0.0, sel_incl - sel, -1.0)
        return tie_incl[ROUTE_BLK - 1:, :], sel_incl[ROUTE_BLK - 1:, :]

    zero = jnp.zeros((1, lanes), F32)
    lax.fori_loop(0, s // ROUTE_BLK, scan_block, (zero, zero))

    slot = lax.broadcasted_iota(jnp.int32, (1, cap), 1).astype(F32)
    for e in range(N_EXPERTS):
        def compact(c, acc, e=e):
            ia, ga = acc
            r0 = pl.multiple_of(c * ROUTE_MATCH, ROUTE_MATCH)
            match = pos_ref[pl.ds(r0, ROUTE_MATCH), e:e + 1] == slot
            tok = (r0 + lax.broadcasted_iota(jnp.int32, (ROUTE_MATCH, 1), 0)).astype(F32)
            ia = ia + jnp.sum(jnp.where(match, tok, 0.0), axis=0, keepdims=True)
            ga = ga + jnp.sum(jnp.where(match, aff_ref[pl.ds(r0, ROUTE_MATCH), e:e + 1], 0.0),
                              axis=0, keepdims=True)
            return ia, ga

        zc = jnp.zeros((1, cap), F32)
        ia, ga = lax.fori_loop(0, s // ROUTE_MATCH, compact, (zc, zc))
        idx_ref[0, e:e + 1, :] = ia.astype(jnp.int32)
        gate_ref[0, e:e + 1, :] = ga


def _route(logits, batch, seq, cap):
    lanes = logits.shape[1]
    tri = jnp.asarray(np.tril(np.ones((ROUTE_BLK, ROUTE_BLK))), dtype=BF16)
    return pl.pallas_call(
        functools.partial(_route_kernel, cap=cap),
        grid=(batch,),
        in_specs=[
            pl.BlockSpec((seq, lanes), lambda b: (b, 0)),
            pl.BlockSpec((ROUTE_BLK, ROUTE_BLK), lambda b: (0, 0)),
        ],
        out_specs=[
            pl.BlockSpec((1, N_EXPERTS, cap), lambda b: (b, 0, 0)),
            pl.BlockSpec((1, N_EXPERTS, cap), lambda b: (b, 0, 0)),
        ],
        out_shape=[
            jax.ShapeDtypeStruct((batch, N_EXPERTS, cap), jnp.int32),
            jax.ShapeDtypeStruct((batch, N_EXPERTS, cap), F32),
        ],
        scratch_shapes=[
            pltpu.VMEM((seq, lanes), F32),
            pltpu.VMEM((seq, lanes), F32),
        ],
        compiler_params=pltpu.CompilerParams(dimension_semantics=("arbitrary",)),
        name="route",
    )(logits, tri)


MOE_FF_SPLIT = 4
ROW_UNROLL = 8


def _moe_kernel(idx_ref, hn_hbm, wg_ref, wu_ref, wd_ref, gate_ref, x1_hbm, out_hbm,
                xg0, xg1, acc0, acc1, y0, y1, sem_x, sem_a, sem_s, *, cap, seq):
    del x1_hbm
    e, f = pl.program_id(0), pl.program_id(1)
    n_e = pl.num_programs(0)
    xg, acc = (xg0, xg1), (acc0, acc1)
    prev_e, next_e = jnp.maximum(e - 1, 0), jnp.minimum(e + 1, n_e - 1)

    def row_of(ee, b):
        base = (b * n_e + ee) * cap
        return lambda p: b * seq + idx_ref[base + p]

    def copy_rows(ee, b, kind, rolled=False):
        row = row_of(ee, b)

        def one(p):
            if kind == "xg":
                src, dst, sem = hn_hbm.at[pl.ds(row(p), 1), :], xg[b].at[pl.ds(p, 1), :], sem_x.at[b]
            elif kind == "ag":
                src, dst, sem = out_hbm.at[pl.ds(row(p), 1), :], acc[b].at[pl.ds(p, 1), :], sem_a.at[b]
            else:
                src, dst, sem = acc[b].at[pl.ds(p, 1), :], out_hbm.at[pl.ds(row(p), 1), :], sem_s.at[b]
            pltpu.make_async_copy(src, dst, sem).start()

        if rolled:
            def body(i, c):
                for k in range(ROW_UNROLL):
                    one(i * ROW_UNROLL + k)
                return c
            lax.fori_loop(0, cap // ROW_UNROLL, body, 0)
        else:
            for p in range(cap):
                one(p)

    def wait_rows(buf, sem):
        pltpu.make_async_copy(hn_hbm.at[pl.ds(0, cap), :], buf, sem).wait()

    def partial_out(b, wg, wu, wd):
        x = xg[b][...].astype(BF16)
        a = jnp.dot(x, wg, preferred_element_type=F32)
        u = jnp.dot(x, wu, preferred_element_type=F32)
        hmid = (a * (1.0 / (1.0 + jnp.exp(-a))) * u).astype(BF16)
        return jnp.dot(hmid, wd, preferred_element_type=F32)

    def weights():
        return wg_ref[0].astype(BF16), wu_ref[0].astype(BF16), wd_ref[0].astype(BF16)

    @pl.when((e == 0) & (f == 0))
    def _():
        copy_rows(e, 0, "xg", rolled=True)
        copy_rows(e, 0, "ag", rolled=True)
        copy_rows(e, 1, "ag", rolled=True)
        wait_rows(acc0, sem_a.at[0])
        wait_rows(acc1, sem_a.at[1])

    @pl.when(f == 0)
    def _():
        w = weights()
        wait_rows(xg0, sem_x.at[0])
        copy_rows(e, 1, "xg")
        y0[...] = partial_out(0, *w)
        wait_rows(xg1, sem_x.at[1])
        copy_rows(prev_e, 0, "sc")
        y1[...] = partial_out(1, *w)

    @pl.when(f == 1)
    def _():
        w = weights()
        copy_rows(prev_e, 1, "sc")
        y0[...] += partial_out(0, *w)
        wait_rows(acc0, sem_s.at[0])
        copy_rows(e, 0, "ag")
        y1[...] += partial_out(1, *w)

    @pl.when(f == 2)
    def _():
        w = weights()
        wait_rows(acc1, sem_s.at[1])
        copy_rows(e, 1, "ag")
        y0[...] += partial_out(0, *w)
        y1[...] += partial_out(1, *w)

    @pl.when(f == 3)
    def _():
        w = weights()
        t0 = y0[...] + partial_out(0, *w)
        wait_rows(acc0, sem_a.at[0])
        acc0[...] += t0 * gate_ref[0, 0]
        copy_rows(next_e, 0, "xg")
        t1 = y1[...] + partial_out(1, *w)
        wait_rows(acc1, sem_a.at[1])
        acc1[...] += t1 * gate_ref[1, 0]

    @pl.when((e == n_e - 1) & (f == MOE_FF_SPLIT - 1))
    def _():
        copy_rows(e, 0, "sc", rolled=True)
        copy_rows(e, 1, "sc", rolled=True)
        wait_rows(acc0, sem_s.at[0])
        wait_rows(acc1, sem_s.at[1])
        wait_rows(xg0, sem_x.at[0])


def _moe(idx, gate, hn, x1, wg, wu, wd, batch, seq):
    n_e, d, ff = wg.shape
    cap = idx.shape[-1]
    assert batch == 2 and MOE_FF_SPLIT == 4 and ff % MOE_FF_SPLIT == 0
    fq = ff // MOE_FF_SPLIT
    grid_spec = pltpu.PrefetchScalarGridSpec(
        num_scalar_prefetch=1,
        grid=(n_e, MOE_FF_SPLIT),
        in_specs=[
            pl.BlockSpec(memory_space=pl.ANY),
            pl.BlockSpec((1, d, fq), lambda i, j, s: (i, 0, j)),
            pl.BlockSpec((1, d, fq), lambda i, j, s: (i, 0, j)),
            pl.BlockSpec((1, fq, d), lambda i, j, s: (i, j, 0)),
            pl.BlockSpec((batch, 1, cap, 1), lambda i, j, s: (0, i, 0, 0)),
            pl.BlockSpec(memory_space=pl.ANY),
        ],
        out_specs=pl.BlockSpec(memory_space=pl.ANY),
        scratch_shapes=(
            [pltpu.VMEM((cap, d), F32) for _ in range(6)]
            + [pltpu.SemaphoreType.DMA((batch,)) for _ in range(3)]),
    )
    return pl.pallas_call(
        functools.partial(_moe_kernel, cap=cap, seq=seq),
        grid_spec=grid_spec,
        out_shape=jax.ShapeDtypeStruct(x1.shape, x1.dtype),
        input_output_aliases={6: 0},
        compiler_params=pltpu.CompilerParams(
            dimension_semantics=("arbitrary", "arbitrary"),
            vmem_limit_bytes=VMEM_LIMIT),
        name="moe_experts",
    )(idx.reshape(-1), hn, wg, wu, wd, gate[..., None], x1)


FFT_N1 = 128
FFT_N2 = 64
FFT_H1 = FFT_N1 // 2
K1_PAD = 72
Z_PITCH = 72
AB_PITCH = 136
HY_CB = 256
FFT_UNROLL = 8
HP = lax.Precision.HIGHEST


def _dft_constants():
    n, n1, n2, h1 = FFT_N1 * FFT_N2, FFT_N1, FFT_N2, FFT_H1
    k1 = np.arange(h1 + 1, dtype=np.float64)
    s1 = np.arange(h1, dtype=np.float64)
    s2 = np.arange(n2, dtype=np.float64)
    tw = s2[:, None, None] * k1[None, :, None] / n

    def stage1(phase_s1):
        th = -2.0 * np.pi * (phase_s1 + tw)
        m = np.zeros((n2, 2 * K1_PAD, h1))
        m[:, :h1 + 1] = np.cos(th)
        m[:, K1_PAD:K1_PAD + h1 + 1] = np.sin(th)
        return m

    f1 = stage1(s1[None, None, :] * k1[None, :, None] / n1)
    f1b = stage1(((n1 - 1) - s1)[None, None, :] * k1[None, :, None] / n1)
    f1b0 = stage1((n1 - s1)[None, None, :] * k1[None, :, None] / n1)[0]
    f1b0[:, 0] = 0.0
    f1b[0] = f1b0

    a = 2.0 * np.pi * np.outer(np.arange(n2), np.arange(n2)) / n2
    c, s = np.cos(a), np.sin(a)
    f2 = np.block([[c, s], [-s, c]])
    f2i = np.block([[c, -s], [s, c]])

    t1 = np.arange(h1, dtype=np.float64)
    th = 2.0 * np.pi * (t1[None, :, None] * k1[None, None, :] / n1 + s2[:, None, None] * k1[None, None, :] / n)
    wgt = np.where((k1 == 0) | (k1 == h1), 1.0, 2.0) / n
    g = np.zeros((n2, h1, 2 * K1_PAD))
    g[:, :, :h1 + 1] = wgt * np.cos(th)
    g[:, :, K1_PAD:K1_PAD + h1 + 1] = -wgt * np.sin(th)
    return tuple(jnp.asarray(m, dtype=F32).astype(BF16) for m in (f1, f1b, f2, f2i, g))


def _hid_kernel(f_ref, w1t_ref, w1c_ref, w1s_ref, b1_ref, w2_ref, b2_ref, fr_ref, o_ref, *, length):
    bm = o_ref.shape[0]
    row = (pl.program_id(0) * bm + lax.broadcasted_iota(jnp.int32, (bm, 1), 0)).astype(F32)
    t = row / (length - 1.0)
    ang = f_ref[...] * (2.0 * math.pi * row / length)
    pre = (t * w1t_ref[...]
           + jnp.dot(jnp.cos(ang), w1c_ref[...], precision=HP, preferred_element_type=F32)
           - jnp.dot(jnp.sin(ang), w1s_ref[...], precision=HP, preferred_element_type=F32))
    fr = fr_ref[...]
    hid = jnp.sin(fr * (pre + b1_ref[...]))
    hid = jnp.sin(fr * (jnp.dot(hid, w2_ref[...], precision=HP, preferred_element_type=F32) + b2_ref[...]))
    o_ref[...] = hid


def _filter_hidden(length, w1, b1, w2, b2, freq, bm=512):
    bands = (FILTER_EMB - 1) // 2
    f = jnp.linspace(1e-4, bands - 1, bands, dtype=F32)[None, :]
    hdim = w1.shape[1]
    full = lambda a: pl.BlockSpec(a.shape, lambda i: (0,) * a.ndim)
    args = (f, w1[:1], w1[1:1 + bands], w1[1 + bands:], b1[None], w2, b2[None], freq[None])
    return pl.pallas_call(
        functools.partial(_hid_kernel, length=length),
        grid=(length // bm,),
        in_specs=[full(a) for a in args],
        out_specs=pl.BlockSpec((bm, hdim), lambda i: (i, 0)),
        out_shape=jax.ShapeDtypeStruct((length, hdim), F32),
        name="filter_hidden",
    )(*args)


def _ld(ref, start, size, stride=None):
    idx = pl.ds(start, size) if stride is None else pl.ds(start, size, stride=stride)
    return jnp.concatenate([ref[h, idx, :] for h in range(ref.shape[0])], axis=-1)


def _st(ref, start, size, val, stride=None):
    idx = pl.ds(start, size) if stride is None else pl.ds(start, size, stride=stride)
    for h in range(ref.shape[0]):
        ref[h, idx, :] = val[:, h * LANES:(h + 1) * LANES]


def _stage1_store(ab_ref, s2, a):
    _st(ab_ref, s2, K1_PAD, a[:K1_PAD], stride=AB_PITCH)
    _st(ab_ref, FFT_N2 + s2, K1_PAD, a[K1_PAD:], stride=AB_PITCH)


def _split_bf16(a):
    hi = a.astype(BF16)
    return hi, (a - hi.astype(F32)).astype(BF16)


def _dot_split(a_hi, a_lo, b_hi, b_lo):
    d = functools.partial(jnp.dot, preferred_element_type=F32)
    return d(a_hi, b_hi) + (d(a_hi, b_lo) + d(a_lo, b_hi))


def _spectrum_kernel(hid_ref, w3f_ref, w3b_ref, dl_ref, skip_ref, f1_ref, f1b_ref, f2_ref, o_ref,
                     hf_ref, hb_ref, ab_ref, *, length):
    blocks = 4
    rows = blocks * FFT_N2
    w3f, w3b = _split_bf16(w3f_ref[...]), _split_bf16(w3b_ref[...])

    def gen(j, c):
        row = (j * rows + lax.broadcasted_iota(jnp.int32, (rows, 1), 0)).astype(F32)
        win = jnp.exp(-(row / (length - 1.0)) * dl_ref[...])
        hid = hid_ref[pl.ds(pl.multiple_of(j * rows, rows), rows), :]
        hid_hi, hid_lo = _split_bf16(hid)
        hf = _dot_split(hid_hi, hid_lo, *w3f) * win
        hb = _dot_split(hid_hi, hid_lo, *w3b) * win
        for q in range(blocks):
            dst = pl.multiple_of((j * blocks + q) * Z_PITCH, 8)
            _st(hf_ref, dst, FFT_N2, hf[q * FFT_N2:(q + 1) * FFT_N2])
            _st(hb_ref, dst, FFT_N2, hb[q * FFT_N2:(q + 1) * FFT_N2])
        return c

    lax.fori_loop(0, FFT_H1 // blocks, gen, 0)

    def stage1(s2, c):
        xf = _ld(hf_ref, s2, FFT_H1, Z_PITCH).astype(BF16)
        xb = _ld(hb_ref, (FFT_N2 - s2) % FFT_N2, FFT_H1, Z_PITCH).astype(BF16)
        a = (jnp.dot(f1_ref[s2], xf, preferred_element_type=F32)
             + jnp.dot(f1b_ref[s2], xb, preferred_element_type=F32))
        _stage1_store(ab_ref, s2, a)
        return c

    lax.fori_loop(0, FFT_N2, stage1, 0, unroll=FFT_UNROLL)

    def stage2(k1, c):
        a = _ld(ab_ref, pl.multiple_of(k1 * AB_PITCH, 8), 2 * FFT_N2).astype(BF16)
        x = jnp.dot(f2_ref[...], a, preferred_element_type=F32)
        o_ref[0, k1, :FFT_N2, :] = (x[:FFT_N2] + skip_ref[0]).astype(BF16)
        o_ref[0, k1, FFT_N2:, :] = x[FFT_N2:].astype(BF16)
        return c

    lax.fori_loop(0, FFT_H1 + 1, stage2, 0, unroll=FFT_UNROLL)


def _filter_spectrum(hid, w3, deltas, filt_bias, consts, length):
    f1, f1b, f2, _, _ = consts
    cb = HY_CB
    ncb = HYENA_WIDTH // cb
    hdim = hid.shape[1]
    full = lambda a: pl.BlockSpec(a.shape, lambda o, c: (0,) * a.ndim)
    return pl.pallas_call(
        functools.partial(_spectrum_kernel, length=length),
        grid=(HYENA_ORDER, ncb),
        in_specs=[
            full(hid),
            pl.BlockSpec((hdim, cb), lambda o, c: (0, (o * N_DIRS) * ncb + c)),
            pl.BlockSpec((hdim, cb), lambda o, c: (0, (o * N_DIRS + 1) * ncb + c)),
            pl.BlockSpec((1, cb), lambda o, c: (0, c)),
            pl.BlockSpec((1, 1, cb), lambda o, c: (o, 0, c)),
            full(f1), full(f1b), full(f2),
        ],
        out_specs=pl.BlockSpec((1, FFT_H1 + 1, 2 * FFT_N2, cb), lambda o, c: (o, 0, 0, c)),
        out_shape=jax.ShapeDtypeStruct((HYENA_ORDER, FFT_H1 + 1, 2 * FFT_N2, HYENA_WIDTH), BF16),
        scratch_shapes=[
            pltpu.VMEM((cb // LANES, FFT_H1 * Z_PITCH, LANES), F32),
            pltpu.VMEM((cb // LANES, FFT_H1 * Z_PITCH, LANES), F32),
            pltpu.VMEM((cb // LANES, K1_PAD * AB_PITCH, LANES), F32),
        ],
        compiler_params=pltpu.CompilerParams(
            dimension_semantics=("arbitrary", "arbitrary"),
            vmem_limit_bytes=VMEM_LIMIT),
        name="filter_spectrum",
    )(hid, w3, w3, deltas, filt_bias[:, None, :], f1, f1b, f2)


def _short_conv_block(s_ref, j, w_ref, b_ref):
    big = s_ref[pl.ds(pl.multiple_of(j * FFT_N2, FFT_N2), FFT_N2 + 16), :]
    rows = FFT_N2 + 16
    prev = pltpu.roll(big, 1, 0)[8:8 + FFT_N2]
    nxt = pltpu.roll(big, rows - 1, 0)[8:8 + FFT_N2]
    cur = big[8:8 + FFT_N2]
    return b_ref[...] + prev * w_ref[0:1, :] + cur * w_ref[1:2, :] + nxt * w_ref[2:3, :]


def _load_stream(s_ref, src_ref, length):
    zeros = jnp.zeros((8, s_ref.shape[1]), F32)
    s_ref[0:8, :] = zeros
    s_ref[length + 8:length + 16, :] = zeros
    chunk = 512

    def cp(i, c):
        r = pl.multiple_of(i * chunk, chunk)
        s_ref[pl.ds(r + 8, chunk), :] = src_ref[pl.ds(r, chunk), :].astype(F32)
        return c

    lax.fori_loop(0, length // chunk, cp, 0)


def _hyena_kernel(z_ref, gs_ref, cwz_ref, cbz_ref, cwg_ref, cbg_ref, h_ref, f1_ref, f2_ref, f2i_ref, g_ref,
                  og_ref, o_ref, s_ref, zy_ref, ab_ref, *, length, conv_z, norm):
    if conv_z:
        _load_stream(s_ref, z_ref, length)

    def prep(j, c):
        if conv_z:
            zblk = _short_conv_block(s_ref, j, cwz_ref, cbz_ref)
        else:
            zblk = z_ref[pl.ds(pl.multiple_of(j * FFT_N2, FFT_N2), FFT_N2), :].astype(F32)
        _st(zy_ref, pl.multiple_of(j * Z_PITCH, 8), FFT_N2, zblk)
        return c

    lax.fori_loop(0, FFT_H1, prep, 0)

    def stage1(s2, c):
        xs = _ld(zy_ref, s2, FFT_H1, Z_PITCH).astype(BF16)
        _stage1_store(ab_ref, s2, jnp.dot(f1_ref[s2], xs, preferred_element_type=F32))
        return c

    lax.fori_loop(0, FFT_N2, stage1, 0, unroll=FFT_UNROLL)

    def stage2(k1, c):
        blk = pl.multiple_of(k1 * AB_PITCH, 8)
        x = jnp.dot(f2_ref[...], _ld(ab_ref, blk, 2 * FFT_N2).astype(BF16), preferred_element_type=F32)
        hk = h_ref[0, k1].astype(F32)
        xr, xi, hr, hi = x[:FFT_N2], x[FFT_N2:], hk[:FFT_N2], hk[FFT_N2:]
        p = jnp.concatenate([xr * hr - xi * hi, xr * hi + xi * hr], axis=0).astype(BF16)
        _st(ab_ref, blk, 2 * FFT_N2, jnp.dot(f2i_ref[...], p, preferred_element_type=F32))
        return c

    lax.fori_loop(0, FFT_H1 + 1, stage2, 0, unroll=FFT_UNROLL)

    def stage3(t2, c):
        br = _ld(ab_ref, t2, K1_PAD, AB_PITCH)
        bi = _ld(ab_ref, FFT_N2 + t2, K1_PAD, AB_PITCH)
        rhs = jnp.concatenate([br, bi], axis=0).astype(BF16)
        _st(zy_ref, t2, FFT_H1, jnp.dot(g_ref[t2], rhs, preferred_element_type=F32), stride=Z_PITCH)
        return c

    lax.fori_loop(0, FFT_N2, stage3, 0, unroll=FFT_UNROLL)

    _load_stream(s_ref, gs_ref, length)

    def fin(j, c):
        gate = _short_conv_block(s_ref, j, cwg_ref, cbg_ref)
        y = gate * _ld(zy_ref, pl.multiple_of(j * Z_PITCH, 8), FFT_N2)
        dst = pl.ds(pl.multiple_of(j * FFT_N2, FFT_N2), FFT_N2)
        if norm:
            gw = HYENA_WIDTH // N_HYENA_GROUPS
            for q in range(y.shape[1] // gw):
                sl = slice(q * gw, (q + 1) * gw)
                o_ref[dst, sl] = (_rms(y[:, sl]) * og_ref[:, sl]).astype(o_ref.dtype)
        else:
            o_ref[dst, :] = y.astype(o_ref.dtype)
        return c

    lax.fori_loop(0, FFT_H1, fin, 0)


def _hyena_order(zsrc, zcol, proj, order, conv_w, conv_b, hspec, consts, out_g, batch, length, conv_z, norm):
    f1, _, f2, f2i, g = consts
    cb = HY_CB
    ncb = HYENA_WIDTH // cb
    pcol = 3 * ATTN_WIDTH // cb
    zconv = HYENA_ORDER * ncb
    full = lambda a: pl.BlockSpec(a.shape, lambda c, b: (0,) * a.ndim)
    kern = functools.partial(_hyena_kernel, length=length, conv_z=conv_z, norm=norm)
    return pl.pallas_call(
        kern,
        grid=(ncb, batch),
        in_specs=[
            pl.BlockSpec((length, cb), lambda c, b: (b, zcol + c)),
            pl.BlockSpec((length, cb), lambda c, b: (b, pcol + order * ncb + c)),
            pl.BlockSpec((SHORT_CONV_W, cb), lambda c, b: (0, zconv + c)),
            pl.BlockSpec((1, cb), lambda c, b: (0, zconv + c)),
            pl.BlockSpec((SHORT_CONV_W, cb), lambda c, b: (0, order * ncb + c)),
            pl.BlockSpec((1, cb), lambda c, b: (0, order * ncb + c)),
            pl.BlockSpec((1, FFT_H1 + 1, 2 * FFT_N2, cb), lambda c, b: (order, 0, 0, c)),
            full(f1), full(f2), full(f2i), full(g),
            pl.BlockSpec((1, cb), lambda c, b: (0, c)),
        ],
        out_specs=pl.BlockSpec((length, cb), lambda c, b: (b, c)),
        out_shape=jax.ShapeDtypeStruct((batch * length, HYENA_WIDTH), BF16),
        scratch_shapes=[
            pltpu.VMEM((length + 16, cb), F32),
            pltpu.VMEM((cb // LANES, FFT_H1 * Z_PITCH, LANES), F32),
            pltpu.VMEM((cb // LANES, K1_PAD * AB_PITCH, LANES), F32),
        ],
        compiler_params=pltpu.CompilerParams(
            dimension_semantics=("arbitrary", "arbitrary"),
            vmem_limit_bytes=VMEM_LIMIT),
        name=f"hyena_order{order}",
    )(zsrc, proj, conv_w, conv_b[None], conv_w, conv_b[None], hspec, f1, f2, f2i, g, out_g)


def _hyena(proj, conv_w, conv_b, w1, b1, w2, b2, w3, freq, filt_bias, out_g, batch, length):
    assert 2 * length == FFT_N1 * FFT_N2 and HYENA_ORDER == 2
    consts = _dft_constants()
    max_decay = math.log(DECAY_TARGET) / FAST_DECAY_PCT
    min_decay = math.log(DECAY_TARGET) / SLOW_DECAY_PCT
    deltas = jnp.abs(jnp.linspace(min_decay, max_decay, HYENA_WIDTH, dtype=F32))[None]
    hid = _filter_hidden(length, w1, b1, w2, b2, freq)
    hspec = _filter_spectrum(hid, w3, deltas, filt_bias, consts, length)
    vcol = (3 * ATTN_WIDTH + HYENA_ORDER * HYENA_WIDTH) // HY_CB
    z1 = _hyena_order(proj, vcol, proj, 0, conv_w, conv_b, hspec, consts, out_g, batch, length, True, False)
    return _hyena_order(z1, 0, proj, 1, conv_w, conv_b, hspec, consts, out_g, batch, length, False, True)


def kernel(x, mix_norm_g, w_in, q_norm_g, k_norm_g, rpb, conv_w, conv_b, filt_w1, filt_b1, filt_w2, filt_b2, filt_w3, filt_freq, filt_bias, attn_out_g, hyena_out_g, w_out, ffn_norm_g, w_router, w_gate, w_up, w_down):
    b, s, d = x.shape
    rows = s // GRID_W
    win_r = min(WIN_ROWS_MAX, rows)
    cap = EC_CAPACITY_FACTOR * s // N_EXPERTS
    x2 = x.reshape(b * s, d)
    for i in range(mix_norm_g.shape[0]):
        proj = _inproj(x2, mix_norm_g[i][None], w_in[i].astype(BF16),
                       q_norm_g[i][None], k_norm_g[i][None])
        attn = _attention(proj, rpb[i], attn_out_g[i][None], b, s)

        hyn = _hyena(proj, conv_w[i], conv_b[i], filt_w1[i], filt_b1[i], filt_w2[i], filt_b2[i], filt_w3[i],
                     filt_freq[i], filt_bias[i], hyena_out_g[i][None], b, s)

        wr_pad = jnp.zeros((d, 128), BF16).at[:, :N_EXPERTS].set(w_router[i].astype(BF16))
        x1, hn, logits = _outproj(attn, hyn, x2, w_out[i].astype(BF16), ffn_norm_g[i][None], wr_pad)

        idx, gate = _route(logits, b, s, cap)
        x2 = _moe(idx, gate, hn, x1, w_gate[i], w_up[i], w_down[i], b, s)
    return x2.reshape(b, s, d)
```

```python
import functools
import math

import numpy as np
import jax
import jax.numpy as jnp
from jax import lax
from jax.experimental import pallas as pl
from jax.experimental.pallas import tpu as pltpu

D_MODEL = 2048
GRID_W = 64
N_ATTN_HEADS = 8
ATTN_WIDTH = D_MODEL // 2
HEAD_DIM = ATTN_WIDTH // N_ATTN_HEADS
HYENA_WIDTH = D_MODEL - ATTN_WIDTH
N_HYENA_GROUPS = 8
HYENA_ORDER = 2
N_DIRS = 2
SHORT_CONV_W = 3
FILTER_EMB = 33
DECAY_TARGET = 1e-2
FAST_DECAY_PCT = 0.3
SLOW_DECAY_PCT = 1.5
WIN_ROWS_MAX = 8
WIN_COLS = 16
N_EXPERTS = 16
EC_CAPACITY_FACTOR = 2
EXPERT_FF = 1024
IN_WIDTH = 3 * ATTN_WIDTH + (HYENA_ORDER + 1) * HYENA_WIDTH
RMS_EPS = 1e-6

F32 = jnp.float32
BF16 = jnp.bfloat16
NEG_BIG = -1e30
VMEM_LIMIT = 56 * 1024 * 1024
LANES = 128


def _rms(x, eps=RMS_EPS):
    return x * lax.rsqrt(jnp.mean(x * x, axis=-1, keepdims=True) + eps)


def _inproj_kernel(x_ref, g_ref, w_ref, qg_ref, kg_ref, o_ref, h_ref):
    n = pl.program_id(1)

    @pl.when(n == 0)
    def _():
        h_ref[...] = (_rms(x_ref[...]) * g_ref[...]).astype(BF16)

    acc = jnp.dot(h_ref[...], w_ref[...].astype(BF16), preferred_element_type=F32)
    bn = acc.shape[1]

    def head_norm(gain):
        for j in range(bn // HEAD_DIM):
            sl = slice(j * HEAD_DIM, (j + 1) * HEAD_DIM)
            o_ref[:, sl] = (_rms(acc[:, sl]) * gain).astype(BF16)

    @pl.when(n == 0)
    def _():
        head_norm(qg_ref[...] * (HEAD_DIM ** -0.5))

    @pl.when(n == 1)
    def _():
        head_norm(kg_ref[...])

    @pl.when(n >= 2)
    def _():
        o_ref[...] = acc.astype(BF16)


def _inproj(x2, g, w_bf, qg, kg, bm=1024, bn=1024):
    m, d = x2.shape
    nw = w_bf.shape[1]
    return pl.pallas_call(
        _inproj_kernel,
        grid=(m // bm, nw // bn),
        in_specs=[
            pl.BlockSpec((bm, d), lambda i, j: (i, 0)),
            pl.BlockSpec((1, d), lambda i, j: (0, 0)),
            pl.BlockSpec((d, bn), lambda i, j: (0, j)),
            pl.BlockSpec((1, HEAD_DIM), lambda i, j: (0, 0)),
            pl.BlockSpec((1, HEAD_DIM), lambda i, j: (0, 0)),
        ],
        out_specs=pl.BlockSpec((bm, bn), lambda i, j: (i, j)),
        out_shape=jax.ShapeDtypeStruct((m, nw), BF16),
        scratch_shapes=[pltpu.VMEM((bm, d), BF16)],
        compiler_params=pltpu.CompilerParams(
            dimension_semantics=("arbitrary", "arbitrary"),
            vmem_limit_bytes=VMEM_LIMIT),
        name="inproj",
    )(x2, g, w_bf, qg, kg)


ATTN_ROWS = 4
ATTN_BAND = 12


def _attn_kernel(q_ref, k_ref, v_ref, rpb_ref, mask_ref, g_ref, o_ref, pt_ref, *, rows, win_r):
    lanes = pt_ref.shape[-1]
    n_dr = 2 * WIN_ROWS_MAX - 1

    def toeplitz(d):
        row = jnp.broadcast_to(rpb_ref[0, d:d + 1, :], (GRID_W, lanes))
        return pltpu.roll(row, lanes - (WIN_COLS - 1), 1, stride=1, stride_axis=0)

    lane = lax.broadcasted_iota(jnp.int32, (GRID_W, lanes), 1)
    left = lane < GRID_W
    tiles = [toeplitz(d) for d in range(n_dr)]
    for d in range(-1, n_dr):
        lo, hi = tiles[max(d, 0)], tiles[min(d + 1, n_dr - 1)]
        pt_ref[d + 1] = jnp.where(left, lo, pltpu.roll(hi, GRID_W, 1)) + mask_ref[...]

    def body(i, carry):
        r0 = i * ATTN_ROWS
        bs = jnp.minimum(jnp.clip(r0 - win_r // 2, 0, rows - win_r), rows - ATTN_BAND)
        q0 = pl.multiple_of(r0 * GRID_W, ATTN_ROWS * GRID_W)
        k0 = pl.multiple_of(bs * GRID_W, 2 * GRID_W)
        q = q_ref[pl.ds(q0, ATTN_ROWS * GRID_W), :]
        k = k_ref[pl.ds(k0, ATTN_BAND * GRID_W), :]
        v = v_ref[pl.ds(k0, ATTN_BAND * GRID_W), :]
        s = lax.dot_general(q, k, (((1,), (1,)), ((), ())), preferred_element_type=F32)

        bias_rows = []
        for j in range(ATTN_ROWS):
            r = r0 + j
            start = jnp.clip(r - win_r // 2, 0, rows - win_r)
            pieces = []
            for m in range(ATTN_BAND // 2):
                kr = bs + 2 * m
                pen_l = jnp.where((kr >= start) & (kr < start + win_r), 0.0, NEG_BIG)
                pen_r = jnp.where((kr + 1 >= start) & (kr + 1 < start + win_r), 0.0, NEG_BIG)
                d = jnp.clip(kr - r + WIN_ROWS_MAX - 1, -1, n_dr - 1)
                pieces.append(pt_ref[d + 1] + jnp.where(left, pen_l, pen_r))
            bias_rows.append(jnp.concatenate(pieces, axis=1))
        s = s + jnp.concatenate(bias_rows, axis=0)

        mx = jnp.max(s, axis=-1, keepdims=True)
        p = jnp.exp(s - mx)
        l = jnp.sum(p, axis=-1, keepdims=True)
        o = jnp.dot(p.astype(BF16), v, preferred_element_type=F32) / l
        o = _rms(o) * g_ref[...]
        o_ref[pl.ds(q0, ATTN_ROWS * GRID_W), :] = o.astype(BF16)
        return carry

    lax.fori_loop(0, rows // ATTN_ROWS, body, 0, unroll=2)


def _attn_window_mask():
    c = np.arange(GRID_W)
    c0 = np.clip(c - WIN_COLS // 2, 0, GRID_W - WIN_COLS)
    in_win = (c[None, :] >= c0[:, None]) & (c[None, :] < c0[:, None] + WIN_COLS)
    return jnp.asarray(np.tile(np.where(in_win, 0.0, NEG_BIG), (1, 2)), dtype=F32)


def _attention(proj, rpb, out_g, batch, seq):
    rows = seq // GRID_W
    win_r = min(WIN_ROWS_MAX, rows)
    assert win_r == WIN_ROWS_MAX and 2 * GRID_W == LANES
    assert rows % ATTN_ROWS == 0 and ATTN_BAND >= win_r + ATTN_ROWS - 1 and ATTN_BAND % 2 == 0
    h = N_ATTN_HEADS
    n_dr, n_dc = rpb.shape[1:]
    rpb_pad = jnp.zeros((h, 2 * WIN_ROWS_MAX, LANES), F32).at[:, :n_dr, :n_dc].set(rpb)
    mask = _attn_window_mask()
    kern = functools.partial(_attn_kernel, rows=rows, win_r=win_r)
    return pl.pallas_call(
        kern,
        grid=(batch, h),
        in_specs=[
            pl.BlockSpec((seq, HEAD_DIM), lambda b, i: (b, i)),
            pl.BlockSpec((seq, HEAD_DIM), lambda b, i: (b, h + i)),
            pl.BlockSpec((seq, HEAD_DIM), lambda b, i: (b, 2 * h + i)),
            pl.BlockSpec((1, 2 * WIN_ROWS_MAX, LANES), lambda b, i: (i, 0, 0)),
            pl.BlockSpec(mask.shape, lambda b, i: (0, 0)),
            pl.BlockSpec((1, HEAD_DIM), lambda b, i: (0, i)),
        ],
        out_specs=pl.BlockSpec((seq, HEAD_DIM), lambda b, i: (b, i)),
        out_shape=jax.ShapeDtypeStruct((batch * seq, ATTN_WIDTH), BF16),
        scratch_shapes=[pltpu.VMEM((2 * WIN_ROWS_MAX, GRID_W, LANES), F32)],
        compiler_params=pltpu.CompilerParams(
            dimension_semantics=("arbitrary", "arbitrary"),
            vmem_limit_bytes=VMEM_LIMIT),
        name="na2d_attn",
    )(proj, proj, proj, rpb_pad, mask, out_g)


def _outproj_kernel(a_ref, hy_ref, x_ref, wo_ref, g_ref, wr_ref, x1_ref, hn_ref, lg_ref):
    half = a_ref.shape[1]
    acc = x_ref[...]
    acc = acc + jnp.dot(a_ref[...], wo_ref[:half, :], preferred_element_type=F32)
    acc = acc + jnp.dot(hy_ref[...], wo_ref[half:, :], preferred_element_type=F32)
    x1_ref[...] = acc
    hn = _rms(acc) * g_ref[...]
    hn_ref[...] = hn
    lg_ref[...] = jnp.dot(hn.astype(BF16), wr_ref[...], preferred_element_type=F32)


def _outproj(attn, hy, x2, wo_bf, g, wr_pad, bm=512):
    m, d = x2.shape
    half = attn.shape[1]
    npad = wr_pad.shape[1]
    return pl.pallas_call(
        _outproj_kernel,
        grid=(m // bm,),
        in_specs=[
            pl.BlockSpec((bm, half), lambda i: (i, 0)),
            pl.BlockSpec((bm, half), lambda i: (i, 0)),
            pl.BlockSpec((bm, d), lambda i: (i, 0)),
            pl.BlockSpec((d, d), lambda i: (0, 0)),
            pl.BlockSpec((1, d), lambda i: (0, 0)),
            pl.BlockSpec((d, npad), lambda i: (0, 0)),
        ],
        out_specs=[
            pl.BlockSpec((bm, d), lambda i: (i, 0)),
            pl.BlockSpec((bm, d), lambda i: (i, 0)),
            pl.BlockSpec((bm, npad), lambda i: (i, 0)),
        ],
        out_shape=[
            jax.ShapeDtypeStruct((m, d), F32),
            jax.ShapeDtypeStruct((m, d), F32),
            jax.ShapeDtypeStruct((m, npad), F32),
        ],
        compiler_params=pltpu.CompilerParams(
            dimension_semantics=("arbitrary",),
            vmem_limit_bytes=VMEM_LIMIT),
        name="outproj",
    )(attn, hy, x2, wo_bf, g, wr_pad)


ROUTE_BLK = 128
ROUTE_CHUNK = 512
ROUTE_MATCH = 256
F32_INF_BITS = 0x7F800000


def _route_kernel(lg_ref, tri_ref, idx_ref, gate_ref, aff_ref, pos_ref, *, cap):
    s, lanes = lg_ref.shape
    lane = lax.broadcasted_iota(jnp.int32, (1, lanes), 1)

    def softmax_chunk(i, c):
        rows = pl.ds(pl.multiple_of(i * ROUTE_CHUNK, ROUTE_CHUNK), ROUTE_CHUNK)
        lg = jnp.where(lane < N_EXPERTS, lg_ref[rows, :], NEG_BIG)
        ex = jnp.exp(lg - jnp.max(lg, axis=-1, keepdims=True))
        aff_ref[rows, :] = ex / jnp.sum(ex, axis=-1, keepdims=True)
        return c

    lax.fori_loop(0, s // ROUTE_CHUNK, softmax_chunk, 0)

    def as_value(bits):
        return pltpu.bitcast(bits, F32)

    def fold8(x):
        parts = [x[i:i + 8] for i in range(0, x.shape[0], 8)]
        while len(parts) > 1:
            parts = [a + b for a, b in zip(parts[::2], parts[1::2])]
        return parts[0]

    def count_ge(th):
        def body(i, acc):
            rows = pl.ds(pl.multiple_of(i * ROUTE_CHUNK, ROUTE_CHUNK), ROUTE_CHUNK)
            return acc + fold8(jnp.where(aff_ref[rows, :] >= th, 1.0, 0.0))
        acc = lax.fori_loop(0, s // ROUTE_CHUNK, body, jnp.zeros((8, lanes), F32))
        return jnp.sum(acc, axis=0, keepdims=True)

    def bisect(_, c):
        lo, hi = c
        mid = lo + ((hi - lo) >> 1)
        ok = count_ge(as_value(mid)) >= cap
        return jnp.where(ok, mid, lo), jnp.where(ok, hi, mid)

    tau_bits, _ = lax.fori_loop(0, 31, bisect, (jnp.zeros((1, lanes), jnp.int32),
                                                jnp.full((1, lanes), F32_INF_BITS, jnp.int32)))
    tau, above = as_value(tau_bits), as_value(tau_bits + 1)
    ties_wanted = cap - count_ge(above)

    def scan_block(j, c):
        tie_carry, sel_carry = c
        rows = pl.ds(pl.multiple_of(j * ROUTE_BLK, ROUTE_BLK), ROUTE_BLK)
        aff = aff_ref[rows, :]
        gt = aff >= above
        eq = jnp.where((aff >= tau) & (aff < above), 1.0, 0.0)
        tie_incl = jnp.dot(tri_ref[...], eq.astype(BF16), preferred_element_type=F32) + tie_carry
        sel = jnp.where(gt | ((eq > 0.0) & (tie_incl - eq < ties_wanted)), 1.0, 0.0)
        sel_incl = jnp.dot(tri_ref[...], sel.astype(BF16), preferred_element_type=F32) + sel_carry
        pos_ref[rows, :] = jnp.where(sel > 0.0, sel_incl - sel, -1.0)
        return tie_incl[ROUTE_BLK - 1:, :], sel_incl[ROUTE_BLK - 1:, :]

    zero = jnp.zeros((1, lanes), F32)
    lax.fori_loop(0, s // ROUTE_BLK, scan_block, (zero, zero))

    slot = lax.broadcasted_iota(jnp.int32, (1, cap), 1).astype(F32)
    for e in range(N_EXPERTS):
        def compact(c, acc, e=e):
            ia, ga = acc
            r0 = pl.multiple_of(c * ROUTE_MATCH, ROUTE_MATCH)
            match = pos_ref[pl.ds(r0, ROUTE_MATCH), e:e + 1] == slot
            tok = (r0 + lax.broadcasted_iota(jnp.int32, (ROUTE_MATCH, 1), 0)).astype(F32)
            ia = ia + fold8(jnp.where(match, tok, 0.0))
            ga = ga + fold8(jnp.where(match, aff_ref[pl.ds(r0, ROUTE_MATCH), e:e + 1], 0.0))
            return ia, ga

        zc = jnp.zeros((8, cap), F32)
        ia, ga = lax.fori_loop(0, s // ROUTE_MATCH, compact, (zc, zc))
        idx_ref[0, e:e + 1, :] = jnp.sum(ia, axis=0, keepdims=True).astype(jnp.int32)
        gate_ref[0, e:e + 1, :] = jnp.sum(ga, axis=0, keepdims=True)


def _route(logits, batch, seq, cap):
    lanes = logits.shape[1]
    tri = jnp.asarray(np.tril(np.ones((ROUTE_BLK, ROUTE_BLK))), dtype=BF16)
    return pl.pallas_call(
        functools.partial(_route_kernel, cap=cap),
        grid=(batch,),
        in_specs=[
            pl.BlockSpec((seq, lanes), lambda b: (b, 0)),
            pl.BlockSpec((ROUTE_BLK, ROUTE_BLK), lambda b: (0, 0)),
        ],
        out_specs=[
            pl.BlockSpec((1, N_EXPERTS, cap), lambda b: (b, 0, 0)),
            pl.BlockSpec((1, N_EXPERTS, cap), lambda b: (b, 0, 0)),
        ],
        out_shape=[
            jax.ShapeDtypeStruct((batch, N_EXPERTS, cap), jnp.int32),
            jax.ShapeDtypeStruct((batch, N_EXPERTS, cap), F32),
        ],
        scratch_shapes=[
            pltpu.VMEM((seq, lanes), F32),
            pltpu.VMEM((seq, lanes), F32),
        ],
        compiler_params=pltpu.CompilerParams(dimension_semantics=("arbitrary",)),
        name="route",
    )(logits, tri)


MOE_FF_SPLIT = 4
ROW_UNROLL = 8


def _moe_kernel(idx_ref, hn_hbm, wg_ref, wu_ref, wd_ref, gate_ref, x1_hbm, out_hbm,
                xg0, xg1, acc0, acc1, y0, y1, sem_x, sem_a, sem_s, *, cap, seq):
    del x1_hbm
    e, f = pl.program_id(0), pl.program_id(1)
    n_e = pl.num_programs(0)
    xg, acc = (xg0, xg1), (acc0, acc1)
    prev_e, next_e = jnp.maximum(e - 1, 0), jnp.minimum(e + 1, n_e - 1)

    def row_of(ee, b):
        base = (b * n_e + ee) * cap
        return lambda p: b * seq + idx_ref[base + p]

    def copy_rows(ee, b, kind, rolled=False):
        row = row_of(ee, b)

        def one(p):
            if kind == "xg":
                src, dst, sem = hn_hbm.at[pl.ds(row(p), 1), :], xg[b].at[pl.ds(p, 1), :], sem_x.at[b]
            elif kind == "ag":
                src, dst, sem = out_hbm.at[pl.ds(row(p), 1), :], acc[b].at[pl.ds(p, 1), :], sem_a.at[b]
            else:
                src, dst, sem = acc[b].at[pl.ds(p, 1), :], out_hbm.at[pl.ds(row(p), 1), :], sem_s.at[b]
            pltpu.make_async_copy(src, dst, sem).start()

        if rolled:
            def body(i, c):
                for k in range(ROW_UNROLL):
                    one(i * ROW_UNROLL + k)
                return c
            lax.fori_loop(0, cap // ROW_UNROLL, body, 0)
        else:
            for p in range(cap):
                one(p)

    def wait_rows(buf, sem):
        pltpu.make_async_copy(hn_hbm.at[pl.ds(0, cap), :], buf, sem).wait()

    def partial_out(b, wg, wu, wd):
        x = xg[b][...].astype(BF16)
        a = jnp.dot(x, wg, preferred_element_type=F32)
        u = jnp.dot(x, wu, preferred_element_type=F32)
        hmid = (a * (1.0 / (1.0 + jnp.exp(-a))) * u).astype(BF16)
        return jnp.dot(hmid, wd, preferred_element_type=F32)

    def weights():
        return wg_ref[0].astype(BF16), wu_ref[0].astype(BF16), wd_ref[0].astype(BF16)

    @pl.when((e == 0) & (f == 0))
    def _():
        copy_rows(e, 0, "xg", rolled=True)
        copy_rows(e, 0, "ag", rolled=True)
        copy_rows(e, 1, "ag", rolled=True)
        wait_rows(acc0, sem_a.at[0])
        wait_rows(acc1, sem_a.at[1])

    @pl.when(f == 0)
    def _():
        w = weights()
        wait_rows(xg0, sem_x.at[0])
        copy_rows(e, 1, "xg")
        y0[...] = partial_out(0, *w)
        wait_rows(xg1, sem_x.at[1])
        copy_rows(prev_e, 0, "sc")
        y1[...] = partial_out(1, *w)

    @pl.when(f == 1)
    def _():
        w = weights()
        copy_rows(prev_e, 1, "sc")
        y0[...] += partial_out(0, *w)
        wait_rows(acc0, sem_s.at[0])
        copy_rows(e, 0, "ag")
        y1[...] += partial_out(1, *w)

    @pl.when(f == 2)
    def _():
        w = weights()
        wait_rows(acc1, sem_s.at[1])
        copy_rows(e, 1, "ag")
        y0[...] += partial_out(0, *w)
        y1[...] += partial_out(1, *w)

    @pl.when(f == 3)
    def _():
        w = weights()
        t0 = y0[...] + partial_out(0, *w)
        wait_rows(acc0, sem_a.at[0])
        acc0[...] += t0 * gate_ref[0, 0]
        copy_rows(next_e, 0, "xg")
        t1 = y1[...] + partial_out(1, *w)
        wait_rows(acc1, sem_a.at[1])
        acc1[...] += t1 * gate_ref[1, 0]

    @pl.when((e == n_e - 1) & (f == MOE_FF_SPLIT - 1))
    def _():
        copy_rows(e, 0, "sc", rolled=True)
        copy_rows(e, 1, "sc", rolled=True)
        wait_rows(acc0, sem_s.at[0])
        wait_rows(acc1, sem_s.at[1])
        wait_rows(xg0, sem_x.at[0])


def _moe(idx, gate, hn, x1, wg, wu, wd, batch, seq):
    n_e, d, ff = wg.shape
    cap = idx.shape[-1]
    assert batch == 2 and MOE_FF_SPLIT == 4 and ff % MOE_FF_SPLIT == 0
    fq = ff // MOE_FF_SPLIT
    grid_spec = pltpu.PrefetchScalarGridSpec(
        num_scalar_prefetch=1,
        grid=(n_e, MOE_FF_SPLIT),
        in_specs=[
            pl.BlockSpec(memory_space=pl.ANY),
            pl.BlockSpec((1, d, fq), lambda i, j, s: (i, 0, j)),
            pl.BlockSpec((1, d, fq), lambda i, j, s: (i, 0, j)),
            pl.BlockSpec((1, fq, d), lambda i, j, s: (i, j, 0)),
            pl.BlockSpec((batch, 1, cap, 1), lambda i, j, s: (0, i, 0, 0)),
            pl.BlockSpec(memory_space=pl.ANY),
        ],
        out_specs=pl.BlockSpec(memory_space=pl.ANY),
        scratch_shapes=(
            [pltpu.VMEM((cap, d), F32) for _ in range(6)]
            + [pltpu.SemaphoreType.DMA((batch,)) for _ in range(3)]),
    )
    return pl.pallas_call(
        functools.partial(_moe_kernel, cap=cap, seq=seq),
        grid_spec=grid_spec,
        out_shape=jax.ShapeDtypeStruct(x1.shape, x1.dtype),
        input_output_aliases={6: 0},
        compiler_params=pltpu.CompilerParams(
            dimension_semantics=("arbitrary", "arbitrary"),
            vmem_limit_bytes=VMEM_LIMIT),
        name="moe_experts",
    )(idx.reshape(-1), hn, wg, wu, wd, gate[..., None], x1)


FFT_N1 = 128
FFT_N2 = 64
FFT_H1 = FFT_N1 // 2
K1_PAD = 72
Z_PITCH = 72
AB_PITCH = 136
HY_CB = 256
FFT_UNROLL = 8
HP = lax.Precision.HIGHEST


def _dft_constants():
    n, n1, n2, h1 = FFT_N1 * FFT_N2, FFT_N1, FFT_N2, FFT_H1
    k1 = np.arange(h1 + 1, dtype=np.float64)
    s1 = np.arange(h1, dtype=np.float64)
    s2 = np.arange(n2, dtype=np.float64)
    tw = s2[:, None, None] * k1[None, :, None] / n

    def stage1(phase_s1):
        th = -2.0 * np.pi * (phase_s1 + tw)
        m = np.zeros((n2, 2 * K1_PAD, h1))
        m[:, :h1 + 1] = np.cos(th)
        m[:, K1_PAD:K1_PAD + h1 + 1] = np.sin(th)
        return m

    f1 = stage1(s1[None, None, :] * k1[None, :, None] / n1)
    f1b = stage1(((n1 - 1) - s1)[None, None, :] * k1[None, :, None] / n1)
    f1b0 = stage1((n1 - s1)[None, None, :] * k1[None, :, None] / n1)[0]
    f1b0[:, 0] = 0.0
    f1b[0] = f1b0

    a = 2.0 * np.pi * np.outer(np.arange(n2), np.arange(n2)) / n2
    c, s = np.cos(a), np.sin(a)
    f2 = np.block([[c, s], [-s, c]])
    f2i = np.block([[c, -s], [s, c]])

    t1 = np.arange(h1, dtype=np.float64)
    th = 2.0 * np.pi * (t1[None, :, None] * k1[None, None, :] / n1 + s2[:, None, None] * k1[None, None, :] / n)
    wgt = np.where((k1 == 0) | (k1 == h1), 1.0, 2.0) / n
    g = np.zeros((n2, h1, 2 * K1_PAD))
    g[:, :, :h1 + 1] = wgt * np.cos(th)
    g[:, :, K1_PAD:K1_PAD + h1 + 1] = -wgt * np.sin(th)
    return tuple(jnp.asarray(m, dtype=F32).astype(BF16) for m in (f1, f1b, f2, f2i, g))


def _hid_kernel(f_ref, w1t_ref, w1c_ref, w1s_ref, b1_ref, w2_ref, b2_ref, fr_ref, o_ref, *, length):
    bm = o_ref.shape[0]
    row = (pl.program_id(0) * bm + lax.broadcasted_iota(jnp.int32, (bm, 1), 0)).astype(F32)
    t = row / (length - 1.0)
    ang = f_ref[...] * (2.0 * math.pi * row / length)
    pre = (t * w1t_ref[...]
           + jnp.dot(jnp.cos(ang), w1c_ref[...], precision=HP, preferred_element_type=F32)
           - jnp.dot(jnp.sin(ang), w1s_ref[...], precision=HP, preferred_element_type=F32))
    fr = fr_ref[...]
    hid = jnp.sin(fr * (pre + b1_ref[...]))
    hid = jnp.sin(fr * (jnp.dot(hid, w2_ref[...], precision=HP, preferred_element_type=F32) + b2_ref[...]))
    o_ref[...] = hid


def _filter_hidden(length, w1, b1, w2, b2, freq, bm=512):
    bands = (FILTER_EMB - 1) // 2
    f = jnp.linspace(1e-4, bands - 1, bands, dtype=F32)[None, :]
    hdim = w1.shape[1]
    full = lambda a: pl.BlockSpec(a.shape, lambda i: (0,) * a.ndim)
    args = (f, w1[:1], w1[1:1 + bands], w1[1 + bands:], b1[None], w2, b2[None], freq[None])
    return pl.pallas_call(
        functools.partial(_hid_kernel, length=length),
        grid=(length // bm,),
        in_specs=[full(a) for a in args],
        out_specs=pl.BlockSpec((bm, hdim), lambda i: (i, 0)),
        out_shape=jax.ShapeDtypeStruct((length, hdim), F32),
        name="filter_hidden",
    )(*args)


def _ld(ref, start, size, stride=None):
    idx = pl.ds(start, size) if stride is None else pl.ds(start, size, stride=stride)
    return jnp.concatenate([ref[h, idx, :] for h in range(ref.shape[0])], axis=-1)


def _st(ref, start, size, val, stride=None):
    idx = pl.ds(start, size) if stride is None else pl.ds(start, size, stride=stride)
    for h in range(ref.shape[0]):
        ref[h, idx, :] = val[:, h * LANES:(h + 1) * LANES]


def _stage1_store(ab_ref, s2, a):
    _st(ab_ref, s2, K1_PAD, a[:K1_PAD], stride=AB_PITCH)
    _st(ab_ref, FFT_N2 + s2, K1_PAD, a[K1_PAD:], stride=AB_PITCH)


def _split_bf16(a):
    hi = a.astype(BF16)
    return hi, (a - hi.astype(F32)).astype(BF16)


def _dot_split(a_hi, a_lo, b_hi, b_lo):
    d = functools.partial(jnp.dot, preferred_element_type=F32)
    return d(a_hi, b_hi) + (d(a_hi, b_lo) + d(a_lo, b_hi))


def _spectrum_kernel(hid_ref, w3f_ref, w3b_ref, dl_ref, skip_ref, f1_ref, f2_ref, o_ref,
                     hf_ref, hb_ref, ab_ref, *, length):
    blocks = 4
    rows = blocks * FFT_N2
    w3f, w3b = _split_bf16(w3f_ref[...]), _split_bf16(w3b_ref[...])

    def gen(j, c):
        row = (j * rows + lax.broadcasted_iota(jnp.int32, (rows, 1), 0)).astype(F32)
        win = jnp.exp(-(row / (length - 1.0)) * dl_ref[...])
        hid = hid_ref[pl.ds(pl.multiple_of(j * rows, rows), rows), :]
        hid_hi, hid_lo = _split_bf16(hid)
        hf = _dot_split(hid_hi, hid_lo, *w3f) * win
        hb = _dot_split(hid_hi, hid_lo, *w3b) * win
        for q in range(blocks):
            dst = pl.multiple_of((j * blocks + q) * Z_PITCH, 8)
            _st(hf_ref, dst, FFT_N2, hf[q * FFT_N2:(q + 1) * FFT_N2])
            _st(hb_ref, dst, FFT_N2, hb[q * FFT_N2:(q + 1) * FFT_N2])
        return c

    lax.fori_loop(0, FFT_H1 // blocks, gen, 0)

    def stage1(s2, c):
        xf = _ld(hf_ref, s2, FFT_H1, Z_PITCH).astype(BF16)
        xb = _ld(hb_ref, (FFT_N2 - s2) % FFT_N2, FFT_H1, Z_PITCH).astype(BF16)
        x = jnp.concatenate([xf, xb], axis=0)
        _stage1_store(ab_ref, s2, jnp.dot(f1_ref[s2], x, preferred_element_type=F32))
        return c

    lax.fori_loop(0, FFT_N2, stage1, 0, unroll=FFT_UNROLL)

    def stage2(k1, c):
        a = _ld(ab_ref, pl.multiple_of(k1 * AB_PITCH, 8), 2 * FFT_N2).astype(BF16)
        x = jnp.dot(f2_ref[...], a, preferred_element_type=F32)
        o_ref[0, k1, :FFT_N2, :] = (x[:FFT_N2] + skip_ref[0]).astype(BF16)
        o_ref[0, k1, FFT_N2:, :] = x[FFT_N2:].astype(BF16)
        return c

    lax.fori_loop(0, FFT_H1 + 1, stage2, 0, unroll=FFT_UNROLL)


def _filter_spectrum(hid, w3, deltas, filt_bias, consts, length):
    f1, f1b, f2, _, _ = consts
    f1 = jnp.concatenate([f1, f1b], axis=2)
    cb = HY_CB
    ncb = HYENA_WIDTH // cb
    hdim = hid.shape[1]
    full = lambda a: pl.BlockSpec(a.shape, lambda o, c: (0,) * a.ndim)
    return pl.pallas_call(
        functools.partial(_spectrum_kernel, length=length),
        grid=(HYENA_ORDER, ncb),
        in_specs=[
            full(hid),
            pl.BlockSpec((hdim, cb), lambda o, c: (0, (o * N_DIRS) * ncb + c)),
            pl.BlockSpec((hdim, cb), lambda o, c: (0, (o * N_DIRS + 1) * ncb + c)),
            pl.BlockSpec((1, cb), lambda o, c: (0, c)),
            pl.BlockSpec((1, 1, cb), lambda o, c: (o, 0, c)),
            full(f1), full(f2),
        ],
        out_specs=pl.BlockSpec((1, FFT_H1 + 1, 2 * FFT_N2, cb), lambda o, c: (o, 0, 0, c)),
        out_shape=jax.ShapeDtypeStruct((HYENA_ORDER, FFT_H1 + 1, 2 * FFT_N2, HYENA_WIDTH), BF16),
        scratch_shapes=[
            pltpu.VMEM((cb // LANES, FFT_H1 * Z_PITCH, LANES), F32),
            pltpu.VMEM((cb // LANES, FFT_H1 * Z_PITCH, LANES), F32),
            pltpu.VMEM((cb // LANES, K1_PAD * AB_PITCH, LANES), F32),
        ],
        compiler_params=pltpu.CompilerParams(
            dimension_semantics=("arbitrary", "arbitrary"),
            vmem_limit_bytes=VMEM_LIMIT),
        name="filter_spectrum",
    )(hid, w3, w3, deltas, filt_bias[:, None, :], f1, f2)


def _short_conv_block(s_ref, j, w_ref, b_ref):
    big = s_ref[pl.ds(pl.multiple_of(j * FFT_N2, FFT_N2), FFT_N2 + 16), :]
    rows = FFT_N2 + 16
    prev = pltpu.roll(big, 1, 0)[8:8 + FFT_N2]
    nxt = pltpu.roll(big, rows - 1, 0)[8:8 + FFT_N2]
    cur = big[8:8 + FFT_N2]
    return b_ref[...] + prev * w_ref[0:1, :] + cur * w_ref[1:2, :] + nxt * w_ref[2:3, :]


def _load_stream(s_ref, src_ref, length):
    zeros = jnp.zeros((8, s_ref.shape[1]), F32)
    s_ref[0:8, :] = zeros
    s_ref[length + 8:length + 16, :] = zeros
    chunk = 512

    def cp(i, c):
        r = pl.multiple_of(i * chunk, chunk)
        s_ref[pl.ds(r + 8, chunk), :] = src_ref[pl.ds(r, chunk), :].astype(F32)
        return c

    lax.fori_loop(0, length // chunk, cp, 0)


def _hyena_kernel(z_ref, gs_ref, cwz_ref, cbz_ref, cwg_ref, cbg_ref, h_ref, f1_ref, f2_ref, f2i_ref, g_ref,
                  og_ref, o_ref, s_ref, zy_ref, ab_ref, *, length, conv_z, norm):
    if conv_z:
        _load_stream(s_ref, z_ref, length)

    def prep(j, c):
        if conv_z:
            zblk = _short_conv_block(s_ref, j, cwz_ref, cbz_ref)
        else:
            zblk = z_ref[pl.ds(pl.multiple_of(j * FFT_N2, FFT_N2), FFT_N2), :].astype(F32)
        _st(zy_ref, pl.multiple_of(j * Z_PITCH, 8), FFT_N2, zblk)
        return c

    lax.fori_loop(0, FFT_H1, prep, 0)

    def stage1(s2, c):
        xs = _ld(zy_ref, s2, FFT_H1, Z_PITCH).astype(BF16)
        _stage1_store(ab_ref, s2, jnp.dot(f1_ref[s2], xs, preferred_element_type=F32))
        return c

    lax.fori_loop(0, FFT_N2, stage1, 0, unroll=FFT_UNROLL)

    def stage2(k1, c):
        blk = pl.multiple_of(k1 * AB_PITCH, 8)
        x = jnp.dot(f2_ref[...], _ld(ab_ref, blk, 2 * FFT_N2).astype(BF16), preferred_element_type=F32)
        hk = h_ref[0, k1].astype(F32)
        xr, xi, hr, hi = x[:FFT_N2], x[FFT_N2:], hk[:FFT_N2], hk[FFT_N2:]
        p = jnp.concatenate([xr * hr - xi * hi, xr * hi + xi * hr], axis=0).astype(BF16)
        _st(ab_ref, blk, 2 * FFT_N2, jnp.dot(f2i_ref[...], p, preferred_element_type=F32))
        return c

    lax.fori_loop(0, FFT_H1 + 1, stage2, 0, unroll=FFT_UNROLL)

    def stage3(t2, c):
        br = _ld(ab_ref, t2, K1_PAD, AB_PITCH)
        bi = _ld(ab_ref, FFT_N2 + t2, K1_PAD, AB_PITCH)
        rhs = jnp.concatenate([br, bi], axis=0).astype(BF16)
        _st(zy_ref, t2, FFT_H1, jnp.dot(g_ref[t2], rhs, preferred_element_type=F32), stride=Z_PITCH)
        return c

    lax.fori_loop(0, FFT_N2, stage3, 0, unroll=FFT_UNROLL)

    _load_stream(s_ref, gs_ref, length)

    def fin(j, c):
        gate = _short_conv_block(s_ref, j, cwg_ref, cbg_ref)
        y = gate * _ld(zy_ref, pl.multiple_of(j * Z_PITCH, 8), FFT_N2)
        dst = pl.ds(pl.multiple_of(j * FFT_N2, FFT_N2), FFT_N2)
        if norm:
            gw = HYENA_WIDTH // N_HYENA_GROUPS
            for q in range(y.shape[1] // gw):
                sl = slice(q * gw, (q + 1) * gw)
                o_ref[dst, sl] = (_rms(y[:, sl]) * og_ref[:, sl]).astype(o_ref.dtype)
        else:
            o_ref[dst, :] = y.astype(o_ref.dtype)
        return c

    lax.fori_loop(0, FFT_H1, fin, 0)


def _hyena_order(zsrc, zcol, proj, order, conv_w, conv_b, hspec, consts, out_g, batch, length, conv_z, norm):
    f1, _, f2, f2i, g = consts
    cb = HY_CB
    ncb = HYENA_WIDTH // cb
    pcol = 3 * ATTN_WIDTH // cb
    zconv = HYENA_ORDER * ncb
    full = lambda a: pl.BlockSpec(a.shape, lambda c, b: (0,) * a.ndim)
    kern = functools.partial(_hyena_kernel, length=length, conv_z=conv_z, norm=norm)
    return pl.pallas_call(
        kern,
        grid=(ncb, batch),
        in_specs=[
            pl.BlockSpec((length, cb), lambda c, b: (b, zcol + c)),
            pl.BlockSpec((length, cb), lambda c, b: (b, pcol + order * ncb + c)),
            pl.BlockSpec((SHORT_CONV_W, cb), lambda c, b: (0, zconv + c)),
            pl.BlockSpec((1, cb), lambda c, b: (0, zconv + c)),
            pl.BlockSpec((SHORT_CONV_W, cb), lambda c, b: (0, order * ncb + c)),
            pl.BlockSpec((1, cb), lambda c, b: (0, order * ncb + c)),
            pl.BlockSpec((1, FFT_H1 + 1, 2 * FFT_N2, cb), lambda c, b: (order, 0, 0, c)),
            full(f1), full(f2), full(f2i), full(g),
            pl.BlockSpec((1, cb), lambda c, b: (0, c)),
        ],
        out_specs=pl.BlockSpec((length, cb), lambda c, b: (b, c)),
        out_shape=jax.ShapeDtypeStruct((batch * length, HYENA_WIDTH), BF16),
        scratch_shapes=[
            pltpu.VMEM((length + 16, cb), F32),
            pltpu.VMEM((cb // LANES, FFT_H1 * Z_PITCH, LANES), F32),
            pltpu.VMEM((cb // LANES, K1_PAD * AB_PITCH, LANES), F32),
        ],
        compiler_params=pltpu.CompilerParams(
            dimension_semantics=("arbitrary", "arbitrary"),
            vmem_limit_bytes=VMEM_LIMIT),
        name=f"hyena_order{order}",
    )(zsrc, proj, conv_w, conv_b[None], conv_w, conv_b[None], hspec, f1, f2, f2i, g, out_g)


def _hyena(proj, conv_w, conv_b, w1, b1, w2, b2, w3, freq, filt_bias, out_g, batch, length):
    assert 2 * length == FFT_N1 * FFT_N2 and HYENA_ORDER == 2
    consts = _dft_constants()
    max_decay = math.log(DECAY_TARGET) / FAST_DECAY_PCT
    min_decay = math.log(DECAY_TARGET) / SLOW_DECAY_PCT
    deltas = jnp.abs(jnp.linspace(min_decay, max_decay, HYENA_WIDTH, dtype=F32))[None]
    hid = _filter_hidden(length, w1, b1, w2, b2, freq)
    hspec = _filter_spectrum(hid, w3, deltas, filt_bias, consts, length)
    vcol = (3 * ATTN_WIDTH + HYENA_ORDER * HYENA_WIDTH) // HY_CB
    z1 = _hyena_order(proj, vcol, proj, 0, conv_w, conv_b, hspec, consts, out_g, batch, length, True, False)
    return _hyena_order(z1, 0, proj, 1, conv_w, conv_b, hspec, consts, out_g, batch, length, False, True)


def kernel(x, mix_norm_g, w_in, q_norm_g, k_norm_g, rpb, conv_w, conv_b, filt_w1, filt_b1, filt_w2, filt_b2, filt_w3, filt_freq, filt_bias, attn_out_g, hyena_out_g, w_out, ffn_norm_g, w_router, w_gate, w_up, w_down):
    b, s, d = x.shape
    rows = s // GRID_W
    win_r = min(WIN_ROWS_MAX, rows)
    cap = EC_CAPACITY_FACTOR * s // N_EXPERTS
    x2 = x.reshape(b * s, d)
    for i in range(mix_norm_g.shape[0]):
        proj = _inproj(x2, mix_norm_g[i][None], w_in[i],
                       q_norm_g[i][None], k_norm_g[i][None])
        attn = _attention(proj, rpb[i], attn_out_g[i][None], b, s)

        hyn = _hyena(proj, conv_w[i], conv_b[i], filt_w1[i], filt_b1[i], filt_w2[i], filt_b2[i], filt_w3[i],
                     filt_freq[i], filt_bias[i], hyena_out_g[i][None], b, s)

        wr_pad = jnp.zeros((d, 128), BF16).at[:, :N_EXPERTS].set(w_router[i].astype(BF16))
        x1, hn, logits = _outproj(attn, hyn, x2, w_out[i].astype(BF16), ffn_norm_g[i][None], wr_pad)

        idx, gate = _route(logits, b, s, cap)
        x2 = _moe(idx, gate, hn, x1, w_gate[i], w_up[i], w_down[i], b, s)
    return x2.reshape(b, s, d)
```

```python
import functools
import math

import numpy as np
import jax
import jax.numpy as jnp
from jax import lax
from jax.experimental import pallas as pl
from jax.experimental.pallas import tpu as pltpu

D_MODEL = 2048
GRID_W = 64
N_ATTN_HEADS = 8
ATTN_WIDTH = D_MODEL // 2
HEAD_DIM = ATTN_WIDTH // N_ATTN_HEADS
HYENA_WIDTH = D_MODEL - ATTN_WIDTH
N_HYENA_GROUPS = 8
HYENA_ORDER = 2
N_DIRS = 2
SHORT_CONV_W = 3
FILTER_EMB = 33
DECAY_TARGET = 1e-2
FAST_DECAY_PCT = 0.3
SLOW_DECAY_PCT = 1.5
WIN_ROWS_MAX = 8
WIN_COLS = 16
N_EXPERTS = 16
EC_CAPACITY_FACTOR = 2
EXPERT_FF = 1024
IN_WIDTH = 3 * ATTN_WIDTH + (HYENA_ORDER + 1) * HYENA_WIDTH
RMS_EPS = 1e-6

F32 = jnp.float32
BF16 = jnp.bfloat16
NEG_BIG = -1e30
VMEM_LIMIT = 56 * 1024 * 1024
LANES = 128


def _rms(x, eps=RMS_EPS):
    return x * lax.rsqrt(jnp.mean(x * x, axis=-1, keepdims=True) + eps)


def _inproj_kernel(x_ref, g_ref, w_ref, qg_ref, kg_ref, o_ref, h_ref):
    n = pl.program_id(1)

    @pl.when(n == 0)
    def _():
        h_ref[...] = (_rms(x_ref[...]) * g_ref[...]).astype(BF16)

    acc = jnp.dot(h_ref[...], w_ref[...].astype(BF16), preferred_element_type=F32)
    bn = acc.shape[1]

    def head_norm(gain):
        for j in range(bn // HEAD_DIM):
            sl = slice(j * HEAD_DIM, (j + 1) * HEAD_DIM)
            o_ref[:, sl] = (_rms(acc[:, sl]) * gain).astype(BF16)

    @pl.when(n == 0)
    def _():
        head_norm(qg_ref[...] * (HEAD_DIM ** -0.5))

    @pl.when(n == 1)
    def _():
        head_norm(kg_ref[...])

    @pl.when(n >= 2)
    def _():
        o_ref[...] = acc.astype(BF16)


def _inproj(x2, g, w_bf, qg, kg, bm=1024, bn=1024):
    m, d = x2.shape
    nw = w_bf.shape[1]
    return pl.pallas_call(
        _inproj_kernel,
        grid=(m // bm, nw // bn),
        in_specs=[
            pl.BlockSpec((bm, d), lambda i, j: (i, 0)),
            pl.BlockSpec((1, d), lambda i, j: (0, 0)),
            pl.BlockSpec((d, bn), lambda i, j: (0, j)),
            pl.BlockSpec((1, HEAD_DIM), lambda i, j: (0, 0)),
            pl.BlockSpec((1, HEAD_DIM), lambda i, j: (0, 0)),
        ],
        out_specs=pl.BlockSpec((bm, bn), lambda i, j: (i, j)),
        out_shape=jax.ShapeDtypeStruct((m, nw), BF16),
        scratch_shapes=[pltpu.VMEM((bm, d), BF16)],
        compiler_params=pltpu.CompilerParams(
            dimension_semantics=("arbitrary", "arbitrary"),
            vmem_limit_bytes=VMEM_LIMIT),
        name="inproj",
    )(x2, g, w_bf, qg, kg)


ATTN_ROWS = 4
ATTN_BAND = 12


def _attn_kernel(q_ref, k_ref, v_ref, rpb_ref, mask_ref, g_ref, o_ref, pt_ref, *, rows, win_r):
    lanes = pt_ref.shape[-1]
    n_dr = 2 * WIN_ROWS_MAX - 1

    def toeplitz(d):
        row = jnp.broadcast_to(rpb_ref[0, d:d + 1, :], (GRID_W, lanes))
        return pltpu.roll(row, lanes - (WIN_COLS - 1), 1, stride=1, stride_axis=0)

    lane = lax.broadcasted_iota(jnp.int32, (GRID_W, lanes), 1)
    left = lane < GRID_W
    tiles = [toeplitz(d) for d in range(n_dr)]
    for d in range(-1, n_dr):
        lo, hi = tiles[max(d, 0)], tiles[min(d + 1, n_dr - 1)]
        pt_ref[d + 1] = jnp.where(left, lo, pltpu.roll(hi, GRID_W, 1)) + mask_ref[...]

    def body(i, carry):
        r0 = i * ATTN_ROWS
        bs = jnp.minimum(jnp.clip(r0 - win_r // 2, 0, rows - win_r), rows - ATTN_BAND)
        q0 = pl.multiple_of(r0 * GRID_W, ATTN_ROWS * GRID_W)
        k0 = pl.multiple_of(bs * GRID_W, 2 * GRID_W)
        q = q_ref[pl.ds(q0, ATTN_ROWS * GRID_W), :]
        k = k_ref[pl.ds(k0, ATTN_BAND * GRID_W), :]
        v = v_ref[pl.ds(k0, ATTN_BAND * GRID_W), :]
        s = lax.dot_general(q, k, (((1,), (1,)), ((), ())), preferred_element_type=F32)

        bias_rows = []
        for j in range(ATTN_ROWS):
            r = r0 + j
            start = jnp.clip(r - win_r // 2, 0, rows - win_r)
            pieces = []
            for m in range(ATTN_BAND // 2):
                kr = bs + 2 * m
                pen_l = jnp.where((kr >= start) & (kr < start + win_r), 0.0, NEG_BIG)
                pen_r = jnp.where((kr + 1 >= start) & (kr + 1 < start + win_r), 0.0, NEG_BIG)
                d = jnp.clip(kr - r + WIN_ROWS_MAX - 1, -1, n_dr - 1)
                pieces.append(pt_ref[d + 1] + jnp.where(left, pen_l, pen_r))
            bias_rows.append(jnp.concatenate(pieces, axis=1))
        s = s + jnp.concatenate(bias_rows, axis=0)

        mx = jnp.max(s, axis=-1, keepdims=True)
        p = jnp.exp(s - mx)
        l = jnp.sum(p, axis=-1, keepdims=True)
        o = jnp.dot(p.astype(BF16), v, preferred_element_type=F32) / l
        o = _rms(o) * g_ref[...]
        o_ref[pl.ds(q0, ATTN_ROWS * GRID_W), :] = o.astype(BF16)
        return carry

    lax.fori_loop(0, rows // ATTN_ROWS, body, 0, unroll=2)


def _attn_window_mask():
    c = np.arange(GRID_W)
    c0 = np.clip(c - WIN_COLS // 2, 0, GRID_W - WIN_COLS)
    in_win = (c[None, :] >= c0[:, None]) & (c[None, :] < c0[:, None] + WIN_COLS)
    return jnp.asarray(np.tile(np.where(in_win, 0.0, NEG_BIG), (1, 2)), dtype=F32)


def _attention(proj, rpb, out_g, batch, seq):
    rows = seq // GRID_W
    win_r = min(WIN_ROWS_MAX, rows)
    assert win_r == WIN_ROWS_MAX and 2 * GRID_W == LANES
    assert rows % ATTN_ROWS == 0 and ATTN_BAND >= win_r + ATTN_ROWS - 1 and ATTN_BAND % 2 == 0
    h = N_ATTN_HEADS
    n_dr, n_dc = rpb.shape[1:]
    rpb_pad = jnp.zeros((h, 2 * WIN_ROWS_MAX, LANES), F32).at[:, :n_dr, :n_dc].set(rpb)
    mask = _attn_window_mask()
    kern = functools.partial(_attn_kernel, rows=rows, win_r=win_r)
    return pl.pallas_call(
        kern,
        grid=(batch, h),
        in_specs=[
            pl.BlockSpec((seq, HEAD_DIM), lambda b, i: (b, i)),
            pl.BlockSpec((seq, HEAD_DIM), lambda b, i: (b, h + i)),
            pl.BlockSpec((seq, HEAD_DIM), lambda b, i: (b, 2 * h + i)),
            pl.BlockSpec((1, 2 * WIN_ROWS_MAX, LANES), lambda b, i: (i, 0, 0)),
            pl.BlockSpec(mask.shape, lambda b, i: (0, 0)),
            pl.BlockSpec((1, HEAD_DIM), lambda b, i: (0, i)),
        ],
        out_specs=pl.BlockSpec((seq, HEAD_DIM), lambda b, i: (b, i)),
        out_shape=jax.ShapeDtypeStruct((batch * seq, ATTN_WIDTH), BF16),
        scratch_shapes=[pltpu.VMEM((2 * WIN_ROWS_MAX, GRID_W, LANES), F32)],
        compiler_params=pltpu.CompilerParams(
            dimension_semantics=("arbitrary", "arbitrary"),
            vmem_limit_bytes=VMEM_LIMIT),
        name="na2d_attn",
    )(proj, proj, proj, rpb_pad, mask, out_g)


OUTPROJ_GROUPS = 2


def _outproj_kernel(a_ref, hy_ref, x_ref, wo_ref, g_ref, wr_ref, x1_ref, hn_ref, lg_ref):
    half = a_ref.shape[1]
    bm = x_ref.shape[0]
    for r in range(OUTPROJ_GROUPS):
        rows = slice(r * bm // OUTPROJ_GROUPS, (r + 1) * bm // OUTPROJ_GROUPS)
        acc = x_ref[rows, :]
        acc = acc + jnp.dot(a_ref[rows, :], wo_ref[:half, :], preferred_element_type=F32)
        acc = acc + jnp.dot(hy_ref[rows, :], wo_ref[half:, :], preferred_element_type=F32)
        x1_ref[rows, :] = acc
        hn = _rms(acc) * g_ref[...]
        hn_ref[rows, :] = hn
        lg_ref[rows, :] = jnp.dot(hn.astype(BF16), wr_ref[...], preferred_element_type=F32)


def _outproj(attn, hy, x2, wo_bf, g, wr_pad, bm=512):
    m, d = x2.shape
    half = attn.shape[1]
    npad = wr_pad.shape[1]
    return pl.pallas_call(
        _outproj_kernel,
        grid=(m // bm,),
        in_specs=[
            pl.BlockSpec((bm, half), lambda i: (i, 0)),
            pl.BlockSpec((bm, half), lambda i: (i, 0)),
            pl.BlockSpec((bm, d), lambda i: (i, 0)),
            pl.BlockSpec((d, d), lambda i: (0, 0)),
            pl.BlockSpec((1, d), lambda i: (0, 0)),
            pl.BlockSpec((d, npad), lambda i: (0, 0)),
        ],
        out_specs=[
            pl.BlockSpec((bm, d), lambda i: (i, 0)),
            pl.BlockSpec((bm, d), lambda i: (i, 0)),
            pl.BlockSpec((bm, npad), lambda i: (i, 0)),
        ],
        out_shape=[
            jax.ShapeDtypeStruct((m, d), F32),
            jax.ShapeDtypeStruct((m, d), F32),
            jax.ShapeDtypeStruct((m, npad), F32),
        ],
        compiler_params=pltpu.CompilerParams(
            dimension_semantics=("arbitrary",),
            vmem_limit_bytes=VMEM_LIMIT),
        name="outproj",
    )(attn, hy, x2, wo_bf, g, wr_pad)


ROUTE_BLK = 128
ROUTE_CHUNK = 512
ROUTE_MATCH = 256
F32_INF_BITS = 0x7F800000


def _route_kernel(lg_ref, tri_ref, idx_ref, gate_ref, aff_ref, pos_ref, *, cap):
    s, lanes = lg_ref.shape
    lane = lax.broadcasted_iota(jnp.int32, (1, lanes), 1)

    def softmax_chunk(i, c):
        rows = pl.ds(pl.multiple_of(i * ROUTE_CHUNK, ROUTE_CHUNK), ROUTE_CHUNK)
        lg = jnp.where(lane < N_EXPERTS, lg_ref[rows, :], NEG_BIG)
        ex = jnp.exp(lg - jnp.max(lg, axis=-1, keepdims=True))
        aff_ref[rows, :] = ex / jnp.sum(ex, axis=-1, keepdims=True)
        return c

    lax.fori_loop(0, s // ROUTE_CHUNK, softmax_chunk, 0)

    def as_value(bits):
        return pltpu.bitcast(bits, F32)

    def fold8(x):
        parts = [x[i:i + 8] for i in range(0, x.shape[0], 8)]
        while len(parts) > 1:
            parts = [a + b for a, b in zip(parts[::2], parts[1::2])]
        return parts[0]

    def count_ge(th):
        def body(i, acc):
            rows = pl.ds(pl.multiple_of(i * ROUTE_CHUNK, ROUTE_CHUNK), ROUTE_CHUNK)
            return acc + fold8(jnp.where(aff_ref[rows, :] >= th, 1.0, 0.0))
        acc = lax.fori_loop(0, s // ROUTE_CHUNK, body, jnp.zeros((8, lanes), F32))
        return jnp.sum(acc, axis=0, keepdims=True)

    def bisect(_, c):
        lo, hi = c
        mid = lo + ((hi - lo) >> 1)
        ok = count_ge(as_value(mid)) >= cap
        return jnp.where(ok, mid, lo), jnp.where(ok, hi, mid)

    tau_bits, _ = lax.fori_loop(0, 31, bisect, (jnp.zeros((1, lanes), jnp.int32),
                                                jnp.full((1, lanes), F32_INF_BITS, jnp.int32)))
    tau, above = as_value(tau_bits), as_value(tau_bits + 1)
    ties_wanted = cap - count_ge(above)

    def scan_block(j, c):
        tie_carry, sel_carry = c
        rows = pl.ds(pl.multiple_of(j * ROUTE_BLK, ROUTE_BLK), ROUTE_BLK)
        aff = aff_ref[rows, :]
        gt = aff >= above
        eq = jnp.where((aff >= tau) & (aff < above), 1.0, 0.0)
        tie_incl = jnp.dot(tri_ref[...], eq.astype(BF16), preferred_element_type=F32) + tie_carry
        sel = jnp.where(gt | ((eq > 0.0) & (tie_incl - eq < ties_wanted)), 1.0, 0.0)
        sel_incl = jnp.dot(tri_ref[...], sel.astype(BF16), preferred_element_type=F32) + sel_carry
        pos_ref[rows, :] = jnp.where(sel > 0.0, sel_incl - sel, -1.0)
        return tie_incl[ROUTE_BLK - 1:, :], sel_incl[ROUTE_BLK - 1:, :]

    zero = jnp.zeros((1, lanes), F32)
    lax.fori_loop(0, s // ROUTE_BLK, scan_block, (zero, zero))

    slot = lax.broadcasted_iota(jnp.int32, (1, cap), 1).astype(F32)
    for e in range(N_EXPERTS):
        def compact(c, acc, e=e):
            ia, ga = acc
            r0 = pl.multiple_of(c * ROUTE_MATCH, ROUTE_MATCH)
            match = pos_ref[pl.ds(r0, ROUTE_MATCH), e:e + 1] == slot
            tok = (r0 + lax.broadcasted_iota(jnp.int32, (ROUTE_MATCH, 1), 0)).astype(F32)
            ia = ia + fold8(jnp.where(match, tok, 0.0))
            ga = ga + fold8(jnp.where(match, aff_ref[pl.ds(r0, ROUTE_MATCH), e:e + 1], 0.0))
            return ia, ga

        zc = jnp.zeros((8, cap), F32)
        ia, ga = lax.fori_loop(0, s // ROUTE_MATCH, compact, (zc, zc))
        idx_ref[0, e:e + 1, :] = jnp.sum(ia, axis=0, keepdims=True).astype(jnp.int32)
        gate_ref[0, e:e + 1, :] = jnp.sum(ga, axis=0, keepdims=True)


def _route(logits, batch, seq, cap):
    lanes = logits.shape[1]
    tri =jnp.asarray(np.tril(np.ones((ROUTE_BLK, ROUTE_BLK))), dtype=BF16)
    return pl.pallas_call(
        functools.partial(_route_kernel, cap=cap),
        grid=(batch,),
        in_specs=[
            pl.BlockSpec((seq, lanes), lambda b: (b, 0)),
            pl.BlockSpec((ROUTE_BLK, ROUTE_BLK), lambda b: (0, 0)),
        ],
        out_specs=[
            pl.BlockSpec((1, N_EXPERTS, cap), lambda b: (b, 0, 0)),
            pl.BlockSpec((1, N_EXPERTS, cap), lambda b: (b, 0, 0)),
        ],
        out_shape=[
            jax.ShapeDtypeStruct((batch, N_EXPERTS, cap), jnp.int32),
            jax.ShapeDtypeStruct((batch, N_EXPERTS, cap), F32),
        ],
        scratch_shapes=[
            pltpu.VMEM((seq, lanes), F32),
            pltpu.VMEM((seq, lanes), F32),
        ],
        compiler_params=pltpu.CompilerParams(dimension_semantics=("arbitrary",)),
        name="route",
    )(logits, tri)


MOE_FF_SPLIT = 4
ROW_UNROLL = 8


def _moe_kernel(idx_ref, hn_hbm, wg_ref, wu_ref, wd_ref, gate_ref, x1_hbm, out_hbm,
                xg0, xg1, acc0, acc1, y0, y1, sem_x, sem_a, sem_s, *, cap, seq):
    del x1_hbm
    e, f = pl.program_id(0), pl.program_id(1)
    n_e = pl.num_programs(0)
    xg, acc = (xg0, xg1), (acc0, acc1)
    prev_e, next_e = jnp.maximum(e - 1, 0), jnp.minimum(e + 1, n_e - 1)

    def row_of(ee, b):
        base = (b * n_e + ee) * cap
        return lambda p: b * seq + idx_ref[base + p]

    def copy_rows(ee, b, kind, rolled=False):
        row = row_of(ee, b)

        def one(p):
            if kind == "xg":
                src, dst, sem = hn_hbm.at[pl.ds(row(p), 1), :], xg[b].at[pl.ds(p, 1), :], sem_x.at[b]
            elif kind == "ag":
                src, dst, sem = out_hbm.at[pl.ds(row(p), 1), :], acc[b].at[pl.ds(p, 1), :], sem_a.at[b]
            else:
                src, dst, sem = acc[b].at[pl.ds(p, 1), :], out_hbm.at[pl.ds(row(p), 1), :], sem_s.at[b]
            pltpu.make_async_copy(src, dst, sem).start()

        if rolled:
            def body(i, c):
                for k in range(ROW_UNROLL):
                    one(i * ROW_UNROLL + k)
                return c
            lax.fori_loop(0, cap // ROW_UNROLL, body, 0)
        else:
            for p in range(cap):
                one(p)

    def wait_rows(buf, sem):
        pltpu.make_async_copy(hn_hbm.at[pl.ds(0, cap), :], buf, sem).wait()

    def partial_out(b, wg, wu, wd):
        x = xg[b][...].astype(BF16)
        a = jnp.dot(x, wg, preferred_element_type=F32)
        u = jnp.dot(x, wu, preferred_element_type=F32)
        hmid = (a * (1.0 / (1.0 + jnp.exp(-a))) * u).astype(BF16)
        return jnp.dot(hmid, wd, preferred_element_type=F32)

    def weights():
        return wg_ref[0].astype(BF16), wu_ref[0].astype(BF16), wd_ref[0].astype(BF16)

    @pl.when((e == 0) & (f == 0))
    def _():
        copy_rows(e, 0, "xg", rolled=True)
        copy_rows(e, 0, "ag", rolled=True)
        copy_rows(e, 1, "ag", rolled=True)
        wait_rows(acc0, sem_a.at[0])
        wait_rows(acc1, sem_a.at[1])

    @pl.when(f == 0)
    def _():
        w = weights()
        wait_rows(xg0, sem_x.at[0])
        copy_rows(e, 1, "xg")
        y0[...] = partial_out(0, *w)
        wait_rows(xg1, sem_x.at[1])
        copy_rows(prev_e, 0, "sc")
        y1[...] = partial_out(1, *w)

    @pl.when(f == 1)
    def _():
        w = weights()
        copy_rows(prev_e, 1, "sc")
        y0[...] += partial_out(0, *w)
        wait_rows(acc0, sem_s.at[0])
        copy_rows(e, 0, "ag")
        y1[...] += partial_out(1, *w)

    @pl.when(f == 2)
    def _():
        w = weights()
        wait_rows(acc1, sem_s.at[1])
        copy_rows(e, 1, "ag")
        y0[...] += partial_out(0, *w)
        y1[...] += partial_out(1, *w)

    @pl.when(f == 3)
    def _():
        w = weights()
        t0 = y0[...] + partial_out(0, *w)
        wait_rows(acc0, sem_a.at[0])
        acc0[...] += t0 * gate_ref[0, 0]
        copy_rows(next_e, 0, "xg")
        t1 = y1[...] + partial_out(1, *w)
        wait_rows(acc1, sem_a.at[1])
        acc1[...] += t1 * gate_ref[1, 0]

    @pl.when((e == n_e - 1) & (f == MOE_FF_SPLIT - 1))
    def _():
        copy_rows(e, 0, "sc", rolled=True)
        copy_rows(e, 1, "sc", rolled=True)
        wait_rows(acc0, sem_s.at[0])
        wait_rows(acc1, sem_s.at[1])
        wait_rows(xg0, sem_x.at[0])


def _moe(idx, gate, hn, x1, wg, wu, wd, batch, seq):
    n_e, d, ff = wg.shape
    cap = idx.shape[-1]
    assert batch == 2 and MOE_FF_SPLIT == 4 and ff % MOE_FF_SPLIT == 0
    fq = ff // MOE_FF_SPLIT
    grid_spec = pltpu.PrefetchScalarGridSpec(
        num_scalar_prefetch=1,
        grid=(n_e, MOE_FF_SPLIT),
        in_specs=[
            pl.BlockSpec(memory_space=pl.ANY),
            pl.BlockSpec((1, d, fq), lambda i, j, s: (i, 0, j)),
            pl.BlockSpec((1, d, fq), lambda i, j, s: (i, 0, j)),
            pl.BlockSpec((1, fq, d), lambda i, j, s: (i, j, 0)),
            pl.BlockSpec((batch, 1, cap, 1), lambda i, j, s: (0, i, 0, 0)),
            pl.BlockSpec(memory_space=pl.ANY),
        ],
        out_specs=pl.BlockSpec(memory_space=pl.ANY),
        scratch_shapes=(
            [pltpu.VMEM((cap, d), F32) for _ in range(6)]
            + [pltpu.SemaphoreType.DMA((batch,)) for _ in range(3)]),
    )
    return pl.pallas_call(
        functools.partial(_moe_kernel, cap=cap, seq=seq),
        grid_spec=grid_spec,
        out_shape=jax.ShapeDtypeStruct(x1.shape, x1.dtype),
        input_output_aliases={6: 0},
        compiler_params=pltpu.CompilerParams(
            dimension_semantics=("arbitrary", "arbitrary"),
            vmem_limit_bytes=VMEM_LIMIT),
        name="moe_experts",
    )(idx.reshape(-1), hn, wg, wu, wd, gate[..., None], x1)


FFT_N1 = 128
FFT_N2 = 64
FFT_H1 = FFT_N1 // 2
K1_PAD = 72
Z_PITCH = 72
AB_PITCH = 136
HY_CB = 256
FFT_UNROLL = 8
HP = lax.Precision.HIGHEST


def _dft_constants():
    n, n1, n2, h1 = FFT_N1 * FFT_N2, FFT_N1, FFT_N2, FFT_H1
    k1 = np.arange(h1 + 1, dtype=np.float64)
    s1 = np.arange(h1, dtype=np.float64)
    s2 = np.arange(n2, dtype=np.float64)
    tw = s2[:, None, None] * k1[None, :, None] / n

    def stage1(phase_s1):
        th = -2.0 * np.pi * (phase_s1 + tw)
        m = np.zeros((n2, 2 * K1_PAD, h1))
        m[:, :h1 + 1] = np.cos(th)
        m[:, K1_PAD:K1_PAD + h1 + 1] = np.sin(th)
        return m

    f1 = stage1(s1[None, None, :] * k1[None, :, None] / n1)
    f1b = stage1(((n1 - 1) - s1)[None, None, :] * k1[None, :, None] / n1)
    f1b0 = stage1((n1 - s1)[None, None, :] * k1[None, :, None] / n1)[0]
    f1b0[:, 0] = 0.0
    f1b[0] = f1b0

    a = 2.0 * np.pi * np.outer(np.arange(n2), np.arange(n2)) / n2
    c, s = np.cos(a), np.sin(a)
    f2 = np.block([[c, s], [-s, c]])
    f2i = np.block([[c, -s], [s, c]])

    t1 = np.arange(h1, dtype=np.float64)
    th = 2.0 * np.pi * (t1[None, :, None] * k1[None, None, :] / n1 + s2[:, None, None] * k1[None, None, :] / n)
    wgt = np.where((k1 == 0) | (k1 == h1), 1.0, 2.0) / n
    g = np.zeros((n2, h1, 2 * K1_PAD))
    g[:, :, :h1 + 1] = wgt * np.cos(th)
    g[:, :, K1_PAD:K1_PAD + h1 + 1] = -wgt * np.sin(th)
    return tuple(jnp.asarray(m, dtype=F32).astype(BF16) for m in (f1, f1b, f2, f2i, g))


def _hid_kernel(f_ref, w1t_ref, w1c_ref, w1s_ref, b1_ref, w2_ref, b2_ref, fr_ref, o_ref, *, length):
    bm = o_ref.shape[0]
    row = (pl.program_id(0) * bm + lax.broadcasted_iota(jnp.int32, (bm, 1), 0)).astype(F32)
    t = row / (length - 1.0)
    ang = f_ref[...] * (2.0 * math.pi * row / length)
    pre = (t * w1t_ref[...]
           + jnp.dot(jnp.cos(ang), w1c_ref[...], precision=HP, preferred_element_type=F32)
           - jnp.dot(jnp.sin(ang), w1s_ref[...], precision=HP, preferred_element_type=F32))
    fr = fr_ref[...]
    hid = jnp.sin(fr * (pre + b1_ref[...]))
    hid = jnp.sin(fr * (jnp.dot(hid, w2_ref[...], precision=HP, preferred_element_type=F32) + b2_ref[...]))
    hi = hid.astype(BF16).astype(F32)
    lo = hid - hi
    o_ref[...] = jnp.concatenate([hi, hi, lo, jnp.zeros_like(hi)], axis=1).astype(BF16)


def _filter_hidden(length, w1, b1, w2, b2, freq, bm=512):
    bands = (FILTER_EMB - 1) // 2
    f = jnp.linspace(1e-4, bands - 1, bands, dtype=F32)[None, :]
    hdim = w1.shape[1]
    full = lambda a: pl.BlockSpec(a.shape, lambda i: (0,) * a.ndim)
    args = (f, w1[:1], w1[1:1 + bands], w1[1 + bands:], b1[None], w2, b2[None], freq[None])
    return pl.pallas_call(
        functools.partial(_hid_kernel, length=length),
        grid=(length // bm,),
        in_specs=[full(a) for a in args],
        out_specs=pl.BlockSpec((bm, 4 * hdim), lambda i: (i, 0)),
        out_shape=jax.ShapeDtypeStruct((length, 4 * hdim), BF16),
        name="filter_hidden",
    )(*args)


def _ld(ref, start, size, stride=None):
    idx = pl.ds(start, size) if stride is None else pl.ds(start, size, stride=stride)
    return jnp.concatenate([ref[h, idx, :] for h in range(ref.shape[0])], axis=-1)


def _st(ref, start, size, val, stride=None):
    idx = pl.ds(start, size) if stride is None else pl.ds(start, size, stride=stride)
    for h in range(ref.shape[0]):
        ref[h, idx, :] = val[:, h * LANES:(h + 1) * LANES]


def _stage1_store(ab_ref, s2, a):
    _st(ab_ref, s2, K1_PAD, a[:K1_PAD], stride=AB_PITCH)
    _st(ab_ref, FFT_N2 + s2, K1_PAD, a[K1_PAD:], stride=AB_PITCH)


def _stack_split_rows(w):
    hi = w.astype(BF16)
    lo = (w - hi.astype(F32)).astype(BF16)
    return jnp.concatenate([hi, lo, hi, jnp.zeros_like(hi)], axis=0)


def _spectrum_kernel(hid_ref, w3f_ref, w3b_ref, dl_ref, skip_ref, f1_ref, f2_ref, o_ref,
                     hf_ref, hb_ref, ab_ref, *, length):
    blocks = 4
    rows = blocks * FFT_N2
    w3f, w3b = _stack_split_rows(w3f_ref[...]), _stack_split_rows(w3b_ref[...])

    def gen(j, c):
        row = (j * rows + lax.broadcasted_iota(jnp.int32, (rows, 1), 0)).astype(F32)
        win = jnp.exp(-(row / (length - 1.0)) * dl_ref[...])
        hid = hid_ref[pl.ds(pl.multiple_of(j * rows, rows), rows), :]
        hf = jnp.dot(hid, w3f, preferred_element_type=F32) * win
        hb = jnp.dot(hid, w3b, preferred_element_type=F32) * win
        for q in range(blocks):
            dst = pl.multiple_of((j * blocks + q) * Z_PITCH, 8)
            _st(hf_ref, dst, FFT_N2, hf[q * FFT_N2:(q + 1) * FFT_N2])
            _st(hb_ref, dst, FFT_N2, hb[q * FFT_N2:(q + 1) * FFT_N2])
        return c

    lax.fori_loop(0, FFT_H1 // blocks, gen, 0)

    def stage1(s2, c):
        xf = _ld(hf_ref, s2, FFT_H1, Z_PITCH).astype(BF16)
        xb = _ld(hb_ref, (FFT_N2 - s2) % FFT_N2, FFT_H1, Z_PITCH).astype(BF16)
        x = jnp.concatenate([xf, xb], axis=0)
        _stage1_store(ab_ref, s2, jnp.dot(f1_ref[s2], x, preferred_element_type=F32))
        return c

    lax.fori_loop(0, FFT_N2, stage1, 0, unroll=2 * FFT_UNROLL)

    def stage2(k1, c):
        a = _ld(ab_ref, pl.multiple_of(k1 * AB_PITCH, 8), 2 * FFT_N2).astype(BF16)
        x = jnp.dot(f2_ref[...], a, preferred_element_type=F32)
        o_ref[0, k1, :FFT_N2, :] = (x[:FFT_N2] + skip_ref[0]).astype(BF16)
        o_ref[0, k1, FFT_N2:, :] = x[FFT_N2:].astype(BF16)
        return c

    lax.fori_loop(0, FFT_H1 + 1, stage2, 0, unroll=FFT_UNROLL)


def _filter_spectrum(hid, w3, deltas, filt_bias, consts, length):
    f1, f1b, f2, _, _ = consts
    f1 = jnp.concatenate([f1, f1b], axis=2)
    cb = HY_CB
    ncb = HYENA_WIDTH // cb
    hdim = w3.shape[0]
    full = lambda a: pl.BlockSpec(a.shape, lambda o, c: (0,) * a.ndim)
    return pl.pallas_call(
        functools.partial(_spectrum_kernel, length=length),
        grid=(HYENA_ORDER, ncb),
        in_specs=[
            full(hid),
            pl.BlockSpec((hdim, cb), lambda o, c: (0, (o * N_DIRS) * ncb + c)),
            pl.BlockSpec((hdim, cb), lambda o, c: (0, (o * N_DIRS + 1) * ncb + c)),
            pl.BlockSpec((1, cb), lambda o, c: (0, c)),
            pl.BlockSpec((1, 1, cb), lambda o, c: (o, 0, c)),
            full(f1), full(f2),
        ],
        out_specs=pl.BlockSpec((1, FFT_H1 + 1, 2 * FFT_N2, cb), lambda o, c: (o, 0, 0, c)),
        out_shape=jax.ShapeDtypeStruct((HYENA_ORDER, FFT_H1 + 1, 2 * FFT_N2, HYENA_WIDTH), BF16),
        scratch_shapes=[
            pltpu.VMEM((cb // LANES, FFT_H1 * Z_PITCH, LANES), F32),
            pltpu.VMEM((cb // LANES, FFT_H1 * Z_PITCH, LANES), F32),
            pltpu.VMEM((cb // LANES, K1_PAD * AB_PITCH, LANES), F32),
        ],
        compiler_params=pltpu.CompilerParams(
            dimension_semantics=("arbitrary", "arbitrary"),
            vmem_limit_bytes=VMEM_LIMIT),
        name="filter_spectrum",
    )(hid, w3, w3, deltas, filt_bias[:, None, :], f1, f2)


def _short_conv_block(s_ref, j, w_ref, b_ref):
    big = s_ref[pl.ds(pl.multiple_of(j * FFT_N2, FFT_N2), FFT_N2 + 16), :]
    rows = FFT_N2 + 16
    prev = pltpu.roll(big, 1, 0)[8:8 + FFT_N2]
    nxt = pltpu.roll(big, rows - 1, 0)[8:8 + FFT_N2]
    cur = big[8:8 + FFT_N2]
    return b_ref[...] + prev * w_ref[0:1, :] + cur * w_ref[1:2, :] + nxt * w_ref[2:3, :]


def _load_stream(s_ref, src_ref, length):
    zeros = jnp.zeros((8, s_ref.shape[1]), F32)
    s_ref[0:8, :] = zeros
    s_ref[length + 8:length + 16, :] = zeros
    chunk = 512

    def cp(i, c):
        r = pl.multiple_of(i * chunk, chunk)
        s_ref[pl.ds(r + 8, chunk), :] = src_ref[pl.ds(r, chunk), :].astype(F32)
        return c

    lax.fori_loop(0, length // chunk, cp, 0)


def _hyena_kernel(z_ref, gs_ref, cwz_ref, cbz_ref, cwg_ref, cbg_ref, h_ref, f1_ref, f2_ref, f2i_ref, g_ref,
                  og_ref, o_ref, s_ref, zy_ref, ab_ref, *, length, conv_z, norm):
    if conv_z:
        _load_stream(s_ref, z_ref, length)

    def prep(j, c):
        if conv_z:
            zblk = _short_conv_block(s_ref, j, cwz_ref, cbz_ref)
        else:
            zblk = z_ref[pl.ds(pl.multiple_of(j * FFT_N2, FFT_N2), FFT_N2), :].astype(F32)
        _st(zy_ref, pl.multiple_of(j * Z_PITCH, 8), FFT_N2, zblk)
        return c

    lax.fori_loop(0, FFT_H1, prep, 0)

    def stage1(s2, c):
        xs = _ld(zy_ref, s2, FFT_H1, Z_PITCH).astype(BF16)
        _stage1_store(ab_ref, s2, jnp.dot(f1_ref[s2], xs, preferred_element_type=F32))
        return c

    lax.fori_loop(0, FFT_N2, stage1, 0, unroll=2 * FFT_UNROLL)

    def stage2(k1, c):
        blk = pl.multiple_of(k1 * AB_PITCH, 8)
        x = jnp.dot(f2_ref[...], _ld(ab_ref, blk, 2 * FFT_N2).astype(BF16), preferred_element_type=F32)
        hk = h_ref[0, k1].astype(F32)
        xr, xi, hr, hi = x[:FFT_N2], x[FFT_N2:], hk[:FFT_N2], hk[FFT_N2:]
        p = jnp.concatenate([xr * hr - xi * hi, xr * hi + xi * hr], axis=0).astype(BF16)
        _st(ab_ref, blk, 2 * FFT_N2, jnp.dot(f2i_ref[...], p, preferred_element_type=F32))
        return c

    lax.fori_loop(0, FFT_H1 + 1, stage2, 0, unroll=FFT_UNROLL)

    def stage3(t2, c):
        br = _ld(ab_ref, t2, K1_PAD, AB_PITCH)
        bi = _ld(ab_ref, FFT_N2 + t2, K1_PAD, AB_PITCH)
        rhs = jnp.concatenate([br, bi], axis=0).astype(BF16)
        _st(zy_ref, t2, FFT_H1, jnp.dot(g_ref[t2], rhs, preferred_element_type=F32), stride=Z_PITCH)
        return c

    lax.fori_loop(0, FFT_N2, stage3, 0, unroll=2 * FFT_UNROLL)

    _load_stream(s_ref, gs_ref, length)

    def fin(j, c):
        gate = _short_conv_block(s_ref, j, cwg_ref, cbg_ref)
        y = gate * _ld(zy_ref, pl.multiple_of(j * Z_PITCH, 8), FFT_N2)
        dst = pl.ds(pl.multiple_of(j * FFT_N2, FFT_N2), FFT_N2)
        if norm:
            gw = HYENA_WIDTH // N_HYENA_GROUPS
            for q in range(y.shape[1] // gw):
                sl = slice(q * gw, (q + 1) * gw)
                o_ref[dst, sl] = (_rms(y[:, sl]) * og_ref[:, sl]).astype(o_ref.dtype)
        else:
            o_ref[dst, :] = y.astype(o_ref.dtype)
        return c

    lax.fori_loop(0, FFT_H1, fin, 0)


def _hyena_order(zsrc, zcol, proj, order, conv_w, conv_b, hspec, consts, out_g, batch, length, conv_z, norm):
    f1, _, f2, f2i, g = consts
    cb = HY_CB
    ncb = HYENA_WIDTH // cb
    pcol = 3 * ATTN_WIDTH // cb
    zconv = HYENA_ORDER * ncb
    full = lambda a: pl.BlockSpec(a.shape, lambda c, b: (0,) * a.ndim)
    kern = functools.partial(_hyena_kernel, length=length, conv_z=conv_z, norm=norm)
    return pl.pallas_call(
        kern,
        grid=(ncb, batch),
        in_specs=[
            pl.BlockSpec((length, cb), lambda c, b: (b, zcol + c)),
            pl.BlockSpec((length, cb), lambda c, b: (b, pcol + order * ncb + c)),
            pl.BlockSpec((SHORT_CONV_W, cb), lambda c, b: (0, zconv + c)),
            pl.BlockSpec((1, cb), lambda c, b: (0, zconv + c)),
            pl.BlockSpec((SHORT_CONV_W, cb), lambda c, b: (0, order * ncb + c)),
            pl.BlockSpec((1, cb), lambda c, b: (0, order * ncb + c)),
            pl.BlockSpec((1, FFT_H1 + 1, 2 * FFT_N2, cb), lambda c, b: (order, 0, 0, c)),
            full(f1), full(f2), full(f2i), full(g),
            pl.BlockSpec((1, cb), lambda c, b: (0, c)),
        ],
        out_specs=pl.BlockSpec((length, cb), lambda c, b: (b, c)),
        out_shape=jax.ShapeDtypeStruct((batch * length, HYENA_WIDTH), BF16),
        scratch_shapes=[
            pltpu.VMEM((length + 16, cb), F32),
            pltpu.VMEM((cb // LANES, FFT_H1 * Z_PITCH, LANES), F32),
            pltpu.VMEM((cb // LANES, K1_PAD * AB_PITCH, LANES), F32),
        ],
        compiler_params=pltpu.CompilerParams(
            dimension_semantics=("arbitrary", "arbitrary"),
            vmem_limit_bytes=VMEM_LIMIT),
        name=f"hyena_order{order}",
    )(zsrc, proj, conv_w, conv_b[None], conv_w, conv_b[None], hspec, f1, f2, f2i, g, out_g)


def _hyena(proj, conv_w, conv_b, w1, b1, w2, b2, w3, freq, filt_bias, out_g, batch, length):
    assert 2 * length == FFT_N1 * FFT_N2 and HYENA_ORDER == 2
    consts = _dft_constants()
    max_decay = math.log(DECAY_TARGET) / FAST_DECAY_PCT
    min_decay = math.log(DECAY_TARGET) / SLOW_DECAY_PCT
    deltas = jnp.abs(jnp.linspace(min_decay, max_decay, HYENA_WIDTH, dtype=F32))[None]
    hid = _filter_hidden(length, w1, b1, w2, b2, freq)
    hspec = _filter_spectrum(hid, w3, deltas, filt_bias, consts, length)
    vcol = (3 * ATTN_WIDTH + HYENA_ORDER * HYENA_WIDTH) // HY_CB
    z1 = _hyena_order(proj, vcol, proj, 0, conv_w, conv_b, hspec, consts, out_g, batch, length, True, False)
    return _hyena_order(z1, 0, proj, 1, conv_w, conv_b, hspec, consts, out_g, batch, length, False, True)


def kernel(x, mix_norm_g, w_in, q_norm_g, k_norm_g, rpb, conv_w, conv_b, filt_w1, filt_b1, filt_w2, filt_b2, filt_w3, filt_freq, filt_bias, attn_out_g, hyena_out_g, w_out, ffn_norm_g, w_router, w_gate, w_up, w_down):
    b, s, d = x.shape
    rows = s // GRID_W
    win_r = min(WIN_ROWS_MAX, rows)
    cap = EC_CAPACITY_FACTOR * s // N_EXPERTS
    x2 = x.reshape(b * s, d)
    for i in range(mix_norm_g.shape[0]):
        proj = _inproj(x2, mix_norm_g[i][None], w_in[i],
                       q_norm_g[i][None], k_norm_g[i][None])
        attn = _attention(proj, rpb[i], attn_out_g[i][None], b, s)

        hyn = _hyena(proj, conv_w[i], conv_b[i], filt_w1[i], filt_b1[i], filt_w2[i], filt_b2[i], filt_w3[i],
                     filt_freq[i], filt_bias[i], hyena_out_g[i][None], b, s)

        wr_pad = jnp.zeros((d, 128), BF16).at[:, :N_EXPERTS].set(w_router[i].astype(BF16))
        x1, hn, logits = _outproj(attn, hyn, x2, w_out[i].astype(BF16), ffn_norm_g[i][None], wr_pad)

        idx, gate = _route(logits, b, s, cap)
        x2 = _moe(idx, gate, hn, x1, w_gate[i], w_up[i], w_down[i], b, s)
    return x2.reshape(b, s, d)
```

```python
import functools
import math

import numpy as np
import jax
import jax.numpy as jnp
from jax import lax
from jax.experimental import pallas as pl
from jax.experimental.pallas import tpu as pltpu

D_MODEL = 2048
GRID_W = 64
N_ATTN_HEADS = 8
ATTN_WIDTH = D_MODEL // 2
HEAD_DIM = ATTN_WIDTH // N_ATTN_HEADS
HYENA_WIDTH = D_MODEL - ATTN_WIDTH
N_HYENA_GROUPS = 8
HYENA_ORDER = 2
N_DIRS = 2
SHORT_CONV_W = 3
FILTER_EMB = 33
DECAY_TARGET = 1e-2
FAST_DECAY_PCT = 0.3
SLOW_DECAY_PCT = 1.5
WIN_ROWS_MAX = 8
WIN_COLS = 16
N_EXPERTS = 16
EC_CAPACITY_FACTOR = 2
EXPERT_FF = 1024
IN_WIDTH = 3 * ATTN_WIDTH + (HYENA_ORDER + 1) * HYENA_WIDTH
RMS_EPS = 1e-6

F32 = jnp.float32
BF16 = jnp.bfloat16
NEG_BIG = -1e30
VMEM_LIMIT = 56 * 1024 * 1024
LANES = 128


def _rms(x, eps=RMS_EPS):
    return x * lax.rsqrt(jnp.mean(x * x, axis=-1, keepdims=True) + eps)


INPROJ_GROUPS = 2


def _inproj_kernel(x_ref, g_ref, w_ref, qg_ref, kg_ref, o_ref, h_ref):
    n = pl.program_id(1)
    bm, bn = o_ref.shape

    def step(normalise_input, head_gain):
        w = w_ref[...].astype(BF16)
        for r in range(INPROJ_GROUPS):
            rows = slice(r * bm // INPROJ_GROUPS, (r + 1) * bm // INPROJ_GROUPS)
            if normalise_input:
                h_ref[rows, :] = (_rms(x_ref[rows, :]) * g_ref[...]).astype(BF16)
            acc = jnp.dot(h_ref[rows, :], w, preferred_element_type=F32)
            if head_gain is None:
                o_ref[rows, :] = acc.astype(BF16)
            else:
                for j in range(bn // HEAD_DIM):
                    sl = slice(j * HEAD_DIM, (j + 1) * HEAD_DIM)
                    o_ref[rows, sl] = (_rms(acc[:, sl]) * head_gain).astype(BF16)

    @pl.when(n == 0)
    def _():
        step(True, qg_ref[...] * (HEAD_DIM ** -0.5))

    @pl.when(n == 1)
    def _():
        step(False, kg_ref[...])

    @pl.when(n >= 2)
    def _():
        step(False, None)


def _inproj(x2, g, w_bf, qg, kg, bm=1024, bn=1024):
    m, d = x2.shape
    nw = w_bf.shape[1]
    return pl.pallas_call(
        _inproj_kernel,
        grid=(m // bm, nw // bn),
        in_specs=[
            pl.BlockSpec((bm, d), lambda i, j: (i, 0)),
            pl.BlockSpec((1, d), lambda i, j: (0, 0)),
            pl.BlockSpec((d, bn), lambda i, j: (0, j)),
            pl.BlockSpec((1, HEAD_DIM), lambda i, j: (0, 0)),
            pl.BlockSpec((1, HEAD_DIM), lambda i, j: (0, 0)),
        ],
        out_specs=pl.BlockSpec((bm, bn), lambda i, j: (i, j)),
        out_shape=jax.ShapeDtypeStruct((m, nw), BF16),
        scratch_shapes=[pltpu.VMEM((bm, d), BF16)],
        compiler_params=pltpu.CompilerParams(
            dimension_semantics=("arbitrary", "arbitrary"),
            vmem_limit_bytes=VMEM_LIMIT),
        name="inproj",
    )(x2, g, w_bf, qg, kg)


ATTN_ROWS = 4
ATTN_BAND = 12


def _attn_kernel(q_ref, k_ref, v_ref, rpb_ref, mask_ref, g_ref, o_ref, pt_ref, *, rows, win_r):
    lanes = pt_ref.shape[-1]
    n_dr = 2 * WIN_ROWS_MAX - 1

    def toeplitz(d):
        row = jnp.broadcast_to(rpb_ref[0, d:d + 1, :], (GRID_W, lanes))
        return pltpu.roll(row, lanes - (WIN_COLS - 1), 1, stride=1, stride_axis=0)

    lane = lax.broadcasted_iota(jnp.int32, (GRID_W, lanes), 1)
    left = lane < GRID_W
    tiles = [toeplitz(d) for d in range(n_dr)]
    for d in range(-1, n_dr):
        lo, hi = tiles[max(d, 0)], tiles[min(d + 1, n_dr - 1)]
        pt_ref[d + 1] = jnp.where(left, lo, pltpu.roll(hi, GRID_W, 1)) + mask_ref[...]

    def body(i, carry):
        r0 = i * ATTN_ROWS
        bs = jnp.minimum(jnp.clip(r0 - win_r // 2, 0, rows - win_r), rows - ATTN_BAND)
        q0 = pl.multiple_of(r0 * GRID_W, ATTN_ROWS * GRID_W)
        k0 = pl.multiple_of(bs * GRID_W, 2 * GRID_W)
        q = q_ref[pl.ds(q0, ATTN_ROWS * GRID_W), :]
        k = k_ref[pl.ds(k0, ATTN_BAND * GRID_W), :]
        v = v_ref[pl.ds(k0, ATTN_BAND * GRID_W), :]
        s = lax.dot_general(q, k, (((1,), (1,)), ((), ())), preferred_element_type=F32)

        bias_rows = []
        for j in range(ATTN_ROWS):
            r = r0 + j
            start = jnp.clip(r - win_r // 2, 0, rows - win_r)
            pieces = []
            for m in range(ATTN_BAND // 2):
                kr = bs + 2 * m
                pen_l = jnp.where((kr >= start) & (kr < start + win_r), 0.0, NEG_BIG)
                pen_r = jnp.where((kr + 1 >= start) & (kr + 1 < start + win_r), 0.0, NEG_BIG)
                d = jnp.clip(kr - r + WIN_ROWS_MAX - 1, -1, n_dr - 1)
                pieces.append(pt_ref[d + 1] + jnp.where(left, pen_l, pen_r))
            bias_rows.append(jnp.concatenate(pieces, axis=1))
        s = s + jnp.concatenate(bias_rows, axis=0)

        mx = jnp.max(s, axis=-1, keepdims=True)
        p = jnp.exp(s - mx)
        l = jnp.sum(p, axis=-1, keepdims=True)
        o = jnp.dot(p.astype(BF16), v, preferred_element_type=F32) / l
        o = _rms(o) * g_ref[...]
        o_ref[pl.ds(q0, ATTN_ROWS * GRID_W), :] = o.astype(BF16)
        return carry

    lax.fori_loop(0, rows // ATTN_ROWS, body, 0, unroll=2)


def _attn_window_mask():
    c = np.arange(GRID_W)
    c0 = np.clip(c - WIN_COLS // 2, 0, GRID_W - WIN_COLS)
    in_win = (c[None, :] >= c0[:, None]) & (c[None, :] < c0[:, None] + WIN_COLS)
    return jnp.asarray(np.tile(np.where(in_win, 0.0, NEG_BIG), (1, 2)), dtype=F32)


def _attention(proj, rpb, out_g, batch, seq):
    rows = seq // GRID_W
    win_r = min(WIN_ROWS_MAX, rows)
    assert win_r == WIN_ROWS_MAX and 2 * GRID_W == LANES
    assert rows % ATTN_ROWS == 0 and ATTN_BAND >= win_r + ATTN_ROWS - 1 and ATTN_BAND % 2 == 0
    h = N_ATTN_HEADS
    n_dr, n_dc = rpb.shape[1:]
    rpb_pad = jnp.zeros((h, 2 * WIN_ROWS_MAX, LANES), F32).at[:, :n_dr, :n_dc].set(rpb)
    mask = _attn_window_mask()
    kern = functools.partial(_attn_kernel, rows=rows, win_r=win_r)
    return pl.pallas_call(
        kern,
        grid=(batch, h),
        in_specs=[
            pl.BlockSpec((seq, HEAD_DIM), lambda b, i: (b, i)),
            pl.BlockSpec((seq, HEAD_DIM), lambda b, i: (b, h + i)),
            pl.BlockSpec((seq, HEAD_DIM), lambda b, i: (b, 2 * h + i)),
            pl.BlockSpec((1, 2 * WIN_ROWS_MAX, LANES), lambda b, i: (i, 0, 0)),
            pl.BlockSpec(mask.shape, lambda b, i: (0, 0)),
            pl.BlockSpec((1, HEAD_DIM), lambda b, i: (0, i)),
        ],
        out_specs=pl.BlockSpec((seq, HEAD_DIM), lambda b, i: (b, i)),
        out_shape=jax.ShapeDtypeStruct((batch * seq, ATTN_WIDTH), BF16),
        scratch_shapes=[pltpu.VMEM((2 * WIN_ROWS_MAX, GRID_W, LANES), F32)],
        compiler_params=pltpu.CompilerParams(
            dimension_semantics=("arbitrary", "arbitrary"),
            vmem_limit_bytes=VMEM_LIMIT),
        name="na2d_attn",
    )(proj, proj, proj, rpb_pad, mask, out_g)


OUTPROJ_GROUPS = 2


def _outproj_kernel(a_ref, hy_ref, x_ref, wo_ref, g_ref, wr_ref, x1_ref, hn_ref, lg_ref):
    half = a_ref.shape[1]
    bm = x_ref.shape[0]
    for r in range(OUTPROJ_GROUPS):
        rows = slice(r * bm // OUTPROJ_GROUPS, (r + 1) * bm // OUTPROJ_GROUPS)
        acc = x_ref[rows, :]
        acc = acc + jnp.dot(a_ref[rows, :], wo_ref[:half, :], preferred_element_type=F32)
        acc = acc + jnp.dot(hy_ref[rows, :], wo_ref[half:, :], preferred_element_type=F32)
        x1_ref[rows, :] = acc
        hn = _rms(acc) * g_ref[...]
        hn_ref[rows, :] = hn
        lg_ref[rows, :] = jnp.dot(hn.astype(BF16), wr_ref[...], preferred_element_type=F32)


def _outproj(attn, hy, x2, wo_bf, g, wr_pad, bm=512):
    m, d = x2.shape
    half = attn.shape[1]
    npad = wr_pad.shape[1]
    return pl.pallas_call(
        _outproj_kernel,
        grid=(m // bm,),
        in_specs=[
            pl.BlockSpec((bm, half), lambda i: (i, 0)),
            pl.BlockSpec((bm, half), lambda i: (i, 0)),
            pl.BlockSpec((bm, d), lambda i: (i, 0)),
            pl.BlockSpec((d, d), lambda i: (0, 0)),
            pl.BlockSpec((1, d), lambda i: (0, 0)),
            pl.BlockSpec((d, npad), lambda i: (0, 0)),
        ],
        out_specs=[
            pl.BlockSpec((bm, d), lambda i: (i, 0)),
            pl.BlockSpec((bm, d), lambda i: (i, 0)),
            pl.BlockSpec((bm, npad), lambda i: (i, 0)),
        ],
        out_shape=[
            jax.ShapeDtypeStruct((m, d), F32),
            jax.ShapeDtypeStruct((m, d), F32),
            jax.ShapeDtypeStruct((m, npad), F32),
        ],
        compiler_params=pltpu.CompilerParams(
            dimension_semantics=("arbitrary",),
            vmem_limit_bytes=VMEM_LIMIT),
        name="outproj",
    )(attn, hy, x2, wo_bf, g, wr_pad)


ROUTE_BLK = 128
ROUTE_CHUNK = 512
ROUTE_MATCH = 256
F32_INF_BITS = 0x7F800000


def _route_kernel(lg_ref, tri_ref, idx_ref, gate_ref, aff_ref, pos_ref, *, cap):
    s, lanes = lg_ref.shape
    lane = lax.broadcasted_iota(jnp.int32, (1, lanes), 1)

    def softmax_chunk(i, c):
        rows = pl.ds(pl.multiple_of(i * ROUTE_CHUNK, ROUTE_CHUNK), ROUTE_CHUNK)
        lg = jnp.where(lane < N_EXPERTS, lg_ref[rows, :], NEG_BIG)
        ex = jnp.exp(lg - jnp.max(lg, axis=-1, keepdims=True))
        aff_ref[rows, :] = ex / jnp.sum(ex, axis=-1, keepdims=True)
        return c

    lax.fori_loop(0, s // ROUTE_CHUNK, softmax_chunk, 0)

    def as_value(bits):
        return pltpu.bitcast(bits, F32)

    def fold8(x):
        parts = [x[i:i + 8] for i in range(0, x.shape[0], 8)]
        while len(parts) > 1:
            parts = [a + b for a, b in zip(parts[::2], parts[1::2])]
        return parts[0]

    def count_ge(th):
        def body(i, acc):
            rows = pl.ds(pl.multiple_of(i * ROUTE_CHUNK, ROUTE_CHUNK), ROUTE_CHUNK)
            return acc + fold8(jnp.where(aff_ref[rows, :] >= th, 1.0, 0.0))
        acc = lax.fori_loop(0, s // ROUTE_CHUNK, body, jnp.zeros((8, lanes), F32))
        return jnp.sum(acc, axis=0, keepdims=True)

    def bisect(_, c):
        lo, hi = c
        mid = lo + ((hi - lo) >> 1)
        ok = count_ge(as_value(mid)) >= cap
        return jnp.where(ok, mid, lo), jnp.where(ok, hi, mid)

    tau_bits, _ = lax.fori_loop(0, 31, bisect, (jnp.zeros((1, lanes), jnp.int32),
                                                jnp.full((1, lanes), F32_INF_BITS, jnp.int32)))
    tau, above = as_value(tau_bits), as_value(tau_bits + 1)
    ties_wanted = cap - count_ge(above)

    def scan_block(j, c):
        tie_carry, sel_carry = c
        rows = pl.ds(pl.multiple_of(j * ROUTE_BLK, ROUTE_BLK), ROUTE_BLK)
        aff = aff_ref[rows, :]
        gt = aff >= above
        eq = jnp.where((aff >= tau) & (aff < above), 1.0, 0.0)
        tie_incl = jnp.dot(tri_ref[...], eq.astype(BF16), preferred_element_type=F32) + tie_carry
        sel = jnp.where(gt | ((eq > 0.0) & (tie_incl - eq < ties_wanted)), 1.0, 0.0)
        sel_incl = jnp.dot(tri_ref[...], sel.astype(BF16), preferred_element_type=F32) + sel_carry
        pos_ref[rows, :] = jnp.where(sel > 0.0, sel_incl - sel, -1.0)
        return tie_incl[ROUTE_BLK - 1:, :], sel_incl[ROUTE_BLK - 1:, :]

    zero = jnp.zeros((1, lanes), F32)
    lax.fori_loop(0, s // ROUTE_BLK, scan_block, (zero, zero))

    slot = lax.broadcasted_iota(jnp.int32, (1, cap), 1).astype(F32)
    for e in range(N_EXPERTS):
        def compact(c, acc, e=e):
            ia, ga = acc
            r0 = pl.multiple_of(c * ROUTE_MATCH, ROUTE_MATCH)
            match = pos_ref[pl.ds(r0, ROUTE_MATCH), e:e + 1] == slot
            tok = (r0 + lax.broadcasted_iota(jnp.int32, (ROUTE_MATCH, 1), 0)).astype(F32)
            ia = ia + fold8(jnp.where(match, tok, 0.0))
            ga = ga + fold8(jnp.where(match, aff_ref[pl.ds(r0, ROUTE_MATCH), e:e + 1], 0.0))
            return ia, ga

        zc = jnp.zeros((8, cap), F32)
        ia, ga = lax.fori_loop(0, s // ROUTE_MATCH, compact, (zc, zc))
        idx_ref[0, e:e + 1, :] = jnp.sum(ia, axis=0, keepdims=True).astype(jnp.int32)
        gate_ref[0, e:e + 1, :] = jnp.sum(ga, axis=0, keepdims=True)


def _route(logits, batch, seq, cap):
    lanes = logits.shape[1]
    tri =jnp.asarray(np.tril(np.ones((ROUTE_BLK, ROUTE_BLK))), dtype=BF16)
    return pl.pallas_call(
        functools.partial(_route_kernel, cap=cap),
        grid=(batch,),
        in_specs=[
            pl.BlockSpec((seq, lanes), lambda b: (b, 0)),
            pl.BlockSpec((ROUTE_BLK, ROUTE_BLK), lambda b: (0, 0)),
        ],
        out_specs=[
            pl.BlockSpec((1, N_EXPERTS, cap), lambda b: (b, 0, 0)),
            pl.BlockSpec((1, N_EXPERTS, cap), lambda b: (b, 0, 0)),
        ],
        out_shape=[
            jax.ShapeDtypeStruct((batch, N_EXPERTS, cap), jnp.int32),
            jax.ShapeDtypeStruct((batch, N_EXPERTS, cap), F32),
        ],
        scratch_shapes=[
            pltpu.VMEM((seq, lanes), F32),
            pltpu.VMEM((seq, lanes), F32),
        ],
        compiler_params=pltpu.CompilerParams(dimension_semantics=("arbitrary",)),
        name="route",
    )(logits, tri)


MOE_FF_SPLIT = 4
ROW_UNROLL = 8


def _moe_kernel(idx_ref, hn_hbm, wg_ref, wu_ref, wd_ref, gate_ref, x1_hbm, out_hbm,
                xg0, xg1, acc0, acc1, y0, y1, sem_x, sem_a, sem_s, *, cap, seq):
    del x1_hbm
    e, f = pl.program_id(0), pl.program_id(1)
    n_e = pl.num_programs(0)
    xg, acc = (xg0, xg1), (acc0, acc1)
    prev_e, next_e = jnp.maximum(e - 1, 0), jnp.minimum(e + 1, n_e - 1)

    def row_of(ee, b):
        base = (b * n_e + ee) * cap
        return lambda p: b * seq + idx_ref[base + p]

    def copy_rows(ee, b, kind, rolled=False):
        row = row_of(ee, b)

        def one(p):
            if kind == "xg":
                src, dst, sem = hn_hbm.at[pl.ds(row(p), 1), :], xg[b].at[pl.ds(p, 1), :], sem_x.at[b]
            elif kind == "ag":
                src, dst, sem = out_hbm.at[pl.ds(row(p), 1), :], acc[b].at[pl.ds(p, 1), :], sem_a.at[b]
            else:
                src, dst, sem = acc[b].at[pl.ds(p, 1), :], out_hbm.at[pl.ds(row(p), 1), :], sem_s.at[b]
            pltpu.make_async_copy(src, dst, sem).start()

        if rolled:
            def body(i, c):
                for k in range(ROW_UNROLL):
                    one(i * ROW_UNROLL + k)
                return c
            lax.fori_loop(0, cap // ROW_UNROLL, body, 0)
        else:
            for p in range(cap):
                one(p)

    def wait_rows(buf, sem):
        pltpu.make_async_copy(hn_hbm.at[pl.ds(0, cap), :], buf, sem).wait()

    def partial_out(b, wg, wu, wd):
        x = xg[b][...].astype(BF16)
        a = jnp.dot(x, wg, preferred_element_type=F32)
        u = jnp.dot(x, wu, preferred_element_type=F32)
        hmid = (a * (1.0 / (1.0 + jnp.exp(-a))) * u).astype(BF16)
        return jnp.dot(hmid, wd, preferred_element_type=F32)

    def weights():
        return wg_ref[0].astype(BF16), wu_ref[0].astype(BF16), wd_ref[0].astype(BF16)

    @pl.when((e == 0) & (f == 0))
    def _():
        copy_rows(e, 0, "xg", rolled=True)
        copy_rows(e, 0, "ag", rolled=True)
        copy_rows(e, 1, "ag", rolled=True)
        wait_rows(acc0, sem_a.at[0])
        wait_rows(acc1, sem_a.at[1])

    @pl.when(f == 0)
    def _():
        w = weights()
        wait_rows(xg0, sem_x.at[0])
        copy_rows(e, 1, "xg")
        y0[...] = partial_out(0, *w)
        wait_rows(xg1, sem_x.at[1])
        copy_rows(prev_e, 0, "sc")
        y1[...] = partial_out(1, *w)

    @pl.when(f == 1)
    def _():
        w = weights()
        copy_rows(prev_e, 1, "sc")
        y0[...] += partial_out(0, *w)
        wait_rows(acc0, sem_s.at[0])
        copy_rows(e, 0, "ag")
        y1[...] += partial_out(1, *w)

    @pl.when(f == 2)
    def _():
        w = weights()
        wait_rows(acc1, sem_s.at[1])
        copy_rows(e, 1, "ag")
        y0[...] += partial_out(0, *w)
        y1[...] += partial_out(1, *w)

    @pl.when(f == 3)
    def _():
        w = weights()
        t0 = y0[...] + partial_out(0, *w)
        wait_rows(acc0, sem_a.at[0])
        acc0[...] += t0 * gate_ref[0, 0]
        copy_rows(next_e, 0, "xg")
        t1 = y1[...] + partial_out(1, *w)
        wait_rows(acc1, sem_a.at[1])
        acc1[...] += t1 * gate_ref[1, 0]

    @pl.when((e == n_e - 1) & (f == MOE_FF_SPLIT - 1))
    def _():
        copy_rows(e, 0, "sc", rolled=True)
        copy_rows(e, 1, "sc", rolled=True)
        wait_rows(acc0, sem_s.at[0])
        wait_rows(acc1, sem_s.at[1])
        wait_rows(xg0, sem_x.at[0])


def _moe(idx, gate, hn, x1, wg, wu, wd, batch, seq):
    n_e, d, ff = wg.shape
    cap = idx.shape[-1]
    assert batch == 2 and MOE_FF_SPLIT == 4 and ff % MOE_FF_SPLIT == 0
    fq = ff // MOE_FF_SPLIT
    grid_spec = pltpu.PrefetchScalarGridSpec(
        num_scalar_prefetch=1,
        grid=(n_e, MOE_FF_SPLIT),
        in_specs=[
            pl.BlockSpec(memory_space=pl.ANY),
            pl.BlockSpec((1, d, fq), lambda i, j, s: (i, 0, j)),
            pl.BlockSpec((1, d, fq), lambda i, j, s: (i, 0, j)),
            pl.BlockSpec((1, fq, d), lambda i, j, s: (i, j, 0)),
            pl.BlockSpec((batch, 1, cap, 1), lambda i, j, s: (0, i, 0, 0)),
            pl.BlockSpec(memory_space=pl.ANY),
        ],
        out_specs=pl.BlockSpec(memory_space=pl.ANY),
        scratch_shapes=(
            [pltpu.VMEM((cap, d), F32) for _ in range(6)]
            + [pltpu.SemaphoreType.DMA((batch,)) for _ in range(3)]),
    )
    return pl.pallas_call(
        functools.partial(_moe_kernel, cap=cap, seq=seq),
        grid_spec=grid_spec,
        out_shape=jax.ShapeDtypeStruct(x1.shape, x1.dtype),
        input_output_aliases={6: 0},
        compiler_params=pltpu.CompilerParams(
            dimension_semantics=("arbitrary", "arbitrary"),
            vmem_limit_bytes=VMEM_LIMIT),
        name="moe_experts",
    )(idx.reshape(-1), hn, wg, wu, wd, gate[..., None], x1)


FFT_N1 = 128
FFT_N2 = 64
FFT_H1 = FFT_N1 // 2
K1_PAD = 72
Z_PITCH = 72
AB_PITCH = 136
HY_CB = 256
FFT_UNROLL = 8
HP = lax.Precision.HIGHEST


def _dft_constants():
    n, n1, n2, h1 = FFT_N1 * FFT_N2, FFT_N1, FFT_N2, FFT_H1
    k1 = np.arange(h1 + 1, dtype=np.float64)
    s1 = np.arange(h1, dtype=np.float64)
    s2 = np.arange(n2, dtype=np.float64)
    tw = s2[:, None, None] * k1[None, :, None] / n

    def stage1(phase_s1):
        th = -2.0 * np.pi * (phase_s1 + tw)
        m = np.zeros((n2, 2 * K1_PAD, h1))
        m[:, :h1 + 1] = np.cos(th)
        m[:, K1_PAD:K1_PAD + h1 + 1] = np.sin(th)
        return m

    f1 = stage1(s1[None, None, :] * k1[None, :, None] / n1)
    f1b = stage1(((n1 - 1) - s1)[None, None, :] * k1[None, :, None] / n1)
    f1b0 = stage1((n1 - s1)[None, None, :] * k1[None, :, None] / n1)[0]
    f1b0[:, 0] = 0.0
    f1b[0] = f1b0

    a = 2.0 * np.pi * np.outer(np.arange(n2), np.arange(n2)) / n2
    c, s = np.cos(a), np.sin(a)
    f2 = np.block([[c, s], [-s, c]])
    f2i = np.block([[c, -s], [s, c]])

    t1 = np.arange(h1, dtype=np.float64)
    th = 2.0 * np.pi * (t1[None, :, None] * k1[None, None, :] / n1 + s2[:, None, None] * k1[None, None, :] / n)
    wgt = np.where((k1 == 0) | (k1 == h1), 1.0, 2.0) / n
    g = np.zeros((n2, h1, 2 * K1_PAD))
    g[:, :, :h1 + 1] = wgt * np.cos(th)
    g[:, :, K1_PAD:K1_PAD + h1 + 1] = -wgt * np.sin(th)
    return tuple(jnp.asarray(m, dtype=F32).astype(BF16) for m in (f1, f1b, f2, f2i, g))


def _hid_kernel(f_ref, w1t_ref, w1c_ref, w1s_ref, b1_ref, w2_ref, b2_ref, fr_ref, o_ref, *, length):
    bm = o_ref.shape[0]
    pos = (pl.program_id(0) * bm + lax.broadcasted_iota(jnp.int32, (1, bm), 1)).astype(F32)
    t = pos / (length - 1.0)
    ang = f_ref[...] * (2.0 * math.pi * pos / length)
    pre = (w1t_ref[...] * t
           + jnp.dot(w1c_ref[...], jnp.cos(ang), precision=HP, preferred_element_type=F32)
           - jnp.dot(w1s_ref[...], jnp.sin(ang), precision=HP, preferred_element_type=F32))
    fr = fr_ref[...]
    hid = jnp.sin(fr * (pre + b1_ref[...]))
    hid = jnp.sin(fr * (jnp.dot(w2_ref[...], hid, precision=HP, preferred_element_type=F32) + b2_ref[...]))
    hi = hid.astype(BF16).astype(F32)
    lo = hid - hi
    o_ref[...] = jnp.concatenate([hi, hi, lo, jnp.zeros_like(hi)], axis=0).T.astype(BF16)


def _filter_hidden(length, w1, b1, w2, b2, freq, bm=512):
    bands = (FILTER_EMB - 1) // 2
    f = jnp.linspace(1e-4, bands - 1, bands, dtype=F32)[:, None]
    hdim = w1.shape[1]
    full = lambda a: pl.BlockSpec(a.shape, lambda i: (0,) * a.ndim)
    args = (f, w1[:1].T, w1[1:1 + bands].T, w1[1 + bands:].T, b1[:, None], w2.T, b2[:, None], freq[:, None])
    return pl.pallas_call(
        functools.partial(_hid_kernel, length=length),
        grid=(length // bm,),
        in_specs=[full(a) for a in args],
        out_specs=pl.BlockSpec((bm, 4 * hdim), lambda i: (i, 0)),
        out_shape=jax.ShapeDtypeStruct((length, 4 * hdim), BF16),
        name="filter_hidden",
    )(*args)


def _ld(ref, start, size, stride=None):
    idx = pl.ds(start, size) if stride is None else pl.ds(start, size, stride=stride)
    return jnp.concatenate([ref[h, idx, :] for h in range(ref.shape[0])], axis=-1)


def _st(ref, start, size, val, stride=None):
    idx = pl.ds(start, size) if stride is None else pl.ds(start, size, stride=stride)
    for h in range(ref.shape[0]):
        ref[h, idx, :] = val[:, h * LANES:(h + 1) * LANES]


def _stage1_store(ab_ref, s2, a):
    _st(ab_ref, s2, K1_PAD, a[:K1_PAD], stride=AB_PITCH)
    _st(ab_ref, FFT_N2 + s2, K1_PAD, a[K1_PAD:], stride=AB_PITCH)


def _stack_split_rows(w):
    hi = w.astype(BF16)
    lo = (w - hi.astype(F32)).astype(BF16)
    return jnp.concatenate([hi, lo, hi, jnp.zeros_like(hi)], axis=0)


def _spectrum_kernel(hid_ref, w3f_ref, w3b_ref, dl_ref, skip_ref, f1_ref, f2_ref, o_ref,
                     hf_ref, hb_ref, ab_ref, *, length):
    blocks = 4
    rows = blocks * FFT_N2
    w3f, w3b = _stack_split_rows(w3f_ref[...]), _stack_split_rows(w3b_ref[...])

    def gen(j, c):
        row = (j * rows + lax.broadcasted_iota(jnp.int32, (rows, 1), 0)).astype(F32)
        win = jnp.exp(-(row / (length - 1.0)) * dl_ref[...])
        hid = hid_ref[pl.ds(pl.multiple_of(j * rows, rows), rows), :]
        hf = jnp.dot(hid, w3f, preferred_element_type=F32) * win
        hb = jnp.dot(hid, w3b, preferred_element_type=F32) * win
        for q in range(blocks):
            dst = pl.multiple_of((j * blocks + q) * Z_PITCH, 8)
            _st(hf_ref, dst, FFT_N2, hf[q * FFT_N2:(q + 1) * FFT_N2])
            _st(hb_ref, dst, FFT_N2, hb[q * FFT_N2:(q + 1) * FFT_N2])
        return c

    lax.fori_loop(0, FFT_H1 // blocks, gen, 0)

    def stage1(s2, c):
        xf = _ld(hf_ref, s2, FFT_H1, Z_PITCH).astype(BF16)
        xb = _ld(hb_ref, (FFT_N2 - s2) % FFT_N2, FFT_H1, Z_PITCH).astype(BF16)
        x = jnp.concatenate([xf, xb], axis=0)
        _stage1_store(ab_ref, s2, jnp.dot(f1_ref[s2], x, preferred_element_type=F32))
        return c

    lax.fori_loop(0, FFT_N2, stage1, 0, unroll=2 * FFT_UNROLL)

    def stage2(k1, c):
        a = _ld(ab_ref, pl.multiple_of(k1 * AB_PITCH, 8), 2 * FFT_N2).astype(BF16)
        x = jnp.dot(f2_ref[...], a, preferred_element_type=F32)
        o_ref[0, k1, :FFT_N2, :] = (x[:FFT_N2] + skip_ref[0]).astype(BF16)
        o_ref[0, k1, FFT_N2:, :] = x[FFT_N2:].astype(BF16)
        return c

    lax.fori_loop(0, FFT_H1 + 1, stage2, 0, unroll=FFT_UNROLL)


def _filter_spectrum(hid, w3, deltas, filt_bias, consts, length):
    f1, f1b, f2, _, _ = consts
    f1 = jnp.concatenate([f1, f1b], axis=2)
    cb = HY_CB
    ncb = HYENA_WIDTH // cb
    hdim = w3.shape[0]
    full = lambda a: pl.BlockSpec(a.shape, lambda o, c: (0,) * a.ndim)
    return pl.pallas_call(
        functools.partial(_spectrum_kernel, length=length),
        grid=(HYENA_ORDER, ncb),
        in_specs=[
            full(hid),
            pl.BlockSpec((hdim, cb), lambda o, c: (0, (o * N_DIRS) * ncb + c)),
            pl.BlockSpec((hdim, cb), lambda o, c: (0, (o * N_DIRS + 1) * ncb + c)),
            pl.BlockSpec((1, cb), lambda o, c: (0, c)),
            pl.BlockSpec((1, 1, cb), lambda o, c: (o, 0, c)),
            full(f1), full(f2),
        ],
        out_specs=pl.BlockSpec((1, FFT_H1 + 1, 2 * FFT_N2, cb), lambda o, c: (o, 0, 0, c)),
        out_shape=jax.ShapeDtypeStruct((HYENA_ORDER, FFT_H1 + 1, 2 * FFT_N2, HYENA_WIDTH), BF16),
        scratch_shapes=[
            pltpu.VMEM((cb // LANES, FFT_H1 * Z_PITCH, LANES), F32),
            pltpu.VMEM((cb // LANES, FFT_H1 * Z_PITCH, LANES), F32),
            pltpu.VMEM((cb // LANES, K1_PAD * AB_PITCH, LANES), F32),
        ],
        compiler_params=pltpu.CompilerParams(
            dimension_semantics=("arbitrary", "arbitrary"),
            vmem_limit_bytes=VMEM_LIMIT),
        name="filter_spectrum",
    )(hid, w3, w3, deltas, filt_bias[:, None, :], f1, f2)


def _short_conv_block(s_ref, j, w_ref, b_ref):
    big = s_ref[pl.ds(pl.multiple_of(j * FFT_N2, FFT_N2), FFT_N2 + 16), :]
    rows = FFT_N2 + 16
    prev = pltpu.roll(big, 1, 0)[8:8 + FFT_N2]
    nxt = pltpu.roll(big, rows - 1, 0)[8:8 + FFT_N2]
    cur = big[8:8 + FFT_N2]
    return b_ref[...] + prev * w_ref[0:1, :] + cur * w_ref[1:2, :] + nxt * w_ref[2:3, :]


def _load_stream(s_ref, src_ref, length):
    zeros = jnp.zeros((8, s_ref.shape[1]), F32)
    s_ref[0:8, :] = zeros
    s_ref[length + 8:length + 16, :] = zeros
    chunk = 512

    def cp(i, c):
        r = pl.multiple_of(i * chunk, chunk)
        s_ref[pl.ds(r + 8, chunk), :] = src_ref[pl.ds(r, chunk), :].astype(F32)
        return c

    lax.fori_loop(0, length // chunk, cp, 0)


def _hyena_kernel(z_ref, gs_ref, cwz_ref, cbz_ref, cwg_ref, cbg_ref, h_ref, f1_ref, f2_ref, f2i_ref, g_ref,
                  og_ref, o_ref, s_ref, zy_ref, ab_ref, *, length, conv_z, norm):
    if conv_z:
        _load_stream(s_ref, z_ref, length)

    def prep(j, c):
        if conv_z:
            zblk = _short_conv_block(s_ref, j, cwz_ref, cbz_ref)
        else:
            zblk = z_ref[pl.ds(pl.multiple_of(j * FFT_N2, FFT_N2), FFT_N2), :].astype(F32)
        _st(zy_ref, pl.multiple_of(j * Z_PITCH, 8), FFT_N2, zblk)
        return c

    lax.fori_loop(0, FFT_H1, prep, 0)

    def stage1(s2, c):
        xs = _ld(zy_ref, s2, FFT_H1, Z_PITCH).astype(BF16)
        _stage1_store(ab_ref, s2, jnp.dot(f1_ref[s2], xs, preferred_element_type=F32))
        return c

    lax.fori_loop(0, FFT_N2, stage1, 0, unroll=2 * FFT_UNROLL)

    def stage2(k1, c):
        blk = pl.multiple_of(k1 * AB_PITCH, 8)
        x = jnp.dot(f2_ref[...], _ld(ab_ref, blk, 2 * FFT_N2).astype(BF16), preferred_element_type=F32)
        hk = h_ref[0, k1].astype(F32)
        xr, xi, hr, hi = x[:FFT_N2], x[FFT_N2:], hk[:FFT_N2], hk[FFT_N2:]
        p = jnp.concatenate([xr * hr - xi * hi, xr * hi + xi * hr], axis=0).astype(BF16)
        _st(ab_ref, blk, 2 * FFT_N2, jnp.dot(f2i_ref[...], p, preferred_element_type=F32))
        return c

    lax.fori_loop(0, FFT_H1 + 1, stage2, 0, unroll=FFT_UNROLL)

    def stage3(t2, c):
        br = _ld(ab_ref, t2, K1_PAD, AB_PITCH)
        bi = _ld(ab_ref, FFT_N2 + t2, K1_PAD, AB_PITCH)
        rhs = jnp.concatenate([br, bi], axis=0).astype(BF16)
        _st(zy_ref, t2, FFT_H1, jnp.dot(g_ref[t2], rhs, preferred_element_type=F32), stride=Z_PITCH)
        return c

    lax.fori_loop(0, FFT_N2, stage3, 0, unroll=2 * FFT_UNROLL)

    _load_stream(s_ref, gs_ref, length)

    def fin(j, c):
        gate = _short_conv_block(s_ref, j, cwg_ref, cbg_ref)
        y = gate * _ld(zy_ref, pl.multiple_of(j * Z_PITCH, 8), FFT_N2)
        dst = pl.ds(pl.multiple_of(j * FFT_N2, FFT_N2), FFT_N2)
        if norm:
            gw = HYENA_WIDTH // N_HYENA_GROUPS
            for q in range(y.shape[1] // gw):
                sl = slice(q * gw, (q + 1) * gw)
                o_ref[dst, sl] = (_rms(y[:, sl]) * og_ref[:, sl]).astype(o_ref.dtype)
        else:
            o_ref[dst, :] = y.astype(o_ref.dtype)
        return c

    lax.fori_loop(0, FFT_H1, fin, 0)


def _hyena_order(zsrc, zcol, proj, order, conv_w, conv_b, hspec, consts, out_g, batch, length, conv_z, norm):
    f1, _, f2, f2i, g = consts
    cb = HY_CB
    ncb = HYENA_WIDTH // cb
    pcol = 3 * ATTN_WIDTH // cb
    zconv = HYENA_ORDER * ncb
    full = lambda a: pl.BlockSpec(a.shape, lambda c, b: (0,) * a.ndim)
    kern = functools.partial(_hyena_kernel, length=length, conv_z=conv_z, norm=norm)
    return pl.pallas_call(
        kern,
        grid=(ncb, batch),
        in_specs=[
            pl.BlockSpec((length, cb), lambda c, b: (b, zcol + c)),
            pl.BlockSpec((length, cb), lambda c, b: (b, pcol + order * ncb + c)),
            pl.BlockSpec((SHORT_CONV_W, cb), lambda c, b: (0, zconv + c)),
            pl.BlockSpec((1, cb), lambda c, b: (0, zconv + c)),
            pl.BlockSpec((SHORT_CONV_W, cb), lambda c, b: (0, order * ncb + c)),
            pl.BlockSpec((1, cb), lambda c, b: (0, order * ncb + c)),
            pl.BlockSpec((1, FFT_H1 + 1, 2 * FFT_N2, cb), lambda c, b: (order, 0, 0, c)),
            full(f1), full(f2), full(f2i), full(g),
            pl.BlockSpec((1, cb), lambda c, b: (0, c)),
        ],
        out_specs=pl.BlockSpec((length, cb), lambda c, b: (b, c)),
        out_shape=jax.ShapeDtypeStruct((batch * length, HYENA_WIDTH), BF16),
        scratch_shapes=[
            pltpu.VMEM((length + 16, cb), F32),
            pltpu.VMEM((cb // LANES, FFT_H1 * Z_PITCH, LANES), F32),
            pltpu.VMEM((cb // LANES, K1_PAD * AB_PITCH, LANES), F32),
        ],
        compiler_params=pltpu.CompilerParams(
            dimension_semantics=("arbitrary", "arbitrary"),
            vmem_limit_bytes=VMEM_LIMIT),
        name=f"hyena_order{order}",
    )(zsrc, proj, conv_w, conv_b[None], conv_w, conv_b[None], hspec, f1, f2, f2i, g, out_g)


def _hyena(proj, conv_w, conv_b, w1, b1, w2, b2, w3, freq, filt_bias, out_g, batch, length):
    assert 2 * length == FFT_N1 * FFT_N2 and HYENA_ORDER == 2
    consts = _dft_constants()
    max_decay = math.log(DECAY_TARGET) / FAST_DECAY_PCT
    min_decay = math.log(DECAY_TARGET) / SLOW_DECAY_PCT
    deltas = jnp.abs(jnp.linspace(min_decay, max_decay, HYENA_WIDTH, dtype=F32))[None]
    hid = _filter_hidden(length, w1, b1, w2, b2, freq)
    hspec = _filter_spectrum(hid, w3, deltas, filt_bias, consts, length)
    vcol = (3 * ATTN_WIDTH + HYENA_ORDER * HYENA_WIDTH) // HY_CB
    z1 = _hyena_order(proj, vcol, proj, 0, conv_w, conv_b, hspec, consts, out_g, batch, length, True, False)
    return _hyena_order(z1, 0, proj, 1, conv_w, conv_b, hspec, consts, out_g, batch, length, False, True)


def kernel(x, mix_norm_g, w_in, q_norm_g, k_norm_g, rpb, conv_w, conv_b, filt_w1, filt_b1, filt_w2, filt_b2, filt_w3, filt_freq, filt_bias, attn_out_g, hyena_out_g, w_out, ffn_norm_g, w_router, w_gate, w_up, w_down):
    b, s, d = x.shape
    rows = s // GRID_W
    win_r = min(WIN_ROWS_MAX, rows)
    cap = EC_CAPACITY_FACTOR * s // N_EXPERTS
    x2 = x.reshape(b * s, d)
    for i in range(mix_norm_g.shape[0]):
        proj = _inproj(x2, mix_norm_g[i][None], w_in[i],
                       q_norm_g[i][None], k_norm_g[i][None])
        attn = _attention(proj, rpb[i], attn_out_g[i][None], b, s)

        hyn = _hyena(proj, conv_w[i], conv_b[i], filt_w1[i], filt_b1[i], filt_w2[i], filt_b2[i], filt_w3[i],
                     filt_freq[i], filt_bias[i], hyena_out_g[i][None], b, s)

        wr_pad = jnp.zeros((d, 128), BF16).at[:, :N_EXPERTS].set(w_router[i].astype(BF16))
        x1, hn, logits = _outproj(attn, hyn, x2, w_out[i].astype(BF16), ffn_norm_g[i][None], wr_pad)

        idx, gate = _route(logits, b, s, cap)
        x2 = _moe(idx, gate, hn, x1, w_gate[i], w_up[i], w_down[i], b, s)
    return x2.reshape(b, s, d)
```

```python
import functools
import math

import numpy as np
import jax
import jax.numpy as jnp
from jax import lax
from jax.experimental import pallas as pl
from jax.experimental.pallas import tpu as pltpu

D_MODEL = 2048
GRID_W = 64
N_ATTN_HEADS = 8
ATTN_WIDTH = D_MODEL // 2
HEAD_DIM = ATTN_WIDTH // N_ATTN_HEADS
HYENA_WIDTH = D_MODEL - ATTN_WIDTH
N_HYENA_GROUPS = 8
HYENA_ORDER = 2
N_DIRS = 2
SHORT_CONV_W = 3
FILTER_EMB = 33
DECAY_TARGET = 1e-2
FAST_DECAY_PCT = 0.3
SLOW_DECAY_PCT = 1.5
WIN_ROWS_MAX = 8
WIN_COLS = 16
N_EXPERTS = 16
EC_CAPACITY_FACTOR = 2
EXPERT_FF = 1024
IN_WIDTH = 3 * ATTN_WIDTH + (HYENA_ORDER + 1) * HYENA_WIDTH
RMS_EPS = 1e-6

F32 = jnp.float32
BF16 = jnp.bfloat16
NEG_BIG = -1e30
VMEM_LIMIT = 56 * 1024 * 1024
LANES = 128


def _rms(x, eps=RMS_EPS):
    return x * lax.rsqrt(jnp.mean(x * x, axis=-1, keepdims=True) + eps)


INPROJ_GROUPS = 2


def _inproj_kernel(x_ref, g_ref, w_ref, qg_ref, kg_ref, o_ref, h_ref):
    n = pl.program_id(1)
    bm, bn = o_ref.shape

    def step(normalise_input, head_gain):
        w = w_ref[...].astype(BF16)
        for r in range(INPROJ_GROUPS):
            rows = slice(r * bm // INPROJ_GROUPS, (r + 1) * bm // INPROJ_GROUPS)
            if normalise_input:
                h_ref[rows, :] = (_rms(x_ref[rows, :]) * g_ref[...]).astype(BF16)
            acc = jnp.dot(h_ref[rows, :], w, preferred_element_type=F32)
            if head_gain is None:
                o_ref[rows, :] = acc.astype(BF16)
            else:
                for j in range(bn // HEAD_DIM):
                    sl = slice(j * HEAD_DIM, (j + 1) * HEAD_DIM)
                    o_ref[rows, sl] = (_rms(acc[:, sl]) * head_gain).astype(BF16)

    @pl.when(n == 0)
    def _():
        step(True, qg_ref[...] * (HEAD_DIM ** -0.5))

    @pl.when(n == 1)
    def _():
        step(False, kg_ref[...])

    @pl.when(n >= 2)
    def _():
        step(False, None)


def _inproj(x2, g, w_bf, qg, kg, bm=1024, bn=1024):
    m, d = x2.shape
    nw = w_bf.shape[1]
    return pl.pallas_call(
        _inproj_kernel,
        grid=(m // bm, nw // bn),
        in_specs=[
            pl.BlockSpec((bm, d), lambda i, j: (i, 0)),
            pl.BlockSpec((1, d), lambda i, j: (0, 0)),
            pl.BlockSpec((d, bn), lambda i, j: (0, j)),
            pl.BlockSpec((1, HEAD_DIM), lambda i, j: (0, 0)),
            pl.BlockSpec((1, HEAD_DIM), lambda i, j: (0, 0)),
        ],
        out_specs=pl.BlockSpec((bm, bn), lambda i, j: (i, j)),
        out_shape=jax.ShapeDtypeStruct((m, nw), BF16),
        scratch_shapes=[pltpu.VMEM((bm, d), BF16)],
        compiler_params=pltpu.CompilerParams(
            dimension_semantics=("arbitrary", "arbitrary"),
            vmem_limit_bytes=VMEM_LIMIT),
        name="inproj",
    )(x2, g, w_bf, qg, kg)


ATTN_ROWS = 4
ATTN_BAND = 12


def _attn_kernel(q_ref, k_ref, v_ref, rpb_ref, mask_ref, g_ref, o_ref, pt_ref, *, rows, win_r):
    lanes = pt_ref.shape[-1]
    n_dr = 2 * WIN_ROWS_MAX - 1

    def toeplitz(d):
        row = jnp.broadcast_to(rpb_ref[0, d:d + 1, :], (GRID_W, lanes))
        return pltpu.roll(row, lanes - (WIN_COLS - 1), 1, stride=1, stride_axis=0)

    lane = lax.broadcasted_iota(jnp.int32, (GRID_W, lanes), 1)
    left = lane < GRID_W
    tiles = [toeplitz(d) for d in range(n_dr)]
    for d in range(-1, n_dr):
        lo, hi = tiles[max(d, 0)], tiles[min(d + 1, n_dr - 1)]
        pt_ref[d + 1] = jnp.where(left, lo, pltpu.roll(hi, GRID_W, 1)) + mask_ref[...]

    def body(i, carry):
        r0 = i * ATTN_ROWS
        bs = jnp.minimum(jnp.clip(r0 - win_r // 2, 0, rows - win_r), rows - ATTN_BAND)
        q0 = pl.multiple_of(r0 * GRID_W, ATTN_ROWS * GRID_W)
        k0 = pl.multiple_of(bs * GRID_W, 2 * GRID_W)
        q = q_ref[pl.ds(q0, ATTN_ROWS * GRID_W), :]
        k = k_ref[pl.ds(k0, ATTN_BAND * GRID_W), :]
        v = v_ref[pl.ds(k0, ATTN_BAND * GRID_W), :]
        s = lax.dot_general(q, k, (((1,), (1,)), ((), ())), preferred_element_type=F32)

        bias_rows = []
        for j in range(ATTN_ROWS):
            r = r0 + j
            start = jnp.clip(r - win_r // 2, 0, rows - win_r)
            pieces = []
            for m in range(ATTN_BAND // 2):
                kr = bs + 2 * m
                pen_l = jnp.where((kr >= start) & (kr < start + win_r), 0.0, NEG_BIG)
                pen_r = jnp.where((kr + 1 >= start) & (kr + 1 < start + win_r), 0.0, NEG_BIG)
                d = jnp.clip(kr - r + WIN_ROWS_MAX - 1, -1, n_dr - 1)
                pieces.append(pt_ref[d + 1] + jnp.where(left, pen_l, pen_r))
            bias_rows.append(jnp.concatenate(pieces, axis=1))
        s = s + jnp.concatenate(bias_rows, axis=0)

        mx = jnp.max(s, axis=-1, keepdims=True)
        p = jnp.exp(s - mx)
        l = jnp.sum(p, axis=-1, keepdims=True)
        o = jnp.dot(p.astype(BF16), v, preferred_element_type=F32) / l
        o = _rms(o) * g_ref[...]
        o_ref[pl.ds(q0, ATTN_ROWS * GRID_W), :] = o.astype(BF16)
        return carry

    lax.fori_loop(0, rows // ATTN_ROWS, body, 0, unroll=2)


def _attn_window_mask():
    c = np.arange(GRID_W)
    c0 = np.clip(c - WIN_COLS // 2, 0, GRID_W - WIN_COLS)
    in_win = (c[None, :] >= c0[:, None]) & (c[None, :] < c0[:, None] + WIN_COLS)
    return jnp.asarray(np.tile(np.where(in_win, 0.0, NEG_BIG), (1, 2)), dtype=F32)


def _attention(proj, rpb, out_g, batch, seq):
    rows = seq // GRID_W
    win_r = min(WIN_ROWS_MAX, rows)
    assert win_r == WIN_ROWS_MAX and 2 * GRID_W == LANES
    assert rows % ATTN_ROWS == 0 and ATTN_BAND >= win_r + ATTN_ROWS - 1 and ATTN_BAND % 2 == 0
    h = N_ATTN_HEADS
    n_dr, n_dc = rpb.shape[1:]
    rpb_pad = jnp.zeros((h, 2 * WIN_ROWS_MAX, LANES), F32).at[:, :n_dr, :n_dc].set(rpb)
    mask = _attn_window_mask()
    kern = functools.partial(_attn_kernel, rows=rows, win_r=win_r)
    return pl.pallas_call(
        kern,
        grid=(batch, h),
        in_specs=[
            pl.BlockSpec((seq, HEAD_DIM), lambda b, i: (b, i)),
            pl.BlockSpec((seq, HEAD_DIM), lambda b, i: (b, h + i)),
            pl.BlockSpec((seq, HEAD_DIM), lambda b, i: (b, 2 * h + i)),
            pl.BlockSpec((1, 2 * WIN_ROWS_MAX, LANES), lambda b, i: (i, 0, 0)),
            pl.BlockSpec(mask.shape, lambda b, i: (0, 0)),
            pl.BlockSpec((1, HEAD_DIM), lambda b, i: (0, i)),
        ],
        out_specs=pl.BlockSpec((seq, HEAD_DIM), lambda b, i: (b, i)),
        out_shape=jax.ShapeDtypeStruct((batch * seq, ATTN_WIDTH), BF16),
        scratch_shapes=[pltpu.VMEM((2 * WIN_ROWS_MAX, GRID_W, LANES), F32)],
        compiler_params=pltpu.CompilerParams(
            dimension_semantics=("arbitrary", "arbitrary"),
            vmem_limit_bytes=VMEM_LIMIT),
        name="na2d_attn",
    )(proj, proj, proj, rpb_pad, mask, out_g)


OUTPROJ_GROUPS = 2


def _outproj_kernel(a_ref, hy_ref, x_ref, wo_ref, g_ref, wr_ref, x1_ref, hn_ref, lg_ref):
    half = a_ref.shape[1]
    bm = x_ref.shape[0]
    for r in range(OUTPROJ_GROUPS):
        rows = slice(r * bm // OUTPROJ_GROUPS, (r + 1) * bm // OUTPROJ_GROUPS)
        acc = x_ref[rows, :]
        acc = acc + jnp.dot(a_ref[rows, :], wo_ref[:half, :], preferred_element_type=F32)
        acc = acc + jnp.dot(hy_ref[rows, :], wo_ref[half:, :], preferred_element_type=F32)
        x1_ref[rows, :] = acc
        hn = _rms(acc) * g_ref[...]
        hn_ref[rows, :] = hn
        lg_ref[rows, :] = jnp.dot(hn.astype(BF16), wr_ref[...], preferred_element_type=F32)


def _outproj(attn, hy, x2, wo_bf, g, wr_pad, bm=512):
    m, d = x2.shape
    half = attn.shape[1]
    npad = wr_pad.shape[1]
    return pl.pallas_call(
        _outproj_kernel,
        grid=(m // bm,),
        in_specs=[
            pl.BlockSpec((bm, half), lambda i: (i, 0)),
            pl.BlockSpec((bm, half), lambda i: (i, 0)),
            pl.BlockSpec((bm, d), lambda i: (i, 0)),
            pl.BlockSpec((d, d), lambda i: (0, 0)),
            pl.BlockSpec((1, d), lambda i: (0, 0)),
            pl.BlockSpec((d, npad), lambda i: (0, 0)),
        ],
        out_specs=[
            pl.BlockSpec((bm, d), lambda i: (i, 0)),
            pl.BlockSpec((bm, d), lambda i: (i, 0)),
            pl.BlockSpec((bm, npad), lambda i: (i, 0)),
        ],
        out_shape=[
            jax.ShapeDtypeStruct((m, d), F32),
            jax.ShapeDtypeStruct((m, d), F32),
            jax.ShapeDtypeStruct((m, npad), F32),
        ],
        compiler_params=pltpu.CompilerParams(
            dimension_semantics=("arbitrary",),
            vmem_limit_bytes=VMEM_LIMIT),
        name="outproj",
    )(attn, hy, x2, wo_bf, g, wr_pad)


ROUTE_BLK = 128
ROUTE_CHUNK = 512
SLOT_GRP = 8
ROUTE_GROUPS_INLINE = 3
F32_INF_BITS = 0x7F800000


def _route_kernel(lg_ref, tri_ref, idx_ref, gate_ref, aff_ref, pos_t_ref, aff_t_ref, off_ref, ia_ref, ga_ref,
                  stage_i, stage_g, off_smem, sem, *, cap):
    s, lanes = lg_ref.shape
    lane = lax.broadcasted_iota(jnp.int32, (1, lanes), 1)

    def softmax_chunk(i, c):
        rows = pl.ds(pl.multiple_of(i * ROUTE_CHUNK, ROUTE_CHUNK), ROUTE_CHUNK)
        lg = jnp.where(lane < N_EXPERTS, lg_ref[rows, :], NEG_BIG)
        ex = jnp.exp(lg - jnp.max(lg, axis=-1, keepdims=True))
        aff_ref[rows, :] = ex / jnp.sum(ex, axis=-1, keepdims=True)
        return c

    lax.fori_loop(0, s // ROUTE_CHUNK, softmax_chunk, 0)

    def as_value(bits):
        return pltpu.bitcast(bits, F32)

    def fold8(x):
        parts = [x[i:i + 8] for i in range(0, x.shape[0], 8)]
        while len(parts) > 1:
            parts = [a + b for a, b in zip(parts[::2], parts[1::2])]
        return parts[0]

    def count_ge(th):
        def body(i, acc):
            rows = pl.ds(pl.multiple_of(i * ROUTE_CHUNK, ROUTE_CHUNK), ROUTE_CHUNK)
            return acc + fold8(jnp.where(aff_ref[rows, :] >= th, 1.0, 0.0))
        acc = lax.fori_loop(0, s // ROUTE_CHUNK, body, jnp.zeros((8, lanes), F32))
        return jnp.sum(acc, axis=0, keepdims=True)

    def bisect(_, c):
        lo, hi = c
        mid = lo + ((hi - lo) >> 1)
        ok = count_ge(as_value(mid)) >= cap
        return jnp.where(ok, mid, lo), jnp.where(ok, hi, mid)

    tau_bits, _ = lax.fori_loop(0, 31, bisect, (jnp.zeros((1, lanes), jnp.int32),
                                                jnp.full((1, lanes), F32_INF_BITS, jnp.int32)))
    tau, above = as_value(tau_bits), as_value(tau_bits + 1)
    ties_wanted = cap - count_ge(above)

    def scan_block(j, c):
        tie_carry, sel_carry = c
        rows = pl.ds(pl.multiple_of(j * ROUTE_BLK, ROUTE_BLK), ROUTE_BLK)
        aff = aff_ref[rows, :]
        gt = aff >= above
        eq = jnp.where((aff >= tau) & (aff < above), 1.0, 0.0)
        tie_incl = jnp.dot(tri_ref[...], eq.astype(BF16), preferred_element_type=F32) + tie_carry
        sel = jnp.where(gt | ((eq > 0.0) & (tie_incl - eq < ties_wanted)), 1.0, 0.0)
        sel_incl = jnp.dot(tri_ref[...], sel.astype(BF16), preferred_element_type=F32) + sel_carry
        pos = jnp.where(sel > 0.0, sel_incl - sel, -1.0)
        pos_t_ref[j] = pos.T[:N_EXPERTS]
        aff_t_ref[j] = aff.T[:N_EXPERTS]
        off_ref[pl.ds(j, 1), :] = sel_carry.astype(jnp.int32)
        return tie_incl[ROUTE_BLK - 1:, :], sel_incl[ROUTE_BLK - 1:, :]

    n_blk = s // ROUTE_BLK
    off_ref[...] = jnp.zeros_like(off_ref)
    zero = jnp.zeros((1, lanes), F32)
    _, total = lax.fori_loop(0, n_blk, scan_block, (zero, zero), unroll=4)
    off_ref[pl.ds(n_blk, 1), :] = total.astype(jnp.int32)

    to_smem = pltpu.make_async_copy(off_ref, off_smem, sem)
    to_smem.start()
    to_smem.wait()

    n_grp = cap // SLOT_GRP
    sub = lax.broadcasted_iota(jnp.int32, (SLOT_GRP, lanes), 0)
    lane_i = lax.broadcasted_iota(jnp.int32, (SLOT_GRP, lanes), 1)
    stage_i[...] = jnp.zeros_like(stage_i)
    stage_g[...] = jnp.zeros_like(stage_g)
    for e in range(N_EXPERTS):
        ia_ref[...] = jnp.zeros_like(ia_ref)
        ga_ref[...] = jnp.zeros_like(ga_ref)

        def block(j, carry, e=e):
            first, last = off_smem[j, e], off_smem[j + 1, e]
            pos = jnp.broadcast_to(pos_t_ref[j, e:e + 1, :], (SLOT_GRP, lanes))
            aff = jnp.broadcast_to(aff_t_ref[j, e:e + 1, :], (SLOT_GRP, lanes))
            tok = (j * ROUTE_BLK + lane_i).astype(F32)
            g0 = first // SLOT_GRP
            groups = jnp.where(last > first, (last - 1) // SLOT_GRP - g0 + 1, 0)

            def update(g):
                match = pos == (g * SLOT_GRP + sub).astype(F32)
                gi = jnp.minimum(g, n_grp - 1)
                ia_ref[gi] += jnp.where(match, tok, 0.0)
                ga_ref[gi] += jnp.where(match, aff, 0.0)

            for k in range(ROUTE_GROUPS_INLINE):
                update(g0 + k)

            def rest(k, c):
                update(g0 + k)
                return c

            lax.fori_loop(ROUTE_GROUPS_INLINE, groups, rest, 0)
            return carry

        lax.fori_loop(0, n_blk, block, 0)
        stage_i[:, e:e + 1] = jnp.sum(ia_ref[...].reshape(cap, lanes), axis=1, keepdims=True)
        stage_g[:, e:e + 1] = jnp.sum(ga_ref[...].reshape(cap, lanes), axis=1, keepdims=True)

    idx_ref[0] = stage_i[...].T[:N_EXPERTS].astype(jnp.int32)
    gate_ref[0] = stage_g[...].T[:N_EXPERTS]


def _route(logits, batch, seq, cap):
    lanes = logits.shape[1]
    assert lanes == LANES == ROUTE_BLK and cap % SLOT_GRP == 0
    n_blk = seq // ROUTE_BLK
    n_off = -(-(n_blk + 1) // 8) * 8
    tri =jnp.asarray(np.tril(np.ones((ROUTE_BLK, ROUTE_BLK))), dtype=BF16)
    return pl.pallas_call(
        functools.partial(_route_kernel, cap=cap),
        grid=(batch,),
        in_specs=[
            pl.BlockSpec((seq, lanes), lambda b: (b, 0)),
            pl.BlockSpec((ROUTE_BLK, ROUTE_BLK), lambda b: (0, 0)),
        ],
        out_specs=[
            pl.BlockSpec((1, N_EXPERTS, cap), lambda b: (b, 0, 0)),
            pl.BlockSpec((1, N_EXPERTS, cap), lambda b: (b, 0, 0)),
        ],
        out_shape=[
            jax.ShapeDtypeStruct((batch, N_EXPERTS, cap), jnp.int32),
            jax.ShapeDtypeStruct((batch, N_EXPERTS, cap), F32),
        ],
        scratch_shapes=[
            pltpu.VMEM((seq, lanes), F32),
            pltpu.VMEM((n_blk, N_EXPERTS, lanes), F32),
            pltpu.VMEM((n_blk, N_EXPERTS, lanes), F32),
            pltpu.VMEM((n_off, lanes), jnp.int32),
            pltpu.VMEM((cap // SLOT_GRP, SLOT_GRP, lanes), F32),
            pltpu.VMEM((cap // SLOT_GRP, SLOT_GRP, lanes), F32),
            pltpu.VMEM((cap, lanes), F32),
            pltpu.VMEM((cap, lanes), F32),
            pltpu.SMEM((n_off, lanes), jnp.int32),
            pltpu.SemaphoreType.DMA(()),
        ],
        compiler_params=pltpu.CompilerParams(dimension_semantics=("arbitrary",)),
        name="route",
    )(logits, tri)


MOE_FF_SPLIT = 4
ROW_UNROLL = 8


def _moe_kernel(idx_ref, hn_hbm, wg_ref, wu_ref, wd_ref, gate_ref, x1_hbm, out_hbm,
                xg0, xg1, acc0, acc1, y0, y1, sem_x, sem_a, sem_s, *, cap, seq):
    del x1_hbm
    e, f = pl.program_id(0), pl.program_id(1)
    n_e = pl.num_programs(0)
    xg, acc = (xg0, xg1), (acc0, acc1)
    prev_e, next_e = jnp.maximum(e - 1, 0), jnp.minimum(e + 1, n_e - 1)

    def row_of(ee, b):
        base = (b * n_e + ee) * cap
        return lambda p: b * seq + idx_ref[base + p]

    def copy_rows(ee, b, kind, rolled=False):
        row = row_of(ee, b)

        def one(p):
            if kind == "xg":
                src, dst, sem = hn_hbm.at[pl.ds(row(p), 1), :], xg[b].at[pl.ds(p, 1), :], sem_x.at[b]
            elif kind == "ag":
                src, dst, sem = out_hbm.at[pl.ds(row(p), 1), :], acc[b].at[pl.ds(p, 1), :], sem_a.at[b]
            else:
                src, dst, sem = acc[b].at[pl.ds(p, 1), :], out_hbm.at[pl.ds(row(p), 1), :], sem_s.at[b]
            pltpu.make_async_copy(src, dst, sem).start()

        if rolled:
            def body(i, c):
                for k in range(ROW_UNROLL):
                    one(i * ROW_UNROLL + k)
                return c
            lax.fori_loop(0, cap // ROW_UNROLL, body, 0)
        else:
            for p in range(cap):
                one(p)

    def wait_rows(buf, sem):
        pltpu.make_async_copy(hn_hbm.at[pl.ds(0, cap), :], buf, sem).wait()

    def partial_out(b, wg, wu, wd):
        x = xg[b][...].astype(BF16)
        a = jnp.dot(x, wg, preferred_element_type=F32)
        u = jnp.dot(x, wu, preferred_element_type=F32)
        hmid = (a * (1.0 / (1.0 + jnp.exp(-a))) * u).astype(BF16)
        return jnp.dot(hmid, wd, preferred_element_type=F32)

    def weights():
        return wg_ref[0].astype(BF16), wu_ref[0].astype(BF16), wd_ref[0].astype(BF16)

    @pl.when((e == 0) & (f == 0))
    def _():
        copy_rows(e, 0, "xg", rolled=True)
        copy_rows(e, 0, "ag", rolled=True)
        copy_rows(e, 1, "ag", rolled=True)
        wait_rows(acc0, sem_a.at[0])
        wait_rows(acc1, sem_a.at[1])

    @pl.when(f == 0)
    def _():
        w = weights()
        wait_rows(xg0, sem_x.at[0])
        copy_rows(e, 1, "xg")
        y0[...] = partial_out(0, *w)
        wait_rows(xg1, sem_x.at[1])
        copy_rows(prev_e, 0, "sc")
        y1[...] = partial_out(1, *w)

    @pl.when(f == 1)
    def _():
        w = weights()
        copy_rows(prev_e, 1, "sc")
        y0[...] += partial_out(0, *w)
        wait_rows(acc0, sem_s.at[0])
        copy_rows(e, 0, "ag")
        y1[...] += partial_out(1, *w)

    @pl.when(f == 2)
    def _():
        w = weights()
        wait_rows(acc1, sem_s.at[1])
        copy_rows(e, 1, "ag")
        y0[...] += partial_out(0, *w)
        y1[...] += partial_out(1, *w)

    @pl.when(f == 3)
    def _():
        w = weights()
        t0 = y0[...] + partial_out(0, *w)
        wait_rows(acc0, sem_a.at[0])
        acc0[...] += t0 * gate_ref[0, 0]
        copy_rows(next_e, 0, "xg")
        t1 = y1[...] + partial_out(1, *w)
        wait_rows(acc1, sem_a.at[1])
        acc1[...] += t1 * gate_ref[1, 0]

    @pl.when((e == n_e - 1) & (f == MOE_FF_SPLIT - 1))
    def _():
        copy_rows(e, 0, "sc", rolled=True)
        copy_rows(e, 1, "sc", rolled=True)
        wait_rows(acc0, sem_s.at[0])
        wait_rows(acc1, sem_s.at[1])
        wait_rows(xg0, sem_x.at[0])


def _moe(idx, gate, hn, x1, wg, wu, wd, batch, seq):
    n_e, d, ff = wg.shape
    cap = idx.shape[-1]
    assert batch == 2 and MOE_FF_SPLIT == 4 and ff % MOE_FF_SPLIT == 0
    fq = ff // MOE_FF_SPLIT
    grid_spec = pltpu.PrefetchScalarGridSpec(
        num_scalar_prefetch=1,
        grid=(n_e, MOE_FF_SPLIT),
        in_specs=[
            pl.BlockSpec(memory_space=pl.ANY),
            pl.BlockSpec((1, d, fq), lambda i, j, s: (i, 0, j)),
            pl.BlockSpec((1, d, fq), lambda i, j, s: (i, 0, j)),
            pl.BlockSpec((1, fq, d), lambda i, j, s: (i, j, 0)),
            pl.BlockSpec((batch, 1, cap, 1), lambda i, j, s: (0, i, 0, 0)),
            pl.BlockSpec(memory_space=pl.ANY),
        ],
        out_specs=pl.BlockSpec(memory_space=pl.ANY),
        scratch_shapes=(
            [pltpu.VMEM((cap, d), F32) for _ in range(6)]
            + [pltpu.SemaphoreType.DMA((batch,)) for _ in range(3)]),
    )
    return pl.pallas_call(
        functools.partial(_moe_kernel, cap=cap, seq=seq),
        grid_spec=grid_spec,
        out_shape=jax.ShapeDtypeStruct(x1.shape, x1.dtype),
        input_output_aliases={6: 0},
        compiler_params=pltpu.CompilerParams(
            dimension_semantics=("arbitrary", "arbitrary"),
            vmem_limit_bytes=VMEM_LIMIT),
        name="moe_experts",
    )(idx.reshape(-1), hn, wg, wu, wd, gate[..., None], x1)


FFT_N1 = 128
FFT_N2 = 64
FFT_H1 = FFT_N1 // 2
K1_PAD = 72
Z_PITCH = 72
AB_PITCH = 136
HY_CB = 256
FFT_UNROLL = 8
HP = lax.Precision.HIGHEST


def _dft_constants():
    n, n1, n2, h1 = FFT_N1 * FFT_N2, FFT_N1, FFT_N2, FFT_H1
    k1 = np.arange(h1 + 1, dtype=np.float64)
    s1 = np.arange(h1, dtype=np.float64)
    s2 = np.arange(n2, dtype=np.float64)
    tw = s2[:, None, None] * k1[None, :, None] / n

    def stage1(phase_s1):
        th = -2.0 * np.pi * (phase_s1 + tw)
        m = np.zeros((n2, 2 * K1_PAD, h1))
        m[:, :h1 + 1] = np.cos(th)
        m[:, K1_PAD:K1_PAD + h1 + 1] = np.sin(th)
        return m

    f1 = stage1(s1[None, None, :] * k1[None, :, None] / n1)
    f1b = stage1(((n1 - 1) - s1)[None, None, :] * k1[None, :, None] / n1)
    f1b0 = stage1((n1 - s1)[None, None, :] * k1[None, :, None] / n1)[0]
    f1b0[:, 0] = 0.0
    f1b[0] = f1b0

    a = 2.0 * np.pi * np.outer(np.arange(n2), np.arange(n2)) / n2
    c, s = np.cos(a), np.sin(a)
    f2 = np.block([[c, s], [-s, c]])
    f2i = np.block([[c, -s], [s, c]])

    t1 = np.arange(h1, dtype=np.float64)
    th = 2.0 * np.pi * (t1[None, :, None] * k1[None, None, :] / n1 + s2[:, None, None] * k1[None, None, :] / n)
    wgt = np.where((k1 == 0) | (k1 == h1), 1.0, 2.0) / n
    g = np.zeros((n2, h1, 2 * K1_PAD))
    g[:, :, :h1 + 1] = wgt * np.cos(th)
    g[:, :, K1_PAD:K1_PAD + h1 + 1] = -wgt * np.sin(th)
    return tuple(jnp.asarray(m, dtype=F32).astype(BF16) for m in (f1, f1b, f2, f2i, g))


def _hid_kernel(f_ref, w1t_ref, w1c_ref, w1s_ref, b1_ref, w2_ref, b2_ref, fr_ref, o_ref, *, length):
    bm = o_ref.shape[0]
    pos = (pl.program_id(0) * bm + lax.broadcasted_iota(jnp.int32, (1, bm), 1)).astype(F32)
    t = pos / (length - 1.0)
    ang = f_ref[...] * (2.0 * math.pi * pos / length)
    pre = (w1t_ref[...] * t
           + jnp.dot(w1c_ref[...], jnp.cos(ang), precision=HP, preferred_element_type=F32)
           - jnp.dot(w1s_ref[...], jnp.sin(ang), precision=HP, preferred_element_type=F32))
    fr = fr_ref[...]
    hid = jnp.sin(fr * (pre + b1_ref[...]))
    hid = jnp.sin(fr * (jnp.dot(w2_ref[...], hid, precision=HP, preferred_element_type=F32) + b2_ref[...]))
    hi = hid.astype(BF16).astype(F32)
    lo = hid - hi
    o_ref[...] = jnp.concatenate([hi, hi, lo, jnp.zeros_like(hi)], axis=0).T.astype(BF16)


def _filter_hidden(length, w1, b1, w2, b2, freq, bm=512):
    bands = (FILTER_EMB - 1) // 2
    f = jnp.linspace(1e-4, bands - 1, bands, dtype=F32)[:, None]
    hdim = w1.shape[1]
    full = lambda a: pl.BlockSpec(a.shape, lambda i: (0,) * a.ndim)
    args = (f, w1[:1].T, w1[1:1 + bands].T, w1[1 + bands:].T, b1[:, None], w2.T, b2[:, None], freq[:, None])
    return pl.pallas_call(
        functools.partial(_hid_kernel, length=length),
        grid=(length // bm,),
        in_specs=[full(a) for a in args],
        out_specs=pl.BlockSpec((bm, 4 * hdim), lambda i: (i, 0)),
        out_shape=jax.ShapeDtypeStruct((length, 4 * hdim), BF16),
        name="filter_hidden",
    )(*args)


def _ld(ref, start, size, stride=None):
    idx = pl.ds(start, size) if stride is None else pl.ds(start, size, stride=stride)
    return jnp.concatenate([ref[h, idx, :] for h in range(ref.shape[0])], axis=-1)


def _st(ref, start, size, val, stride=None):
    idx = pl.ds(start, size) if stride is None else pl.ds(start, size, stride=stride)
    for h in range(ref.shape[0]):
        ref[h, idx, :] = val[:, h * LANES:(h + 1) * LANES]


def _stage1_store(ab_ref, s2, a):
    _st(ab_ref, s2, K1_PAD, a[:K1_PAD], stride=AB_PITCH)
    _st(ab_ref, FFT_N2 + s2, K1_PAD, a[K1_PAD:], stride=AB_PITCH)


def _stack_split_rows(w):
    hi = w.astype(BF16)
    lo = (w - hi.astype(F32)).astype(BF16)
    return jnp.concatenate([hi, lo, hi, jnp.zeros_like(hi)], axis=0)


def _spectrum_kernel(hid_ref, w3f_ref, w3b_ref, dl_ref, skip_ref, f1_ref, f2_ref, o_ref,
                     hf_ref, hb_ref, ab_ref, *, length):
    blocks = 4
    rows = blocks * FFT_N2
    w3f, w3b = _stack_split_rows(w3f_ref[...]), _stack_split_rows(w3b_ref[...])

    def gen(j, c):
        row = (j * rows + lax.broadcasted_iota(jnp.int32, (rows, 1), 0)).astype(F32)
        win = jnp.exp(-(row / (length - 1.0)) * dl_ref[...])
        hid = hid_ref[pl.ds(pl.multiple_of(j * rows, rows), rows), :]
        hf = jnp.dot(hid, w3f, preferred_element_type=F32) * win
        hb = jnp.dot(hid, w3b, preferred_element_type=F32) * win
        for q in range(blocks):
            dst = pl.multiple_of((j * blocks + q) * Z_PITCH, 8)
            _st(hf_ref, dst, FFT_N2, hf[q * FFT_N2:(q + 1) * FFT_N2])
            _st(hb_ref, dst, FFT_N2, hb[q * FFT_N2:(q + 1) * FFT_N2])
        return c

    lax.fori_loop(0, FFT_H1 // blocks, gen, 0)

    def stage1(s2, c):
        xf = _ld(hf_ref, s2, FFT_H1, Z_PITCH).astype(BF16)
        xb = _ld(hb_ref, (FFT_N2 - s2) % FFT_N2, FFT_H1, Z_PITCH).astype(BF16)
        x = jnp.concatenate([xf, xb], axis=0)
        _stage1_store(ab_ref, s2, jnp.dot(f1_ref[s2], x, preferred_element_type=F32))
        return c

    lax.fori_loop(0, FFT_N2, stage1, 0, unroll=2 * FFT_UNROLL)

    def stage2(k1, c):
        a = _ld(ab_ref, pl.multiple_of(k1 * AB_PITCH, 8), 2 * FFT_N2).astype(BF16)
        x = jnp.dot(f2_ref[...], a, preferred_element_type=F32)
        o_ref[0, k1, :FFT_N2, :] = (x[:FFT_N2] + skip_ref[0]).astype(BF16)
        o_ref[0, k1, FFT_N2:, :] = x[FFT_N2:].astype(BF16)
        return c

    lax.fori_loop(0, FFT_H1 + 1, stage2, 0, unroll=FFT_UNROLL)


def _filter_spectrum(hid, w3, deltas, filt_bias, consts, length):
    f1, f1b, f2, _, _ = consts
    f1 = jnp.concatenate([f1, f1b], axis=2)
    cb = HY_CB
    ncb = HYENA_WIDTH // cb
    hdim = w3.shape[0]
    full = lambda a: pl.BlockSpec(a.shape, lambda o, c: (0,) * a.ndim)
    return pl.pallas_call(
        functools.partial(_spectrum_kernel, length=length),
        grid=(HYENA_ORDER, ncb),
        in_specs=[
            full(hid),
            pl.BlockSpec((hdim, cb), lambda o, c: (0, (o * N_DIRS) * ncb + c)),
            pl.BlockSpec((hdim, cb), lambda o, c: (0, (o * N_DIRS + 1) * ncb + c)),
            pl.BlockSpec((1, cb), lambda o, c: (0, c)),
            pl.BlockSpec((1, 1, cb), lambda o, c: (o, 0, c)),
            full(f1), full(f2),
        ],
        out_specs=pl.BlockSpec((1, FFT_H1 + 1, 2 * FFT_N2, cb), lambda o, c: (o, 0, 0, c)),
        out_shape=jax.ShapeDtypeStruct((HYENA_ORDER, FFT_H1 + 1, 2 * FFT_N2, HYENA_WIDTH), BF16),
        scratch_shapes=[
            pltpu.VMEM((cb // LANES, FFT_H1 * Z_PITCH, LANES), F32),
            pltpu.VMEM((cb // LANES, FFT_H1 * Z_PITCH, LANES), F32),
            pltpu.VMEM((cb // LANES, K1_PAD * AB_PITCH, LANES), F32),
        ],
        compiler_params=pltpu.CompilerParams(
            dimension_semantics=("arbitrary", "arbitrary"),
            vmem_limit_bytes=VMEM_LIMIT),
        name="filter_spectrum",
    )(hid, w3, w3, deltas, filt_bias[:, None, :], f1, f2)


def _short_conv_block(s_ref, j, w_ref, b_ref):
    big = s_ref[pl.ds(pl.multiple_of(j * FFT_N2, FFT_N2), FFT_N2 + 16), :]
    rows = FFT_N2 + 16
    prev = pltpu.roll(big, 1, 0)[8:8 + FFT_N2]
    nxt = pltpu.roll(big, rows - 1, 0)[8:8 + FFT_N2]
    cur = big[8:8 + FFT_N2]
    return b_ref[...] + prev * w_ref[0:1, :] + cur * w_ref[1:2, :] + nxt * w_ref[2:3, :]


def _load_stream(s_ref, src_ref, length):
    zeros = jnp.zeros((8, s_ref.shape[1]), F32)
    s_ref[0:8, :] = zeros
    s_ref[length + 8:length + 16, :] = zeros
    chunk = 512

    def cp(i, c):
        r = pl.multiple_of(i * chunk, chunk)
        s_ref[pl.ds(r + 8, chunk), :] = src_ref[pl.ds(r, chunk), :].astype(F32)
        return c

    lax.fori_loop(0, length // chunk, cp, 0)


def _hyena_kernel(z_ref, gs_ref, cwz_ref, cbz_ref, cwg_ref, cbg_ref, h_ref, f1_ref, f2_ref, f2i_ref, g_ref,
                  og_ref, o_ref, s_ref, zy_ref, ab_ref, *, length, conv_z, norm):
    if conv_z:
        _load_stream(s_ref, z_ref, length)

    def prep(j, c):
        if conv_z:
            zblk = _short_conv_block(s_ref, j, cwz_ref, cbz_ref)
        else:
            zblk = z_ref[pl.ds(pl.multiple_of(j * FFT_N2, FFT_N2), FFT_N2), :].astype(F32)
        _st(zy_ref, pl.multiple_of(j * Z_PITCH, 8), FFT_N2, zblk)
        return c

    lax.fori_loop(0, FFT_H1, prep, 0)

    def stage1(s2, c):
        xs = _ld(zy_ref, s2, FFT_H1, Z_PITCH).astype(BF16)
        _stage1_store(ab_ref, s2, jnp.dot(f1_ref[s2], xs, preferred_element_type=F32))
        return c

    lax.fori_loop(0, FFT_N2, stage1, 0, unroll=2 * FFT_UNROLL)

    def stage2(k1, c):
        blk = pl.multiple_of(k1 * AB_PITCH, 8)
        x = jnp.dot(f2_ref[...], _ld(ab_ref, blk, 2 * FFT_N2).astype(BF16), preferred_element_type=F32)
        hk = h_ref[0, k1].astype(F32)
        xr, xi, hr, hi = x[:FFT_N2], x[FFT_N2:], hk[:FFT_N2], hk[FFT_N2:]
        p = jnp.concatenate([xr * hr - xi * hi, xr * hi + xi * hr], axis=0).astype(BF16)
        _st(ab_ref, blk, 2 * FFT_N2, jnp.dot(f2i_ref[...], p, preferred_element_type=F32))
        return c

    lax.fori_loop(0, FFT_H1 + 1, stage2, 0, unroll=FFT_UNROLL)

    def stage3(t2, c):
        br = _ld(ab_ref, t2, K1_PAD, AB_PITCH)
        bi = _ld(ab_ref, FFT_N2 + t2, K1_PAD, AB_PITCH)
        rhs = jnp.concatenate([br, bi], axis=0).astype(BF16)
        _st(zy_ref, t2, FFT_H1, jnp.dot(g_ref[t2], rhs, preferred_element_type=F32), stride=Z_PITCH)
        return c

    lax.fori_loop(0, FFT_N2, stage3, 0, unroll=2 * FFT_UNROLL)

    _load_stream(s_ref, gs_ref, length)

    def fin(j, c):
        gate = _short_conv_block(s_ref, j, cwg_ref, cbg_ref)
        y = gate * _ld(zy_ref, pl.multiple_of(j * Z_PITCH, 8), FFT_N2)
        dst = pl.ds(pl.multiple_of(j * FFT_N2, FFT_N2), FFT_N2)
        if norm:
            gw = HYENA_WIDTH // N_HYENA_GROUPS
            for q in range(y.shape[1] // gw):
                sl = slice(q * gw, (q + 1) * gw)
                o_ref[dst, sl] = (_rms(y[:, sl]) * og_ref[:, sl]).astype(o_ref.dtype)
        else:
            o_ref[dst, :] = y.astype(o_ref.dtype)
        return c

    lax.fori_loop(0, FFT_H1, fin, 0)


def _hyena_order(zsrc, zcol, proj, order, conv_w, conv_b, hspec, consts, out_g, batch, length, conv_z, norm):
    f1, _, f2, f2i, g = consts
    cb = HY_CB
    ncb = HYENA_WIDTH // cb
    pcol = 3 * ATTN_WIDTH // cb
    zconv = HYENA_ORDER * ncb
    full = lambda a: pl.BlockSpec(a.shape, lambda c, b: (0,) * a.ndim)
    kern = functools.partial(_hyena_kernel, length=length, conv_z=conv_z, norm=norm)
    return pl.pallas_call(
        kern,
        grid=(ncb, batch),
        in_specs=[
            pl.BlockSpec((length, cb), lambda c, b: (b, zcol + c)),
            pl.BlockSpec((length, cb), lambda c, b: (b, pcol + order * ncb + c)),
            pl.BlockSpec((SHORT_CONV_W, cb), lambda c, b: (0, zconv + c)),
            pl.BlockSpec((1, cb), lambda c, b: (0, zconv + c)),
            pl.BlockSpec((SHORT_CONV_W, cb), lambda c, b: (0, order * ncb + c)),
            pl.BlockSpec((1, cb), lambda c, b: (0, order * ncb + c)),
            pl.BlockSpec((1, FFT_H1 + 1, 2 * FFT_N2, cb), lambda c, b: (order, 0, 0, c)),
            full(f1), full(f2), full(f2i), full(g),
            pl.BlockSpec((1, cb), lambda c, b: (0, c)),
        ],
        out_specs=pl.BlockSpec((length, cb), lambda c, b: (b, c)),
        out_shape=jax.ShapeDtypeStruct((batch * length, HYENA_WIDTH), BF16),
        scratch_shapes=[
            pltpu.VMEM((length + 16, cb), F32),
            pltpu.VMEM((cb // LANES, FFT_H1 * Z_PITCH, LANES), F32),
            pltpu.VMEM((cb // LANES, K1_PAD * AB_PITCH, LANES), F32),
        ],
        compiler_params=pltpu.CompilerParams(
            dimension_semantics=("arbitrary", "arbitrary"),
            vmem_limit_bytes=VMEM_LIMIT),
        name=f"hyena_order{order}",
    )(zsrc, proj, conv_w, conv_b[None], conv_w, conv_b[None], hspec, f1, f2, f2i, g, out_g)


def _hyena(proj, conv_w, conv_b, w1, b1, w2, b2, w3, freq, filt_bias, out_g, batch, length):
    assert 2 * length == FFT_N1 * FFT_N2 and HYENA_ORDER == 2
    consts = _dft_constants()
    max_decay = math.log(DECAY_TARGET) / FAST_DECAY_PCT
    min_decay = math.log(DECAY_TARGET) / SLOW_DECAY_PCT
    deltas = jnp.abs(jnp.linspace(min_decay, max_decay, HYENA_WIDTH, dtype=F32))[None]
    hid = _filter_hidden(length, w1, b1, w2, b2, freq)
    hspec = _filter_spectrum(hid, w3, deltas, filt_bias, consts, length)
    vcol = (3 * ATTN_WIDTH + HYENA_ORDER * HYENA_WIDTH) // HY_CB
    z1 = _hyena_order(proj, vcol, proj, 0, conv_w, conv_b, hspec, consts, out_g, batch, length, True, False)
    return _hyena_order(z1, 0, proj, 1, conv_w, conv_b, hspec, consts, out_g, batch, length, False, True)


def kernel(x, mix_norm_g, w_in, q_norm_g, k_norm_g, rpb, conv_w, conv_b, filt_w1, filt_b1, filt_w2, filt_b2, filt_w3, filt_freq, filt_bias, attn_out_g, hyena_out_g, w_out, ffn_norm_g, w_router, w_gate, w_up, w_down):
    b, s, d = x.shape
    rows = s // GRID_W
    win_r = min(WIN_ROWS_MAX, rows)
    cap = EC_CAPACITY_FACTOR * s // N_EXPERTS
    x2 = x.reshape(b * s, d)
    for i in range(mix_norm_g.shape[0]):
        proj = _inproj(x2, mix_norm_g[i][None], w_in[i],
                       q_norm_g[i][None], k_norm_g[i][None])
        attn = _attention(proj, rpb[i], attn_out_g[i][None], b, s)

        hyn = _hyena(proj, conv_w[i], conv_b[i], filt_w1[i], filt_b1[i], filt_w2[i], filt_b2[i], filt_w3[i],
                     filt_freq[i], filt_bias[i], hyena_out_g[i][None], b, s)

        wr_pad = jnp.zeros((d, 128), BF16).at[:, :N_EXPERTS].set(w_router[i].astype(BF16))
        x1, hn, logits = _outproj(attn, hyn, x2, w_out[i].astype(BF16), ffn_norm_g[i][None], wr_pad)

        idx, gate = _route(logits, b, s, cap)
        x2 = _moe(idx, gate, hn, x1, w_gate[i], w_up[i], w_down[i], b, s)
    return x2.reshape(b, s, d)
```

```python
import functools
import math

import numpy as np
import jax
import jax.numpy as jnp
from jax import lax
from jax.experimental import pallas as pl
from jax.experimental.pallas import tpu as pltpu

D_MODEL = 2048
GRID_W = 64
N_ATTN_HEADS = 8
ATTN_WIDTH = D_MODEL // 2
HEAD_DIM = ATTN_WIDTH // N_ATTN_HEADS
HYENA_WIDTH = D_MODEL - ATTN_WIDTH
N_HYENA_GROUPS = 8
HYENA_ORDER = 2
N_DIRS = 2
SHORT_CONV_W = 3
FILTER_EMB = 33
DECAY_TARGET = 1e-2
FAST_DECAY_PCT = 0.3
SLOW_DECAY_PCT = 1.5
WIN_ROWS_MAX = 8
WIN_COLS = 16
N_EXPERTS = 16
EC_CAPACITY_FACTOR = 2
EXPERT_FF = 1024
IN_WIDTH = 3 * ATTN_WIDTH + (HYENA_ORDER + 1) * HYENA_WIDTH
RMS_EPS = 1e-6

F32 = jnp.float32
BF16 = jnp.bfloat16
NEG_BIG = -1e30
VMEM_LIMIT = 56 * 1024 * 1024
LANES = 128
LOG2_E = math.log2(math.e)


def _rms(x, eps=RMS_EPS):
    return x * lax.rsqrt(jnp.mean(x * x, axis=-1, keepdims=True) + eps)


INPROJ_GROUPS = 2


def _inproj_kernel(x_ref, g_ref, w_ref, qg_ref, kg_ref, o_ref, h_ref):
    n = pl.program_id(1)
    bm, bn = o_ref.shape

    def step(normalise_input, head_gain):
        w = w_ref[...].astype(BF16)
        for r in range(INPROJ_GROUPS):
            rows = slice(r * bm // INPROJ_GROUPS, (r + 1) * bm // INPROJ_GROUPS)
            if normalise_input:
                h_ref[rows, :] = (_rms(x_ref[rows, :]) * g_ref[...]).astype(BF16)
            acc = jnp.dot(h_ref[rows, :], w, preferred_element_type=F32)
            if head_gain is None:
                o_ref[rows, :] = acc.astype(BF16)
            else:
                for j in range(bn // HEAD_DIM):
                    sl = slice(j * HEAD_DIM, (j + 1) * HEAD_DIM)
                    o_ref[rows, sl] = (_rms(acc[:, sl]) * head_gain).astype(BF16)

    @pl.when(n == 0)
    def _():
        step(True, qg_ref[...] * (HEAD_DIM ** -0.5 * LOG2_E))

    @pl.when(n == 1)
    def _():
        step(False, kg_ref[...])

    @pl.when(n >= 2)
    def _():
        step(False, None)


def _inproj(x2, g, w_bf, qg, kg, bm=1024, bn=1024):
    m, d = x2.shape
    nw = w_bf.shape[1]
    return pl.pallas_call(
        _inproj_kernel,
        grid=(m // bm, nw // bn),
        in_specs=[
            pl.BlockSpec((bm, d), lambda i, j: (i, 0)),
            pl.BlockSpec((1, d), lambda i, j: (0, 0)),
            pl.BlockSpec((d, bn), lambda i, j: (0, j)),
            pl.BlockSpec((1, HEAD_DIM), lambda i, j: (0, 0)),
            pl.BlockSpec((1, HEAD_DIM), lambda i, j: (0, 0)),
        ],
        out_specs=pl.BlockSpec((bm, bn), lambda i, j: (i, j)),
        out_shape=jax.ShapeDtypeStruct((m, nw), BF16),
        scratch_shapes=[pltpu.VMEM((bm, d), BF16)],
        compiler_params=pltpu.CompilerParams(
            dimension_semantics=("arbitrary", "arbitrary"),
            vmem_limit_bytes=VMEM_LIMIT),
        name="inproj",
    )(x2, g, w_bf, qg, kg)


ATTN_ROWS = 4
ATTN_BAND = 12


def _attn_kernel(q_ref, k_ref, v_ref, rpb_ref, mask_ref, g_ref, o_ref, pt_ref, *, rows, win_r):
    lanes = pt_ref.shape[-1]
    n_dr = 2 * WIN_ROWS_MAX - 1

    def toeplitz(d):
        row = jnp.broadcast_to(rpb_ref[0, d:d + 1, :], (GRID_W, lanes))
        return pltpu.roll(row, lanes - (WIN_COLS - 1), 1, stride=1, stride_axis=0)

    lane = lax.broadcasted_iota(jnp.int32, (GRID_W, lanes), 1)
    left = lane < GRID_W
    tiles = [toeplitz(d) for d in range(n_dr)]
    for d in range(-1, n_dr):
        lo, hi = tiles[max(d, 0)], tiles[min(d + 1, n_dr - 1)]
        pt_ref[d + 1] = jnp.where(left, lo, pltpu.roll(hi, GRID_W, 1)) * LOG2_E + mask_ref[...]

    def body(i, carry):
        r0 = i * ATTN_ROWS
        bs = jnp.minimum(jnp.clip(r0 - win_r // 2, 0, rows - win_r), rows - ATTN_BAND)
        q0 = pl.multiple_of(r0 * GRID_W, ATTN_ROWS * GRID_W)
        k0 = pl.multiple_of(bs * GRID_W, 2 * GRID_W)
        q = q_ref[pl.ds(q0, ATTN_ROWS * GRID_W), :]
        k = k_ref[pl.ds(k0, ATTN_BAND * GRID_W), :]
        v = v_ref[pl.ds(k0, ATTN_BAND * GRID_W), :]
        s = lax.dot_general(q, k, (((1,), (1,)), ((), ())), preferred_element_type=F32)

        bias_rows = []
        for j in range(ATTN_ROWS):
            r = r0 + j
            start = jnp.clip(r - win_r // 2, 0, rows - win_r)
            pieces = []
            for m in range(ATTN_BAND // 2):
                kr = bs + 2 * m
                pen_l = jnp.where((kr >= start) & (kr < start + win_r), 0.0, NEG_BIG)
                pen_r = jnp.where((kr + 1 >= start) & (kr + 1 < start + win_r), 0.0, NEG_BIG)
                d = jnp.clip(kr - r + WIN_ROWS_MAX - 1, -1, n_dr - 1)
                pieces.append(pt_ref[d + 1] + jnp.where(left, pen_l, pen_r))
            bias_rows.append(jnp.concatenate(pieces, axis=1))
        s = s + jnp.concatenate(bias_rows, axis=0)

        mx = jnp.max(s, axis=-1, keepdims=True)
        p = jnp.exp2(s - mx)
        l = jnp.sum(p, axis=-1, keepdims=True)
        o = jnp.dot(p.astype(BF16), v, preferred_element_type=F32) / l
        o = _rms(o) * g_ref[...]
        o_ref[pl.ds(q0, ATTN_ROWS * GRID_W), :] = o.astype(BF16)
        return carry

    lax.fori_loop(0, rows // ATTN_ROWS, body, 0, unroll=2)


def _attn_window_mask():
    c = np.arange(GRID_W)
    c0 = np.clip(c - WIN_COLS // 2, 0, GRID_W - WIN_COLS)
    in_win = (c[None, :] >= c0[:, None]) & (c[None, :] < c0[:, None] + WIN_COLS)
    return jnp.asarray(np.tile(np.where(in_win, 0.0, NEG_BIG), (1, 2)), dtype=F32)


def _attention(proj, rpb, out_g, batch, seq):
    rows = seq // GRID_W
    win_r = min(WIN_ROWS_MAX, rows)
    assert win_r == WIN_ROWS_MAX and 2 * GRID_W == LANES
    assert rows % ATTN_ROWS == 0 and ATTN_BAND >= win_r + ATTN_ROWS - 1 and ATTN_BAND % 2 == 0
    h = N_ATTN_HEADS
    n_dr, n_dc = rpb.shape[1:]
    rpb_pad = jnp.zeros((h, 2 * WIN_ROWS_MAX, LANES), F32).at[:, :n_dr, :n_dc].set(rpb)
    mask = _attn_window_mask()
    kern = functools.partial(_attn_kernel, rows=rows, win_r=win_r)
    return pl.pallas_call(
        kern,
        grid=(batch, h),
        in_specs=[
            pl.BlockSpec((seq, HEAD_DIM), lambda b, i: (b, i)),
            pl.BlockSpec((seq, HEAD_DIM), lambda b, i: (b, h + i)),
            pl.BlockSpec((seq, HEAD_DIM), lambda b, i: (b, 2 * h + i)),
            pl.BlockSpec((1, 2 * WIN_ROWS_MAX, LANES), lambda b, i: (i, 0, 0)),
            pl.BlockSpec(mask.shape, lambda b, i: (0, 0)),
            pl.BlockSpec((1, HEAD_DIM), lambda b, i: (0, i)),
        ],
        out_specs=pl.BlockSpec((seq, HEAD_DIM), lambda b, i: (b, i)),
        out_shape=jax.ShapeDtypeStruct((batch * seq, ATTN_WIDTH), BF16),
        scratch_shapes=[pltpu.VMEM((2 * WIN_ROWS_MAX, GRID_W, LANES), F32)],
        compiler_params=pltpu.CompilerParams(
            dimension_semantics=("arbitrary", "arbitrary"),
            vmem_limit_bytes=VMEM_LIMIT),
        name="na2d_attn",
    )(proj, proj, proj, rpb_pad, mask, out_g)


OUTPROJ_GROUPS = 2


def _outproj_kernel(a_ref, hy_ref, x_ref, wo_ref, g_ref, wr_ref, x1_ref, hn_ref, lg_ref):
    half = a_ref.shape[1]
    bm = x_ref.shape[0]
    for r in range(OUTPROJ_GROUPS):
        rows = slice(r * bm // OUTPROJ_GROUPS, (r + 1) * bm // OUTPROJ_GROUPS)
        acc = x_ref[rows, :]
        acc = acc + jnp.dot(a_ref[rows, :], wo_ref[:half, :], preferred_element_type=F32)
        acc = acc + jnp.dot(hy_ref[rows, :], wo_ref[half:, :], preferred_element_type=F32)
        x1_ref[rows, :] = acc
        hn = _rms(acc) * g_ref[...]
        hn_ref[rows, :] = hn
        lg_ref[rows, :] = jnp.dot(hn.astype(BF16), wr_ref[...], preferred_element_type=F32)


def _outproj(attn, hy, x2, wo_bf, g, wr_pad, bm=512):
    m, d = x2.shape
    half = attn.shape[1]
    npad = wr_pad.shape[1]
    return pl.pallas_call(
        _outproj_kernel,
        grid=(m // bm,),
        in_specs=[
            pl.BlockSpec((bm, half), lambda i: (i, 0)),
            pl.BlockSpec((bm, half), lambda i: (i, 0)),
            pl.BlockSpec((bm, d), lambda i: (i, 0)),
            pl.BlockSpec((d, d), lambda i: (0, 0)),
            pl.BlockSpec((1, d), lambda i: (0, 0)),
            pl.BlockSpec((d, npad), lambda i: (0, 0)),
        ],
        out_specs=[
            pl.BlockSpec((bm, d), lambda i: (i, 0)),
            pl.BlockSpec((bm, d), lambda i: (i, 0)),
            pl.BlockSpec((bm, npad), lambda i: (i, 0)),
        ],
        out_shape=[
            jax.ShapeDtypeStruct((m, d), F32),
            jax.ShapeDtypeStruct((m, d), F32),
            jax.ShapeDtypeStruct((m, npad), F32),
        ],
        compiler_params=pltpu.CompilerParams(
            dimension_semantics=("arbitrary",),
            vmem_limit_bytes=VMEM_LIMIT),
        name="outproj",
    )(attn, hy, x2, wo_bf, g, wr_pad)


ROUTE_BLK = 128
ROUTE_CHUNK = 512
SLOT_GRP = 8
ROUTE_GROUPS_INLINE = 3
F32_INF_BITS = 0x7F800000


def _route_kernel(lg_ref, tri_ref, idx_ref, gate_ref, aff_ref, pos_t_ref, aff_t_ref, off_ref, ia_ref, ga_ref,
                  stage_i, stage_g, off_smem, sem, *, cap):
    s, lanes = lg_ref.shape
    lane = lax.broadcasted_iota(jnp.int32, (1, lanes), 1)

    def softmax_chunk(i, c):
        rows = pl.ds(pl.multiple_of(i * ROUTE_CHUNK, ROUTE_CHUNK), ROUTE_CHUNK)
        lg = jnp.where(lane < N_EXPERTS, lg_ref[rows, :], NEG_BIG)
        ex = jnp.exp(lg - jnp.max(lg, axis=-1, keepdims=True))
        aff_ref[rows, :] = ex / jnp.sum(ex, axis=-1, keepdims=True)
        return c

    lax.fori_loop(0, s // ROUTE_CHUNK, softmax_chunk, 0)

    def as_value(bits):
        return pltpu.bitcast(bits, F32)

    def fold8(x):
        parts = [x[i:i + 8] for i in range(0, x.shape[0], 8)]
        while len(parts) > 1:
            parts = [a + b for a, b in zip(parts[::2], parts[1::2])]
        return parts[0]

    def count_ge(th):
        def body(i, acc):
            rows = pl.ds(pl.multiple_of(i * ROUTE_CHUNK, ROUTE_CHUNK), ROUTE_CHUNK)
            return acc + fold8(jnp.where(aff_ref[rows, :] >= th, 1.0, 0.0))
        acc = lax.fori_loop(0, s // ROUTE_CHUNK, body, jnp.zeros((8, lanes), F32))
        return jnp.sum(acc, axis=0, keepdims=True)

    def bisect(_, c):
        lo, hi = c
        mid = lo + ((hi - lo) >> 1)
        ok = count_ge(as_value(mid)) >= cap
        return jnp.where(ok, mid, lo), jnp.where(ok, hi, mid)

    tau_bits, _ = lax.fori_loop(0, 31, bisect, (jnp.zeros((1, lanes), jnp.int32),
                                                jnp.full((1, lanes), F32_INF_BITS, jnp.int32)))
    tau, above = as_value(tau_bits), as_value(tau_bits + 1)
    ties_wanted = cap - count_ge(above)

    def scan_block(j, c):
        tie_carry, sel_carry = c
        rows = pl.ds(pl.multiple_of(j * ROUTE_BLK, ROUTE_BLK), ROUTE_BLK)
        aff = aff_ref[rows, :]
        gt = aff >= above
        eq = jnp.where((aff >= tau) & (aff < above), 1.0, 0.0)
        tie_incl = jnp.dot(tri_ref[...], eq.astype(BF16), preferred_element_type=F32) + tie_carry
        sel = jnp.where(gt | ((eq > 0.0) & (tie_incl - eq < ties_wanted)), 1.0, 0.0)
        sel_incl = jnp.dot(tri_ref[...], sel.astype(BF16), preferred_element_type=F32) + sel_carry
        pos = jnp.where(sel > 0.0, sel_incl - sel, -1.0)
        pos_t_ref[j] = pos.T[:N_EXPERTS]
        aff_t_ref[j] = aff.T[:N_EXPERTS]
        off_ref[pl.ds(j, 1), :] = sel_carry.astype(jnp.int32)
        return tie_incl[ROUTE_BLK - 1:, :], sel_incl[ROUTE_BLK - 1:, :]

    n_blk = s // ROUTE_BLK
    off_ref[...] = jnp.zeros_like(off_ref)
    zero = jnp.zeros((1, lanes), F32)
    _, total = lax.fori_loop(0, n_blk, scan_block, (zero, zero), unroll=4)
    off_ref[pl.ds(n_blk, 1), :] = total.astype(jnp.int32)

    to_smem = pltpu.make_async_copy(off_ref, off_smem, sem)
    to_smem.start()
    to_smem.wait()

    n_grp = cap // SLOT_GRP
    sub = lax.broadcasted_iota(jnp.int32, (SLOT_GRP, lanes), 0)
    lane_i = lax.broadcasted_iota(jnp.int32, (SLOT_GRP, lanes), 1)
    stage_i[...] = jnp.zeros_like(stage_i)
    stage_g[...] = jnp.zeros_like(stage_g)
    for e in range(N_EXPERTS):
        ia_ref[...] = jnp.zeros_like(ia_ref)
        ga_ref[...] = jnp.zeros_like(ga_ref)

        def block(j, carry, e=e):
            first, last = off_smem[j, e], off_smem[j + 1, e]
            pos = jnp.broadcast_to(pos_t_ref[j, e:e + 1, :], (SLOT_GRP, lanes))
            aff = jnp.broadcast_to(aff_t_ref[j, e:e + 1, :], (SLOT_GRP, lanes))
            tok = (j * ROUTE_BLK + lane_i).astype(F32)
            g0 = first // SLOT_GRP
            groups = jnp.where(last > first, (last - 1) // SLOT_GRP - g0 + 1, 0)

            def update(g):
                match = pos == (g * SLOT_GRP + sub).astype(F32)
                gi = jnp.minimum(g, n_grp - 1)
                ia_ref[gi] += jnp.where(match, tok, 0.0)
                ga_ref[gi] += jnp.where(match, aff, 0.0)

            for k in range(ROUTE_GROUPS_INLINE):
                update(g0 + k)

            def rest(k, c):
                update(g0 + k)
                return c

            lax.fori_loop(ROUTE_GROUPS_INLINE, groups, rest, 0)
            return carry

        lax.fori_loop(0, n_blk, block, 0)
        stage_i[:, e:e + 1] = jnp.sum(ia_ref[...].reshape(cap, lanes), axis=1, keepdims=True)
        stage_g[:, e:e + 1] = jnp.sum(ga_ref[...].reshape(cap, lanes), axis=1, keepdims=True)

    idx_ref[0] = stage_i[...].T[:N_EXPERTS].astype(jnp.int32)
    gate_ref[0] = stage_g[...].T[:N_EXPERTS]


def _route(logits, batch, seq, cap):
    lanes = logits.shape[1]
    assert lanes == LANES == ROUTE_BLK and cap % SLOT_GRP == 0
    n_blk = seq // ROUTE_BLK
    n_off = -(-(n_blk + 1) // 8) * 8
    tri =jnp.asarray(np.tril(np.ones((ROUTE_BLK, ROUTE_BLK))), dtype=BF16)
    return pl.pallas_call(
        functools.partial(_route_kernel, cap=cap),
        grid=(batch,),
        in_specs=[
            pl.BlockSpec((seq, lanes), lambda b: (b, 0)),
            pl.BlockSpec((ROUTE_BLK, ROUTE_BLK), lambda b: (0, 0)),
        ],
        out_specs=[
            pl.BlockSpec((1, N_EXPERTS, cap), lambda b: (b, 0, 0)),
            pl.BlockSpec((1, N_EXPERTS, cap), lambda b: (b, 0, 0)),
        ],
        out_shape=[
            jax.ShapeDtypeStruct((batch, N_EXPERTS, cap), jnp.int32),
            jax.ShapeDtypeStruct((batch, N_EXPERTS, cap), F32),
        ],
        scratch_shapes=[
            pltpu.VMEM((seq, lanes), F32),
            pltpu.VMEM((n_blk, N_EXPERTS, lanes), F32),
            pltpu.VMEM((n_blk, N_EXPERTS, lanes), F32),
            pltpu.VMEM((n_off, lanes), jnp.int32),
            pltpu.VMEM((cap // SLOT_GRP, SLOT_GRP, lanes), F32),
            pltpu.VMEM((cap // SLOT_GRP, SLOT_GRP, lanes), F32),
            pltpu.VMEM((cap, lanes), F32),
            pltpu.VMEM((cap, lanes), F32),
            pltpu.SMEM((n_off, lanes), jnp.int32),
            pltpu.SemaphoreType.DMA(()),
        ],
        compiler_params=pltpu.CompilerParams(dimension_semantics=("arbitrary",)),
        name="route",
    )(logits, tri)


MOE_FF_SPLIT = 4
ROW_UNROLL = 8


def _moe_kernel(idx_ref, hn_hbm, wg_ref, wu_ref, wd_ref, gate_ref, x1_hbm, out_hbm,
                xg0, xg1, acc0, acc1, y0, y1, sem_x, sem_a, sem_s, *, cap, seq):
    del x1_hbm
    e, f = pl.program_id(0), pl.program_id(1)
    n_e = pl.num_programs(0)
    xg, acc = (xg0, xg1), (acc0, acc1)
    prev_e, next_e = jnp.maximum(e - 1, 0), jnp.minimum(e + 1, n_e - 1)

    def row_of(ee, b):
        base = (b * n_e + ee) * cap
        return lambda p: b * seq + idx_ref[base + p]

    def copy_rows(ee, b, kind, rolled=False):
        row = row_of(ee, b)

        def one(p):
            if kind == "xg":
                src, dst, sem = hn_hbm.at[pl.ds(row(p), 1), :], xg[b].at[pl.ds(p, 1), :], sem_x.at[b]
            elif kind == "ag":
                src, dst, sem = out_hbm.at[pl.ds(row(p), 1), :], acc[b].at[pl.ds(p, 1), :], sem_a.at[b]
            else:
                src, dst, sem = acc[b].at[pl.ds(p, 1), :], out_hbm.at[pl.ds(row(p), 1), :], sem_s.at[b]
            pltpu.make_async_copy(src, dst, sem).start()

        if rolled:
            def body(i, c):
                for k in range(ROW_UNROLL):
                    one(i * ROW_UNROLL + k)
                return c
            lax.fori_loop(0, cap // ROW_UNROLL, body, 0)
        else:
            for p in range(cap):
                one(p)

    def wait_rows(buf, sem):
        pltpu.make_async_copy(hn_hbm.at[pl.ds(0, cap), :], buf, sem).wait()

    def partial_out(b, wg, wu, wd):
        x = xg[b][...].astype(BF16)
        a = jnp.dot(x, wg, preferred_element_type=F32)
        u = jnp.dot(x, wu, preferred_element_type=F32)
        hmid = (a * (1.0 / (1.0 + jnp.exp(-a))) * u).astype(BF16)
        return jnp.dot(hmid, wd, preferred_element_type=F32)

    def weights():
        return wg_ref[0].astype(BF16), wu_ref[0].astype(BF16), wd_ref[0].astype(BF16)

    @pl.when((e == 0) & (f == 0))
    def _():
        copy_rows(e, 0, "xg", rolled=True)
        copy_rows(e, 0, "ag", rolled=True)
        copy_rows(e, 1, "ag", rolled=True)
        wait_rows(acc0, sem_a.at[0])
        wait_rows(acc1, sem_a.at[1])

    @pl.when(f == 0)
    def _():
        w = weights()
        wait_rows(xg0, sem_x.at[0])
        copy_rows(e, 1, "xg")
        y0[...] = partial_out(0, *w)
        wait_rows(xg1, sem_x.at[1])
        copy_rows(prev_e, 0, "sc")
        y1[...] = partial_out(1, *w)

    @pl.when(f == 1)
    def _():
        w = weights()
        copy_rows(prev_e, 1, "sc")
        y0[...] += partial_out(0, *w)
        wait_rows(acc0, sem_s.at[0])
        copy_rows(e, 0, "ag")
        y1[...] += partial_out(1, *w)

    @pl.when(f == 2)
    def _():
        w = weights()
        wait_rows(acc1, sem_s.at[1])
        copy_rows(e, 1, "ag")
        y0[...] += partial_out(0, *w)
        y1[...] += partial_out(1, *w)

    @pl.when(f == 3)
    def _():
        w = weights()
        t0 = y0[...] + partial_out(0, *w)
        wait_rows(acc0, sem_a.at[0])
        acc0[...] += t0 * gate_ref[0, 0]
        copy_rows(next_e, 0, "xg")
        t1 = y1[...] + partial_out(1, *w)
        wait_rows(acc1, sem_a.at[1])
        acc1[...] += t1 * gate_ref[1, 0]

    @pl.when((e == n_e - 1) & (f == MOE_FF_SPLIT - 1))
    def _():
        copy_rows(e, 0, "sc", rolled=True)
        copy_rows(e, 1, "sc", rolled=True)
        wait_rows(acc0, sem_s.at[0])
        wait_rows(acc1, sem_s.at[1])
        wait_rows(xg0, sem_x.at[0])


def _moe(idx, gate, hn, x1, wg, wu, wd, batch, seq):
    n_e, d, ff = wg.shape
    cap = idx.shape[-1]
    assert batch == 2 and MOE_FF_SPLIT == 4 and ff % MOE_FF_SPLIT == 0
    fq = ff // MOE_FF_SPLIT
    grid_spec = pltpu.PrefetchScalarGridSpec(
        num_scalar_prefetch=1,
        grid=(n_e, MOE_FF_SPLIT),
        in_specs=[
            pl.BlockSpec(memory_space=pl.ANY),
            pl.BlockSpec((1, d, fq), lambda i, j, s: (i, 0, j)),
            pl.BlockSpec((1, d, fq), lambda i, j, s: (i, 0, j)),
            pl.BlockSpec((1, fq, d), lambda i, j, s: (i, j, 0)),
            pl.BlockSpec((batch, 1, cap, 1), lambda i, j, s: (0, i, 0, 0)),
            pl.BlockSpec(memory_space=pl.ANY),
        ],
        out_specs=pl.BlockSpec(memory_space=pl.ANY),
        scratch_shapes=(
            [pltpu.VMEM((cap, d), F32) for _ in range(6)]
            + [pltpu.SemaphoreType.DMA((batch,)) for _ in range(3)]),
    )
    return pl.pallas_call(
        functools.partial(_moe_kernel, cap=cap, seq=seq),
        grid_spec=grid_spec,
        out_shape=jax.ShapeDtypeStruct(x1.shape, x1.dtype),
        input_output_aliases={6: 0},
        compiler_params=pltpu.CompilerParams(
            dimension_semantics=("arbitrary", "arbitrary"),
            vmem_limit_bytes=VMEM_LIMIT),
        name="moe_experts",
    )(idx.reshape(-1), hn, wg, wu, wd, gate[..., None], x1)


FFT_N1 = 128
FFT_N2 = 64
FFT_H1 = FFT_N1 // 2
K1_PAD = 72
Z_PITCH = 72
AB_PITCH = 136
HY_CB = 256
FFT_UNROLL = 8
HP = lax.Precision.HIGHEST


def _dft_constants():
    n, n1, n2, h1 = FFT_N1 * FFT_N2, FFT_N1, FFT_N2, FFT_H1
    k1 = np.arange(h1 + 1, dtype=np.float64)
    s1 = np.arange(h1, dtype=np.float64)
    s2 = np.arange(n2, dtype=np.float64)
    tw = s2[:, None, None] * k1[None, :, None] / n

    def stage1(phase_s1):
        th = -2.0 * np.pi * (phase_s1 + tw)
        m = np.zeros((n2, 2 * K1_PAD, h1))
        m[:, :h1 + 1] = np.cos(th)
        m[:, K1_PAD:K1_PAD + h1 + 1] = np.sin(th)
        return m

    f1 = stage1(s1[None, None, :] * k1[None, :, None] / n1)
    f1b = stage1(((n1 - 1) - s1)[None, None, :] * k1[None, :, None] / n1)
    f1b0 = stage1((n1 - s1)[None, None, :] * k1[None, :, None] / n1)[0]
    f1b0[:, 0] = 0.0
    f1b[0] = f1b0

    a = 2.0 * np.pi * np.outer(np.arange(n2), np.arange(n2)) / n2
    c, s = np.cos(a), np.sin(a)
    f2 = np.block([[c, s], [-s, c]])
    f2i = np.block([[c, -s], [s, c]])

    t1 = np.arange(h1, dtype=np.float64)
    th = 2.0 * np.pi * (t1[None, :, None] * k1[None, None, :] / n1 + s2[:, None, None] * k1[None, None, :] / n)
    wgt = np.where((k1 == 0) | (k1 == h1), 1.0, 2.0) / n
    g = np.zeros((n2, h1, 2 * K1_PAD))
    g[:, :, :h1 + 1] = wgt * np.cos(th)
    g[:, :, K1_PAD:K1_PAD + h1 + 1] = -wgt * np.sin(th)
    return tuple(jnp.asarray(m, dtype=F32).astype(BF16) for m in (f1, f1b, f2, f2i, g))


def _hid_kernel(f_ref, w1t_ref, w1c_ref, w1s_ref, b1_ref, w2_ref, b2_ref, fr_ref, o_ref, *, length):
    bm = o_ref.shape[0]
    pos = (pl.program_id(0) * bm + lax.broadcasted_iota(jnp.int32, (1, bm), 1)).astype(F32)
    t = pos / (length - 1.0)
    ang = f_ref[...] * (2.0 * math.pi * pos / length)
    pre = (w1t_ref[...] * t
           + jnp.dot(w1c_ref[...], jnp.cos(ang), precision=HP, preferred_element_type=F32)
           - jnp.dot(w1s_ref[...], jnp.sin(ang), precision=HP, preferred_element_type=F32))
    fr = fr_ref[...]
    hid = jnp.sin(fr * (pre + b1_ref[...]))
    hid = jnp.sin(fr * (jnp.dot(w2_ref[...], hid, precision=HP, preferred_element_type=F32) + b2_ref[...]))
    hi = hid.astype(BF16).astype(F32)
    lo = hid - hi
    o_ref[...] = jnp.concatenate([hi, hi, lo, jnp.zeros_like(hi)], axis=0).T.astype(BF16)


def _filter_hidden(length, w1, b1, w2, b2, freq, bm=512):
    bands = (FILTER_EMB - 1) // 2
    f = jnp.linspace(1e-4, bands - 1, bands, dtype=F32)[:, None]
    hdim = w1.shape[1]
    full = lambda a: pl.BlockSpec(a.shape, lambda i: (0,) * a.ndim)
    args = (f, w1[:1].T, w1[1:1 + bands].T, w1[1 + bands:].T, b1[:, None], w2.T, b2[:, None], freq[:, None])
    return pl.pallas_call(
        functools.partial(_hid_kernel, length=length),
        grid=(length // bm,),
        in_specs=[full(a) for a in args],
        out_specs=pl.BlockSpec((bm, 4 * hdim), lambda i: (i, 0)),
        out_shape=jax.ShapeDtypeStruct((length, 4 * hdim), BF16),
        name="filter_hidden",
    )(*args)


def _ld(ref, start, size, stride=None):
    idx = pl.ds(start, size) if stride is None else pl.ds(start, size, stride=stride)
    return jnp.concatenate([ref[h, idx, :] for h in range(ref.shape[0])], axis=-1)


def _st(ref, start, size, val, stride=None):
    idx = pl.ds(start, size) if stride is None else pl.ds(start, size, stride=stride)
    for h in range(ref.shape[0]):
        ref[h, idx, :] = val[:, h * LANES:(h + 1) * LANES]


def _stage1_store(ab_ref, s2, a):
    _st(ab_ref, s2, K1_PAD, a[:K1_PAD], stride=AB_PITCH)
    _st(ab_ref, FFT_N2 + s2, K1_PAD, a[K1_PAD:], stride=AB_PITCH)


def _stack_split_rows(w):
    hi = w.astype(BF16)
    lo = (w - hi.astype(F32)).astype(BF16)
    return jnp.concatenate([hi, lo, hi, jnp.zeros_like(hi)], axis=0)


def _spectrum_kernel(hid_ref, w3f_ref, w3b_ref, dl_ref, skip_ref, f1_ref, f2_ref, o_ref,
                     hf_ref, hb_ref, ab_ref, *, length):
    blocks = 4
    rows = blocks * FFT_N2
    w3f, w3b = _stack_split_rows(w3f_ref[...]), _stack_split_rows(w3b_ref[...])

    def gen(j, c):
        row = (j * rows + lax.broadcasted_iota(jnp.int32, (rows, 1), 0)).astype(F32)
        win = jnp.exp(-(row / (length - 1.0)) * dl_ref[...])
        hid = hid_ref[pl.ds(pl.multiple_of(j * rows, rows), rows), :]
        hf = jnp.dot(hid, w3f, preferred_element_type=F32) * win
        hb = jnp.dot(hid, w3b, preferred_element_type=F32) * win
        for q in range(blocks):
            dst = pl.multiple_of((j * blocks + q) * Z_PITCH, 8)
            _st(hf_ref, dst, FFT_N2, hf[q * FFT_N2:(q + 1) * FFT_N2])
            _st(hb_ref, dst, FFT_N2, hb[q * FFT_N2:(q + 1) * FFT_N2])
        return c

    lax.fori_loop(0, FFT_H1 // blocks, gen, 0)

    def stage1(s2, c):
        xf = _ld(hf_ref, s2, FFT_H1, Z_PITCH).astype(BF16)
        xb = _ld(hb_ref, (FFT_N2 - s2) % FFT_N2, FFT_H1, Z_PITCH).astype(BF16)
        x = jnp.concatenate([xf, xb], axis=0)
        _stage1_store(ab_ref, s2, jnp.dot(f1_ref[s2], x, preferred_element_type=F32))
        return c

    lax.fori_loop(0, FFT_N2, stage1, 0, unroll=2 * FFT_UNROLL)

    def stage2(k1, c):
        a = _ld(ab_ref, pl.multiple_of(k1 * AB_PITCH, 8), 2 * FFT_N2).astype(BF16)
        x = jnp.dot(f2_ref[...], a, preferred_element_type=F32)
        o_ref[0, k1, :FFT_N2, :] = (x[:FFT_N2] + skip_ref[0]).astype(BF16)
        o_ref[0, k1, FFT_N2:, :] = x[FFT_N2:].astype(BF16)
        return c

    lax.fori_loop(0, FFT_H1 + 1, stage2, 0, unroll=FFT_UNROLL)


def _filter_spectrum(hid, w3, deltas, filt_bias, consts, length):
    f1, f1b, f2, _, _ = consts
    f1 = jnp.concatenate([f1, f1b], axis=2)
    cb = HY_CB
    ncb = HYENA_WIDTH // cb
    hdim = w3.shape[0]
    full = lambda a: pl.BlockSpec(a.shape, lambda o, c: (0,) * a.ndim)
    return pl.pallas_call(
        functools.partial(_spectrum_kernel, length=length),
        grid=(HYENA_ORDER, ncb),
        in_specs=[
            full(hid),
            pl.BlockSpec((hdim, cb), lambda o, c: (0, (o * N_DIRS) * ncb + c)),
            pl.BlockSpec((hdim, cb), lambda o, c: (0, (o * N_DIRS + 1) * ncb + c)),
            pl.BlockSpec((1, cb), lambda o, c: (0, c)),
            pl.BlockSpec((1, 1, cb), lambda o, c: (o, 0, c)),
            full(f1), full(f2),
        ],
        out_specs=pl.BlockSpec((1, FFT_H1 + 1, 2 * FFT_N2, cb), lambda o, c: (o, 0, 0, c)),
        out_shape=jax.ShapeDtypeStruct((HYENA_ORDER, FFT_H1 + 1, 2 * FFT_N2, HYENA_WIDTH), BF16),
        scratch_shapes=[
            pltpu.VMEM((cb // LANES, FFT_H1 * Z_PITCH, LANES), F32),
            pltpu.VMEM((cb // LANES, FFT_H1 * Z_PITCH, LANES), F32),
            pltpu.VMEM((cb // LANES, K1_PAD * AB_PITCH, LANES), F32),
        ],
        compiler_params=pltpu.CompilerParams(
            dimension_semantics=("arbitrary", "arbitrary"),
            vmem_limit_bytes=VMEM_LIMIT),
        name="filter_spectrum",
    )(hid, w3, w3, deltas, filt_bias[:, None, :], f1, f2)


def _short_conv_block(s_ref, j, w_ref, b_ref):
    big = s_ref[pl.ds(pl.multiple_of(j * FFT_N2, FFT_N2), FFT_N2 + 16), :]
    rows = FFT_N2 + 16
    prev = pltpu.roll(big, 1, 0)[8:8 + FFT_N2]
    nxt = pltpu.roll(big, rows - 1, 0)[8:8 + FFT_N2]
    cur = big[8:8 + FFT_N2]
    return b_ref[...] + prev * w_ref[0:1, :] + cur * w_ref[1:2, :] + nxt * w_ref[2:3, :]


def _load_stream(s_ref, src_ref, length):
    zeros = jnp.zeros((8, s_ref.shape[1]), F32)
    s_ref[0:8, :] = zeros
    s_ref[length + 8:length + 16, :] = zeros
    chunk = 512

    def cp(i, c):
        r = pl.multiple_of(i * chunk, chunk)
        s_ref[pl.ds(r + 8, chunk), :] = src_ref[pl.ds(r, chunk), :].astype(F32)
        return c

    lax.fori_loop(0, length // chunk, cp, 0)


def _hyena_kernel(z_ref, gs_ref, cwz_ref, cbz_ref, cwg_ref, cbg_ref, h_ref, f1_ref, f2_ref, f2i_ref, g_ref,
                  og_ref, o_ref, s_ref, zy_ref, ab_ref, *, length, conv_z, norm):
    if conv_z:
        _load_stream(s_ref, z_ref, length)

    def prep(j, c):
        if conv_z:
            zblk = _short_conv_block(s_ref, j, cwz_ref, cbz_ref)
        else:
            zblk = z_ref[pl.ds(pl.multiple_of(j * FFT_N2, FFT_N2), FFT_N2), :].astype(F32)
        _st(zy_ref, pl.multiple_of(j * Z_PITCH, 8), FFT_N2, zblk)
        return c

    lax.fori_loop(0, FFT_H1, prep, 0, unroll=2)

    def stage1(s2, c):
        xs = _ld(zy_ref, s2, FFT_H1, Z_PITCH).astype(BF16)
        _stage1_store(ab_ref, s2, jnp.dot(f1_ref[s2], xs, preferred_element_type=F32))
        return c

    lax.fori_loop(0, FFT_N2, stage1, 0, unroll=2 * FFT_UNROLL)

    def stage2(k1, c):
        blk = pl.multiple_of(k1 * AB_PITCH, 8)
        x = jnp.dot(f2_ref[...], _ld(ab_ref, blk, 2 * FFT_N2).astype(BF16), preferred_element_type=F32)
        hk = h_ref[0, k1].astype(F32)
        xr, xi, hr, hi = x[:FFT_N2], x[FFT_N2:], hk[:FFT_N2], hk[FFT_N2:]
        p = jnp.concatenate([xr * hr - xi * hi, xr * hi + xi * hr], axis=0).astype(BF16)
        _st(ab_ref, blk, 2 * FFT_N2, jnp.dot(f2i_ref[...], p, preferred_element_type=F32))
        return c

    lax.fori_loop(0, FFT_H1 + 1, stage2, 0, unroll=FFT_UNROLL)

    def stage3(t2, c):
        br = _ld(ab_ref, t2, K1_PAD, AB_PITCH)
        bi = _ld(ab_ref, FFT_N2 + t2, K1_PAD, AB_PITCH)
        rhs = jnp.concatenate([br, bi], axis=0).astype(BF16)
        _st(zy_ref, t2, FFT_H1, jnp.dot(g_ref[t2], rhs, preferred_element_type=F32), stride=Z_PITCH)
        return c

    lax.fori_loop(0, FFT_N2, stage3, 0, unroll=2 * FFT_UNROLL)

    _load_stream(s_ref, gs_ref, length)

    def fin(j, c):
        gate = _short_conv_block(s_ref, j, cwg_ref, cbg_ref)
        y = gate * _ld(zy_ref, pl.multiple_of(j * Z_PITCH, 8), FFT_N2)
        dst = pl.ds(pl.multiple_of(j * FFT_N2, FFT_N2), FFT_N2)
        if norm:
            gw = HYENA_WIDTH // N_HYENA_GROUPS
            for q in range(y.shape[1] // gw):
                sl = slice(q * gw, (q + 1) * gw)
                o_ref[dst, sl] = (_rms(y[:, sl]) * og_ref[:, sl]).astype(o_ref.dtype)
        else:
            o_ref[dst, :] = y.astype(o_ref.dtype)
        return c

    lax.fori_loop(0, FFT_H1, fin, 0, unroll=4)


def _hyena_order(zsrc, zcol, proj, order, conv_w, conv_b, hspec, consts, out_g, batch, length, conv_z, norm):
    f1, _, f2, f2i, g = consts
    cb = HY_CB
    ncb = HYENA_WIDTH // cb
    pcol = 3 * ATTN_WIDTH // cb
    zconv = HYENA_ORDER * ncb
    full = lambda a: pl.BlockSpec(a.shape, lambda c, b: (0,) * a.ndim)
    kern = functools.partial(_hyena_kernel, length=length, conv_z=conv_z, norm=norm)
    return pl.pallas_call(
        kern,
        grid=(ncb, batch),
        in_specs=[
            pl.BlockSpec((length, cb), lambda c, b: (b, zcol + c)),
            pl.BlockSpec((length, cb), lambda c, b: (b, pcol + order * ncb + c)),
            pl.BlockSpec((SHORT_CONV_W, cb), lambda c, b: (0, zconv + c)),
            pl.BlockSpec((1, cb), lambda c, b: (0, zconv + c)),
            pl.BlockSpec((SHORT_CONV_W, cb), lambda c, b: (0, order * ncb + c)),
            pl.BlockSpec((1, cb), lambda c, b: (0, order * ncb + c)),
            pl.BlockSpec((1, FFT_H1 + 1, 2 * FFT_N2, cb), lambda c, b: (order, 0, 0, c)),
            full(f1), full(f2), full(f2i), full(g),
            pl.BlockSpec((1, cb), lambda c, b: (0, c)),
        ],
        out_specs=pl.BlockSpec((length, cb), lambda c, b: (b, c)),
        out_shape=jax.ShapeDtypeStruct((batch * length, HYENA_WIDTH), BF16),
        scratch_shapes=[
            pltpu.VMEM((length + 16, cb), F32),
            pltpu.VMEM((cb // LANES, FFT_H1 * Z_PITCH, LANES), F32),
            pltpu.VMEM((cb // LANES, K1_PAD * AB_PITCH, LANES), F32),
        ],
        compiler_params=pltpu.CompilerParams(
            dimension_semantics=("arbitrary", "arbitrary"),
            vmem_limit_bytes=VMEM_LIMIT),
        name=f"hyena_order{order}",
    )(zsrc, proj, conv_w, conv_b[None], conv_w, conv_b[None], hspec, f1, f2, f2i, g, out_g)


def _hyena(proj, conv_w, conv_b, w1, b1, w2, b2, w3, freq, filt_bias, out_g, batch, length):
    assert 2 * length == FFT_N1 * FFT_N2 and HYENA_ORDER == 2
    consts = _dft_constants()
    max_decay = math.log(DECAY_TARGET) / FAST_DECAY_PCT
    min_decay = math.log(DECAY_TARGET) / SLOW_DECAY_PCT
    deltas = jnp.abs(jnp.linspace(min_decay, max_decay, HYENA_WIDTH, dtype=F32))[None]
    hid = _filter_hidden(length, w1, b1, w2, b2, freq)
    hspec = _filter_spectrum(hid, w3, deltas, filt_bias, consts, length)
    vcol = (3 * ATTN_WIDTH + HYENA_ORDER * HYENA_WIDTH) // HY_CB
    z1 = _hyena_order(proj, vcol, proj, 0, conv_w, conv_b, hspec, consts, out_g, batch, length, True, False)
    return _hyena_order(z1, 0, proj, 1, conv_w, conv_b, hspec, consts, out_g, batch, length, False, True)


def kernel(x, mix_norm_g, w_in, q_norm_g, k_norm_g, rpb, conv_w, conv_b, filt_w1, filt_b1, filt_w2, filt_b2, filt_w3, filt_freq, filt_bias, attn_out_g, hyena_out_g, w_out, ffn_norm_g, w_router, w_gate, w_up, w_down):
    b, s, d = x.shape
    rows = s // GRID_W
    win_r = min(WIN_ROWS_MAX, rows)
    cap = EC_CAPACITY_FACTOR * s // N_EXPERTS
    x2 = x.reshape(b * s, d)
    for i in range(mix_norm_g.shape[0]):
        proj = _inproj(x2, mix_norm_g[i][None], w_in[i],
                       q_norm_g[i][None], k_norm_g[i][None])
        attn = _attention(proj, rpb[i], attn_out_g[i][None], b, s)

        hyn = _hyena(proj, conv_w[i], conv_b[i], filt_w1[i], filt_b1[i], filt_w2[i], filt_b2[i], filt_w3[i],
                     filt_freq[i], filt_bias[i], hyena_out_g[i][None], b, s)

        wr_pad = jnp.zeros((d, 128), BF16).at[:, :N_EXPERTS].set(w_router[i].astype(BF16))
        x1, hn, logits = _outproj(attn, hyn, x2, w_out[i].astype(BF16), ffn_norm_g[i][None], wr_pad)

        idx, gate = _route(logits, b, s, cap)
        x2 = _moe(idx, gate, hn, x1, w_gate[i], w_up[i], w_down[i], b, s)
    return x2.reshape(b, s, d)
```

```python
import functools
import math

import numpy as np
import jax
import jax.numpy as jnp
from jax import lax
from jax.experimental import pallas as pl
from jax.experimental.pallas import tpu as pltpu

D_MODEL = 2048
GRID_W = 64
N_ATTN_HEADS = 8
ATTN_WIDTH = D_MODEL // 2
HEAD_DIM = ATTN_WIDTH // N_ATTN_HEADS
HYENA_WIDTH = D_MODEL - ATTN_WIDTH
N_HYENA_GROUPS = 8
HYENA_ORDER = 2
N_DIRS = 2
SHORT_CONV_W = 3
FILTER_EMB = 33
DECAY_TARGET = 1e-2
FAST_DECAY_PCT = 0.3
SLOW_DECAY_PCT = 1.5
WIN_ROWS_MAX = 8
WIN_COLS = 16
N_EXPERTS = 16
EC_CAPACITY_FACTOR = 2
EXPERT_FF = 1024
IN_WIDTH = 3 * ATTN_WIDTH + (HYENA_ORDER + 1) * HYENA_WIDTH
RMS_EPS = 1e-6

F32 = jnp.float32
BF16 = jnp.bfloat16
NEG_BIG = -1e30
VMEM_LIMIT = 56 * 1024 * 1024
LANES = 128
LOG2_E = math.log2(math.e)


def _rms(x, eps=RMS_EPS):
    return x * lax.rsqrt(jnp.mean(x * x, axis=-1, keepdims=True) + eps)


INPROJ_GROUPS = 2


def _inproj_kernel(x_ref, g_ref, w_ref, qg_ref, kg_ref, o_ref, h_ref):
    n = pl.program_id(1)
    bm, bn = o_ref.shape

    def step(normalise_input, head_gain):
        w = w_ref[...].astype(BF16)
        for r in range(INPROJ_GROUPS):
            rows = slice(r * bm // INPROJ_GROUPS, (r + 1) * bm // INPROJ_GROUPS)
            if normalise_input:
                h_ref[rows, :] = (_rms(x_ref[rows, :]) * g_ref[...]).astype(BF16)
            acc = jnp.dot(h_ref[rows, :], w, preferred_element_type=F32)
            if head_gain is None:
                o_ref[rows, :] = acc.astype(BF16)
            else:
                for j in range(bn // HEAD_DIM):
                    sl = slice(j * HEAD_DIM, (j + 1) * HEAD_DIM)
                    o_ref[rows, sl] = (_rms(acc[:, sl]) * head_gain).astype(BF16)

    @pl.when(n == 0)
    def _():
        step(True, qg_ref[...] * (HEAD_DIM ** -0.5 * LOG2_E))

    @pl.when(n == 1)
    def _():
        step(False, kg_ref[...])

    @pl.when(n >= 2)
    def _():
        step(False, None)


def _inproj(x2, g, w_bf, qg, kg, bm=1024, bn=1024):
    m, d = x2.shape
    nw = w_bf.shape[1]
    return pl.pallas_call(
        _inproj_kernel,
        grid=(m // bm, nw // bn),
        in_specs=[
            pl.BlockSpec((bm, d), lambda i, j: (i, 0)),
            pl.BlockSpec((1, d), lambda i, j: (0, 0)),
            pl.BlockSpec((d, bn), lambda i, j: (0, j)),
            pl.BlockSpec((1, HEAD_DIM), lambda i, j: (0, 0)),
            pl.BlockSpec((1, HEAD_DIM), lambda i, j: (0, 0)),
        ],
        out_specs=pl.BlockSpec((bm, bn), lambda i, j: (i, j)),
        out_shape=jax.ShapeDtypeStruct((m, nw), BF16),
        scratch_shapes=[pltpu.VMEM((bm, d), BF16)],
        compiler_params=pltpu.CompilerParams(
            dimension_semantics=("arbitrary", "arbitrary"),
            vmem_limit_bytes=VMEM_LIMIT),
        name="inproj",
    )(x2, g, w_bf, qg, kg)


ATTN_ROWS = 4
ATTN_BAND = 12


def _attn_kernel(q_ref, k_ref, v_ref, rpb_ref, mask_ref, g_ref, o_ref, pt_ref, *, rows, win_r):
    lanes = pt_ref.shape[-1]
    n_dr = 2 * WIN_ROWS_MAX - 1

    def toeplitz(d):
        row = jnp.broadcast_to(rpb_ref[0, d:d + 1, :], (GRID_W, lanes))
        return pltpu.roll(row, lanes - (WIN_COLS - 1), 1, stride=1, stride_axis=0)

    lane = lax.broadcasted_iota(jnp.int32, (GRID_W, lanes), 1)
    left = lane < GRID_W
    tiles = [toeplitz(d) for d in range(n_dr)]
    for d in range(-1, n_dr):
        lo, hi = tiles[max(d, 0)], tiles[min(d + 1, n_dr - 1)]
        pt_ref[d + 1] = jnp.where(left, lo, pltpu.roll(hi, GRID_W, 1)) * LOG2_E + mask_ref[...]

    def body(i, carry):
        r0 = i * ATTN_ROWS
        bs = jnp.minimum(jnp.clip(r0 - win_r // 2, 0, rows - win_r), rows - ATTN_BAND)
        q0 = pl.multiple_of(r0 * GRID_W, ATTN_ROWS * GRID_W)
        k0 = pl.multiple_of(bs * GRID_W, 2 * GRID_W)
        q = q_ref[pl.ds(q0, ATTN_ROWS * GRID_W), :]
        k = k_ref[pl.ds(k0, ATTN_BAND * GRID_W), :]
        v = v_ref[pl.ds(k0, ATTN_BAND * GRID_W), :]
        s = lax.dot_general(q, k, (((1,), (1,)), ((), ())), preferred_element_type=F32)

        bias_rows = []
        for j in range(ATTN_ROWS):
            r = r0 + j
            start = jnp.clip(r - win_r // 2, 0, rows - win_r)
            pieces = []
            for m in range(ATTN_BAND // 2):
                kr = bs + 2 * m
                pen_l = jnp.where((kr >= start) & (kr < start + win_r), 0.0, NEG_BIG)
                pen_r = jnp.where((kr + 1 >= start) & (kr + 1 < start + win_r), 0.0, NEG_BIG)
                d = jnp.clip(kr - r + WIN_ROWS_MAX - 1, -1, n_dr - 1)
                pieces.append(pt_ref[d + 1] + jnp.where(left, pen_l, pen_r))
            bias_rows.append(jnp.concatenate(pieces, axis=1))
        s = s + jnp.concatenate(bias_rows, axis=0)

        mx = jnp.max(s, axis=-1, keepdims=True)
        p = jnp.exp2(s - mx)
        l = jnp.sum(p, axis=-1, keepdims=True)
        o = jnp.dot(p.astype(BF16), v, preferred_element_type=F32) / l
        o = _rms(o) * g_ref[...]
        o_ref[pl.ds(q0, ATTN_ROWS * GRID_W), :] = o.astype(BF16)
        return carry

    lax.fori_loop(0, rows // ATTN_ROWS, body, 0, unroll=2)


def _attn_window_mask():
    c = np.arange(GRID_W)
    c0 = np.clip(c - WIN_COLS // 2, 0, GRID_W - WIN_COLS)
    in_win = (c[None, :] >= c0[:, None]) & (c[None, :] < c0[:, None] + WIN_COLS)
    return jnp.asarray(np.tile(np.where(in_win, 0.0, NEG_BIG), (1, 2)), dtype=F32)


def _attention(proj, rpb, out_g, batch, seq):
    rows = seq // GRID_W
    win_r = min(WIN_ROWS_MAX, rows)
    assert win_r == WIN_ROWS_MAX and 2 * GRID_W == LANES
    assert rows % ATTN_ROWS == 0 and ATTN_BAND >= win_r + ATTN_ROWS - 1 and ATTN_BAND % 2 == 0
    h = N_ATTN_HEADS
    n_dr, n_dc = rpb.shape[1:]
    rpb_pad = jnp.zeros((h, 2 * WIN_ROWS_MAX, LANES), F32).at[:, :n_dr, :n_dc].set(rpb)
    mask = _attn_window_mask()
    kern = functools.partial(_attn_kernel, rows=rows, win_r=win_r)
    return pl.pallas_call(
        kern,
        grid=(batch, h),
        in_specs=[
            pl.BlockSpec((seq, HEAD_DIM), lambda b, i: (b, i)),
            pl.BlockSpec((seq, HEAD_DIM), lambda b, i: (b, h + i)),
            pl.BlockSpec((seq, HEAD_DIM), lambda b, i: (b, 2 * h + i)),
            pl.BlockSpec((1, 2 * WIN_ROWS_MAX, LANES), lambda b, i: (i, 0, 0)),
            pl.BlockSpec(mask.shape, lambda b, i: (0, 0)),
            pl.BlockSpec((1, HEAD_DIM), lambda b, i: (0, i)),
        ],
        out_specs=pl.BlockSpec((seq, HEAD_DIM), lambda b, i: (b, i)),
        out_shape=jax.ShapeDtypeStruct((batch * seq, ATTN_WIDTH), BF16),
        scratch_shapes=[pltpu.VMEM((2 * WIN_ROWS_MAX, GRID_W, LANES), F32)],
        compiler_params=pltpu.CompilerParams(
            dimension_semantics=("arbitrary", "arbitrary"),
            vmem_limit_bytes=VMEM_LIMIT),
        name="na2d_attn",
    )(proj, proj, proj, rpb_pad, mask, out_g)


OUTPROJ_GROUPS = 2


def _outproj_kernel(a_ref, hy_ref, x_ref, wo_ref, g_ref, wr_ref, x1_ref, hn_ref, lg_ref):
    half = a_ref.shape[1]
    bm = x_ref.shape[0]
    for r in range(OUTPROJ_GROUPS):
        rows = slice(r * bm // OUTPROJ_GROUPS, (r + 1) * bm // OUTPROJ_GROUPS)
        acc = x_ref[rows, :]
        acc = acc + jnp.dot(a_ref[rows, :], wo_ref[:half, :], preferred_element_type=F32)
        acc = acc + jnp.dot(hy_ref[rows, :], wo_ref[half:, :], preferred_element_type=F32)
        x1_ref[rows, :] = acc
        hn = _rms(acc) * g_ref[...]
        hn_ref[rows, :] = hn
        lg_ref[rows, :] = jnp.dot(hn.astype(BF16), wr_ref[...], preferred_element_type=F32)


def _outproj(attn, hy, x2, wo_bf, g, wr_pad, bm=512):
    m, d = x2.shape
    half = attn.shape[1]
    npad = wr_pad.shape[1]
    return pl.pallas_call(
        _outproj_kernel,
        grid=(m // bm,),
        in_specs=[
            pl.BlockSpec((bm, half), lambda i: (i, 0)),
            pl.BlockSpec((bm, half), lambda i: (i, 0)),
            pl.BlockSpec((bm, d), lambda i: (i, 0)),
            pl.BlockSpec((d, d), lambda i: (0, 0)),
            pl.BlockSpec((1, d), lambda i: (0, 0)),
            pl.BlockSpec((d, npad), lambda i: (0, 0)),
        ],
        out_specs=[
            pl.BlockSpec((bm, d), lambda i: (i, 0)),
            pl.BlockSpec((bm, d), lambda i: (i, 0)),
            pl.BlockSpec((bm, npad), lambda i: (i, 0)),
        ],
        out_shape=[
            jax.ShapeDtypeStruct((m, d), F32),
            jax.ShapeDtypeStruct((m, d), F32),
            jax.ShapeDtypeStruct((m, npad), F32),
        ],
        compiler_params=pltpu.CompilerParams(
            dimension_semantics=("arbitrary",),
            vmem_limit_bytes=VMEM_LIMIT),
        name="outproj",
    )(attn, hy, x2, wo_bf, g, wr_pad)


ROUTE_BLK = 128
ROUTE_CHUNK = 512
SLOT_GRP = 8
ROUTE_GROUPS_INLINE = 3
F32_INF_BITS = 0x7F800000


def _route_kernel(lg_ref, tri_ref, idx_ref, gate_ref, aff_ref, pos_t_ref, aff_t_ref, off_ref, ia_ref, ga_ref,
                  stage_i, stage_g, off_smem, sem, *, cap):
    s, lanes = lg_ref.shape
    lane = lax.broadcasted_iota(jnp.int32, (1, lanes), 1)

    def softmax_chunk(i, c):
        rows = pl.ds(pl.multiple_of(i * ROUTE_CHUNK, ROUTE_CHUNK), ROUTE_CHUNK)
        lg = jnp.where(lane < N_EXPERTS, lg_ref[rows, :], NEG_BIG)
        ex = jnp.exp(lg - jnp.max(lg, axis=-1, keepdims=True))
        aff_ref[rows, :] = ex / jnp.sum(ex, axis=-1, keepdims=True)
        return c

    lax.fori_loop(0, s // ROUTE_CHUNK, softmax_chunk, 0)

    def as_value(bits):
        return pltpu.bitcast(bits, F32)

    def fold8(x):
        parts = [x[i:i + 8] for i in range(0, x.shape[0], 8)]
        while len(parts) > 1:
            parts = [a + b for a, b in zip(parts[::2], parts[1::2])]
        return parts[0]

    def count_ge(th):
        def body(i, acc):
            rows = pl.ds(pl.multiple_of(i * ROUTE_CHUNK, ROUTE_CHUNK), ROUTE_CHUNK)
            return acc + fold8(jnp.where(aff_ref[rows, :] >= th, 1.0, 0.0))
        acc = lax.fori_loop(0, s // ROUTE_CHUNK, body, jnp.zeros((8, lanes), F32))
        return jnp.sum(acc, axis=0, keepdims=True)

    def bisect(_, c):
        lo, hi = c
        mid = lo + ((hi - lo) >> 1)
        ok = count_ge(as_value(mid)) >= cap
        return jnp.where(ok, mid, lo), jnp.where(ok, hi, mid)

    tau_bits, _ = lax.fori_loop(0, 31, bisect, (jnp.zeros((1, lanes), jnp.int32),
                                                jnp.full((1, lanes), F32_INF_BITS, jnp.int32)))
    tau, above = as_value(tau_bits), as_value(tau_bits + 1)
    ties_wanted = cap - count_ge(above)

    def scan_block(j, c):
        tie_carry, sel_carry = c
        rows = pl.ds(pl.multiple_of(j * ROUTE_BLK, ROUTE_BLK), ROUTE_BLK)
        aff = aff_ref[rows, :]
        gt = aff >= above
        eq = jnp.where((aff >= tau) & (aff < above), 1.0, 0.0)
        tie_incl = jnp.dot(tri_ref[...], eq.astype(BF16), preferred_element_type=F32) + tie_carry
        sel = jnp.where(gt | ((eq > 0.0) & (tie_incl - eq < ties_wanted)), 1.0, 0.0)
        sel_incl = jnp.dot(tri_ref[...], sel.astype(BF16), preferred_element_type=F32) + sel_carry
        pos = jnp.where(sel > 0.0, sel_incl - sel, -1.0)
        pos_t_ref[j] = pos.T[:N_EXPERTS]
        aff_t_ref[j] = aff.T[:N_EXPERTS]
        off_ref[pl.ds(j, 1), :] = sel_carry.astype(jnp.int32)
        return tie_incl[ROUTE_BLK - 1:, :], sel_incl[ROUTE_BLK - 1:, :]

    n_blk = s // ROUTE_BLK
    off_ref[...] = jnp.zeros_like(off_ref)
    zero = jnp.zeros((1, lanes), F32)
    _, total = lax.fori_loop(0, n_blk, scan_block, (zero, zero), unroll=4)
    off_ref[pl.ds(n_blk, 1), :] = total.astype(jnp.int32)

    to_smem = pltpu.make_async_copy(off_ref, off_smem, sem)
    to_smem.start()
    to_smem.wait()

    n_grp = cap // SLOT_GRP
    sub = lax.broadcasted_iota(jnp.int32, (SLOT_GRP, lanes), 0)
    lane_i = lax.broadcasted_iota(jnp.int32, (SLOT_GRP, lanes), 1)
    stage_i[...] = jnp.zeros_like(stage_i)
    stage_g[...] = jnp.zeros_like(stage_g)
    for e in range(N_EXPERTS):
        ia_ref[...] = jnp.zeros_like(ia_ref)
        ga_ref[...] = jnp.zeros_like(ga_ref)

        def block(j, carry, e=e):
            first, last = off_smem[j, e], off_smem[j + 1, e]
            pos = jnp.broadcast_to(pos_t_ref[j, e:e + 1, :], (SLOT_GRP, lanes))
            aff = jnp.broadcast_to(aff_t_ref[j, e:e + 1, :], (SLOT_GRP, lanes))
            tok = (j * ROUTE_BLK + lane_i).astype(F32)
            g0 = first // SLOT_GRP
            groups = jnp.where(last > first, (last - 1) // SLOT_GRP - g0 + 1, 0)

            def update(g):
                match = pos == (g * SLOT_GRP + sub).astype(F32)
                gi = jnp.minimum(g, n_grp - 1)
                ia_ref[gi] += jnp.where(match, tok, 0.0)
                ga_ref[gi] += jnp.where(match, aff, 0.0)

            for k in range(ROUTE_GROUPS_INLINE):
                update(g0 + k)

            def rest(k, c):
                update(g0 + k)
                return c

            lax.fori_loop(ROUTE_GROUPS_INLINE, groups, rest, 0)
            return carry

        lax.fori_loop(0, n_blk, block, 0)
        stage_i[:, e:e + 1] = jnp.sum(ia_ref[...].reshape(cap, lanes), axis=1, keepdims=True)
        stage_g[:, e:e + 1] = jnp.sum(ga_ref[...].reshape(cap, lanes), axis=1, keepdims=True)

    idx_ref[0] = stage_i[...].T[:N_EXPERTS].astype(jnp.int32)
    gate_ref[0] = stage_g[...].T[:N_EXPERTS]


def _route(logits, batch, seq, cap):
    lanes = logits.shape[1]
    assert lanes == LANES == ROUTE_BLK and cap % SLOT_GRP == 0
    n_blk = seq // ROUTE_BLK
    n_off = -(-(n_blk + 1) // 8) * 8
    tri =jnp.asarray(np.tril(np.ones((ROUTE_BLK, ROUTE_BLK))), dtype=BF16)
    return pl.pallas_call(
        functools.partial(_route_kernel, cap=cap),
        grid=(batch,),
        in_specs=[
            pl.BlockSpec((seq, lanes), lambda b: (b, 0)),
            pl.BlockSpec((ROUTE_BLK, ROUTE_BLK), lambda b: (0, 0)),
        ],
        out_specs=[
            pl.BlockSpec((1, N_EXPERTS, cap), lambda b: (b, 0, 0)),
            pl.BlockSpec((1, N_EXPERTS, cap), lambda b: (b, 0, 0)),
        ],
        out_shape=[
            jax.ShapeDtypeStruct((batch, N_EXPERTS, cap), jnp.int32),
            jax.ShapeDtypeStruct((batch, N_EXPERTS, cap), F32),
        ],
        scratch_shapes=[
            pltpu.VMEM((seq, lanes), F32),
            pltpu.VMEM((n_blk, N_EXPERTS, lanes), F32),
            pltpu.VMEM((n_blk, N_EXPERTS, lanes), F32),
            pltpu.VMEM((n_off, lanes), jnp.int32),
            pltpu.VMEM((cap // SLOT_GRP, SLOT_GRP, lanes), F32),
            pltpu.VMEM((cap // SLOT_GRP, SLOT_GRP, lanes), F32),
            pltpu.VMEM((cap, lanes), F32),
            pltpu.VMEM((cap, lanes), F32),
            pltpu.SMEM((n_off, lanes), jnp.int32),
            pltpu.SemaphoreType.DMA(()),
        ],
        compiler_params=pltpu.CompilerParams(dimension_semantics=("arbitrary",)),
        name="route",
    )(logits, tri)


MOE_FF_SPLIT = 4
ROW_UNROLL = 8


def _moe_kernel(idx_ref, hn_hbm, wg_ref, wu_ref, wd_ref, gate_ref, x1_hbm, out_hbm,
                xg0, xg1, acc0, acc1, y0, y1, xb0, xb1, sem_x, sem_a, sem_s, *, cap, seq):
    del x1_hbm
    e, f = pl.program_id(0), pl.program_id(1)
    n_e = pl.num_programs(0)
    xg, acc, xb = (xg0, xg1), (acc0, acc1), (xb0, xb1)
    prev_e, next_e = jnp.maximum(e - 1, 0), jnp.minimum(e + 1, n_e - 1)

    def row_of(ee, b):
        base = (b * n_e + ee) * cap
        return lambda p: b * seq + idx_ref[base + p]

    def copy_rows(ee, b, kind, rolled=False):
        row = row_of(ee, b)

        def one(p):
            if kind == "xg":
                src, dst, sem = hn_hbm.at[pl.ds(row(p), 1), :], xg[b].at[pl.ds(p, 1), :], sem_x.at[b]
            elif kind == "ag":
                src, dst, sem = out_hbm.at[pl.ds(row(p), 1), :], acc[b].at[pl.ds(p, 1), :], sem_a.at[b]
            else:
                src, dst, sem = acc[b].at[pl.ds(p, 1), :], out_hbm.at[pl.ds(row(p), 1), :], sem_s.at[b]
            pltpu.make_async_copy(src, dst, sem).start()

        if rolled:
            def body(i, c):
                for k in range(ROW_UNROLL):
                    one(i * ROW_UNROLL + k)
                return c
            lax.fori_loop(0, cap // ROW_UNROLL, body, 0)
        else:
            for p in range(cap):
                one(p)

    def wait_rows(buf, sem):
        pltpu.make_async_copy(hn_hbm.at[pl.ds(0, cap), :], buf, sem).wait()

    def partial_out(b, wg, wu, wd):
        x = xb[b][...]
        a = jnp.dot(x, wg, preferred_element_type=F32)
        u = jnp.dot(x, wu, preferred_element_type=F32)
        hmid = (a * (1.0 / (1.0 + jnp.exp(-a))) * u).astype(BF16)
        return jnp.dot(hmid, wd, preferred_element_type=F32)

    def weights():
        return wg_ref[0].astype(BF16), wu_ref[0].astype(BF16), wd_ref[0].astype(BF16)

    @pl.when((e == 0) & (f == 0))
    def _():
        copy_rows(e, 0, "xg", rolled=True)
        copy_rows(e, 1, "xg", rolled=True)
        copy_rows(e, 0, "ag", rolled=True)
        copy_rows(e, 1, "ag", rolled=True)
        wait_rows(acc0, sem_a.at[0])
        wait_rows(acc1, sem_a.at[1])
        copy_rows(e, 0, "sc", rolled=True)

    @pl.when(f == 0)
    def _():
        w = weights()
        wait_rows(xg0, sem_x.at[0])
        xb0[...] = xg0[...].astype(BF16)
        copy_rows(prev_e, 1, "sc")
        y0[...] = partial_out(0, *w)
        wait_rows(xg1, sem_x.at[1])
        xb1[...] = xg1[...].astype(BF16)
        y1[...] = partial_out(1, *w)

    @pl.when(f == 1)
    def _():
        w = weights()
        wait_rows(acc0, sem_s.at[0])
        copy_rows(e, 0, "ag")
        y0[...] += partial_out(0, *w)
        wait_rows(acc1, sem_s.at[1])
        copy_rows(e, 1, "ag")
        y1[...] += partial_out(1, *w)

    @pl.when(f == 2)
    def _():
        w = weights()
        copy_rows(next_e, 0, "xg")
        y0[...] += partial_out(0, *w)
        copy_rows(next_e, 1, "xg")
        y1[...] += partial_out(1, *w)

    @pl.when(f == 3)
    def _():
        w = weights()
        t0 = y0[...] + partial_out(0, *w)
        wait_rows(acc0, sem_a.at[0])
        acc0[...] += t0 * gate_ref[0, 0]
        copy_rows(e, 0, "sc")
        t1 = y1[...] + partial_out(1, *w)
        wait_rows(acc1, sem_a.at[1])
        acc1[...] += t1 * gate_ref[1, 0]

    @pl.when((e == n_e - 1) & (f == MOE_FF_SPLIT - 1))
    def _():
        copy_rows(e, 1, "sc", rolled=True)
        wait_rows(acc0, sem_s.at[0])
        wait_rows(acc1, sem_s.at[1])
        wait_rows(xg0, sem_x.at[0])
        wait_rows(xg1, sem_x.at[1])


def _moe(idx, gate, hn, x1, wg, wu, wd, batch, seq):
    n_e, d, ff = wg.shape
    cap = idx.shape[-1]
    assert batch == 2 and MOE_FF_SPLIT == 4 and ff % MOE_FF_SPLIT == 0
    fq = ff // MOE_FF_SPLIT
    grid_spec = pltpu.PrefetchScalarGridSpec(
        num_scalar_prefetch=1,
        grid=(n_e, MOE_FF_SPLIT),
        in_specs=[
            pl.BlockSpec(memory_space=pl.ANY),
            pl.BlockSpec((1, d, fq), lambda i, j, s: (i, 0, j)),
            pl.BlockSpec((1, d, fq), lambda i, j, s: (i, 0, j)),
            pl.BlockSpec((1, fq, d), lambda i, j, s: (i, j, 0)),
            pl.BlockSpec((batch, 1, cap, 1), lambda i, j, s: (0, i, 0, 0)),
            pl.BlockSpec(memory_space=pl.ANY),
        ],
        out_specs=pl.BlockSpec(memory_space=pl.ANY),
        scratch_shapes=(
            [pltpu.VMEM((cap, d), F32) for _ in range(6)]
            + [pltpu.VMEM((cap, d), BF16) for _ in range(2)]
            + [pltpu.SemaphoreType.DMA((batch,)) for _ in range(3)]),
    )
    return pl.pallas_call(
        functools.partial(_moe_kernel, cap=cap, seq=seq),
        grid_spec=grid_spec,
        out_shape=jax.ShapeDtypeStruct(x1.shape, x1.dtype),
        input_output_aliases={6: 0},
        compiler_params=pltpu.CompilerParams(
            dimension_semantics=("arbitrary", "arbitrary"),
            vmem_limit_bytes=VMEM_LIMIT),
        name="moe_experts",
    )(idx.reshape(-1), hn, wg, wu, wd, gate[..., None], x1)


FFT_N1 = 128
FFT_N2 = 64
FFT_H1 = FFT_N1 // 2
K1_PAD = 72
Z_PITCH = 72
AB_PITCH = 136
HY_CB = 256
FFT_UNROLL = 8
HP = lax.Precision.HIGHEST


def _dft_constants():
    n, n1, n2, h1 = FFT_N1 * FFT_N2, FFT_N1, FFT_N2, FFT_H1
    k1 = np.arange(h1 + 1, dtype=np.float64)
    s1 = np.arange(h1, dtype=np.float64)
    s2 = np.arange(n2, dtype=np.float64)
    tw = s2[:, None, None] * k1[None, :, None] / n

    def stage1(phase_s1):
        th = -2.0 * np.pi * (phase_s1 + tw)
        m = np.zeros((n2, 2 * K1_PAD, h1))
        m[:, :h1 + 1] = np.cos(th)
        m[:, K1_PAD:K1_PAD + h1 + 1] = np.sin(th)
        return m

    f1 = stage1(s1[None, None, :] * k1[None, :, None] / n1)
    f1b = stage1(((n1 - 1) - s1)[None, None, :] * k1[None, :, None] / n1)
    f1b0 = stage1((n1 - s1)[None, None, :] * k1[None, :, None] / n1)[0]
    f1b0[:, 0] = 0.0
    f1b[0] = f1b0

    a = 2.0 * np.pi * np.outer(np.arange(n2), np.arange(n2)) / n2
    c, s = np.cos(a), np.sin(a)
    f2 = np.block([[c, s], [-s, c]])
    f2i = np.block([[c, -s], [s, c]])

    t1 = np.arange(h1, dtype=np.float64)
    th = 2.0 * np.pi * (t1[None, :, None] * k1[None, None, :] / n1 + s2[:, None, None] * k1[None, None, :] / n)
    wgt = np.where((k1 == 0) | (k1 == h1), 1.0, 2.0) / n
    g = np.zeros((n2, h1, 2 * K1_PAD))
    g[:, :, :h1 + 1] = wgt * np.cos(th)
    g[:, :, K1_PAD:K1_PAD + h1 + 1] = -wgt * np.sin(th)
    return tuple(jnp.asarray(m, dtype=F32).astype(BF16) for m in (f1, f1b, f2, f2i, g))


def _hid_kernel(f_ref, w1t_ref, w1c_ref, w1s_ref, b1_ref, w2_ref, b2_ref, fr_ref, o_ref, *, length):
    bm = o_ref.shape[0]
    pos = (pl.program_id(0) * bm + lax.broadcasted_iota(jnp.int32, (1, bm), 1)).astype(F32)
    t = pos / (length - 1.0)
    ang = f_ref[...] * (2.0 * math.pi * pos / length)
    pre = (w1t_ref[...] * t
           + jnp.dot(w1c_ref[...], jnp.cos(ang), precision=HP, preferred_element_type=F32)
           - jnp.dot(w1s_ref[...], jnp.sin(ang), precision=HP, preferred_element_type=F32))
    fr = fr_ref[...]
    hid = jnp.sin(fr * (pre + b1_ref[...]))
    hid = jnp.sin(fr * (jnp.dot(w2_ref[...], hid, precision=HP, preferred_element_type=F32) + b2_ref[...]))
    hi = hid.astype(BF16).astype(F32)
    lo = hid - hi
    o_ref[...] = jnp.concatenate([hi, hi, lo, jnp.zeros_like(hi)], axis=0).T.astype(BF16)


def _filter_hidden(length, w1, b1, w2, b2, freq, bm=512):
    bands = (FILTER_EMB - 1) // 2
    f = jnp.linspace(1e-4, bands - 1, bands, dtype=F32)[:, None]
    hdim = w1.shape[1]
    full = lambda a: pl.BlockSpec(a.shape, lambda i: (0,) * a.ndim)
    args = (f, w1[:1].T, w1[1:1 + bands].T, w1[1 + bands:].T, b1[:, None], w2.T, b2[:, None], freq[:, None])
    return pl.pallas_call(
        functools.partial(_hid_kernel, length=length),
        grid=(length // bm,),
        in_specs=[full(a) for a in args],
        out_specs=pl.BlockSpec((bm, 4 * hdim), lambda i: (i, 0)),
        out_shape=jax.ShapeDtypeStruct((length, 4 * hdim), BF16),
        name="filter_hidden",
    )(*args)


def _ld(ref, start, size, stride=None):
    idx = pl.ds(start, size) if stride is None else pl.ds(start, size, stride=stride)
    return jnp.concatenate([ref[h, idx, :] for h in range(ref.shape[0])], axis=-1)


def _st(ref, start, size, val, stride=None):
    idx = pl.ds(start, size) if stride is None else pl.ds(start, size, stride=stride)
    for h in range(ref.shape[0]):
        ref[h, idx, :] = val[:, h * LANES:(h + 1) * LANES]


def _stage1_store(ab_ref, s2, a):
    _st(ab_ref, s2, K1_PAD, a[:K1_PAD], stride=AB_PITCH)
    _st(ab_ref, FFT_N2 + s2, K1_PAD, a[K1_PAD:], stride=AB_PITCH)


def _stack_split_rows(w):
    hi = w.astype(BF16)
    lo = (w - hi.astype(F32)).astype(BF16)
    return jnp.concatenate([hi, lo, hi, jnp.zeros_like(hi)], axis=0)


def _spectrum_kernel(hid_ref, w3f_ref, w3b_ref, dl_ref, skip_ref, f1_ref, f2_ref, o_ref,
                     hf_ref, hb_ref, ab_ref, *, length):
    blocks = 4
    rows = blocks * FFT_N2
    w3f, w3b = _stack_split_rows(w3f_ref[...]), _stack_split_rows(w3b_ref[...])

    def gen(j, c):
        row = (j * rows + lax.broadcasted_iota(jnp.int32, (rows, 1), 0)).astype(F32)
        win = jnp.exp(-(row / (length - 1.0)) * dl_ref[...])
        hid = hid_ref[pl.ds(pl.multiple_of(j * rows, rows), rows), :]
        hf = jnp.dot(hid, w3f, preferred_element_type=F32) * win
        hb = jnp.dot(hid, w3b, preferred_element_type=F32) * win
        for q in range(blocks):
            dst = pl.multiple_of((j * blocks + q) * Z_PITCH, 8)
            _st(hf_ref, dst, FFT_N2, hf[q * FFT_N2:(q + 1) * FFT_N2])
            _st(hb_ref, dst, FFT_N2, hb[q * FFT_N2:(q + 1) * FFT_N2])
        return c

    lax.fori_loop(0, FFT_H1 // blocks, gen, 0)

    def stage1(s2, c):
        xf = _ld(hf_ref, s2, FFT_H1, Z_PITCH).astype(BF16)
        xb = _ld(hb_ref, (FFT_N2 - s2) % FFT_N2, FFT_H1, Z_PITCH).astype(BF16)
        x = jnp.concatenate([xf, xb], axis=0)
        _stage1_store(ab_ref, s2, jnp.dot(f1_ref[s2], x, preferred_element_type=F32))
        return c

    lax.fori_loop(0, FFT_N2, stage1, 0, unroll=2 * FFT_UNROLL)

    def stage2(k1, c):
        a = _ld(ab_ref, pl.multiple_of(k1 * AB_PITCH, 8), 2 * FFT_N2).astype(BF16)
        x = jnp.dot(f2_ref[...], a, preferred_element_type=F32)
        o_ref[0, k1, :FFT_N2, :] = (x[:FFT_N2] + skip_ref[0]).astype(BF16)
        o_ref[0, k1, FFT_N2:, :] = x[FFT_N2:].astype(BF16)
        return c

    lax.fori_loop(0, FFT_H1 + 1, stage2, 0, unroll=FFT_UNROLL)


def _filter_spectrum(hid, w3, deltas, filt_bias, consts, length):
    f1, f1b, f2, _, _ = consts
    f1 = jnp.concatenate([f1, f1b], axis=2)
    cb = HY_CB
    ncb = HYENA_WIDTH // cb
    hdim = w3.shape[0]
    full = lambda a: pl.BlockSpec(a.shape, lambda o, c: (0,) * a.ndim)
    return pl.pallas_call(
        functools.partial(_spectrum_kernel, length=length),
        grid=(HYENA_ORDER, ncb),
        in_specs=[
            full(hid),
            pl.BlockSpec((hdim, cb), lambda o, c: (0, (o * N_DIRS) * ncb + c)),
            pl.BlockSpec((hdim, cb), lambda o, c: (0, (o * N_DIRS + 1) * ncb + c)),
            pl.BlockSpec((1, cb), lambda o, c: (0, c)),
            pl.BlockSpec((1, 1, cb), lambda o, c: (o, 0, c)),
            full(f1), full(f2),
        ],
        out_specs=pl.BlockSpec((1, FFT_H1 + 1, 2 * FFT_N2, cb), lambda o, c: (o, 0, 0, c)),
        out_shape=jax.ShapeDtypeStruct((HYENA_ORDER, FFT_H1 + 1, 2 * FFT_N2, HYENA_WIDTH), BF16),
        scratch_shapes=[
            pltpu.VMEM((cb // LANES, FFT_H1 * Z_PITCH, LANES), F32),
            pltpu.VMEM((cb // LANES, FFT_H1 * Z_PITCH, LANES), F32),
            pltpu.VMEM((cb // LANES, K1_PAD * AB_PITCH, LANES), F32),
        ],
        compiler_params=pltpu.CompilerParams(
            dimension_semantics=("arbitrary", "arbitrary"),
            vmem_limit_bytes=VMEM_LIMIT),
        name="filter_spectrum",
    )(hid, w3, w3, deltas, filt_bias[:, None, :], f1, f2)


def _short_conv_block(s_ref, j, w_ref, b_ref):
    big = s_ref[pl.ds(pl.multiple_of(j * FFT_N2, FFT_N2), FFT_N2 + 16), :]
    rows = FFT_N2 + 16
    prev = pltpu.roll(big, 1, 0)[8:8 + FFT_N2]
    nxt = pltpu.roll(big, rows - 1, 0)[8:8 + FFT_N2]
    cur = big[8:8 + FFT_N2]
    return b_ref[...] + prev * w_ref[0:1, :] + cur * w_ref[1:2, :] + nxt * w_ref[2:3, :]


def _load_stream(s_ref, src_ref, length):
    zeros = jnp.zeros((8, s_ref.shape[1]), F32)
    s_ref[0:8, :] = zeros
    s_ref[length + 8:length + 16, :] = zeros
    chunk = 512

    def cp(i, c):
        r = pl.multiple_of(i * chunk, chunk)
        s_ref[pl.ds(r + 8, chunk), :] = src_ref[pl.ds(r, chunk), :].astype(F32)
        return c

    lax.fori_loop(0, length // chunk, cp, 0)


def _hyena_kernel(z_ref, gs_ref, cwz_ref, cbz_ref, cwg_ref, cbg_ref, h_ref, f1_ref, f2_ref, f2i_ref, g_ref,
                  og_ref, o_ref, s_ref, zy_ref, ab_ref, *, length, conv_z, norm):
    if conv_z:
        _load_stream(s_ref, z_ref, length)

    def prep(j, c):
        if conv_z:
            zblk = _short_conv_block(s_ref, j, cwz_ref, cbz_ref)
        else:
            zblk = z_ref[pl.ds(pl.multiple_of(j * FFT_N2, FFT_N2), FFT_N2), :].astype(F32)
        _st(zy_ref, pl.multiple_of(j * Z_PITCH, 8), FFT_N2, zblk)
        return c

    lax.fori_loop(0, FFT_H1, prep, 0, unroll=2)

    def stage1(s2, c):
        xs = _ld(zy_ref, s2, FFT_H1, Z_PITCH).astype(BF16)
        _stage1_store(ab_ref, s2, jnp.dot(f1_ref[s2], xs, preferred_element_type=F32))
        return c

    lax.fori_loop(0, FFT_N2, stage1, 0, unroll=2 * FFT_UNROLL)

    def stage2(k1, c):
        blk = pl.multiple_of(k1 * AB_PITCH, 8)
        x = jnp.dot(f2_ref[...], _ld(ab_ref, blk, 2 * FFT_N2).astype(BF16), preferred_element_type=F32)
        hk = h_ref[0, k1].astype(F32)
        xr, xi, hr, hi = x[:FFT_N2], x[FFT_N2:], hk[:FFT_N2], hk[FFT_N2:]
        p = jnp.concatenate([xr * hr - xi * hi, xr * hi + xi * hr], axis=0).astype(BF16)
        _st(ab_ref, blk, 2 * FFT_N2, jnp.dot(f2i_ref[...], p, preferred_element_type=F32))
        return c

    lax.fori_loop(0, FFT_H1 + 1, stage2, 0, unroll=FFT_UNROLL)

    def stage3(t2, c):
        br = _ld(ab_ref, t2, K1_PAD, AB_PITCH)
        bi = _ld(ab_ref, FFT_N2 + t2, K1_PAD, AB_PITCH)
        rhs = jnp.concatenate([br, bi], axis=0).astype(BF16)
        _st(zy_ref, t2, FFT_H1, jnp.dot(g_ref[t2], rhs, preferred_element_type=F32), stride=Z_PITCH)
        return c

    lax.fori_loop(0, FFT_N2, stage3, 0, unroll=2 * FFT_UNROLL)

    _load_stream(s_ref, gs_ref, length)

    def fin(j, c):
        gate = _short_conv_block(s_ref, j, cwg_ref, cbg_ref)
        y = gate * _ld(zy_ref, pl.multiple_of(j * Z_PITCH, 8), FFT_N2)
        dst = pl.ds(pl.multiple_of(j * FFT_N2, FFT_N2), FFT_N2)
        if norm:
            gw = HYENA_WIDTH // N_HYENA_GROUPS
            for q in range(y.shape[1] // gw):
                sl = slice(q * gw, (q + 1) * gw)
                o_ref[dst, sl] = (_rms(y[:, sl]) * og_ref[:, sl]).astype(o_ref.dtype)
        else:
            o_ref[dst, :] = y.astype(o_ref.dtype)
        return c

    lax.fori_loop(0, FFT_H1, fin, 0, unroll=4)


def _hyena_order(zsrc, zcol, proj, order, conv_w, conv_b, hspec, consts, out_g, batch, length, conv_z, norm):
    f1, _, f2, f2i, g = consts
    cb = HY_CB
    ncb = HYENA_WIDTH // cb
    pcol = 3 * ATTN_WIDTH // cb
    zconv = HYENA_ORDER * ncb
    full = lambda a: pl.BlockSpec(a.shape, lambda c, b: (0,) * a.ndim)
    kern = functools.partial(_hyena_kernel, length=length, conv_z=conv_z, norm=norm)
    return pl.pallas_call(
        kern,
        grid=(ncb, batch),
        in_specs=[
            pl.BlockSpec((length, cb), lambda c, b: (b, zcol + c)),
            pl.BlockSpec((length, cb), lambda c, b: (b, pcol + order * ncb + c)),
            pl.BlockSpec((SHORT_CONV_W, cb), lambda c, b: (0, zconv + c)),
            pl.BlockSpec((1, cb), lambda c, b: (0, zconv + c)),
            pl.BlockSpec((SHORT_CONV_W, cb), lambda c, b: (0, order * ncb + c)),
            pl.BlockSpec((1, cb), lambda c, b: (0, order * ncb + c)),
            pl.BlockSpec((1, FFT_H1 + 1, 2 * FFT_N2, cb), lambda c, b: (order, 0, 0, c)),
            full(f1), full(f2), full(f2i), full(g),
            pl.BlockSpec((1, cb), lambda c, b: (0, c)),
        ],
        out_specs=pl.BlockSpec((length, cb), lambda c, b: (b, c)),
        out_shape=jax.ShapeDtypeStruct((batch * length, HYENA_WIDTH), BF16),
        scratch_shapes=[
            pltpu.VMEM((length + 16, cb), F32),
            pltpu.VMEM((cb // LANES, FFT_H1 * Z_PITCH, LANES), F32),
            pltpu.VMEM((cb // LANES, K1_PAD * AB_PITCH, LANES), F32),
        ],
        compiler_params=pltpu.CompilerParams(
            dimension_semantics=("arbitrary", "arbitrary"),
            vmem_limit_bytes=VMEM_LIMIT),
        name=f"hyena_order{order}",
    )(zsrc, proj, conv_w, conv_b[None], conv_w, conv_b[None], hspec, f1, f2, f2i, g, out_g)


def _hyena(proj, conv_w, conv_b, w1, b1, w2, b2, w3, freq, filt_bias, out_g, batch, length):
    assert 2 * length == FFT_N1 * FFT_N2 and HYENA_ORDER == 2
    consts = _dft_constants()
    max_decay = math.log(DECAY_TARGET) / FAST_DECAY_PCT
    min_decay = math.log(DECAY_TARGET) / SLOW_DECAY_PCT
    deltas = jnp.abs(jnp.linspace(min_decay, max_decay, HYENA_WIDTH, dtype=F32))[None]
    hid = _filter_hidden(length, w1, b1, w2, b2, freq)
    hspec = _filter_spectrum(hid, w3, deltas, filt_bias, consts, length)
    vcol = (3 * ATTN_WIDTH + HYENA_ORDER * HYENA_WIDTH) // HY_CB
    z1 = _hyena_order(proj, vcol, proj, 0, conv_w, conv_b, hspec, consts, out_g, batch, length, True, False)
    return _hyena_order(z1, 0, proj, 1, conv_w, conv_b, hspec, consts, out_g, batch, length, False, True)


def kernel(x, mix_norm_g, w_in, q_norm_g, k_norm_g, rpb, conv_w, conv_b, filt_w1, filt_b1, filt_w2, filt_b2, filt_w3, filt_freq, filt_bias, attn_out_g, hyena_out_g, w_out, ffn_norm_g, w_router, w_gate, w_up, w_down):
    b, s, d = x.shape
    rows = s // GRID_W
    win_r = min(WIN_ROWS_MAX, rows)
    cap = EC_CAPACITY_FACTOR * s // N_EXPERTS
    x2 = x.reshape(b * s, d)
    for i in range(mix_norm_g.shape[0]):
        proj = _inproj(x2, mix_norm_g[i][None], w_in[i],
                       q_norm_g[i][None], k_norm_g[i][None])
        attn = _attention(proj, rpb[i], attn_out_g[i][None], b, s)

        hyn = _hyena(proj, conv_w[i], conv_b[i], filt_w1[i], filt_b1[i], filt_w2[i], filt_b2[i], filt_w3[i],
                     filt_freq[i], filt_bias[i], hyena_out_g[i][None], b, s)

        wr_pad = jnp.zeros((d, 128), BF16).at[:, :N_EXPERTS].set(w_router[i].astype(BF16))
        x1, hn, logits = _outproj(attn, hyn, x2, w_out[i].astype(BF16), ffn_norm_g[i][None], wr_pad)

        idx, gate = _route(logits, b, s, cap)
        x2 = _moe(idx, gate, hn, x1, w_gate[i], w_up[i], w_down[i], b, s)
    return x2.reshape(b, s, d)
```

```python
import functools
import math

import numpy as np
import jax
import jax.numpy as jnp
from jax import lax
from jax.experimental import pallas as pl
from jax.experimental.pallas import tpu as pltpu

D_MODEL = 2048
GRID_W = 64
N_ATTN_HEADS = 8
ATTN_WIDTH = D_MODEL // 2
HEAD_DIM = ATTN_WIDTH // N_ATTN_HEADS
HYENA_WIDTH = D_MODEL - ATTN_WIDTH
N_HYENA_GROUPS = 8
HYENA_ORDER = 2
N_DIRS = 2
SHORT_CONV_W = 3
FILTER_EMB = 33
DECAY_TARGET = 1e-2
FAST_DECAY_PCT = 0.3
SLOW_DECAY_PCT = 1.5
WIN_ROWS_MAX = 8
WIN_COLS = 16
N_EXPERTS = 16
EC_CAPACITY_FACTOR = 2
EXPERT_FF = 1024
IN_WIDTH = 3 * ATTN_WIDTH + (HYENA_ORDER + 1) * HYENA_WIDTH
RMS_EPS = 1e-6

F32 = jnp.float32
BF16 = jnp.bfloat16
NEG_BIG = -1e30
VMEM_LIMIT = 56 * 1024 * 1024
LANES = 128
LOG2_E = math.log2(math.e)


def _rms(x, eps=RMS_EPS):
    return x * lax.rsqrt(jnp.mean(x * x, axis=-1, keepdims=True) + eps)


INPROJ_GROUPS = 2


def _inproj_kernel(x_ref, g_ref, w_ref, qg_ref, kg_ref, o_ref, h_ref):
    n = pl.program_id(1)
    bm, bn = o_ref.shape

    def step(normalise_input, head_gain):
        w = w_ref[...].astype(BF16)
        for r in range(INPROJ_GROUPS):
            rows = slice(r * bm // INPROJ_GROUPS, (r + 1) * bm // INPROJ_GROUPS)
            if normalise_input:
                h_ref[rows, :] = (_rms(x_ref[rows, :]) * g_ref[...]).astype(BF16)
            acc = jnp.dot(h_ref[rows, :], w, preferred_element_type=F32)
            if head_gain is None:
                o_ref[rows, :] = acc.astype(BF16)
            else:
                for j in range(bn // HEAD_DIM):
                    sl = slice(j * HEAD_DIM, (j + 1) * HEAD_DIM)
                    o_ref[rows, sl] = (_rms(acc[:, sl]) * head_gain).astype(BF16)

    @pl.when(n == 0)
    def _():
        step(True, qg_ref[...] * (HEAD_DIM ** -0.5 * LOG2_E))

    @pl.when(n == 1)
    def _():
        step(False, kg_ref[...])

    @pl.when(n >= 2)
    def _():
        step(False, None)


def _inproj(x2, g, w_bf, qg, kg, bm=1024, bn=1024):
    m, d = x2.shape
    nw = w_bf.shape[1]
    return pl.pallas_call(
        _inproj_kernel,
        grid=(m // bm, nw // bn),
        in_specs=[
            pl.BlockSpec((bm, d), lambda i, j: (i, 0)),
            pl.BlockSpec((1, d), lambda i, j: (0, 0)),
            pl.BlockSpec((d, bn), lambda i, j: (0, j)),
            pl.BlockSpec((1, HEAD_DIM), lambda i, j: (0, 0)),
            pl.BlockSpec((1, HEAD_DIM), lambda i, j: (0, 0)),
        ],
        out_specs=pl.BlockSpec((bm, bn), lambda i, j: (i, j)),
        out_shape=jax.ShapeDtypeStruct((m, nw), BF16),
        scratch_shapes=[pltpu.VMEM((bm, d), BF16)],
        compiler_params=pltpu.CompilerParams(
            dimension_semantics=("arbitrary", "arbitrary"),
            vmem_limit_bytes=VMEM_LIMIT),
        name="inproj",
    )(x2, g, w_bf, qg, kg)


ATTN_ROWS = 4
ATTN_BAND = 12


def _attn_kernel(q_ref, k_ref, v_ref, rpb_ref, mask_ref, g_ref, o_ref, pt_ref, *, rows, win_r):
    lanes = pt_ref.shape[-1]
    n_dr = 2 * WIN_ROWS_MAX - 1

    def toeplitz(d):
        row = jnp.broadcast_to(rpb_ref[0, d:d + 1, :], (GRID_W, lanes))
        return pltpu.roll(row, lanes - (WIN_COLS - 1), 1, stride=1, stride_axis=0)

    lane = lax.broadcasted_iota(jnp.int32, (GRID_W, lanes), 1)
    left = lane < GRID_W
    tiles = [toeplitz(d) for d in range(n_dr)]
    for d in range(-1, n_dr):
        lo, hi = tiles[max(d, 0)], tiles[min(d + 1, n_dr - 1)]
        pt_ref[d + 1] = jnp.where(left, lo, pltpu.roll(hi, GRID_W, 1)) * LOG2_E + mask_ref[...]

    def body(i, carry):
        r0 = i * ATTN_ROWS
        bs = jnp.minimum(jnp.clip(r0 - win_r // 2, 0, rows - win_r), rows - ATTN_BAND)
        q0 = pl.multiple_of(r0 * GRID_W, ATTN_ROWS * GRID_W)
        k0 = pl.multiple_of(bs * GRID_W, 2 * GRID_W)
        q = q_ref[pl.ds(q0, ATTN_ROWS * GRID_W), :]
        k = k_ref[pl.ds(k0, ATTN_BAND * GRID_W), :]
        v = v_ref[pl.ds(k0, ATTN_BAND * GRID_W), :]
        s = lax.dot_general(q, k, (((1,), (1,)), ((), ())), preferred_element_type=F32)

        bias_rows = []
        for j in range(ATTN_ROWS):
            r = r0 + j
            start = jnp.clip(r - win_r // 2, 0, rows - win_r)
            pieces = []
            for m in range(ATTN_BAND // 2):
                kr = bs + 2 * m
                pen_l = jnp.where((kr >= start) & (kr < start + win_r), 0.0, NEG_BIG)
                pen_r = jnp.where((kr + 1 >= start) & (kr + 1 < start + win_r), 0.0, NEG_BIG)
                d = jnp.clip(kr - r + WIN_ROWS_MAX - 1, -1, n_dr - 1)
                pieces.append(pt_ref[d + 1] + jnp.where(left, pen_l, pen_r))
            bias_rows.append(jnp.concatenate(pieces, axis=1))
        s = s + jnp.concatenate(bias_rows, axis=0)

        mx = jnp.max(s, axis=-1, keepdims=True)
        p = jnp.exp2(s - mx)
        l = jnp.sum(p, axis=-1, keepdims=True)
        o = jnp.dot(p.astype(BF16), v, preferred_element_type=F32) / l
        o = _rms(o) * g_ref[...]
        o_ref[pl.ds(q0, ATTN_ROWS * GRID_W), :] = o.astype(BF16)
        return carry

    lax.fori_loop(0, rows // ATTN_ROWS, body, 0, unroll=2)


def _attn_window_mask():
    c = np.arange(GRID_W)
    c0 = np.clip(c - WIN_COLS // 2, 0, GRID_W - WIN_COLS)
    in_win = (c[None, :] >= c0[:, None]) & (c[None, :] < c0[:, None] + WIN_COLS)
    return jnp.asarray(np.tile(np.where(in_win, 0.0, NEG_BIG), (1, 2)), dtype=F32)


def _attention(proj, rpb, out_g, batch, seq):
    rows = seq // GRID_W
    win_r = min(WIN_ROWS_MAX, rows)
    assert win_r == WIN_ROWS_MAX and 2 * GRID_W == LANES
    assert rows % ATTN_ROWS == 0 and ATTN_BAND >= win_r + ATTN_ROWS - 1 and ATTN_BAND % 2 == 0
    h = N_ATTN_HEADS
    n_dr, n_dc = rpb.shape[1:]
    rpb_pad = jnp.zeros((h, 2 * WIN_ROWS_MAX, LANES), F32).at[:, :n_dr, :n_dc].set(rpb)
    mask = _attn_window_mask()
    kern = functools.partial(_attn_kernel, rows=rows, win_r=win_r)
    return pl.pallas_call(
        kern,
        grid=(batch, h),
        in_specs=[
            pl.BlockSpec((seq, HEAD_DIM), lambda b, i: (b, i)),
            pl.BlockSpec((seq, HEAD_DIM), lambda b, i: (b, h + i)),
            pl.BlockSpec((seq, HEAD_DIM), lambda b, i: (b, 2 * h + i)),
            pl.BlockSpec((1, 2 * WIN_ROWS_MAX, LANES), lambda b, i: (i, 0, 0)),
            pl.BlockSpec(mask.shape, lambda b, i: (0, 0)),
            pl.BlockSpec((1, HEAD_DIM), lambda b, i: (0, i)),
        ],
        out_specs=pl.BlockSpec((seq, HEAD_DIM), lambda b, i: (b, i)),
        out_shape=jax.ShapeDtypeStruct((batch * seq, ATTN_WIDTH), BF16),
        scratch_shapes=[pltpu.VMEM((2 * WIN_ROWS_MAX, GRID_W, LANES), F32)],
        compiler_params=pltpu.CompilerParams(
            dimension_semantics=("arbitrary", "arbitrary"),
            vmem_limit_bytes=VMEM_LIMIT),
        name="na2d_attn",
    )(proj, proj, proj, rpb_pad, mask, out_g)


OUTPROJ_GROUPS = 2


def _outproj_kernel(a_ref, hy_ref, x_ref, wo_ref, g_ref, wr_ref, x1_ref, hn_ref, lg_ref):
    half = a_ref.shape[1]
    bm = x_ref.shape[0]
    for r in range(OUTPROJ_GROUPS):
        rows = slice(r * bm // OUTPROJ_GROUPS, (r + 1) * bm // OUTPROJ_GROUPS)
        acc = x_ref[rows, :]
        acc = acc + jnp.dot(a_ref[rows, :], wo_ref[:half, :], preferred_element_type=F32)
        acc = acc + jnp.dot(hy_ref[rows, :], wo_ref[half:, :], preferred_element_type=F32)
        x1_ref[rows, :] = acc
        hn = _rms(acc) * g_ref[...]
        hn_ref[rows, :] = hn
        lg_ref[rows, :] = jnp.dot(hn.astype(BF16), wr_ref[...], preferred_element_type=F32)


def _outproj(attn, hy, x2, wo_bf, g, wr_pad, bm=512):
    m, d = x2.shape
    half = attn.shape[1]
    npad = wr_pad.shape[1]
    return pl.pallas_call(
        _outproj_kernel,
        grid=(m // bm,),
        in_specs=[
            pl.BlockSpec((bm, half), lambda i: (i, 0)),
            pl.BlockSpec((bm, half), lambda i: (i, 0)),
            pl.BlockSpec((bm, d), lambda i: (i, 0)),
            pl.BlockSpec((d, d), lambda i: (0, 0)),
            pl.BlockSpec((1, d), lambda i: (0, 0)),
            pl.BlockSpec((d, npad), lambda i: (0, 0)),
        ],
        out_specs=[
            pl.BlockSpec((bm, d), lambda i: (i, 0)),
            pl.BlockSpec((bm, d), lambda i: (i, 0)),
            pl.BlockSpec((bm, npad), lambda i: (i, 0)),
        ],
        out_shape=[
            jax.ShapeDtypeStruct((m, d), F32),
            jax.ShapeDtypeStruct((m, d), F32),
            jax.ShapeDtypeStruct((m, npad), F32),
        ],
        compiler_params=pltpu.CompilerParams(
            dimension_semantics=("arbitrary",),
            vmem_limit_bytes=VMEM_LIMIT),
        name="outproj",
    )(attn, hy, x2, wo_bf, g, wr_pad)


ROUTE_BLK = 128
ROUTE_CHUNK = 512
SLOT_GRP = 8
ROUTE_GROUPS_INLINE = 3
F32_INF_BITS = 0x7F800000


def _route_kernel(lg_ref, tri_ref, idx_ref, gate_ref, aff_ref, pos_t_ref, aff_t_ref, off_ref, ia_ref, ga_ref,
                  stage_i, stage_g, off_smem, sem, *, cap):
    s, lanes = lg_ref.shape
    lane = lax.broadcasted_iota(jnp.int32, (1, lanes), 1)

    def softmax_chunk(i, c):
        rows = pl.ds(pl.multiple_of(i * ROUTE_CHUNK, ROUTE_CHUNK), ROUTE_CHUNK)
        lg = jnp.where(lane < N_EXPERTS, lg_ref[rows, :], NEG_BIG)
        ex = jnp.exp(lg - jnp.max(lg, axis=-1, keepdims=True))
        aff_ref[rows, :] = ex / jnp.sum(ex, axis=-1, keepdims=True)
        return c

    lax.fori_loop(0, s // ROUTE_CHUNK, softmax_chunk, 0)

    def as_value(bits):
        return pltpu.bitcast(bits, F32)

    def fold8(x):
        parts = [x[i:i + 8] for i in range(0, x.shape[0], 8)]
        while len(parts) > 1:
            parts = [a + b for a, b in zip(parts[::2], parts[1::2])]
        return parts[0]

    def count_ge(th):
        def body(i, acc):
            rows = pl.ds(pl.multiple_of(i * ROUTE_CHUNK, ROUTE_CHUNK), ROUTE_CHUNK)
            return acc + fold8(jnp.where(aff_ref[rows, :] >= th, 1.0, 0.0))
        acc = lax.fori_loop(0, s // ROUTE_CHUNK, body, jnp.zeros((8, lanes), F32))
        return jnp.sum(acc, axis=0, keepdims=True)

    def bisect(_, c):
        lo, hi = c
        mid = lo + ((hi - lo) >> 1)
        ok = count_ge(as_value(mid)) >= cap
        return jnp.where(ok, mid, lo), jnp.where(ok, hi, mid)

    tau_bits, _ = lax.fori_loop(0, 31, bisect, (jnp.zeros((1, lanes), jnp.int32),
                                                jnp.full((1, lanes), F32_INF_BITS, jnp.int32)))
    tau, above = as_value(tau_bits), as_value(tau_bits + 1)
    ties_wanted = cap - count_ge(above)

    def scan_block(j, c):
        tie_carry, sel_carry = c
        rows = pl.ds(pl.multiple_of(j * ROUTE_BLK, ROUTE_BLK), ROUTE_BLK)
        aff = aff_ref[rows, :]
        gt = aff >= above
        eq = jnp.where((aff >= tau) & (aff < above), 1.0, 0.0)
        tie_incl = jnp.dot(tri_ref[...], eq.astype(BF16), preferred_element_type=F32) + tie_carry
        sel = jnp.where(gt | ((eq > 0.0) & (tie_incl - eq < ties_wanted)), 1.0, 0.0)
        sel_incl = jnp.dot(tri_ref[...], sel.astype(BF16), preferred_element_type=F32) + sel_carry
        pos = jnp.where(sel > 0.0, sel_incl - sel, -1.0)
        pos_t_ref[j] = pos.T[:N_EXPERTS]
        aff_t_ref[j] = aff.T[:N_EXPERTS]
        off_ref[pl.ds(j, 1), :] = sel_carry.astype(jnp.int32)
        return tie_incl[ROUTE_BLK - 1:, :], sel_incl[ROUTE_BLK - 1:, :]

    n_blk = s // ROUTE_BLK
    off_ref[...] = jnp.zeros_like(off_ref)
    zero = jnp.zeros((1, lanes), F32)
    _, total = lax.fori_loop(0, n_blk, scan_block, (zero, zero), unroll=4)
    off_ref[pl.ds(n_blk, 1), :] = total.astype(jnp.int32)

    to_smem = pltpu.make_async_copy(off_ref, off_smem, sem)
    to_smem.start()
    to_smem.wait()

    n_grp = cap // SLOT_GRP
    sub = lax.broadcasted_iota(jnp.int32, (SLOT_GRP, lanes), 0)
    lane_i = lax.broadcasted_iota(jnp.int32, (SLOT_GRP, lanes), 1)
    stage_i[...] = jnp.zeros_like(stage_i)
    stage_g[...] = jnp.zeros_like(stage_g)
    for e in range(N_EXPERTS):
        ia_ref[...] = jnp.zeros_like(ia_ref)
        ga_ref[...] = jnp.zeros_like(ga_ref)

        def block(j, carry, e=e):
            first, last = off_smem[j, e], off_smem[j + 1, e]
            pos = jnp.broadcast_to(pos_t_ref[j, e:e + 1, :], (SLOT_GRP, lanes))
            aff = jnp.broadcast_to(aff_t_ref[j, e:e + 1, :], (SLOT_GRP, lanes))
            tok = (j * ROUTE_BLK + lane_i).astype(F32)
            g0 = first // SLOT_GRP
            groups = jnp.where(last > first, (last - 1) // SLOT_GRP - g0 + 1, 0)

            def update(g):
                match = pos == (g * SLOT_GRP + sub).astype(F32)
                gi = jnp.minimum(g, n_grp - 1)
                ia_ref[gi] += jnp.where(match, tok, 0.0)
                ga_ref[gi] += jnp.where(match, aff, 0.0)

            for k in range(ROUTE_GROUPS_INLINE):
                update(g0 + k)

            def rest(k, c):
                update(g0 + k)
                return c

            lax.fori_loop(ROUTE_GROUPS_INLINE, groups, rest, 0)
            return carry

        lax.fori_loop(0, n_blk, block, 0)
        stage_i[:, e:e + 1] = jnp.sum(ia_ref[...].reshape(cap, lanes), axis=1, keepdims=True)
        stage_g[:, e:e + 1] = jnp.sum(ga_ref[...].reshape(cap, lanes), axis=1, keepdims=True)

    idx_ref[0] = stage_i[...].T[:N_EXPERTS].astype(jnp.int32)
    gate_ref[0] = stage_g[...].T[:N_EXPERTS]


def _route(logits, batch, seq, cap):
    lanes = logits.shape[1]
    assert lanes == LANES == ROUTE_BLK and cap % SLOT_GRP == 0
    n_blk = seq // ROUTE_BLK
    n_off = -(-(n_blk + 1) // 8) * 8
    tri =jnp.asarray(np.tril(np.ones((ROUTE_BLK, ROUTE_BLK))), dtype=BF16)
    return pl.pallas_call(
        functools.partial(_route_kernel, cap=cap),
        grid=(batch,),
        in_specs=[
            pl.BlockSpec((seq, lanes), lambda b: (b, 0)),
            pl.BlockSpec((ROUTE_BLK, ROUTE_BLK), lambda b: (0, 0)),
        ],
        out_specs=[
            pl.BlockSpec((1, N_EXPERTS, cap), lambda b: (b, 0, 0)),
            pl.BlockSpec((1, N_EXPERTS, cap), lambda b: (b, 0, 0)),
        ],
        out_shape=[
            jax.ShapeDtypeStruct((batch, N_EXPERTS, cap), jnp.int32),
            jax.ShapeDtypeStruct((batch, N_EXPERTS, cap), F32),
        ],
        scratch_shapes=[
            pltpu.VMEM((seq, lanes), F32),
            pltpu.VMEM((n_blk, N_EXPERTS, lanes), F32),
            pltpu.VMEM((n_blk, N_EXPERTS, lanes), F32),
            pltpu.VMEM((n_off, lanes), jnp.int32),
            pltpu.VMEM((cap // SLOT_GRP, SLOT_GRP, lanes), F32),
            pltpu.VMEM((cap // SLOT_GRP, SLOT_GRP, lanes), F32),
            pltpu.VMEM((cap, lanes), F32),
            pltpu.VMEM((cap, lanes), F32),
            pltpu.SMEM((n_off, lanes), jnp.int32),
            pltpu.SemaphoreType.DMA(()),
        ],
        compiler_params=pltpu.CompilerParams(dimension_semantics=("arbitrary",)),
        name="route",
    )(logits, tri)


MOE_FF_SPLIT = 4
ROW_UNROLL = 8


def _moe_kernel(idx_ref, hn_hbm, wg_ref, wu_ref, wd_ref, gate_ref, x1_hbm, out_hbm,
                xg0, xg1, acc0, acc1, y0, y1, sem_x, sem_a, sem_s, *, cap, seq):
    del x1_hbm
    e, f = pl.program_id(0), pl.program_id(1)
    n_e = pl.num_programs(0)
    xg, acc = (xg0, xg1), (acc0, acc1)
    prev_e, next_e = jnp.maximum(e - 1, 0), jnp.minimum(e + 1, n_e - 1)

    def row_of(ee, b):
        base = (b * n_e + ee) * cap
        return lambda p: b * seq + idx_ref[base + p]

    def copy_rows(ee, b, kind, rolled=False):
        row = row_of(ee, b)

        def one(p):
            if kind == "xg":
                src, dst, sem = hn_hbm.at[pl.ds(row(p), 1), :], xg[b].at[pl.ds(p, 1), :], sem_x.at[b]
            elif kind == "ag":
                src, dst, sem = out_hbm.at[pl.ds(row(p), 1), :], acc[b].at[pl.ds(p, 1), :], sem_a.at[b]
            else:
                src, dst, sem = acc[b].at[pl.ds(p, 1), :], out_hbm.at[pl.ds(row(p), 1), :], sem_s.at[b]
            pltpu.make_async_copy(src, dst, sem).start()

        if rolled:
            def body(i, c):
                for k in range(ROW_UNROLL):
                    one(i * ROW_UNROLL + k)
                return c
            lax.fori_loop(0, cap // ROW_UNROLL, body, 0)
        else:
            for p in range(cap):
                one(p)

    def wait_rows(buf, sem):
        pltpu.make_async_copy(hn_hbm.at[pl.ds(0, cap), :], buf, sem).wait()

    def partial_out(b, wg, wu, wd):
        x = xg[b][...].astype(BF16)
        a = jnp.dot(x, wg, preferred_element_type=F32)
        u = jnp.dot(x, wu, preferred_element_type=F32)
        hmid = (a * (1.0 / (1.0 + jnp.exp(-a))) * u).astype(BF16)
        return jnp.dot(hmid, wd, preferred_element_type=F32)

    def weights():
        return wg_ref[0].astype(BF16), wu_ref[0].astype(BF16), wd_ref[0].astype(BF16)

    @pl.when((e == 0) & (f == 0))
    def _():
        copy_rows(e, 0, "xg", rolled=True)
        copy_rows(e, 0, "ag", rolled=True)
        copy_rows(e, 1, "ag", rolled=True)
        wait_rows(acc0, sem_a.at[0])
        wait_rows(acc1, sem_a.at[1])

    @pl.when(f == 0)
    def _():
        w = weights()
        wait_rows(xg0, sem_x.at[0])
        copy_rows(e, 1, "xg")
        y0[...] = partial_out(0, *w)
        wait_rows(xg1, sem_x.at[1])
        copy_rows(prev_e, 0, "sc")
        y1[...] = partial_out(1, *w)

    @pl.when(f == 1)
    def _():
        w = weights()
        copy_rows(prev_e, 1, "sc")
        y0[...] += partial_out(0, *w)
        wait_rows(acc0, sem_s.at[0])
        copy_rows(e, 0, "ag")
        y1[...] += partial_out(1, *w)

    @pl.when(f == 2)
    def _():
        w = weights()
        wait_rows(acc1, sem_s.at[1])
        copy_rows(e, 1, "ag")
        y0[...] += partial_out(0, *w)
        y1[...] += partial_out(1, *w)

    @pl.when(f == 3)
    def _():
        w = weights()
        t0 = y0[...] + partial_out(0, *w)
        wait_rows(acc0, sem_a.at[0])
        acc0[...] += t0 * gate_ref[0, 0]
        copy_rows(next_e, 0, "xg")
        t1 = y1[...] + partial_out(1, *w)
        wait_rows(acc1, sem_a.at[1])
        acc1[...] += t1 * gate_ref[1, 0]

    @pl.when((e == n_e - 1) & (f == MOE_FF_SPLIT - 1))
    def _():
        copy_rows(e, 0, "sc", rolled=True)
        copy_rows(e, 1, "sc", rolled=True)
        wait_rows(acc0, sem_s.at[0])
        wait_rows(acc1, sem_s.at[1])
        wait_rows(xg0, sem_x.at[0])


def _moe(idx, gate, hn, x1, wg, wu, wd, batch, seq):
    n_e, d, ff = wg.shape
    cap = idx.shape[-1]
    assert batch == 2 and MOE_FF_SPLIT == 4 and ff % MOE_FF_SPLIT == 0
    fq = ff // MOE_FF_SPLIT
    grid_spec = pltpu.PrefetchScalarGridSpec(
        num_scalar_prefetch=1,
        grid=(n_e, MOE_FF_SPLIT),
        in_specs=[
            pl.BlockSpec(memory_space=pl.ANY),
            pl.BlockSpec((1, d, fq), lambda i, j, s: (i, 0, j)),
            pl.BlockSpec((1, d, fq), lambda i, j, s: (i, 0, j)),
            pl.BlockSpec((1, fq, d), lambda i, j, s: (i, j, 0)),
            pl.BlockSpec((batch, 1, cap, 1), lambda i, j, s: (0, i, 0, 0)),
            pl.BlockSpec(memory_space=pl.ANY),
        ],
        out_specs=pl.BlockSpec(memory_space=pl.ANY),
        scratch_shapes=(
            [pltpu.VMEM((cap, d), F32) for _ in range(6)]
            + [pltpu.SemaphoreType.DMA((batch,)) for _ in range(3)]),
    )
    return pl.pallas_call(
        functools.partial(_moe_kernel, cap=cap, seq=seq),
        grid_spec=grid_spec,
        out_shape=jax.ShapeDtypeStruct(x1.shape, x1.dtype),
        input_output_aliases={6: 0},
        compiler_params=pltpu.CompilerParams(
            dimension_semantics=("arbitrary", "arbitrary"),
            vmem_limit_bytes=VMEM_LIMIT),
        name="moe_experts",
    )(idx.reshape(-1), hn, wg, wu, wd, gate[..., None], x1)


FFT_N1 = 128
FFT_N2 = 64
FFT_H1 = FFT_N1 // 2
K1_PAD = 72
Z_PITCH = 72
AB_PITCH = 136
HY_CB = 256
FFT_UNROLL = 8
HP = lax.Precision.HIGHEST


def _dft_constants():
    n, n1, n2, h1 = FFT_N1 * FFT_N2, FFT_N1, FFT_N2, FFT_H1
    k1 = np.arange(h1 + 1, dtype=np.float64)
    s1 = np.arange(h1, dtype=np.float64)
    s2 = np.arange(n2, dtype=np.float64)
    tw = s2[:, None, None] * k1[None, :, None] / n

    def stage1(phase_s1):
        th = -2.0 * np.pi * (phase_s1 + tw)
        m = np.zeros((n2, 2 * K1_PAD, h1))
        m[:, :h1 + 1] = np.cos(th)
        m[:, K1_PAD:K1_PAD + h1 + 1] = np.sin(th)
        return m

    f1 = stage1(s1[None, None, :] * k1[None, :, None] / n1)
    f1b = stage1(((n1 - 1) - s1)[None, None, :] * k1[None, :, None] / n1)
    f1b0 = stage1((n1 - s1)[None, None, :] * k1[None, :, None] / n1)[0]
    f1b0[:, 0] = 0.0
    f1b[0] = f1b0

    a = 2.0 * np.pi * np.outer(np.arange(n2), np.arange(n2)) / n2
    c, s = np.cos(a), np.sin(a)
    f2 = np.block([[c, s], [-s, c]])
    f2i = np.block([[c, -s], [s, c]])

    t1 = np.arange(h1, dtype=np.float64)
    th = 2.0 * np.pi * (t1[None, :, None] * k1[None, None, :] / n1 + s2[:, None, None] * k1[None, None, :] / n)
    wgt = np.where((k1 == 0) | (k1 == h1), 1.0, 2.0) / n
    g = np.zeros((n2, h1, 2 * K1_PAD))
    g[:, :, :h1 + 1] = wgt * np.cos(th)
    g[:, :, K1_PAD:K1_PAD + h1 + 1] = -wgt * np.sin(th)
    return tuple(jnp.asarray(m, dtype=F32).astype(BF16) for m in (f1, f1b, f2, f2i, g))


def _hid_kernel(f_ref, w1t_ref, w1c_ref, w1s_ref, b1_ref, w2_ref, b2_ref, fr_ref, o_ref, *, length):
    bm = o_ref.shape[0]
    pos = (pl.program_id(0) * bm + lax.broadcasted_iota(jnp.int32, (1, bm), 1)).astype(F32)
    t = pos / (length - 1.0)
    ang = f_ref[...] * (2.0 * math.pi * pos / length)
    pre = (w1t_ref[...] * t
           + jnp.dot(w1c_ref[...], jnp.cos(ang), precision=HP, preferred_element_type=F32)
           - jnp.dot(w1s_ref[...], jnp.sin(ang), precision=HP, preferred_element_type=F32))
    fr = fr_ref[...]
    hid = jnp.sin(fr * (pre + b1_ref[...]))
    hid = jnp.sin(fr * (jnp.dot(w2_ref[...], hid, precision=HP, preferred_element_type=F32) + b2_ref[...]))
    hi = hid.astype(BF16).astype(F32)
    lo = hid - hi
    o_ref[...] = jnp.concatenate([hi, hi, lo, jnp.zeros_like(hi)], axis=0).T.astype(BF16)


def _filter_hidden(length, w1, b1, w2, b2, freq, bm=512):
    bands = (FILTER_EMB - 1) // 2
    f = jnp.linspace(1e-4, bands - 1, bands, dtype=F32)[:, None]
    hdim = w1.shape[1]
    full = lambda a: pl.BlockSpec(a.shape, lambda i: (0,) * a.ndim)
    args = (f, w1[:1].T, w1[1:1 + bands].T, w1[1 + bands:].T, b1[:, None], w2.T, b2[:, None], freq[:, None])
    return pl.pallas_call(
        functools.partial(_hid_kernel, length=length),
        grid=(length // bm,),
        in_specs=[full(a) for a in args],
        out_specs=pl.BlockSpec((bm, 4 * hdim), lambda i: (i, 0)),
        out_shape=jax.ShapeDtypeStruct((length, 4 * hdim), BF16),
        name="filter_hidden",
    )(*args)


def _ld(ref, start, size, stride=None):
    idx = pl.ds(start, size) if stride is None else pl.ds(start, size, stride=stride)
    return jnp.concatenate([ref[h, idx, :] for h in range(ref.shape[0])], axis=-1)


def _st(ref, start, size, val, stride=None):
    idx = pl.ds(start, size) if stride is None else pl.ds(start, size, stride=stride)
    for h in range(ref.shape[0]):
        ref[h, idx, :] = val[:, h * LANES:(h + 1) * LANES]


def _stage1_store(ab_ref, s2, a):
    _st(ab_ref, s2, K1_PAD, a[:K1_PAD], stride=AB_PITCH)
    _st(ab_ref, FFT_N2 + s2, K1_PAD, a[K1_PAD:], stride=AB_PITCH)


def _stack_split_rows(w):
    hi = w.astype(BF16)
    lo = (w - hi.astype(F32)).astype(BF16)
    return jnp.concatenate([hi, lo, hi, jnp.zeros_like(hi)], axis=0)


def _spectrum_kernel(hid_ref, w3f_ref, w3b_ref, dl_ref, skip_ref, f1_ref, f2_ref, o_ref,
                     hf_ref, hb_ref, ab_ref, *, length):
    blocks = 4
    rows = blocks * FFT_N2
    w3f, w3b = _stack_split_rows(w3f_ref[...]), _stack_split_rows(w3b_ref[...])

    def gen(j, c):
        row = (j * rows + lax.broadcasted_iota(jnp.int32, (rows, 1), 0)).astype(F32)
        win = jnp.exp(-(row / (length - 1.0)) * dl_ref[...])
        hid = hid_ref[pl.ds(pl.multiple_of(j * rows, rows), rows), :]
        hf = jnp.dot(hid, w3f, preferred_element_type=F32) * win
        hb = jnp.dot(hid, w3b, preferred_element_type=F32) * win
        for q in range(blocks):
            dst = pl.multiple_of((j * blocks + q) * Z_PITCH, 8)
            _st(hf_ref, dst, FFT_N2, hf[q * FFT_N2:(q + 1) * FFT_N2])
            _st(hb_ref, dst, FFT_N2, hb[q * FFT_N2:(q + 1) * FFT_N2])
        return c

    lax.fori_loop(0, FFT_H1 // blocks, gen, 0)

    def stage1(s2, c):
        xf = _ld(hf_ref, s2, FFT_H1, Z_PITCH).astype(BF16)
        xb = _ld(hb_ref, (FFT_N2 - s2) % FFT_N2, FFT_H1, Z_PITCH).astype(BF16)
        x = jnp.concatenate([xf, xb], axis=0)
        _stage1_store(ab_ref, s2, jnp.dot(f1_ref[s2], x, preferred_element_type=F32))
        return c

    lax.fori_loop(0, FFT_N2, stage1, 0, unroll=2 * FFT_UNROLL)

    def stage2(k1, c):
        a = _ld(ab_ref, pl.multiple_of(k1 * AB_PITCH, 8), 2 * FFT_N2).astype(BF16)
        x = jnp.dot(f2_ref[...], a, preferred_element_type=F32)
        o_ref[0, k1, :FFT_N2, :] = (x[:FFT_N2] + skip_ref[0]).astype(BF16)
        o_ref[0, k1, FFT_N2:, :] = x[FFT_N2:].astype(BF16)
        return c

    lax.fori_loop(0, FFT_H1 + 1, stage2, 0, unroll=13)


def _filter_spectrum(hid, w3, deltas, filt_bias, consts, length):
    f1, f1b, f2, _, _ = consts
    f1 = jnp.concatenate([f1, f1b], axis=2)
    cb = HY_CB
    ncb = HYENA_WIDTH // cb
    hdim = w3.shape[0]
    full = lambda a: pl.BlockSpec(a.shape, lambda o, c: (0,) * a.ndim)
    return pl.pallas_call(
        functools.partial(_spectrum_kernel, length=length),
        grid=(HYENA_ORDER, ncb),
        in_specs=[
            full(hid),
            pl.BlockSpec((hdim, cb), lambda o, c: (0, (o * N_DIRS) * ncb + c)),
            pl.BlockSpec((hdim, cb), lambda o, c: (0, (o * N_DIRS + 1) * ncb + c)),
            pl.BlockSpec((1, cb), lambda o, c: (0, c)),
            pl.BlockSpec((1, 1, cb), lambda o, c: (o, 0, c)),
            full(f1), full(f2),
        ],
        out_specs=pl.BlockSpec((1, FFT_H1 + 1, 2 * FFT_N2, cb), lambda o, c: (o, 0, 0, c)),
        out_shape=jax.ShapeDtypeStruct((HYENA_ORDER, FFT_H1 + 1, 2 * FFT_N2, HYENA_WIDTH), BF16),
        scratch_shapes=[
            pltpu.VMEM((cb // LANES, FFT_H1 * Z_PITCH, LANES), F32),
            pltpu.VMEM((cb // LANES, FFT_H1 * Z_PITCH, LANES), F32),
            pltpu.VMEM((cb // LANES, K1_PAD * AB_PITCH, LANES), F32),
        ],
        compiler_params=pltpu.CompilerParams(
            dimension_semantics=("arbitrary", "arbitrary"),
            vmem_limit_bytes=VMEM_LIMIT),
        name="filter_spectrum",
    )(hid, w3, w3, deltas, filt_bias[:, None, :], f1, f2)


def _short_conv_block(s_ref, j, w_ref, b_ref):
    big = s_ref[pl.ds(pl.multiple_of(j * FFT_N2, FFT_N2), FFT_N2 + 16), :]
    rows = FFT_N2 + 16
    prev = pltpu.roll(big, 1, 0)[8:8 + FFT_N2]
    nxt = pltpu.roll(big, rows - 1, 0)[8:8 + FFT_N2]
    cur = big[8:8 + FFT_N2]
    return b_ref[...] + prev * w_ref[0:1, :] + cur * w_ref[1:2, :] + nxt * w_ref[2:3, :]


def _load_stream(s_ref, src_ref, length):
    zeros = jnp.zeros((8, s_ref.shape[1]), F32)
    s_ref[0:8, :] = zeros
    s_ref[length + 8:length + 16, :] = zeros
    chunk = 512

    def cp(i, c):
        r = pl.multiple_of(i * chunk, chunk)
        s_ref[pl.ds(r + 8, chunk), :] = src_ref[pl.ds(r, chunk), :].astype(F32)
        return c

    lax.fori_loop(0, length // chunk, cp, 0)


def _hyena_kernel(z_ref, gs_ref, cwz_ref, cbz_ref, cwg_ref, cbg_ref, h_ref, f1_ref, f2_ref, f2i_ref, g_ref,
                  og_ref, o_ref, s_ref, zy_ref, ab_ref, *, length, conv_z, norm):
    if conv_z:
        _load_stream(s_ref, z_ref, length)

    def prep(j, c):
        if conv_z:
            zblk = _short_conv_block(s_ref, j, cwz_ref, cbz_ref)
        else:
            zblk = z_ref[pl.ds(pl.multiple_of(j * FFT_N2, FFT_N2), FFT_N2), :].astype(F32)
        _st(zy_ref, pl.multiple_of(j * Z_PITCH, 8), FFT_N2, zblk)
        return c

    lax.fori_loop(0, FFT_H1, prep, 0, unroll=2)

    def stage1(s2, c):
        xs = _ld(zy_ref, s2, FFT_H1, Z_PITCH).astype(BF16)
        _stage1_store(ab_ref, s2, jnp.dot(f1_ref[s2], xs, preferred_element_type=F32))
        return c

    lax.fori_loop(0, FFT_N2, stage1, 0, unroll=2 * FFT_UNROLL)

    def stage2(k1, c):
        blk = pl.multiple_of(k1 * AB_PITCH, 8)
        x = jnp.dot(f2_ref[...], _ld(ab_ref, blk, 2 * FFT_N2).astype(BF16), preferred_element_type=F32)
        hk = h_ref[0, k1].astype(F32)
        xr, xi, hr, hi = x[:FFT_N2], x[FFT_N2:], hk[:FFT_N2], hk[FFT_N2:]
        p = jnp.concatenate([xr * hr - xi * hi, xr * hi + xi * hr], axis=0).astype(BF16)
        _st(ab_ref, blk, 2 * FFT_N2, jnp.dot(f2i_ref[...], p, preferred_element_type=F32))
        return c

    lax.fori_loop(0, FFT_H1 + 1, stage2, 0, unroll=13)

    def stage3(t2, c):
        br = _ld(ab_ref, t2, K1_PAD, AB_PITCH)
        bi = _ld(ab_ref, FFT_N2 + t2, K1_PAD, AB_PITCH)
        rhs = jnp.concatenate([br, bi], axis=0).astype(BF16)
        _st(zy_ref, t2, FFT_H1, jnp.dot(g_ref[t2], rhs, preferred_element_type=F32), stride=Z_PITCH)
        return c

    lax.fori_loop(0, FFT_N2, stage3, 0, unroll=2 * FFT_UNROLL)

    _load_stream(s_ref, gs_ref, length)

    def fin(j, c):
        gate = _short_conv_block(s_ref, j, cwg_ref, cbg_ref)
        y = gate * _ld(zy_ref, pl.multiple_of(j * Z_PITCH, 8), FFT_N2)
        dst = pl.ds(pl.multiple_of(j * FFT_N2, FFT_N2), FFT_N2)
        if norm:
            gw = HYENA_WIDTH // N_HYENA_GROUPS
            for q in range(y.shape[1] // gw):
                sl = slice(q * gw, (q + 1) * gw)
                o_ref[dst, sl] = (_rms(y[:, sl]) * og_ref[:, sl]).astype(o_ref.dtype)
        else:
            o_ref[dst, :] = y.astype(o_ref.dtype)
        return c

    lax.fori_loop(0, FFT_H1, fin, 0, unroll=4)


def _hyena_order(zsrc, zcol, proj, order, conv_w, conv_b, hspec, consts, out_g, batch, length, conv_z, norm):
    f1, _, f2, f2i, g = consts
    cb = HY_CB
    ncb = HYENA_WIDTH // cb
    pcol = 3 * ATTN_WIDTH // cb
    zconv = HYENA_ORDER * ncb
    full = lambda a: pl.BlockSpec(a.shape, lambda c, b: (0,) * a.ndim)
    kern = functools.partial(_hyena_kernel, length=length, conv_z=conv_z, norm=norm)
    return pl.pallas_call(
        kern,
        grid=(ncb, batch),
        in_specs=[
            pl.BlockSpec((length, cb), lambda c, b: (b, zcol + c)),
            pl.BlockSpec((length, cb), lambda c, b: (b, pcol + order * ncb + c)),
            pl.BlockSpec((SHORT_CONV_W, cb), lambda c, b: (0, zconv + c)),
            pl.BlockSpec((1, cb), lambda c, b: (0, zconv + c)),
            pl.BlockSpec((SHORT_CONV_W, cb), lambda c, b: (0, order * ncb + c)),
            pl.BlockSpec((1, cb), lambda c, b: (0, order * ncb + c)),
            pl.BlockSpec((1, FFT_H1 + 1, 2 * FFT_N2, cb), lambda c, b: (order, 0, 0, c)),
            full(f1), full(f2), full(f2i), full(g),
            pl.BlockSpec((1, cb), lambda c, b: (0, c)),
        ],
        out_specs=pl.BlockSpec((length, cb), lambda c, b: (b, c)),
        out_shape=jax.ShapeDtypeStruct((batch * length, HYENA_WIDTH), BF16),
        scratch_shapes=[
            pltpu.VMEM((length + 16, cb), F32),
            pltpu.VMEM((cb // LANES, FFT_H1 * Z_PITCH, LANES), F32),
            pltpu.VMEM((cb // LANES, K1_PAD * AB_PITCH, LANES), F32),
        ],
        compiler_params=pltpu.CompilerParams(
            dimension_semantics=("arbitrary", "arbitrary"),
            vmem_limit_bytes=VMEM_LIMIT),
        name=f"hyena_order{order}",
    )(zsrc, proj, conv_w, conv_b[None], conv_w, conv_b[None], hspec, f1, f2, f2i, g, out_g)


def _hyena(proj, conv_w, conv_b, w1, b1, w2, b2, w3, freq, filt_bias, out_g, batch, length):
    assert 2 * length == FFT_N1 * FFT_N2 and HYENA_ORDER == 2
    consts = _dft_constants()
    max_decay = math.log(DECAY_TARGET) / FAST_DECAY_PCT
    min_decay = math.log(DECAY_TARGET) / SLOW_DECAY_PCT
    deltas = jnp.abs(jnp.linspace(min_decay, max_decay, HYENA_WIDTH, dtype=F32))[None]
    hid = _filter_hidden(length, w1, b1, w2, b2, freq)
    hspec = _filter_spectrum(hid, w3, deltas, filt_bias, consts, length)
    vcol = (3 * ATTN_WIDTH + HYENA_ORDER * HYENA_WIDTH) // HY_CB
    z1 = _hyena_order(proj, vcol, proj, 0, conv_w, conv_b, hspec, consts, out_g, batch, length, True, False)
    return _hyena_order(z1, 0, proj, 1, conv_w, conv_b, hspec, consts, out_g, batch, length, False, True)


def kernel(x, mix_norm_g, w_in, q_norm_g, k_norm_g, rpb, conv_w, conv_b, filt_w1, filt_b1, filt_w2, filt_b2, filt_w3, filt_freq, filt_bias, attn_out_g, hyena_out_g, w_out, ffn_norm_g, w_router, w_gate, w_up, w_down):
    b, s, d = x.shape
    rows = s // GRID_W
    win_r = min(WIN_ROWS_MAX, rows)
    cap = EC_CAPACITY_FACTOR * s // N_EXPERTS
    x2 = x.reshape(b * s, d)
    for i in range(mix_norm_g.shape[0]):
        proj = _inproj(x2, mix_norm_g[i][None], w_in[i],
                       q_norm_g[i][None], k_norm_g[i][None])
        attn = _attention(proj, rpb[i], attn_out_g[i][None], b, s)

        hyn = _hyena(proj, conv_w[i], conv_b[i], filt_w1[i], filt_b1[i], filt_w2[i], filt_b2[i], filt_w3[i],
                     filt_freq[i], filt_bias[i], hyena_out_g[i][None], b, s)

        wr_pad = jnp.zeros((d, 128), BF16).at[:, :N_EXPERTS].set(w_router[i].astype(BF16))
        x1, hn, logits = _outproj(attn, hyn, x2, w_out[i].astype(BF16), ffn_norm_g[i][None], wr_pad)

        idx, gate = _route(logits, b, s, cap)
        x2 = _moe(idx, gate, hn, x1, w_gate[i], w_up[i], w_down[i], b, s)
    return x2.reshape(b, s, d)
```

```python
import functools
import math

import numpy as np
import jax
import jax.numpy as jnp
from jax import lax
from jax.experimental import pallas as pl
from jax.experimental.pallas import tpu as pltpu

D_MODEL = 2048
GRID_W = 64
N_ATTN_HEADS = 8
ATTN_WIDTH = D_MODEL // 2
HEAD_DIM = ATTN_WIDTH // N_ATTN_HEADS
HYENA_WIDTH = D_MODEL - ATTN_WIDTH
N_HYENA_GROUPS = 8
HYENA_ORDER = 2
N_DIRS = 2
SHORT_CONV_W = 3
FILTER_EMB = 33
DECAY_TARGET = 1e-2
FAST_DECAY_PCT = 0.3
SLOW_DECAY_PCT = 1.5
WIN_ROWS_MAX = 8
WIN_COLS = 16
N_EXPERTS = 16
EC_CAPACITY_FACTOR = 2
EXPERT_FF = 1024
IN_WIDTH = 3 * ATTN_WIDTH + (HYENA_ORDER + 1) * HYENA_WIDTH
RMS_EPS = 1e-6

F32 = jnp.float32
BF16 = jnp.bfloat16
NEG_BIG = -1e30
VMEM_LIMIT = 56 * 1024 * 1024
LANES = 128
LOG2_E = math.log2(math.e)


def _rms(x, eps=RMS_EPS):
    return x * lax.rsqrt(jnp.mean(x * x, axis=-1, keepdims=True) + eps)


INPROJ_GROUPS = 2


def _inproj_kernel(x_ref, g_ref, w_ref, qg_ref, kg_ref, o_ref, h_ref):
    n = pl.program_id(1)
    bm, bn = o_ref.shape

    def step(normalise_input, head_gain):
        w = w_ref[...].astype(BF16)
        for r in range(INPROJ_GROUPS):
            rows = slice(r * bm // INPROJ_GROUPS, (r + 1) * bm // INPROJ_GROUPS)
            if normalise_input:
                h_ref[rows, :] = (_rms(x_ref[rows, :]) * g_ref[...]).astype(BF16)
            acc = jnp.dot(h_ref[rows, :], w, preferred_element_type=F32)
            if head_gain is None:
                o_ref[rows, :] = acc.astype(BF16)
            else:
                for j in range(bn // HEAD_DIM):
                    sl = slice(j * HEAD_DIM, (j + 1) * HEAD_DIM)
                    o_ref[rows, sl] = (_rms(acc[:, sl]) * head_gain).astype(BF16)

    @pl.when(n == 0)
    def _():
        step(True, qg_ref[...] * (HEAD_DIM ** -0.5 * LOG2_E))

    @pl.when(n == 1)
    def _():
        step(False, kg_ref[...])

    @pl.when(n >= 2)
    def _():
        step(False, None)


def _inproj(x2, g, w_bf, qg, kg, bm=1024, bn=1024):
    m, d = x2.shape
    nw = w_bf.shape[1]
    return pl.pallas_call(
        _inproj_kernel,
        grid=(m // bm, nw // bn),
        in_specs=[
            pl.BlockSpec((bm, d), lambda i, j: (i, 0)),
            pl.BlockSpec((1, d), lambda i, j: (0, 0)),
            pl.BlockSpec((d, bn), lambda i, j: (0, j)),
            pl.BlockSpec((1, HEAD_DIM), lambda i, j: (0, 0)),
            pl.BlockSpec((1, HEAD_DIM), lambda i, j: (0, 0)),
        ],
        out_specs=pl.BlockSpec((bm, bn), lambda i, j: (i, j)),
        out_shape=jax.ShapeDtypeStruct((m, nw), BF16),
        scratch_shapes=[pltpu.VMEM((bm, d), BF16)],
        compiler_params=pltpu.CompilerParams(
            dimension_semantics=("arbitrary", "arbitrary"),
            vmem_limit_bytes=VMEM_LIMIT),
        name="inproj",
    )(x2, g, w_bf, qg, kg)


ATTN_ROWS = 4
ATTN_BAND = 12


def _attn_kernel(q_ref, k_ref, v_ref, rpb_ref, mask_ref, g_ref, o_ref, pt_ref, *, rows, win_r):
    lanes = pt_ref.shape[-1]
    n_dr = 2 * WIN_ROWS_MAX - 1

    def toeplitz(d):
        row = jnp.broadcast_to(rpb_ref[0, d:d + 1, :], (GRID_W, lanes))
        return pltpu.roll(row, lanes - (WIN_COLS - 1), 1, stride=1, stride_axis=0)

    lane = lax.broadcasted_iota(jnp.int32, (GRID_W, lanes), 1)
    left = lane < GRID_W
    tiles = [toeplitz(d) for d in range(n_dr)]
    for d in range(-1, n_dr):
        lo, hi = tiles[max(d, 0)], tiles[min(d + 1, n_dr - 1)]
        pt_ref[d + 1] = jnp.where(left, lo, pltpu.roll(hi, GRID_W, 1)) * LOG2_E + mask_ref[...]

    def body(i, carry):
        r0 = i * ATTN_ROWS
        bs = jnp.minimum(jnp.clip(r0 - win_r // 2, 0, rows - win_r), rows - ATTN_BAND)
        q0 = pl.multiple_of(r0 * GRID_W, ATTN_ROWS * GRID_W)
        k0 = pl.multiple_of(bs * GRID_W, 2 * GRID_W)
        q = q_ref[pl.ds(q0, ATTN_ROWS * GRID_W), :]
        k = k_ref[pl.ds(k0, ATTN_BAND * GRID_W), :]
        v = v_ref[pl.ds(k0, ATTN_BAND * GRID_W), :]
        s = lax.dot_general(q, k, (((1,), (1,)), ((), ())), preferred_element_type=F32)

        bias_rows = []
        for j in range(ATTN_ROWS):
            r = r0 + j
            start = jnp.clip(r - win_r // 2, 0, rows - win_r)
            pieces = []
            for m in range(ATTN_BAND // 2):
                kr = bs + 2 * m
                pen_l = jnp.where((kr >= start) & (kr < start + win_r), 0.0, NEG_BIG)
                pen_r = jnp.where((kr + 1 >= start) & (kr + 1 < start + win_r), 0.0, NEG_BIG)
                d = jnp.clip(kr - r + WIN_ROWS_MAX - 1, -1, n_dr - 1)
                pieces.append(pt_ref[d + 1] + jnp.where(left, pen_l, pen_r))
            bias_rows.append(jnp.concatenate(pieces, axis=1))
        s = s + jnp.concatenate(bias_rows, axis=0)

        mx = jnp.max(s, axis=-1, keepdims=True)
        p = jnp.exp2(s - mx)
        l = jnp.sum(p, axis=-1, keepdims=True)
        o = jnp.dot(p.astype(BF16), v, preferred_element_type=F32) / l
        o = _rms(o) * g_ref[...]
        o_ref[pl.ds(q0, ATTN_ROWS * GRID_W), :] = o.astype(BF16)
        return carry

    lax.fori_loop(0, rows // ATTN_ROWS, body, 0, unroll=2)


def _attn_window_mask():
    c = np.arange(GRID_W)
    c0 = np.clip(c - WIN_COLS // 2, 0, GRID_W - WIN_COLS)
    in_win = (c[None, :] >= c0[:, None]) & (c[None, :] < c0[:, None] + WIN_COLS)
    return jnp.asarray(np.tile(np.where(in_win, 0.0, NEG_BIG), (1, 2)), dtype=F32)


def _attention(proj, rpb, out_g, batch, seq):
    rows = seq // GRID_W
    win_r = min(WIN_ROWS_MAX, rows)
    assert win_r == WIN_ROWS_MAX and 2 * GRID_W == LANES
    assert rows % ATTN_ROWS == 0 and ATTN_BAND >= win_r + ATTN_ROWS - 1 and ATTN_BAND % 2 == 0
    h = N_ATTN_HEADS
    n_dr, n_dc = rpb.shape[1:]
    rpb_pad = jnp.zeros((h, 2 * WIN_ROWS_MAX, LANES), F32).at[:, :n_dr, :n_dc].set(rpb)
    mask = _attn_window_mask()
    kern = functools.partial(_attn_kernel, rows=rows, win_r=win_r)
    return pl.pallas_call(
        kern,
        grid=(batch, h),
        in_specs=[
            pl.BlockSpec((seq, HEAD_DIM), lambda b, i: (b, i)),
            pl.BlockSpec((seq, HEAD_DIM), lambda b, i: (b, h + i)),
            pl.BlockSpec((seq, HEAD_DIM), lambda b, i: (b, 2 * h + i)),
            pl.BlockSpec((1, 2 * WIN_ROWS_MAX, LANES), lambda b, i: (i, 0, 0)),
            pl.BlockSpec(mask.shape, lambda b, i: (0, 0)),
            pl.BlockSpec((1, HEAD_DIM), lambda b, i: (0, i)),
        ],
        out_specs=pl.BlockSpec((seq, HEAD_DIM), lambda b, i: (b, i)),
        out_shape=jax.ShapeDtypeStruct((batch * seq, ATTN_WIDTH), BF16),
        scratch_shapes=[pltpu.VMEM((2 * WIN_ROWS_MAX, GRID_W, LANES), F32)],
        compiler_params=pltpu.CompilerParams(
            dimension_semantics=("arbitrary", "arbitrary"),
            vmem_limit_bytes=VMEM_LIMIT),
        name="na2d_attn",
    )(proj, proj, proj, rpb_pad, mask, out_g)


OUTPROJ_GROUPS = 2


def _outproj_kernel(a_ref, hy_ref, x_ref, wo_ref, g_ref, wr_ref, x1_ref, hn_ref, lg_ref):
    half = a_ref.shape[1]
    bm = x_ref.shape[0]
    for r in range(OUTPROJ_GROUPS):
        rows = slice(r * bm // OUTPROJ_GROUPS, (r + 1) * bm // OUTPROJ_GROUPS)
        acc = x_ref[rows, :]
        acc = acc + jnp.dot(a_ref[rows, :], wo_ref[:half, :], preferred_element_type=F32)
        acc = acc + jnp.dot(hy_ref[rows, :], wo_ref[half:, :], preferred_element_type=F32)
        x1_ref[rows, :] = acc
        hn = _rms(acc) * g_ref[...]
        hn_ref[rows, :] = hn
        lg_ref[rows, :] = jnp.dot(hn.astype(BF16), wr_ref[...], preferred_element_type=F32)


def _outproj(attn, hy, x2, wo_bf, g, wr_pad, bm=512):
    m, d = x2.shape
    half = attn.shape[1]
    npad = wr_pad.shape[1]
    return pl.pallas_call(
        _outproj_kernel,
        grid=(m // bm,),
        in_specs=[
            pl.BlockSpec((bm, half), lambda i: (i, 0)),
            pl.BlockSpec((bm, half), lambda i: (i, 0)),
            pl.BlockSpec((bm, d), lambda i: (i, 0)),
            pl.BlockSpec((d, d), lambda i: (0, 0)),
            pl.BlockSpec((1, d), lambda i: (0, 0)),
            pl.BlockSpec((d, npad), lambda i: (0, 0)),
        ],
        out_specs=[
            pl.BlockSpec((bm, d), lambda i: (i, 0)),
            pl.BlockSpec((bm, d), lambda i: (i, 0)),
            pl.BlockSpec((bm, npad), lambda i: (i, 0)),
        ],
        out_shape=[
            jax.ShapeDtypeStruct((m, d), F32),
            jax.ShapeDtypeStruct((m, d), F32),
            jax.ShapeDtypeStruct((m, npad), F32),
        ],
        compiler_params=pltpu.CompilerParams(
            dimension_semantics=("arbitrary",),
            vmem_limit_bytes=VMEM_LIMIT),
        name="outproj",
    )(attn, hy, x2, wo_bf, g, wr_pad)


ROUTE_BLK = 128
ROUTE_CHUNK = 512
SLOT_GRP = 8
ROUTE_GROUPS_INLINE = 3
F32_INF_BITS = 0x7F800000


def _route_kernel(lg_ref, tri_ref, idx_ref, gate_ref, aff_ref, pos_t_ref, aff_t_ref, off_ref, ia_ref, ga_ref,
                  stage_i, stage_g, off_smem, sem, *, cap):
    s, lanes = lg_ref.shape
    lane = lax.broadcasted_iota(jnp.int32, (1, lanes), 1)

    def softmax_chunk(i, c):
        rows = pl.ds(pl.multiple_of(i * ROUTE_CHUNK, ROUTE_CHUNK), ROUTE_CHUNK)
        lg = jnp.where(lane < N_EXPERTS, lg_ref[rows, :], NEG_BIG)
        ex = jnp.exp(lg - jnp.max(lg, axis=-1, keepdims=True))
        aff_ref[rows, :] = ex / jnp.sum(ex, axis=-1, keepdims=True)
        return c

    lax.fori_loop(0, s // ROUTE_CHUNK, softmax_chunk, 0)

    def as_value(bits):
        return pltpu.bitcast(bits, F32)

    def fold8(x):
        parts = [x[i:i + 8] for i in range(0, x.shape[0], 8)]
        while len(parts) > 1:
            parts = [a + b for a, b in zip(parts[::2], parts[1::2])]
        return parts[0]

    def count_ge(th):
        def body(i, acc):
            rows = pl.ds(pl.multiple_of(i * ROUTE_CHUNK, ROUTE_CHUNK), ROUTE_CHUNK)
            return acc + fold8(jnp.where(aff_ref[rows, :] >= th, 1.0, 0.0))
        acc = lax.fori_loop(0, s // ROUTE_CHUNK, body, jnp.zeros((8, lanes), F32))
        return jnp.sum(acc, axis=0, keepdims=True)

    def bisect(_, c):
        lo, hi = c
        mid = lo + ((hi - lo) >> 1)
        ok = count_ge(as_value(mid)) >= cap
        return jnp.where(ok, mid, lo), jnp.where(ok, hi, mid)

    tau_bits, _ = lax.fori_loop(0, 31, bisect, (jnp.zeros((1, lanes), jnp.int32),
                                                jnp.full((1, lanes), F32_INF_BITS, jnp.int32)))
    tau, above = as_value(tau_bits), as_value(tau_bits + 1)
    ties_wanted = cap - count_ge(above)

    def scan_block(j, c):
        tie_carry, sel_carry = c
        rows = pl.ds(pl.multiple_of(j * ROUTE_BLK, ROUTE_BLK), ROUTE_BLK)
        aff = aff_ref[rows, :]
        gt = aff >= above
        eq = jnp.where((aff >= tau) & (aff < above), 1.0, 0.0)
        tie_incl = jnp.dot(tri_ref[...], eq.astype(BF16), preferred_element_type=F32) + tie_carry
        sel = jnp.where(gt | ((eq > 0.0) & (tie_incl - eq < ties_wanted)), 1.0, 0.0)
        sel_incl = jnp.dot(tri_ref[...], sel.astype(BF16), preferred_element_type=F32) + sel_carry
        pos = jnp.where(sel > 0.0, sel_incl - sel, -1.0)
        pos_t_ref[j] = pos.T[:N_EXPERTS]
        aff_t_ref[j] = aff.T[:N_EXPERTS]
        off_ref[pl.ds(j, 1), :] = sel_carry.astype(jnp.int32)
        return tie_incl[ROUTE_BLK - 1:, :], sel_incl[ROUTE_BLK - 1:, :]

    n_blk = s // ROUTE_BLK
    off_ref[...] = jnp.zeros_like(off_ref)
    zero = jnp.zeros((1, lanes), F32)
    _, total = lax.fori_loop(0, n_blk, scan_block, (zero, zero), unroll=4)
    off_ref[pl.ds(n_blk, 1), :] = total.astype(jnp.int32)

    to_smem = pltpu.make_async_copy(off_ref, off_smem, sem)
    to_smem.start()
    to_smem.wait()

    n_grp = cap // SLOT_GRP
    sub = lax.broadcasted_iota(jnp.int32, (SLOT_GRP, lanes), 0)
    lane_i = lax.broadcasted_iota(jnp.int32, (SLOT_GRP, lanes), 1)
    stage_i[...] = jnp.zeros_like(stage_i)
    stage_g[...] = jnp.zeros_like(stage_g)
    for e in range(N_EXPERTS):
        ia_ref[...] = jnp.zeros_like(ia_ref)
        ga_ref[...] = jnp.zeros_like(ga_ref)

        def block(j, carry, e=e):
            first, last = off_smem[j, e], off_smem[j + 1, e]
            pos = jnp.broadcast_to(pos_t_ref[j, e:e + 1, :], (SLOT_GRP, lanes))
            aff = jnp.broadcast_to(aff_t_ref[j, e:e + 1, :], (SLOT_GRP, lanes))
            tok = (j * ROUTE_BLK + lane_i).astype(F32)
            g0 = first // SLOT_GRP
            groups = jnp.where(last > first, (last - 1) // SLOT_GRP - g0 + 1, 0)

            def update(g):
                match = pos == (g * SLOT_GRP + sub).astype(F32)
                gi = jnp.minimum(g, n_grp - 1)
                ia_ref[gi] += jnp.where(match, tok, 0.0)
                ga_ref[gi] += jnp.where(match, aff, 0.0)

            for k in range(ROUTE_GROUPS_INLINE):
                update(g0 + k)

            def rest(k, c):
                update(g0 + k)
                return c

            lax.fori_loop(ROUTE_GROUPS_INLINE, groups, rest, 0)
            return carry

        lax.fori_loop(0, n_blk, block, 0)
        stage_i[:, e:e + 1] = jnp.sum(ia_ref[...].reshape(cap, lanes), axis=1, keepdims=True)
        stage_g[:, e:e + 1] = jnp.sum(ga_ref[...].reshape(cap, lanes), axis=1, keepdims=True)

    idx_ref[0] = stage_i[...].T[:N_EXPERTS].astype(jnp.int32)
    gate_ref[0] = stage_g[...].T[:N_EXPERTS]


def _route(logits, batch, seq, cap):
    lanes = logits.shape[1]
    assert lanes == LANES == ROUTE_BLK and cap % SLOT_GRP == 0
    n_blk = seq // ROUTE_BLK
    n_off = -(-(n_blk + 1) // 8) * 8
    tri =jnp.asarray(np.tril(np.ones((ROUTE_BLK, ROUTE_BLK))), dtype=BF16)
    return pl.pallas_call(
        functools.partial(_route_kernel, cap=cap),
        grid=(batch,),
        in_specs=[
            pl.BlockSpec((seq, lanes), lambda b: (b, 0)),
            pl.BlockSpec((ROUTE_BLK, ROUTE_BLK), lambda b: (0, 0)),
        ],
        out_specs=[
            pl.BlockSpec((1, N_EXPERTS, cap), lambda b: (b, 0, 0)),
            pl.BlockSpec((1, N_EXPERTS, cap), lambda b: (b, 0, 0)),
        ],
        out_shape=[
            jax.ShapeDtypeStruct((batch, N_EXPERTS, cap), jnp.int32),
            jax.ShapeDtypeStruct((batch, N_EXPERTS, cap), F32),
        ],
        scratch_shapes=[
            pltpu.VMEM((seq, lanes), F32),
            pltpu.VMEM((n_blk, N_EXPERTS, lanes), F32),
            pltpu.VMEM((n_blk, N_EXPERTS, lanes), F32),
            pltpu.VMEM((n_off, lanes), jnp.int32),
            pltpu.VMEM((cap // SLOT_GRP, SLOT_GRP, lanes), F32),
            pltpu.VMEM((cap // SLOT_GRP, SLOT_GRP, lanes), F32),
            pltpu.VMEM((cap, lanes), F32),
            pltpu.VMEM((cap, lanes), F32),
            pltpu.SMEM((n_off, lanes), jnp.int32),
            pltpu.SemaphoreType.DMA(()),
        ],
        compiler_params=pltpu.CompilerParams(dimension_semantics=("arbitrary",)),
        name="route",
    )(logits, tri)


MOE_FF_SPLIT = 4
ROW_UNROLL = 8
DMA_PRIORITIES = 2


def _moe_kernel(idx_ref, hn_hbm, wg_ref, wu_ref, wd_ref, gate_ref, x1_hbm, out_hbm,
                xg0, xg1, acc0, acc1, y0, y1, sem_x, sem_a, sem_s, *, cap, seq):
    del x1_hbm
    e, f = pl.program_id(0), pl.program_id(1)
    n_e = pl.num_programs(0)
    xg, acc = (xg0, xg1), (acc0, acc1)
    prev_e, next_e = jnp.maximum(e - 1, 0), jnp.minimum(e + 1, n_e - 1)

    def row_of(ee, b):
        base = (b * n_e + ee) * cap
        return lambda p: b * seq + idx_ref[base + p]

    def copy_rows(ee, b, kind, rolled=False):
        row = row_of(ee, b)

        def one(p, lane):
            if kind == "xg":
                src, dst, sem = hn_hbm.at[pl.ds(row(p), 1), :], xg[b].at[pl.ds(p, 1), :], sem_x.at[b]
            elif kind == "ag":
                src, dst, sem = out_hbm.at[pl.ds(row(p), 1), :], acc[b].at[pl.ds(p, 1), :], sem_a.at[b]
            else:
                src, dst, sem = acc[b].at[pl.ds(p, 1), :], out_hbm.at[pl.ds(row(p), 1), :], sem_s.at[b]
            pltpu.make_async_copy(src, dst, sem).start(priority=lane % DMA_PRIORITIES)

        if rolled:
            def body(i, c):
                for k in range(ROW_UNROLL):
                    one(i * ROW_UNROLL + k, k)
                return c
            lax.fori_loop(0, cap // ROW_UNROLL, body, 0)
        else:
            for p in range(cap):
                one(p, p)

    def wait_rows(buf, sem):
        pltpu.make_async_copy(hn_hbm.at[pl.ds(0, cap), :], buf, sem).wait()

    def partial_out(b, wg, wu, wd):
        x = xg[b][...].astype(BF16)
        a = jnp.dot(x, wg, preferred_element_type=F32)
        u = jnp.dot(x, wu, preferred_element_type=F32)
        hmid = (a * (1.0 / (1.0 + jnp.exp(-a))) * u).astype(BF16)
        return jnp.dot(hmid, wd, preferred_element_type=F32)

    def weights():
        return wg_ref[0].astype(BF16), wu_ref[0].astype(BF16), wd_ref[0].astype(BF16)

    @pl.when((e == 0) & (f == 0))
    def _():
        copy_rows(e, 0, "xg", rolled=True)
        copy_rows(e, 0, "ag", rolled=True)
        copy_rows(e, 1, "ag", rolled=True)
        wait_rows(acc0, sem_a.at[0])
        wait_rows(acc1, sem_a.at[1])

    @pl.when(f == 0)
    def _():
        w = weights()
        wait_rows(xg0, sem_x.at[0])
        copy_rows(e, 1, "xg")
        y0[...] = partial_out(0, *w)
        wait_rows(xg1, sem_x.at[1])
        copy_rows(prev_e, 0, "sc")
        y1[...] = partial_out(1, *w)

    @pl.when(f == 1)
    def _():
        w = weights()
        copy_rows(prev_e, 1, "sc")
        y0[...] += partial_out(0, *w)
        wait_rows(acc0, sem_s.at[0])
        copy_rows(e, 0, "ag")
        y1[...] += partial_out(1, *w)

    @pl.when(f == 2)
    def _():
        w = weights()
        wait_rows(acc1, sem_s.at[1])
        copy_rows(e, 1, "ag")
        y0[...] += partial_out(0, *w)
        y1[...] += partial_out(1, *w)

    @pl.when(f == 3)
    def _():
        w = weights()
        t0 = y0[...] + partial_out(0, *w)
        wait_rows(acc0, sem_a.at[0])
        acc0[...] += t0 * gate_ref[0, 0]
        copy_rows(next_e, 0, "xg")
        t1 = y1[...] + partial_out(1, *w)
        wait_rows(acc1, sem_a.at[1])
        acc1[...] += t1 * gate_ref[1, 0]

    @pl.when((e == n_e - 1) & (f == MOE_FF_SPLIT - 1))
    def _():
        copy_rows(e, 0, "sc", rolled=True)
        copy_rows(e, 1, "sc", rolled=True)
        wait_rows(acc0, sem_s.at[0])
        wait_rows(acc1, sem_s.at[1])
        wait_rows(xg0, sem_x.at[0])


def _moe(idx, gate, hn, x1, wg, wu, wd, batch, seq):
    n_e, d, ff = wg.shape
    cap = idx.shape[-1]
    assert batch == 2 and MOE_FF_SPLIT == 4 and ff % MOE_FF_SPLIT == 0
    fq = ff // MOE_FF_SPLIT
    grid_spec = pltpu.PrefetchScalarGridSpec(
        num_scalar_prefetch=1,
        grid=(n_e, MOE_FF_SPLIT),
        in_specs=[
            pl.BlockSpec(memory_space=pl.ANY),
            pl.BlockSpec((1, d, fq), lambda i, j, s: (i, 0, j)),
            pl.BlockSpec((1, d, fq), lambda i, j, s: (i, 0, j)),
            pl.BlockSpec((1, fq, d), lambda i, j, s: (i, j, 0)),
            pl.BlockSpec((batch, 1, cap, 1), lambda i, j, s: (0, i, 0, 0)),
            pl.BlockSpec(memory_space=pl.ANY),
        ],
        out_specs=pl.BlockSpec(memory_space=pl.ANY),
        scratch_shapes=(
            [pltpu.VMEM((cap, d), F32) for _ in range(6)]
            + [pltpu.SemaphoreType.DMA((batch,)) for _ in range(3)]),
    )
    return pl.pallas_call(
        functools.partial(_moe_kernel, cap=cap, seq=seq),
        grid_spec=grid_spec,
        out_shape=jax.ShapeDtypeStruct(x1.shape, x1.dtype),
        input_output_aliases={6: 0},
        compiler_params=pltpu.CompilerParams(
            dimension_semantics=("arbitrary", "arbitrary"),
            vmem_limit_bytes=VMEM_LIMIT),
        name="moe_experts",
    )(idx.reshape(-1), hn, wg, wu, wd, gate[..., None], x1)


FFT_N1 = 128
FFT_N2 = 64
FFT_H1 = FFT_N1 // 2
K1_PAD = 72
Z_PITCH = 72
AB_PITCH = 136
HY_CB = 256
FFT_UNROLL = 8
HP = lax.Precision.HIGHEST


def _dft_constants():
    n, n1, n2, h1 = FFT_N1 * FFT_N2, FFT_N1, FFT_N2, FFT_H1
    k1 = np.arange(h1 + 1, dtype=np.float64)
    s1 = np.arange(h1, dtype=np.float64)
    s2 = np.arange(n2, dtype=np.float64)
    tw = s2[:, None, None] * k1[None, :, None] / n

    def stage1(phase_s1):
        th = -2.0 * np.pi * (phase_s1 + tw)
        m = np.zeros((n2, 2 * K1_PAD, h1))
        m[:, :h1 + 1] = np.cos(th)
        m[:, K1_PAD:K1_PAD + h1 + 1] = np.sin(th)
        return m

    f1 = stage1(s1[None, None, :] * k1[None, :, None] / n1)
    f1b = stage1(((n1 - 1) - s1)[None, None, :] * k1[None, :, None] / n1)
    f1b0 = stage1((n1 - s1)[None, None, :] * k1[None, :, None] / n1)[0]
    f1b0[:, 0] = 0.0
    f1b[0] = f1b0

    a = 2.0 * np.pi * np.outer(np.arange(n2), np.arange(n2)) / n2
    c, s = np.cos(a), np.sin(a)
    f2 = np.block([[c, s], [-s, c]])
    f2i = np.block([[c, -s], [s, c]])

    t1 = np.arange(h1, dtype=np.float64)
    th = 2.0 * np.pi * (t1[None, :, None] * k1[None, None, :] / n1 + s2[:, None, None] * k1[None, None, :] / n)
    wgt = np.where((k1 == 0) | (k1 == h1), 1.0, 2.0) / n
    g = np.zeros((n2, h1, 2 * K1_PAD))
    g[:, :, :h1 + 1] = wgt * np.cos(th)
    g[:, :, K1_PAD:K1_PAD + h1 + 1] = -wgt * np.sin(th)
    return tuple(jnp.asarray(m, dtype=F32).astype(BF16) for m in (f1, f1b, f2, f2i, g))


def _hid_kernel(f_ref, w1t_ref, w1c_ref, w1s_ref, b1_ref, w2_ref, b2_ref, fr_ref, o_ref, *, length):
    bm = o_ref.shape[0]
    pos = (pl.program_id(0) * bm + lax.broadcasted_iota(jnp.int32, (1, bm), 1)).astype(F32)
    t = pos / (length - 1.0)
    ang = f_ref[...] * (2.0 * math.pi * pos / length)
    pre = (w1t_ref[...] * t
           + jnp.dot(w1c_ref[...], jnp.cos(ang), precision=HP, preferred_element_type=F32)
           - jnp.dot(w1s_ref[...], jnp.sin(ang), precision=HP, preferred_element_type=F32))
    fr = fr_ref[...]
    hid = jnp.sin(fr * (pre + b1_ref[...]))
    hid = jnp.sin(fr * (jnp.dot(w2_ref[...], hid, precision=HP, preferred_element_type=F32) + b2_ref[...]))
    hi = hid.astype(BF16).astype(F32)
    lo = hid - hi
    o_ref[...] = jnp.concatenate([hi, hi, lo, jnp.zeros_like(hi)], axis=0).T.astype(BF16)


def _filter_hidden(length, w1, b1, w2, b2, freq, bm=512):
    bands = (FILTER_EMB - 1) // 2
    f = jnp.linspace(1e-4, bands - 1, bands, dtype=F32)[:, None]
    hdim = w1.shape[1]
    full = lambda a: pl.BlockSpec(a.shape, lambda i: (0,) * a.ndim)
    args = (f, w1[:1].T, w1[1:1 + bands].T, w1[1 + bands:].T, b1[:, None], w2.T, b2[:, None], freq[:, None])
    return pl.pallas_call(
        functools.partial(_hid_kernel, length=length),
        grid=(length // bm,),
        in_specs=[full(a) for a in args],
        out_specs=pl.BlockSpec((bm, 4 * hdim), lambda i: (i, 0)),
        out_shape=jax.ShapeDtypeStruct((length, 4 * hdim), BF16),
        name="filter_hidden",
    )(*args)


def _ld(ref, start, size, stride=None):
    idx = pl.ds(start, size) if stride is None else pl.ds(start, size, stride=stride)
    return jnp.concatenate([ref[h, idx, :] for h in range(ref.shape[0])], axis=-1)


def _st(ref, start, size, val, stride=None):
    idx = pl.ds(start, size) if stride is None else pl.ds(start, size, stride=stride)
    for h in range(ref.shape[0]):
        ref[h, idx, :] = val[:, h * LANES:(h + 1) * LANES]


def _stage1_store(ab_ref, s2, a):
    _st(ab_ref, s2, K1_PAD, a[:K1_PAD], stride=AB_PITCH)
    _st(ab_ref, FFT_N2 + s2, K1_PAD, a[K1_PAD:], stride=AB_PITCH)


def _stack_split_rows(w):
    hi = w.astype(BF16)
    lo = (w - hi.astype(F32)).astype(BF16)
    return jnp.concatenate([hi, lo, hi, jnp.zeros_like(hi)], axis=0)


def _spectrum_kernel(hid_ref, w3f_ref, w3b_ref, dl_ref, skip_ref, f1_ref, f2_ref, o_ref,
                     hf_ref, hb_ref, ab_ref, *, length):
    blocks = 4
    rows = blocks * FFT_N2
    w3f, w3b = _stack_split_rows(w3f_ref[...]), _stack_split_rows(w3b_ref[...])

    def gen(j, c):
        row = (j * rows + lax.broadcasted_iota(jnp.int32, (rows, 1), 0)).astype(F32)
        win = jnp.exp(-(row / (length - 1.0)) * dl_ref[...])
        hid = hid_ref[pl.ds(pl.multiple_of(j * rows, rows), rows), :]
        hf = jnp.dot(hid, w3f, preferred_element_type=F32) * win
        hb = jnp.dot(hid, w3b, preferred_element_type=F32) * win
        for q in range(blocks):
            dst = pl.multiple_of((j * blocks + q) * Z_PITCH, 8)
            _st(hf_ref, dst, FFT_N2, hf[q * FFT_N2:(q + 1) * FFT_N2])
            _st(hb_ref, dst, FFT_N2, hb[q * FFT_N2:(q + 1) * FFT_N2])
        return c

    lax.fori_loop(0, FFT_H1 // blocks, gen, 0)

    def stage1(s2, c):
        xf = _ld(hf_ref, s2, FFT_H1, Z_PITCH).astype(BF16)
        xb = _ld(hb_ref, (FFT_N2 - s2) % FFT_N2, FFT_H1, Z_PITCH).astype(BF16)
        x = jnp.concatenate([xf, xb], axis=0)
        _stage1_store(ab_ref, s2, jnp.dot(f1_ref[s2], x, preferred_element_type=F32))
        return c

    lax.fori_loop(0, FFT_N2, stage1, 0, unroll=2 * FFT_UNROLL)

    def stage2(k1, c):
        a = _ld(ab_ref, pl.multiple_of(k1 * AB_PITCH, 8), 2 * FFT_N2).astype(BF16)
        x = jnp.dot(f2_ref[...], a, preferred_element_type=F32)
        o_ref[0, k1, :FFT_N2, :] = (x[:FFT_N2] + skip_ref[0]).astype(BF16)
        o_ref[0, k1, FFT_N2:, :] = x[FFT_N2:].astype(BF16)
        return c

    lax.fori_loop(0, FFT_H1 + 1, stage2, 0, unroll=13)


def _filter_spectrum(hid, w3, deltas, filt_bias, consts, length):
    f1, f1b, f2, _, _ = consts
    f1 = jnp.concatenate([f1, f1b], axis=2)
    cb = HY_CB
    ncb = HYENA_WIDTH // cb
    hdim = w3.shape[0]
    full = lambda a: pl.BlockSpec(a.shape, lambda o, c: (0,) * a.ndim)
    return pl.pallas_call(
        functools.partial(_spectrum_kernel, length=length),
        grid=(HYENA_ORDER, ncb),
        in_specs=[
            full(hid),
            pl.BlockSpec((hdim, cb), lambda o, c: (0, (o * N_DIRS) * ncb + c)),
            pl.BlockSpec((hdim, cb), lambda o, c: (0, (o * N_DIRS + 1) * ncb + c)),
            pl.BlockSpec((1, cb), lambda o, c: (0, c)),
            pl.BlockSpec((1, 1, cb), lambda o, c: (o, 0, c)),
            full(f1), full(f2),
        ],
        out_specs=pl.BlockSpec((1, FFT_H1 + 1, 2 * FFT_N2, cb), lambda o, c: (o, 0, 0, c)),
        out_shape=jax.ShapeDtypeStruct((HYENA_ORDER, FFT_H1 + 1, 2 * FFT_N2, HYENA_WIDTH), BF16),
        scratch_shapes=[
            pltpu.VMEM((cb // LANES, FFT_H1 * Z_PITCH, LANES), F32),
            pltpu.VMEM((cb // LANES, FFT_H1 * Z_PITCH, LANES), F32),
            pltpu.VMEM((cb // LANES, K1_PAD * AB_PITCH, LANES), F32),
        ],
        compiler_params=pltpu.CompilerParams(
            dimension_semantics=("arbitrary", "arbitrary"),
            vmem_limit_bytes=VMEM_LIMIT),
        name="filter_spectrum",
    )(hid, w3, w3, deltas, filt_bias[:, None, :], f1, f2)


def _short_conv_block(s_ref, j, w_ref, b_ref):
    big = s_ref[pl.ds(pl.multiple_of(j * FFT_N2, FFT_N2), FFT_N2 + 16), :]
    rows = FFT_N2 + 16
    prev = pltpu.roll(big, 1, 0)[8:8 + FFT_N2]
    nxt = pltpu.roll(big, rows - 1, 0)[8:8 + FFT_N2]
    cur = big[8:8 + FFT_N2]
    return b_ref[...] + prev * w_ref[0:1, :] + cur * w_ref[1:2, :] + nxt * w_ref[2:3, :]


def _load_stream(s_ref, src_ref, length):
    zeros = jnp.zeros((8, s_ref.shape[1]), F32)
    s_ref[0:8, :] = zeros
    s_ref[length + 8:length + 16, :] = zeros
    chunk = 512

    def cp(i, c):
        r = pl.multiple_of(i * chunk, chunk)
        s_ref[pl.ds(r + 8, chunk), :] = src_ref[pl.ds(r, chunk), :].astype(F32)
        return c

    lax.fori_loop(0, length // chunk, cp, 0)


def _hyena_kernel(z_ref, gs_ref, cwz_ref, cbz_ref, cwg_ref, cbg_ref, h_ref, f1_ref, f2_ref, f2i_ref, g_ref,
                  og_ref, o_ref, s_ref, zy_ref, ab_ref, *, length, conv_z, norm):
    if conv_z:
        _load_stream(s_ref, z_ref, length)

    def prep(j, c):
        if conv_z:
            zblk = _short_conv_block(s_ref, j, cwz_ref, cbz_ref)
        else:
            zblk = z_ref[pl.ds(pl.multiple_of(j * FFT_N2, FFT_N2), FFT_N2), :].astype(F32)
        _st(zy_ref, pl.multiple_of(j * Z_PITCH, 8), FFT_N2, zblk)
        return c

    lax.fori_loop(0, FFT_H1, prep, 0, unroll=2)

    def stage1(s2, c):
        xs = _ld(zy_ref, s2, FFT_H1, Z_PITCH).astype(BF16)
        _stage1_store(ab_ref, s2, jnp.dot(f1_ref[s2], xs, preferred_element_type=F32))
        return c

    lax.fori_loop(0, FFT_N2, stage1, 0, unroll=2 * FFT_UNROLL)

    def stage2(k1, c):
        blk = pl.multiple_of(k1 * AB_PITCH, 8)
        x = jnp.dot(f2_ref[...], _ld(ab_ref, blk, 2 * FFT_N2).astype(BF16), preferred_element_type=F32)
        hk = h_ref[0, k1].astype(F32)
        xr, xi, hr, hi = x[:FFT_N2], x[FFT_N2:], hk[:FFT_N2], hk[FFT_N2:]
        p = jnp.concatenate([xr * hr - xi * hi, xr * hi + xi * hr], axis=0).astype(BF16)
        _st(ab_ref, blk, 2 * FFT_N2, jnp.dot(f2i_ref[...], p, preferred_element_type=F32))
        return c

    lax.fori_loop(0, FFT_H1 + 1, stage2, 0, unroll=13)

    def stage3(t2, c):
        br = _ld(ab_ref, t2, K1_PAD, AB_PITCH)
        bi = _ld(ab_ref, FFT_N2 + t2, K1_PAD, AB_PITCH)
        rhs = jnp.concatenate([br, bi], axis=0).astype(BF16)
        _st(zy_ref, t2, FFT_H1, jnp.dot(g_ref[t2], rhs, preferred_element_type=F32), stride=Z_PITCH)
        return c

    lax.fori_loop(0, FFT_N2, stage3, 0, unroll=2 * FFT_UNROLL)

    _load_stream(s_ref, gs_ref, length)

    def fin(j, c):
        gate = _short_conv_block(s_ref, j, cwg_ref, cbg_ref)
        y = gate * _ld(zy_ref, pl.multiple_of(j * Z_PITCH, 8), FFT_N2)
        dst = pl.ds(pl.multiple_of(j * FFT_N2, FFT_N2), FFT_N2)
        if norm:
            gw = HYENA_WIDTH // N_HYENA_GROUPS
            for q in range(y.shape[1] // gw):
                sl = slice(q * gw, (q + 1) * gw)
                o_ref[dst, sl] = (_rms(y[:, sl]) * og_ref[:, sl]).astype(o_ref.dtype)
        else:
            o_ref[dst, :] = y.astype(o_ref.dtype)
        return c

    lax.fori_loop(0, FFT_H1, fin, 0, unroll=4)


def _hyena_order(zsrc, zcol, proj, order, conv_w, conv_b, hspec, consts, out_g, batch, length, conv_z, norm):
    f1, _, f2, f2i, g = consts
    cb = HY_CB
    ncb = HYENA_WIDTH // cb
    pcol = 3 * ATTN_WIDTH // cb
    zconv = HYENA_ORDER * ncb
    full = lambda a: pl.BlockSpec(a.shape, lambda c, b: (0,) * a.ndim)
    kern = functools.partial(_hyena_kernel, length=length, conv_z=conv_z, norm=norm)
    return pl.pallas_call(
        kern,
        grid=(ncb, batch),
        in_specs=[
            pl.BlockSpec((length, cb), lambda c, b: (b, zcol + c)),
            pl.BlockSpec((length, cb), lambda c, b: (b, pcol + order * ncb + c)),
            pl.BlockSpec((SHORT_CONV_W, cb), lambda c, b: (0, zconv + c)),
            pl.BlockSpec((1, cb), lambda c, b: (0, zconv + c)),
            pl.BlockSpec((SHORT_CONV_W, cb), lambda c, b: (0, order * ncb + c)),
            pl.BlockSpec((1, cb), lambda c, b: (0, order * ncb + c)),
            pl.BlockSpec((1, FFT_H1 + 1, 2 * FFT_N2, cb), lambda c, b: (order, 0, 0, c)),
            full(f1), full(f2), full(f2i), full(g),
            pl.BlockSpec((1, cb), lambda c, b: (0, c)),
        ],
        out_specs=pl.BlockSpec((length, cb), lambda c, b: (b, c)),
        out_shape=jax.ShapeDtypeStruct((batch * length, HYENA_WIDTH), BF16),
        scratch_shapes=[
            pltpu.VMEM((length + 16, cb), F32),
            pltpu.VMEM((cb // LANES, FFT_H1 * Z_PITCH, LANES), F32),
            pltpu.VMEM((cb // LANES, K1_PAD * AB_PITCH, LANES), F32),
        ],
        compiler_params=pltpu.CompilerParams(
            dimension_semantics=("arbitrary", "arbitrary"),
            vmem_limit_bytes=VMEM_LIMIT),
        name=f"hyena_order{order}",
    )(zsrc, proj, conv_w, conv_b[None], conv_w, conv_b[None], hspec, f1, f2, f2i, g, out_g)


def _hyena(proj, conv_w, conv_b, w1, b1, w2, b2, w3, freq, filt_bias, out_g, batch, length):
    assert 2 * length == FFT_N1 * FFT_N2 and HYENA_ORDER == 2
    consts = _dft_constants()
    max_decay = math.log(DECAY_TARGET) / FAST_DECAY_PCT
    min_decay = math.log(DECAY_TARGET) / SLOW_DECAY_PCT
    deltas = jnp.abs(jnp.linspace(min_decay, max_decay, HYENA_WIDTH, dtype=F32))[None]
    hid = _filter_hidden(length, w1, b1, w2, b2, freq)
    hspec = _filter_spectrum(hid, w3, deltas, filt_bias, consts, length)
    vcol = (3 * ATTN_WIDTH + HYENA_ORDER * HYENA_WIDTH) // HY_CB
    z1 = _hyena_order(proj, vcol, proj, 0, conv_w, conv_b, hspec, consts, out_g, batch, length, True, False)
    return _hyena_order(z1, 0, proj, 1, conv_w, conv_b, hspec, consts, out_g, batch, length, False, True)


def kernel(x, mix_norm_g, w_in, q_norm_g, k_norm_g, rpb, conv_w, conv_b, filt_w1, filt_b1, filt_w2, filt_b2, filt_w3, filt_freq, filt_bias, attn_out_g, hyena_out_g, w_out, ffn_norm_g, w_router, w_gate, w_up, w_down):
    b, s, d = x.shape
    rows = s // GRID_W
    win_r = min(WIN_ROWS_MAX, rows)
    cap = EC_CAPACITY_FACTOR * s // N_EXPERTS
    x2 = x.reshape(b * s, d)
    for i in range(mix_norm_g.shape[0]):
        proj = _inproj(x2, mix_norm_g[i][None], w_in[i],
                       q_norm_g[i][None], k_norm_g[i][None])
        attn = _attention(proj, rpb[i], attn_out_g[i][None], b, s)

        hyn = _hyena(proj, conv_w[i], conv_b[i], filt_w1[i], filt_b1[i], filt_w2[i], filt_b2[i], filt_w3[i],
                     filt_freq[i], filt_bias[i], hyena_out_g[i][None], b, s)

        wr_pad = jnp.zeros((d, 128), BF16).at[:, :N_EXPERTS].set(w_router[i].astype(BF16))
        x1, hn, logits = _outproj(attn, hyn, x2, w_out[i].astype(BF16), ffn_norm_g[i][None], wr_pad)

        idx, gate = _route(logits, b, s, cap)
        x2 = _moe(idx, gate, hn, x1, w_gate[i], w_up[i], w_down[i], b, s)
    return x2.reshape(b, s, d)
```
